```python
import math
import jax, jax.numpy as jnp
from jax import lax
import numpy as np

D_MODEL = 4096
BATCH = 2
SEQ = 4096
DEPTH = 2

MEM_LEN = 256
N_BRANCH = 4
BRANCH_WIDTH = D_MODEL // 4
POOL_WINDOWS = (2, 4, 8, 16)
POOL_GROUP = BRANCH_WIDTH // 4
GLA_HEADS = 4
GLA_DK = BRANCH_WIDTH // 2
GLA_DV = BRANCH_WIDTH
GLA_HEAD_DK = GLA_DK // GLA_HEADS
GLA_HEAD_DV = GLA_DV // GLA_HEADS
GLA_RANK = 16
GLA_TAU = 16.0
GLA_CHUNK = 64
SWA_HEAD_DIM = 64
SWA_Q_HEADS = BRANCH_WIDTH // SWA_HEAD_DIM
SWA_KV_HEADS = SWA_Q_HEADS // 8
SWA_GROUP = SWA_Q_HEADS // SWA_KV_HEADS
SWA_KV_WIDTH = SWA_KV_HEADS * SWA_HEAD_DIM
WINDOW = 128
SWA_BLOCK = 128
ROPE_THETA = 500000.0
ROPE_DIM = SWA_HEAD_DIM // 4
XA_HEADS = 4
XA_HEAD_DIM = BRANCH_WIDTH // XA_HEADS
LN_EPS = 1e-5
DEEPNORM_ALPHA = (2 * DEPTH) ** 0.25
DEEPNORM_BETA = (8 * DEPTH) ** -0.25

IN_SPLITS = (
    BRANCH_WIDTH, BRANCH_WIDTH,
    GLA_DK, GLA_DK, GLA_DV, GLA_DV, GLA_RANK,
    BRANCH_WIDTH, SWA_KV_WIDTH, SWA_KV_WIDTH, BRANCH_WIDTH,
    BRANCH_WIDTH, BRANCH_WIDTH,
    N_BRANCH * D_MODEL,
)
D_IN = sum(IN_SPLITS)

kernel_name = "hybrid_pool_gla_swa_mem_deepnorm"


def _split_points():
    return [int(v) for v in np.cumsum(IN_SPLITS)[:-1]]


def layer_norm(x, g, b):
    xf = x.astype(jnp.float32)
    mu = jnp.mean(xf, axis=-1, keepdims=True)
    var = jnp.mean(jnp.square(xf - mu), axis=-1, keepdims=True)
    y = (xf - mu) * lax.rsqrt(var + LN_EPS) * g.astype(jnp.float32) + b.astype(jnp.float32)
    return y.astype(x.dtype)


def rope_partial(x, positions):
    half = ROPE_DIM // 2
    inv_freq = ROPE_THETA ** (-jnp.arange(0, ROPE_DIM, 2, dtype=jnp.float32) / ROPE_DIM)
    ang = positions.astype(jnp.float32)[:, :, None] * inv_freq
    cos = jnp.cos(ang)[:, :, None, :]
    sin = jnp.sin(ang)[:, :, None, :]
    xf = x.astype(jnp.float32)
    x1, x2, rest = xf[..., :half], xf[..., half:ROPE_DIM], xf[..., ROPE_DIM:]
    out = jnp.concatenate([x1 * cos - x2 * sin, x2 * cos + x1 * sin, rest], axis=-1)
    return out.astype(x.dtype)


def pool_mixer(u, w_pool, pool_scale):
    S = u.shape[1]
    t = jnp.arange(S)
    outs = []
    for gi, w in enumerate(POOL_WINDOWS):
        ug = u[..., gi * POOL_GROUP:(gi + 1) * POOL_GROUP].astype(jnp.float32)
        c = jnp.cumsum(ug, axis=1)
        c_lag = jnp.pad(c, ((0, 0), (w, 0), (0, 0)))[:, :S]
        cnt = jnp.minimum(t + 1, w).astype(jnp.float32)[None, :, None]
        pooled = (c - c_lag) / cnt - ug
        outs.append(jnp.einsum('bsc,cd->bsd', pooled.astype(u.dtype), w_pool[gi]))
    return jnp.concatenate(outs, axis=-1) * pool_scale


def gla_mixer(q, k, v, lr, w_gla_up, b_gla, gla_norm):
    B, S, _ = q.shape
    C = GLA_CHUNK
    n = S // C
    log_a = jax.nn.log_sigmoid((lr @ w_gla_up + b_gla).astype(jnp.float32)) / GLA_TAU

    def to_chunks(a, hd):
        return a.astype(jnp.float32).reshape(B, n, C, GLA_HEADS, hd).transpose(1, 0, 3, 2, 4)

    qc = to_chunks(q, GLA_HEAD_DK) * (GLA_HEAD_DK ** -0.5)
    kc = to_chunks(k, GLA_HEAD_DK)
    vc = to_chunks(v, GLA_HEAD_DV)
    gc = to_chunks(log_a, GLA_HEAD_DK)
    tri = jnp.tril(jnp.ones((C, C), dtype=bool))

    def step(state, inp):
        qb, kb, vb, gb = inp
        b = jnp.cumsum(gb, axis=2)
        o_inter = jnp.einsum('bhcd,bhde->bhce', qb * jnp.exp(b), state)
        diff = b[:, :, :, None, :] - b[:, :, None, :, :]
        diff = jnp.where(tri[None, None, :, :, None], diff, -jnp.inf)
        attn = jnp.sum(qb[:, :, :, None, :] * kb[:, :, None, :, :] * jnp.exp(diff), axis=-1)
        o_intra = jnp.einsum('bhij,bhje->bhie', attn, vb)
        b_last = b[:, :, -1:, :]
        k_dec = kb * jnp.exp(b_last - b)
        state = jnp.exp(b_last[:, :, 0, :])[..., None] * state + jnp.einsum('bhcd,bhce->bhde', k_dec, vb)
        return state, o_inter + o_intra

    state0 = jnp.zeros((B, GLA_HEADS, GLA_HEAD_DK, GLA_HEAD_DV), jnp.float32)
    _, o = lax.scan(step, state0, (qc, kc, vc, gc))
    o = o.transpose(1, 0, 3, 2, 4).reshape(B, S, GLA_HEADS, GLA_HEAD_DV)
    o = o * lax.rsqrt(jnp.mean(jnp.square(o), axis=-1, keepdims=True) + LN_EPS)
    o = o.reshape(B, S, GLA_DV) * gla_norm.astype(jnp.float32)
    return o.astype(q.dtype)


def swa_mixer(q, k, v, positions, sinks):
    B, S, _ = q.shape
    nb = S // SWA_BLOCK
    q = rope_partial(q.reshape(B, S, SWA_Q_HEADS, SWA_HEAD_DIM), positions) * (SWA_HEAD_DIM ** -0.5)
    k = rope_partial(k.reshape(B, S, SWA_KV_HEADS, SWA_HEAD_DIM), positions)
    v = v.reshape(B, S, SWA_KV_HEADS, SWA_HEAD_DIM)
    qb = q.reshape(B, nb, SWA_BLOCK, SWA_KV_HEADS, SWA_GROUP, SWA_HEAD_DIM)

    def band(a):
        ap = jnp.pad(a, ((0, 0), (SWA_BLOCK, 0), (0, 0), (0, 0)))
        ap = ap.reshape(B, nb + 1, SWA_BLOCK, SWA_KV_HEADS, SWA_HEAD_DIM)
        return jnp.concatenate([ap[:, :-1], ap[:, 1:]], axis=2)

    kb, vb = band(k), band(v)
    s = jnp.einsum('bnqkgd,bnskd->bnkgqs', qb, kb).astype(jnp.float32)
    qi = jnp.arange(SWA_BLOCK)
    sj = jnp.arange(2 * SWA_BLOCK)
    blk = jnp.arange(nb)
    diff = qi[:, None] + SWA_BLOCK - sj[None, :]
    key_abs = blk[:, None] * SWA_BLOCK - SWA_BLOCK + sj[None, :]
    valid = ((diff >= 0) & (diff < WINDOW))[None, :, :] & (key_abs >= 0)[:, None, :]
    s = jnp.where(valid[None, :, None, None, :, :], s, -jnp.inf)
    sink = sinks.astype(jnp.float32).reshape(SWA_KV_HEADS, SWA_GROUP)[None, None, :, :, None, None]
    sink = jnp.broadcast_to(sink, s.shape[:-1] + (1,))
    p = jax.nn.softmax(jnp.concatenate([s, sink], axis=-1), axis=-1)[..., :-1]
    o = jnp.einsum('bnkgqs,bnskd->bnqkgd', p.astype(vb.dtype), vb)
    return o.reshape(B, S, BRANCH_WIDTH)


def mem_attn(q, mem, w_mem_kv):
    B, S, _ = q.shape
    M = mem.shape[1]
    mkv = mem @ w_mem_kv
    mk = mkv[..., :BRANCH_WIDTH].reshape(B, M, XA_HEADS, XA_HEAD_DIM)
    mv = mkv[..., BRANCH_WIDTH:].reshape(B, M, XA_HEADS, XA_HEAD_DIM)
    qh = q.reshape(B, S, XA_HEADS, XA_HEAD_DIM)
    s = jnp.einsum('bshd,bmhd->bhsm', qh, mk).astype(jnp.float32) * (XA_HEAD_DIM ** -0.5)
    p = jax.nn.softmax(s, axis=-1)
    o = jnp.einsum('bhsm,bmhd->bshd', p.astype(mv.dtype), mv)
    return o.reshape(B, S, BRANCH_WIDTH)


def hybrid_layer(x, mem, positions, w_in, w_pool, pool_scale, w_gla_up, b_gla, gla_norm,
                 sinks, w_mem_kv, w_branch, w_out, ln_g, ln_b):
    h = x @ w_in
    (pool_u, pool_g, gq, gk, gv, gg, glr, sq, sk, sv, sg, xq, xg, mg) = jnp.split(h, _split_points(), axis=-1)
    o_pool = pool_mixer(pool_u, w_pool, pool_scale) * jax.nn.silu(pool_g)
    o_gla = gla_mixer(gq, gk, gv, glr, w_gla_up, b_gla, gla_norm) * jax.nn.silu(gg)
    o_swa = swa_mixer(sq, sk, sv, positions, sinks) * jax.nn.silu(sg)
    o_mem = mem_attn(xq, mem, w_mem_kv) * jax.nn.silu(xg)
    branches = (o_pool, o_gla, o_swa, o_mem)
    y = None
    for bi in range(N_BRANCH):
        gate = jax.nn.sigmoid(mg[..., bi * D_MODEL:(bi + 1) * D_MODEL])
        term = gate * (branches[bi] @ w_branch[bi])
        y = term if y is None else y + term
    out = y @ w_out
    return layer_norm(DEEPNORM_ALPHA * x + out, ln_g, ln_b)


def setup_inputs(seed: int = 0) -> dict:
    key = jax.random.key(seed)
    ks = jax.random.split(key, 16)
    f32 = jnp.float32
    x = jax.random.normal(ks[0], (BATCH, SEQ, D_MODEL), f32)
    mem = jax.random.normal(ks[1], (BATCH, MEM_LEN, D_MODEL), f32)
    offset = jax.random.randint(ks[2], (BATCH, 1), 0, 1024, dtype=jnp.int32)
    positions = (offset + jnp.arange(SEQ, dtype=jnp.int32)[None, :]).astype(jnp.int32)
    w_in = jax.random.normal(ks[3], (DEPTH, D_MODEL, D_IN), f32) * D_MODEL ** -0.5
    w_pool = jax.random.normal(ks[4], (DEPTH, len(POOL_WINDOWS), POOL_GROUP, POOL_GROUP), f32) * POOL_GROUP ** -0.5
    pool_scale = 1.0 + 0.02 * jax.random.normal(ks[5], (DEPTH, BRANCH_WIDTH), f32)
    w_gla_up = jax.random.normal(ks[6], (DEPTH, GLA_RANK, GLA_DK), f32) * GLA_RANK ** -0.5
    b_gla = 0.01 * jax.random.normal(ks[7], (DEPTH, GLA_DK), f32)
    gla_norm = 1.0 + 0.02 * jax.random.normal(ks[8], (DEPTH, GLA_DV), f32)
    sinks = 0.5 * jax.random.normal(ks[9], (DEPTH, SWA_Q_HEADS), f32)
    w_mem_kv = jax.random.normal(ks[10], (DEPTH, D_MODEL, 2 * BRANCH_WIDTH), f32) * D_MODEL ** -0.5
    w_branch = jax.random.normal(ks[11], (DEPTH, N_BRANCH, BRANCH_WIDTH, D_MODEL), f32) * (BRANCH_WIDTH ** -0.5 * DEEPNORM_BETA)
    w_out = jax.random.normal(ks[12], (DEPTH, D_MODEL, D_MODEL), f32) * (D_MODEL ** -0.5 * DEEPNORM_BETA)
    ln_g = 1.0 + 0.02 * jax.random.normal(ks[13], (DEPTH, D_MODEL), f32)
    ln_b = 0.02 * jax.random.normal(ks[14], (DEPTH, D_MODEL), f32)
    return {"x": x, "mem": mem, "positions": positions, "w_in": w_in, "w_pool": w_pool,
            "pool_scale": pool_scale, "w_gla_up": w_gla_up, "b_gla": b_gla, "gla_norm": gla_norm,
            "sinks": sinks, "w_mem_kv": w_mem_kv, "w_branch": w_branch, "w_out": w_out,
            "ln_g": ln_g, "ln_b": ln_b}


def reference(x, mem, positions, w_in, w_pool, pool_scale, w_gla_up, b_gla, gla_norm,
              sinks, w_mem_kv, w_branch, w_out, ln_g, ln_b):
    for l in range(DEPTH):
        x = hybrid_layer(x, mem, positions, w_in[l], w_pool[l], pool_scale[l], w_gla_up[l],
                         b_gla[l], gla_norm[l], sinks[l], w_mem_kv[l], w_branch[l], w_out[l],
                         ln_g[l], ln_b[l])
    return x
```

```python
import functools
import math

import jax
import jax.numpy as jnp
import numpy as np
from jax import lax
from jax.experimental import pallas as pl
from jax.experimental.pallas import tpu as pltpu

F32 = jnp.float32
BF16 = jnp.bfloat16

N_BRANCH = 4
POOL_WINDOWS = (2, 4, 8, 16)
GLA_HEADS = 4
GLA_RANK = 16
GLA_TAU = 16.0
GLA_CHUNK = 64
GLA_SUB = 16
SWA_HEAD_DIM = 64
SWA_GROUP = 8
WINDOW = 128
SWA_BLOCK = 128
ROPE_THETA = 500000.0
ROPE_DIM = SWA_HEAD_DIM // 4
XA_HEADS = 4
LN_EPS = 1e-5

LANES = 128
V7X_VMEM_BYTES = 64 * 1024 * 1024
VMEM_CAP = V7X_VMEM_BYTES - 8 * 1024 * 1024


def _vmem_limit(block_bytes):
    return int(min(VMEM_CAP, 2 * block_bytes + 16 * 1024 * 1024))


def _nbytes(shape, dtype):
    return int(np.prod(shape)) * jnp.dtype(dtype).itemsize


def _silu(g):
    return g * jax.nn.sigmoid(g)


def _div_pow2(x, n):
    assert n & (n - 1) == 0
    return x >> (n.bit_length() - 1)


def _mod_pow2(x, n):
    assert n & (n - 1) == 0
    return x & (n - 1)


def _mm_kernel(a_ref, b_ref, o_ref):
    o_ref[...] = jnp.dot(a_ref[...], b_ref[...], preferred_element_type=F32).astype(o_ref.dtype)


def _mm_resid_kernel(a_ref, b_ref, x_ref, o_ref, *, alpha):
    acc = jnp.dot(a_ref[...], b_ref[...], preferred_element_type=F32)
    o_ref[...] = alpha * x_ref[...] + acc


def _matmul(a, b, out_dtype, tm, tn, name, resid=None, alpha=None):
    m, k = a.shape
    _, n = b.shape
    assert m % tm == 0 and n % tn == 0
    blocks = _nbytes((tm, k), a.dtype) + _nbytes((k, tn), b.dtype) + _nbytes((tm, tn), out_dtype)
    in_specs = [pl.BlockSpec((tm, k), lambda i, j: (i, 0)),
                pl.BlockSpec((k, tn), lambda i, j: (0, j))]
    args = [a, b]
    if resid is None:
        kern = _mm_kernel
    else:
        kern = functools.partial(_mm_resid_kernel, alpha=alpha)
        in_specs.append(pl.BlockSpec((tm, tn), lambda i, j: (i, j)))
        args.append(resid)
        blocks += _nbytes((tm, tn), resid.dtype)
    return pl.pallas_call(
        kern,
        out_shape=jax.ShapeDtypeStruct((m, n), out_dtype),
        grid=(m // tm, n // tn),
        in_specs=in_specs,
        out_specs=pl.BlockSpec((tm, tn), lambda i, j: (i, j)),
        compiler_params=pltpu.CompilerParams(
            dimension_semantics=("parallel", "parallel"), vmem_limit_bytes=_vmem_limit(blocks)),
        name=name,
    )(*args)


def _merge_kernel(x_ref, o0, o1, o2, o3, g0, g1, g2, g3, w0, w1, w2, w3, y_ref):
    x = x_ref[...]
    acc = None
    for o_ref, g_ref, w_ref in ((o0, g0, w0), (o1, g1, w1), (o2, g2, w2), (o3, g3, w3)):
        gate = jnp.dot(x, g_ref[...], preferred_element_type=F32)
        proj = jnp.dot(o_ref[...], w_ref[...], preferred_element_type=F32)
        term = jax.nn.sigmoid(gate) * proj
        acc = term if acc is None else acc + term
    y_ref[...] = acc.astype(y_ref.dtype)


def _merge(xb, branches, w_mg, w_br, tm, tn):
    m, d = xb.shape
    bw = branches[0].shape[1]
    nj = d // tn
    in_specs = [pl.BlockSpec((tm, d), lambda i, j: (i, 0))]
    in_specs += [pl.BlockSpec((tm, bw), lambda i, j: (i, 0)) for _ in range(N_BRANCH)]
    in_specs += [pl.BlockSpec((d, tn), functools.partial(lambda i, j, b: (0, b * nj + j), b=b))
                 for b in range(N_BRANCH)]
    in_specs += [pl.BlockSpec((None, bw, tn), functools.partial(lambda i, j, b: (b, 0, j), b=b))
                 for b in range(N_BRANCH)]
    blocks = (_nbytes((tm, d), BF16) + N_BRANCH * _nbytes((tm, bw), BF16)
              + N_BRANCH * _nbytes((d + bw, tn), BF16) + _nbytes((tm, tn), BF16))
    return pl.pallas_call(
        _merge_kernel,
        out_shape=jax.ShapeDtypeStruct((m, d), BF16),
        grid=(m // tm, nj),
        in_specs=in_specs,
        out_specs=pl.BlockSpec((tm, tn), lambda i, j: (i, j)),
        compiler_params=pltpu.CompilerParams(
            dimension_semantics=("parallel", "parallel"), vmem_limit_bytes=_vmem_limit(blocks)),
        name="merge",
    )(xb, *branches, w_mg, w_mg, w_mg, w_mg, w_br, w_br, w_br, w_br)


def _ln_kernel(z_ref, g_ref, b_ref, o_ref, ob_ref=None):
    z = z_ref[...]
    mu = jnp.mean(z, axis=-1, keepdims=True)
    zc = z - mu
    var = jnp.mean(zc * zc, axis=-1, keepdims=True)
    y = zc * lax.rsqrt(var + LN_EPS) * g_ref[...] + b_ref[...]
    o_ref[...] = y
    if ob_ref is not None:
        ob_ref[...] = y.astype(BF16)


def _layer_norm(z, g, b, tm, emit_bf16):
    m, d = z.shape
    out_shape = [jax.ShapeDtypeStruct((m, d), F32)]
    out_specs = [pl.BlockSpec((tm, d), lambda i: (i, 0))]
    if emit_bf16:
        out_shape.append(jax.ShapeDtypeStruct((m, d), BF16))
        out_specs.append(pl.BlockSpec((tm, d), lambda i: (i, 0)))
    blocks = 2 * _nbytes((tm, d), F32) + _nbytes((tm, d), BF16)
    res = pl.pallas_call(
        _ln_kernel,
        out_shape=out_shape,
        grid=(m // tm,),
        in_specs=[pl.BlockSpec((tm, d), lambda i: (i, 0)),
                  pl.BlockSpec((1, d), lambda i: (0, 0)),
                  pl.BlockSpec((1, d), lambda i: (0, 0))],
        out_specs=out_specs,
        compiler_params=pltpu.CompilerParams(
            dimension_semantics=("parallel",), vmem_limit_bytes=_vmem_limit(blocks)),
        name="layer_norm",
    )(z, g.reshape(1, d), b.reshape(1, d))
    return res if emit_bf16 else (res[0], None)


POOL_HIST = 16


def _pool_kernel(u_ref, g_ref, wp_ref, sc_ref, o_ref, buf, *, ts, group):
    i = pl.program_id(1)

    @pl.when(i == 0)
    def _():
        buf[0:POOL_HIST, :] = jnp.zeros((POOL_HIST, buf.shape[1]), F32)

    buf[POOL_HIST:POOL_HIST + ts, :] = u_ref[...]
    t = i * ts + lax.broadcasted_iota(jnp.int32, (ts, 1), 0)
    outs = []
    for gi, w in enumerate(POOL_WINDOWS):
        cols = slice(gi * group, (gi + 1) * group)
        cur = buf[POOL_HIST:POOL_HIST + ts, cols]
        acc = cur
        for lag in range(1, w):
            acc = acc + buf[POOL_HIST - lag:POOL_HIST - lag + ts, cols]
        cnt = jnp.minimum(t + 1, w).astype(F32)
        pooled = acc / cnt - cur
        outs.append(jnp.dot(pooled.astype(BF16), wp_ref[gi], preferred_element_type=F32))
    o = jnp.concatenate(outs, axis=1) * sc_ref[...] * _silu(g_ref[...])
    o_ref[...] = o.astype(o_ref.dtype)
    buf[0:POOL_HIST, :] = buf[ts:ts + POOL_HIST, :]


def _pool(h, w_pool_b, pool_scale, bsz, seq, col_u, col_g, ts):
    bw = pool_scale.shape[-1]
    group = bw // len(POOL_WINDOWS)
    nblk = seq // ts
    blocks = 2 * _nbytes((ts, bw), F32) + _nbytes((ts, bw), BF16) + _nbytes((ts + POOL_HIST, bw), F32)
    return pl.pallas_call(
        functools.partial(_pool_kernel, ts=ts, group=group),
        out_shape=jax.ShapeDtypeStruct((bsz * seq, bw), BF16),
        grid=(bsz, nblk),
        in_specs=[pl.BlockSpec((ts, bw), lambda b, i: (b * nblk + i, col_u // bw)),
                  pl.BlockSpec((ts, bw), lambda b, i: (b * nblk + i, col_g // bw)),
                  pl.BlockSpec(w_pool_b.shape, lambda b, i: (0, 0, 0)),
                  pl.BlockSpec((1, bw), lambda b, i: (0, 0))],
        out_specs=pl.BlockSpec((ts, bw), lambda b, i: (b * nblk + i, 0)),
        scratch_shapes=[pltpu.VMEM((ts + POOL_HIST, bw), F32)],
        compiler_params=pltpu.CompilerParams(
            dimension_semantics=("parallel", "arbitrary"), vmem_limit_bytes=_vmem_limit(blocks)),
        name="pool_mixer",
    )(h, h, w_pool_b, pool_scale.reshape(1, bw))


def _split_bf16(x, parts):
    out = []
    r = x
    for _ in range(parts):
        p = r.astype(BF16)
        out.append(p)
        r = r - p.astype(F32)
    return out


def _gla_kernel(q_ref, k_ref, v_ref, gg_ref, lr_ref, wup_ref, bg_ref, nrm_ref, o_ref,
                state_ref, b_ref, *, rows, dk, dv):
    i = pl.program_id(1)
    C = GLA_CHUNK
    nsub = C // GLA_SUB

    @pl.when(i == 0)
    def _():
        state_ref[...] = jnp.zeros(state_ref.shape, F32)

    lr_h, lr_m = _split_bf16(lr_ref[...], 2)
    w_h, w_m = _split_bf16(wup_ref[...], 2)
    z = (jnp.dot(lr_h, w_h, preferred_element_type=F32)
         + jnp.dot(lr_h, w_m, preferred_element_type=F32)
         + jnp.dot(lr_m, w_h, preferred_element_type=F32)) + bg_ref[...]
    log_a = -(jnp.maximum(-z, 0.0) + jnp.log1p(jnp.exp(-jnp.abs(z)))) / GLA_TAU

    rr = lax.broadcasted_iota(jnp.int32, (rows, rows), 0)
    cc = lax.broadcasted_iota(jnp.int32, (rows, rows), 1)
    tri = jnp.where((_div_pow2(rr, C) == _div_pow2(cc, C)) & (cc <= rr), 1.0, 0.0).astype(BF16)
    bsum = None
    for part in _split_bf16(log_a, 3):
        t = jnp.dot(tri, part, preferred_element_type=F32)
        bsum = t if bsum is None else bsum + t
    b_ref[...] = bsum

    row = lax.broadcasted_iota(jnp.int32, (C, 1), 0)
    col = lax.broadcasted_iota(jnp.int32, (1, C), 1)
    row_sub = _div_pow2(row, GLA_SUB)
    col_sub = _div_pow2(col, GLA_SUB)
    lane_c = lax.broadcasted_iota(jnp.int32, (GLA_SUB, C), 1)
    scale = dk ** -0.5

    def chunk_body(c, carry):
        r0 = pl.multiple_of(c * C, C)
        for h in range(GLA_HEADS):
            kc = slice(h * dk, (h + 1) * dk)
            vc = slice(h * dv, (h + 1) * dv)
            qs = q_ref[pl.ds(r0, C), kc] * scale
            k = k_ref[pl.ds(r0, C), kc]
            v = v_ref[pl.ds(r0, C), vc].astype(BF16)
            b = b_ref[pl.ds(r0, C), kc]
            b_last = b[C - 1:C, :]
            st = state_ref[h]

            q_in = (qs * jnp.exp(b)).astype(BF16)
            o = lax.dot_general(q_in, st.astype(BF16), (((1,), (1,)), ((), ())),
                                preferred_element_type=F32)

            b_heads = [b[a * GLA_SUB:a * GLA_SUB + 1, :] for a in range(nsub)]
            r_rows = jnp.concatenate(
                [jnp.broadcast_to(bh, (GLA_SUB, dk)) for bh in b_heads], axis=0)
            q_t = qs * jnp.exp(b - r_rows)
            lhs = jnp.concatenate(
                [jnp.where(row_sub == a, q_t, 0.0) for a in range(nsub)], axis=1).astype(BF16)
            rhs = jnp.concatenate(
                [k * jnp.exp(jnp.minimum(b_heads[a] - b, 0.0)) for a in range(nsub)],
                axis=1).astype(BF16)
            a_off = lax.dot_general(lhs, rhs, (((1,), (1,)), ((), ())),
                                    preferred_element_type=F32)

            diag_blocks = []
            for a in range(nsub):
                rs = slice(a * GLA_SUB, (a + 1) * GLA_SUB)
                qa = qs[rs, :]
                ba = b[rs, :]
                blk = jnp.zeros((GLA_SUB, C), F32)
                for jj in range(GLA_SUB):
                    j = a * GLA_SUB + jj
                    kj = k[j:j + 1, :]
                    bj = b[j:j + 1, :]
                    tj = qa * kj * jnp.exp(jnp.minimum(ba - bj, 0.0))
                    cj = jnp.sum(tj, axis=1, keepdims=True)
                    blk = jnp.where(lane_c == j, cj, blk)
                diag_blocks.append(blk)
            a_diag = jnp.concatenate(diag_blocks, axis=0)
            attn = jnp.where(row_sub > col_sub, a_off,
                             jnp.where((row_sub == col_sub) & (row >= col), a_diag, 0.0))
            o = o + jnp.dot(attn.astype(BF16), v, preferred_element_type=F32)

            k_dec = (k * jnp.exp(b_last - b)).astype(BF16)
            upd = lax.dot_general(v, k_dec, (((0,), (0,)), ((), ())),
                                  preferred_element_type=F32)
            state_ref[h] = st * jnp.exp(b_last) + upd

            ms = jnp.mean(o * o, axis=-1, keepdims=True)
            on = o * lax.rsqrt(ms + LN_EPS) * nrm_ref[:, vc]
            o_ref[pl.ds(r0, C), vc] = (on * _silu(gg_ref[pl.ds(r0, C), vc])).astype(o_ref.dtype)
        return carry

    lax.fori_loop(0, rows // C, chunk_body, 0)


def _gla(h, w_up_pad, b_gla, gla_norm, bsz, seq, cols, rows):
    col_q, col_k, col_v, col_g, col_lr = cols
    dkt = b_gla.shape[-1]
    dvt = gla_norm.shape[-1]
    dk, dv = dkt // GLA_HEADS, dvt // GLA_HEADS
    lrw = w_up_pad.shape[0]
    nblk = seq // rows
    blocks = (2 * _nbytes((rows, dkt), F32) + 2 * _nbytes((rows, dvt), F32)
              + _nbytes((rows, lrw), F32) + _nbytes((rows, dvt), BF16)
              + _nbytes((rows, dkt), F32) + _nbytes((GLA_HEADS, dv, dk), F32))
    return pl.pallas_call(
        functools.partial(_gla_kernel, rows=rows, dk=dk, dv=dv),
        out_shape=jax.ShapeDtypeStruct((bsz * seq, dvt), BF16),
        grid=(bsz, nblk),
        in_specs=[pl.BlockSpec((rows, dkt), lambda b, i: (b * nblk + i, col_q // dkt)),
                  pl.BlockSpec((rows, dkt), lambda b, i: (b * nblk + i, col_k // dkt)),
                  pl.BlockSpec((rows, dvt), lambda b, i: (b * nblk + i, col_v // dvt)),
                  pl.BlockSpec((rows, dvt), lambda b, i: (b * nblk + i, col_g // dvt)),
                  pl.BlockSpec((rows, lrw), lambda b, i: (b * nblk + i, col_lr // lrw)),
                  pl.BlockSpec((lrw, dkt), lambda b, i: (0, 0)),
                  pl.BlockSpec((1, dkt), lambda b, i: (0, 0)),
                  pl.BlockSpec((1, dvt), lambda b, i: (0, 0))],
        out_specs=pl.BlockSpec((rows, dvt), lambda b, i: (b * nblk + i, 0)),
        scratch_shapes=[pltpu.VMEM((GLA_HEADS, dv, dk), F32),
                        pltpu.VMEM((rows, dkt), F32)],
        compiler_params=pltpu.CompilerParams(
            dimension_semantics=("parallel", "arbitrary"), vmem_limit_bytes=_vmem_limit(blocks)),
        name="gla_mixer",
    )(h, h, h, h, h, w_up_pad, b_gla.reshape(1, dkt), gla_norm.reshape(1, dvt))


def _rope_lane_table():
    half = ROPE_DIM // 2
    inv_freq = ROPE_THETA ** (-np.arange(0, ROPE_DIM, 2, dtype=np.float32) / ROPE_DIM)
    lane = np.arange(LANES) % SWA_HEAD_DIM
    tab = np.where(lane < ROPE_DIM, inv_freq[lane % half], 0.0).astype(np.float32)
    return jnp.asarray(tab.reshape(1, LANES))


def _swa_kernel(sink_ref, q_ref, k_ref, v_ref, g_ref, pos_ref, inv_ref, o_ref, wk_ref, wv_ref,
                *, n_pairs):
    n = pl.program_id(1)
    blk = SWA_BLOCK
    half = ROPE_DIM // 2
    hd = SWA_HEAD_DIM

    @pl.when(n == 0)
    def _():
        wk_ref[...] = jnp.zeros(wk_ref.shape, BF16)
        wv_ref[...] = jnp.zeros(wv_ref.shape, BF16)

    lane = lax.broadcasted_iota(jnp.int32, (1, LANES), 1)
    lane_h = _mod_pow2(lane, hd)
    lo = lane < hd
    ang = pos_ref[...] * inv_ref[...]
    cos = jnp.cos(ang)
    sin = jnp.sin(ang)

    def rope(x):
        x_up = pltpu.roll(x, LANES - half, axis=1)
        x_dn = pltpu.roll(x, half, axis=1)
        return x * cos + jnp.where(lane_h < half, -x_up, x_dn) * sin

    for g in range(2):
        for part in range(2):
            base = part * 2 * blk
            wk_ref[g, base:base + blk, :] = wk_ref[g, base + blk:base + 2 * blk, :]
            wv_ref[g, base:base + blk, :] = wv_ref[g, base + blk:base + 2 * blk, :]
    k_r = rope(k_ref[...])
    v_c = v_ref[...]
    k_sw = pltpu.roll(k_r, hd, axis=1)
    v_sw = pltpu.roll(v_c, hd, axis=1)
    zero = jnp.zeros_like(k_r)
    for g in range(2):
        k_lo, k_hi = (k_r, k_sw) if g == 0 else (k_sw, k_r)
        v_lo, v_hi = (v_c, v_sw) if g == 0 else (v_sw, v_c)
        wk_ref[g, blk:2 * blk, :] = jnp.where(lo, k_lo, zero).astype(BF16)
        wk_ref[g, 3 * blk:4 * blk, :] = jnp.where(lo, zero, k_hi).astype(BF16)
        wv_ref[g, blk:2 * blk, :] = jnp.where(lo, v_lo, zero).astype(BF16)
        wv_ref[g, 3 * blk:4 * blk, :] = jnp.where(lo, zero, v_hi).astype(BF16)

    r = lax.broadcasted_iota(jnp.int32, (blk, 2 * blk), 0)
    c = lax.broadcasted_iota(jnp.int32, (blk, 2 * blk), 1)
    valid = (c > r) & (c <= r + WINDOW) & ((n > 0) | (c >= blk))
    neg_inf = jnp.float32(-jnp.inf)
    qscale = hd ** -0.5
    pairs_per_group = n_pairs // 2
    for p in range(n_pairs):
        g = p // pairs_per_group
        cols = slice(p * LANES, (p + 1) * LANES)
        q_pair = (rope(q_ref[:, cols]) * qscale).astype(BF16)
        s = lax.dot_general(q_pair, wk_ref[g], (((1,), (1,)), ((), ())),
                            preferred_element_type=F32)
        probs = []
        for hh in range(2):
            sink = sink_ref[2 * p + hh]
            sh = jnp.where(valid, s[:, hh * 2 * blk:(hh + 1) * 2 * blk], neg_inf)
            m = jnp.maximum(jnp.max(sh, axis=-1, keepdims=True), sink)
            e = jnp.exp(sh - m)
            denom = jnp.sum(e, axis=-1, keepdims=True) + jnp.exp(sink - m)
            probs.append((e / denom).astype(BF16))
        pcat = jnp.concatenate(probs, axis=1)
        o = jnp.dot(pcat, wv_ref[g], preferred_element_type=F32)
        o_ref[:, cols] = (o * _silu(g_ref[:, cols])).astype(o_ref.dtype)


def _swa(h, pos_f, sinks, bsz, seq, cols):
    col_q, col_k, col_v, col_g = cols
    n_heads = sinks.shape[0]
    bw = n_heads * SWA_HEAD_DIM
    n_pairs = bw // LANES
    blk = SWA_BLOCK
    nblk = seq // blk
    blocks = (2 * _nbytes((blk, bw), F32) + 2 * _nbytes((blk, LANES), F32) + _nbytes((blk, LANES), F32)
              + _nbytes((blk, bw), BF16) + 2 * _nbytes((2, 4 * blk, LANES), BF16))
    return pl.pallas_call(
        functools.partial(_swa_kernel, n_pairs=n_pairs),
        out_shape=jax.ShapeDtypeStruct((bsz * seq, bw), BF16),
        grid=(bsz, nblk),
        in_specs=[pl.BlockSpec(memory_space=pltpu.SMEM),
                  pl.BlockSpec((blk, bw), lambda b, i: (b * nblk + i, col_q // bw)),
                  pl.BlockSpec((blk, LANES), lambda b, i: (b * nblk + i, col_k // LANES)),
                  pl.BlockSpec((blk, LANES), lambda b, i: (b * nblk + i, col_v // LANES)),
                  pl.BlockSpec((blk, bw), lambda b, i: (b * nblk + i, col_g // bw)),
                  pl.BlockSpec((blk, 1), lambda b, i: (b * nblk + i, 0)),
                  pl.BlockSpec((1, LANES), lambda b, i: (0, 0))],
        out_specs=pl.BlockSpec((blk, bw), lambda b, i: (b * nblk + i, 0)),
        scratch_shapes=[pltpu.VMEM((2, 4 * blk, LANES), BF16),
                        pltpu.VMEM((2, 4 * blk, LANES), BF16)],
        compiler_params=pltpu.CompilerParams(
            dimension_semantics=("parallel", "arbitrary"), vmem_limit_bytes=_vmem_limit(blocks)),
        name="swa_mixer",
    )(sinks, h, h, h, h, pos_f, _rope_lane_table())


def _mem_kernel(q_ref, g_ref, mk_ref, mv_ref, o_ref, *, hd):
    scale = hd ** -0.5
    for h in range(XA_HEADS):
        cols = slice(h * hd, (h + 1) * hd)
        qh = q_ref[:, cols].astype(BF16)
        s = lax.dot_general(qh, mk_ref[:, cols], (((1,), (1,)), ((), ())),
                            preferred_element_type=F32) * scale
        m = jnp.max(s, axis=-1, keepdims=True)
        e = jnp.exp(s - m)
        p = e / jnp.sum(e, axis=-1, keepdims=True)
        o = jnp.dot(p.astype(BF16), mv_ref[:, cols], preferred_element_type=F32)
        o_ref[:, cols] = (o * _silu(g_ref[:, cols])).astype(o_ref.dtype)


def _mem_attn(h, mkv, bsz, seq, mem_len, cols, ts):
    col_q, col_g = cols
    bw = mkv.shape[1] // 2
    hd = bw // XA_HEADS
    nblk = seq // ts
    blocks = 2 * _nbytes((ts, bw), F32) + 2 * _nbytes((mem_len, bw), BF16) + _nbytes((ts, bw), BF16)
    return pl.pallas_call(
        functools.partial(_mem_kernel, hd=hd),
        out_shape=jax.ShapeDtypeStruct((bsz * seq, bw), BF16),
        grid=(bsz, nblk),
        in_specs=[pl.BlockSpec((ts, bw), lambda b, i: (b * nblk + i, col_q // bw)),
                  pl.BlockSpec((ts, bw), lambda b, i: (b * nblk + i, col_g // bw)),
                  pl.BlockSpec((mem_len, bw), lambda b, i: (b, 0)),
                  pl.BlockSpec((mem_len, bw), lambda b, i: (b, 1))],
        out_specs=pl.BlockSpec((ts, bw), lambda b, i: (b * nblk + i, 0)),
        compiler_params=pltpu.CompilerParams(
            dimension_semantics=("parallel", "parallel"), vmem_limit_bytes=_vmem_limit(blocks)),
        name="mem_attn",
    )(h, h, mkv, mkv)


def _main_layout(d_model):
    bw = d_model // 4
    dkt, dvt = bw // 2, bw
    kvw = 2 * SWA_HEAD_DIM
    names = ["pool_u", "pool_g", "gq", "gk", "gv", "gg", "glr", "sq", "sk", "sv", "sg", "xq", "xg"]
    widths = [bw, bw, dkt, dkt, dvt, dvt, GLA_RANK, bw, kvw, kvw, bw, bw, bw]
    src = {}
    off = 0
    for nme, w in zip(names, widths):
        src[nme] = (off, w)
        off += w
    order = ["pool_u", "pool_g", "gv", "gg", "sq", "sg", "xq", "xg", "gq", "gk", "sk", "sv", "glr"]
    dst = {}
    pieces = []
    off = 0
    for nme in order:
        w = src[nme][1]
        assert off % (w if nme != "glr" else 2 * LANES) == 0
        dst[nme] = off
        pieces.append(src[nme])
        off += w
    lr_block = 2 * LANES
    padded = dst["glr"] + lr_block
    return pieces, dst, padded, off, lr_block


def _hybrid_layer(x, xb, mem_b, pos_f, w_in, w_pool, pool_scale, w_gla_up, b_gla, gla_norm,
                  sinks, w_mem_kv, w_branch, w_out, ln_g, ln_b, alpha, emit_bf16, bsz, seq):
    m, d = x.shape
    pieces, dst, padded, used, lr_block = _main_layout(d)
    n_main = used

    w_main = jnp.concatenate(
        [w_in[:, o:o + w] for o, w in pieces] + [jnp.zeros((d, padded - used), w_in.dtype)],
        axis=1).astype(BF16)
    w_mg = w_in[:, n_main:].astype(BF16)
    w_up_pad = jnp.concatenate(
        [w_gla_up, jnp.zeros((lr_block - GLA_RANK, w_gla_up.shape[1]), w_gla_up.dtype)], axis=0)

    h = _matmul(xb, w_main, F32, tm=1024, tn=512, name="in_proj")
    mkv = _matmul(mem_b, w_mem_kv.astype(BF16), BF16, tm=mem_b.shape[0], tn=512, name="mem_kv")

    o_pool = _pool(h, w_pool.astype(BF16), pool_scale, bsz, seq, dst["pool_u"], dst["pool_g"], ts=512)
    o_gla = _gla(h, w_up_pad, b_gla, gla_norm, bsz, seq,
                 (dst["gq"], dst["gk"], dst["gv"], dst["gg"], dst["glr"]), rows=512)
    o_swa = _swa(h, pos_f, sinks, bsz, seq, (dst["sq"], dst["sk"], dst["sv"], dst["sg"]))
    o_mem = _mem_attn(h, mkv, bsz, seq, mem_b.shape[0] // bsz, (dst["xq"], dst["xg"]), ts=512)

    y = _merge(xb, (o_pool, o_gla, o_swa, o_mem), w_mg, w_branch.astype(BF16), tm=512, tn=256)
    z = _matmul(y, w_out.astype(BF16), F32, tm=1024, tn=512, name="out_proj", resid=x, alpha=alpha)
    return _layer_norm(z, ln_g, ln_b, tm=256, emit_bf16=emit_bf16)


def kernel(x, mem, positions, w_in, w_pool, pool_scale, w_gla_up, b_gla, gla_norm, sinks,
           w_mem_kv, w_branch, w_out, ln_g, ln_b):
    bsz, seq, d = x.shape
    depth = w_in.shape[0]
    alpha = (2 * depth) ** 0.25
    xf = x.reshape(bsz * seq, d)
    xb = xf.astype(BF16)
    mem_b = mem.reshape(bsz * mem.shape[1], d).astype(BF16)
    pos_f = positions.astype(F32).reshape(bsz * seq, 1)
    for l in range(depth):
        xf, xb = _hybrid_layer(
            xf, xb, mem_b, pos_f, w_in[l], w_pool[l], pool_scale[l], w_gla_up[l], b_gla[l],
            gla_norm[l], sinks[l], w_mem_kv[l], w_branch[l], w_out[l], ln_g[l], ln_b[l],
            alpha, l + 1 < depth, bsz, seq)
    return xf.reshape(bsz, seq, d)
```

```python
import functools
import math

import jax
import jax.numpy as jnp
import numpy as np
from jax import lax
from jax.experimental import pallas as pl
from jax.experimental.pallas import tpu as pltpu

F32 = jnp.float32
BF16 = jnp.bfloat16

N_BRANCH = 4
POOL_WINDOWS = (2, 4, 8, 16)
GLA_HEADS = 4
GLA_RANK = 16
GLA_TAU = 16.0
GLA_CHUNK = 64
GLA_SUB = 16
SWA_HEAD_DIM = 64
SWA_GROUP = 8
WINDOW = 128
SWA_BLOCK = 128
ROPE_THETA = 500000.0
ROPE_DIM = SWA_HEAD_DIM // 4
XA_HEADS = 4
LN_EPS = 1e-5

LANES = 128
V7X_VMEM_BYTES = 64 * 1024 * 1024
VMEM_CAP = V7X_VMEM_BYTES - 8 * 1024 * 1024


def _vmem_limit(block_bytes):
    return int(min(VMEM_CAP, 2 * block_bytes + 16 * 1024 * 1024))


def _nbytes(shape, dtype):
    return int(np.prod(shape)) * jnp.dtype(dtype).itemsize


def _silu(g):
    return g * jax.nn.sigmoid(g)


def _div_pow2(x, n):
    assert n & (n - 1) == 0
    return x >> (n.bit_length() - 1)


def _mod_pow2(x, n):
    assert n & (n - 1) == 0
    return x & (n - 1)


def _mm_kernel(a_ref, b_ref, o_ref):
    o_ref[...] = jnp.dot(a_ref[...], b_ref[...], preferred_element_type=F32).astype(o_ref.dtype)


def _mm_resid_kernel(a_ref, b_ref, x_ref, o_ref, *, alpha):
    acc = jnp.dot(a_ref[...], b_ref[...], preferred_element_type=F32)
    o_ref[...] = alpha * x_ref[...] + acc


def _matmul(a, b, out_dtype, tm, tn, name, resid=None, alpha=None):
    m, k = a.shape
    _, n = b.shape
    assert m % tm == 0 and n % tn == 0
    blocks = _nbytes((tm, k), a.dtype) + _nbytes((k, tn), b.dtype) + _nbytes((tm, tn), out_dtype)
    in_specs = [pl.BlockSpec((tm, k), lambda i, j: (i, 0)),
                pl.BlockSpec((k, tn), lambda i, j: (0, j))]
    args = [a, b]
    if resid is None:
        kern = _mm_kernel
    else:
        kern = functools.partial(_mm_resid_kernel, alpha=alpha)
        in_specs.append(pl.BlockSpec((tm, tn), lambda i, j: (i, j)))
        args.append(resid)
        blocks += _nbytes((tm, tn), resid.dtype)
    return pl.pallas_call(
        kern,
        out_shape=jax.ShapeDtypeStruct((m, n), out_dtype),
        grid=(m // tm, n // tn),
        in_specs=in_specs,
        out_specs=pl.BlockSpec((tm, tn), lambda i, j: (i, j)),
        compiler_params=pltpu.CompilerParams(
            dimension_semantics=("parallel", "parallel"), vmem_limit_bytes=_vmem_limit(blocks)),
        name=name,
    )(*args)


def _merge_kernel(x_ref, o0, o1, o2, o3, g0, g1, g2, g3, w0, w1, w2, w3, y_ref):
    x = x_ref[...]
    acc = None
    for o_ref, g_ref, w_ref in ((o0, g0, w0), (o1, g1, w1), (o2, g2, w2), (o3, g3, w3)):
        gate = jnp.dot(x, g_ref[...], preferred_element_type=F32)
        proj = jnp.dot(o_ref[...], w_ref[...], preferred_element_type=F32)
        term = jax.nn.sigmoid(gate) * proj
        acc = term if acc is None else acc + term
    y_ref[...] = acc.astype(y_ref.dtype)


def _merge(xb, branches, w_mg, w_br, tm, tn):
    m, d = xb.shape
    bw = branches[0].shape[1]
    nj = d // tn
    in_specs = [pl.BlockSpec((tm, d), lambda i, j: (i, 0))]
    in_specs += [pl.BlockSpec((tm, bw), lambda i, j: (i, 0)) for _ in range(N_BRANCH)]
    in_specs += [pl.BlockSpec((d, tn), functools.partial(lambda i, j, b: (0, b * nj + j), b=b))
                 for b in range(N_BRANCH)]
    in_specs += [pl.BlockSpec((None, bw, tn), functools.partial(lambda i, j, b: (b, 0, j), b=b))
                 for b in range(N_BRANCH)]
    blocks = (_nbytes((tm, d), BF16) + N_BRANCH * _nbytes((tm, bw), BF16)
              + N_BRANCH * _nbytes((d + bw, tn), BF16) + _nbytes((tm, tn), BF16))
    return pl.pallas_call(
        _merge_kernel,
        out_shape=jax.ShapeDtypeStruct((m, d), BF16),
        grid=(m // tm, nj),
        in_specs=in_specs,
        out_specs=pl.BlockSpec((tm, tn), lambda i, j: (i, j)),
        compiler_params=pltpu.CompilerParams(
            dimension_semantics=("parallel", "parallel"), vmem_limit_bytes=_vmem_limit(blocks)),
        name="merge",
    )(xb, *branches, w_mg, w_mg, w_mg, w_mg, w_br, w_br, w_br, w_br)


def _ln_kernel(z_ref, g_ref, b_ref, o_ref, ob_ref=None):
    z = z_ref[...]
    mu = jnp.mean(z, axis=-1, keepdims=True)
    zc = z - mu
    var = jnp.mean(zc * zc, axis=-1, keepdims=True)
    y = zc * lax.rsqrt(var + LN_EPS) * g_ref[...] + b_ref[...]
    o_ref[...] = y
    if ob_ref is not None:
        ob_ref[...] = y.astype(BF16)


def _layer_norm(z, g, b, tm, emit_bf16):
    m, d = z.shape
    out_shape = [jax.ShapeDtypeStruct((m, d), F32)]
    out_specs = [pl.BlockSpec((tm, d), lambda i: (i, 0))]
    if emit_bf16:
        out_shape.append(jax.ShapeDtypeStruct((m, d), BF16))
        out_specs.append(pl.BlockSpec((tm, d), lambda i: (i, 0)))
    blocks = 2 * _nbytes((tm, d), F32) + _nbytes((tm, d), BF16)
    res = pl.pallas_call(
        _ln_kernel,
        out_shape=out_shape,
        grid=(m // tm,),
        in_specs=[pl.BlockSpec((tm, d), lambda i: (i, 0)),
                  pl.BlockSpec((1, d), lambda i: (0, 0)),
                  pl.BlockSpec((1, d), lambda i: (0, 0))],
        out_specs=out_specs,
        compiler_params=pltpu.CompilerParams(
            dimension_semantics=("parallel",), vmem_limit_bytes=_vmem_limit(blocks)),
        name="layer_norm",
    )(z, g.reshape(1, d), b.reshape(1, d))
    return res if emit_bf16 else (res[0], None)


POOL_HIST = 16


def _pool_kernel(u_ref, g_ref, wp_ref, sc_ref, o_ref, buf, *, ts, group):
    i = pl.program_id(1)

    @pl.when(i == 0)
    def _():
        buf[0:POOL_HIST, :] = jnp.zeros((POOL_HIST, buf.shape[1]), F32)

    buf[POOL_HIST:POOL_HIST + ts, :] = u_ref[...]
    t = i * ts + lax.broadcasted_iota(jnp.int32, (ts, 1), 0)
    outs = []
    for gi, w in enumerate(POOL_WINDOWS):
        cols = slice(gi * group, (gi + 1) * group)
        cur = buf[POOL_HIST:POOL_HIST + ts, cols]
        acc = cur
        for lag in range(1, w):
            acc = acc + buf[POOL_HIST - lag:POOL_HIST - lag + ts, cols]
        cnt = jnp.minimum(t + 1, w).astype(F32)
        pooled = acc / cnt - cur
        outs.append(jnp.dot(pooled.astype(BF16), wp_ref[gi], preferred_element_type=F32))
    o = jnp.concatenate(outs, axis=1) * sc_ref[...] * _silu(g_ref[...])
    o_ref[...] = o.astype(o_ref.dtype)
    buf[0:POOL_HIST, :] = buf[ts:ts + POOL_HIST, :]


def _pool(h, w_pool_b, pool_scale, bsz, seq, col_u, col_g, ts):
    bw = pool_scale.shape[-1]
    group = bw // len(POOL_WINDOWS)
    nblk = seq // ts
    blocks = 2 * _nbytes((ts, bw), F32) + _nbytes((ts, bw), BF16) + _nbytes((ts + POOL_HIST, bw), F32)
    return pl.pallas_call(
        functools.partial(_pool_kernel, ts=ts, group=group),
        out_shape=jax.ShapeDtypeStruct((bsz * seq, bw), BF16),
        grid=(bsz, nblk),
        in_specs=[pl.BlockSpec((ts, bw), lambda b, i: (b * nblk + i, col_u // bw)),
                  pl.BlockSpec((ts, bw), lambda b, i: (b * nblk + i, col_g // bw)),
                  pl.BlockSpec(w_pool_b.shape, lambda b, i: (0, 0, 0)),
                  pl.BlockSpec((1, bw), lambda b, i: (0, 0))],
        out_specs=pl.BlockSpec((ts, bw), lambda b, i: (b * nblk + i, 0)),
        scratch_shapes=[pltpu.VMEM((ts + POOL_HIST, bw), F32)],
        compiler_params=pltpu.CompilerParams(
            dimension_semantics=("parallel", "arbitrary"), vmem_limit_bytes=_vmem_limit(blocks)),
        name="pool_mixer",
    )(h, h, w_pool_b, pool_scale.reshape(1, bw))


def _split_bf16(x, parts):
    out = []
    r = x
    for _ in range(parts):
        p = r.astype(BF16)
        out.append(p)
        r = r - p.astype(F32)
    return out


def _gla_kernel(q_ref, k_ref, v_ref, gg_ref, lr_ref, wup_ref, bg_ref, nrm_ref, o_ref,
                state_ref, b_ref, *, rows, dk, dv):
    i = pl.program_id(1)
    C = GLA_CHUNK
    nsub = C // GLA_SUB

    @pl.when(i == 0)
    def _():
        state_ref[...] = jnp.zeros(state_ref.shape, F32)

    lr_h, lr_m = _split_bf16(lr_ref[...], 2)
    w_h, w_m = _split_bf16(wup_ref[...], 2)
    z = (jnp.dot(lr_h, w_h, preferred_element_type=F32)
         + jnp.dot(lr_h, w_m, preferred_element_type=F32)
         + jnp.dot(lr_m, w_h, preferred_element_type=F32)) + bg_ref[...]
    log_a = -(jnp.maximum(-z, 0.0) + jnp.log1p(jnp.exp(-jnp.abs(z)))) / GLA_TAU

    rr = lax.broadcasted_iota(jnp.int32, (rows, rows), 0)
    cc = lax.broadcasted_iota(jnp.int32, (rows, rows), 1)
    tri = jnp.where((_div_pow2(rr, C) == _div_pow2(cc, C)) & (cc <= rr), 1.0, 0.0).astype(BF16)
    bsum = None
    for part in _split_bf16(log_a, 3):
        t = jnp.dot(tri, part, preferred_element_type=F32)
        bsum = t if bsum is None else bsum + t
    b_ref[...] = bsum

    row = lax.broadcasted_iota(jnp.int32, (C, 1), 0)
    col = lax.broadcasted_iota(jnp.int32, (1, C), 1)
    row_sub = _div_pow2(row, GLA_SUB)
    col_sub = _div_pow2(col, GLA_SUB)
    lane_c = lax.broadcasted_iota(jnp.int32, (GLA_SUB, C), 1)
    scale = dk ** -0.5

    def chunk_body(c, carry):
        r0 = pl.multiple_of(c * C, C)
        for h in range(GLA_HEADS):
            kc = slice(h * dk, (h + 1) * dk)
            vc = slice(h * dv, (h + 1) * dv)
            qs = q_ref[pl.ds(r0, C), kc] * scale
            k = k_ref[pl.ds(r0, C), kc]
            v = v_ref[pl.ds(r0, C), vc].astype(BF16)
            b = b_ref[pl.ds(r0, C), kc]
            b_last = b[C - 1:C, :]
            st = state_ref[h]

            q_in = (qs * jnp.exp(b)).astype(BF16)
            o = lax.dot_general(q_in, st.astype(BF16), (((1,), (1,)), ((), ())),
                                preferred_element_type=F32)

            b_heads = [b[a * GLA_SUB:a * GLA_SUB + 1, :] for a in range(nsub)]
            r_rows = jnp.concatenate(
                [jnp.broadcast_to(bh, (GLA_SUB, dk)) for bh in b_heads], axis=0)
            q_t = qs * jnp.exp(b - r_rows)
            lhs = jnp.concatenate(
                [jnp.where(row_sub == a, q_t, 0.0) for a in range(nsub)], axis=1).astype(BF16)
            rhs = jnp.concatenate(
                [k * jnp.exp(jnp.minimum(b_heads[a] - b, 0.0)) for a in range(nsub)],
                axis=1).astype(BF16)
            a_off = lax.dot_general(lhs, rhs, (((1,), (1,)), ((), ())),
                                    preferred_element_type=F32)

            diag_blocks = []
            for a in range(nsub):
                rs = slice(a * GLA_SUB, (a + 1) * GLA_SUB)
                qa = qs[rs, :]
                ba = b[rs, :]
                blk = jnp.zeros((GLA_SUB, C), F32)
                for jj in range(GLA_SUB):
                    j = a * GLA_SUB + jj
                    kj = k[j:j + 1, :]
                    bj = b[j:j + 1, :]
                    tj = qa * kj * jnp.exp(jnp.minimum(ba - bj, 0.0))
                    cj = jnp.sum(tj, axis=1, keepdims=True)
                    blk = jnp.where(lane_c == j, cj, blk)
                diag_blocks.append(blk)
            a_diag = jnp.concatenate(diag_blocks, axis=0)
            attn = jnp.where(row_sub > col_sub, a_off,
                             jnp.where((row_sub == col_sub) & (row >= col), a_diag, 0.0))
            o = o + jnp.dot(attn.astype(BF16), v, preferred_element_type=F32)

            k_dec = (k * jnp.exp(b_last - b)).astype(BF16)
            upd = lax.dot_general(v, k_dec, (((0,), (0,)), ((), ())),
                                  preferred_element_type=F32)
            state_ref[h] = st * jnp.exp(b_last) + upd

            ms = jnp.mean(o * o, axis=-1, keepdims=True)
            on = o * lax.rsqrt(ms + LN_EPS) * nrm_ref[:, vc]
            o_ref[pl.ds(r0, C), vc] = (on * _silu(gg_ref[pl.ds(r0, C), vc])).astype(o_ref.dtype)
        return carry

    lax.fori_loop(0, rows // C, chunk_body, 0)


def _gla(h, w_up_pad, b_gla, gla_norm, bsz, seq, cols, rows):
    col_q, col_k, col_v, col_g, col_lr = cols
    dkt = b_gla.shape[-1]
    dvt = gla_norm.shape[-1]
    dk, dv = dkt // GLA_HEADS, dvt // GLA_HEADS
    lrw = w_up_pad.shape[0]
    nblk = seq // rows
    blocks = (2 * _nbytes((rows, dkt), F32) + 2 * _nbytes((rows, dvt), F32)
              + _nbytes((rows, lrw), F32) + _nbytes((rows, dvt), BF16)
              + _nbytes((rows, dkt), F32) + _nbytes((GLA_HEADS, dv, dk), F32))
    return pl.pallas_call(
        functools.partial(_gla_kernel, rows=rows, dk=dk, dv=dv),
        out_shape=jax.ShapeDtypeStruct((bsz * seq, dvt), BF16),
        grid=(bsz, nblk),
        in_specs=[pl.BlockSpec((rows, dkt), lambda b, i: (b * nblk + i, col_q // dkt)),
                  pl.BlockSpec((rows, dkt), lambda b, i: (b * nblk + i, col_k // dkt)),
                  pl.BlockSpec((rows, dvt), lambda b, i: (b * nblk + i, col_v // dvt)),
                  pl.BlockSpec((rows, dvt), lambda b, i: (b * nblk + i, col_g // dvt)),
                  pl.BlockSpec((rows, lrw), lambda b, i: (b * nblk + i, col_lr // lrw)),
                  pl.BlockSpec((lrw, dkt), lambda b, i: (0, 0)),
                  pl.BlockSpec((1, dkt), lambda b, i: (0, 0)),
                  pl.BlockSpec((1, dvt), lambda b, i: (0, 0))],
        out_specs=pl.BlockSpec((rows, dvt), lambda b, i: (b * nblk + i, 0)),
        scratch_shapes=[pltpu.VMEM((GLA_HEADS, dv, dk), F32),
                        pltpu.VMEM((rows, dkt), F32)],
        compiler_params=pltpu.CompilerParams(
            dimension_semantics=("parallel", "arbitrary"), vmem_limit_bytes=_vmem_limit(blocks)),
        name="gla_mixer",
    )(h, h, h, h, h, w_up_pad, b_gla.reshape(1, dkt), gla_norm.reshape(1, dvt))


def _rope_lane_table():
    half = ROPE_DIM // 2
    inv_freq = ROPE_THETA ** (-np.arange(0, ROPE_DIM, 2, dtype=np.float32) / ROPE_DIM)
    lane = np.arange(LANES) % SWA_HEAD_DIM
    tab = np.where(lane < ROPE_DIM, inv_freq[lane % half], 0.0).astype(np.float32)
    return jnp.asarray(tab.reshape(1, LANES))


def _swa_kernel(sink_ref, q_ref, k_ref, v_ref, g_ref, pos_ref, inv_ref, o_ref, wk_ref, wv_ref,
                *, n_pairs):
    n = pl.program_id(1)
    blk = SWA_BLOCK
    half = ROPE_DIM // 2
    hd = SWA_HEAD_DIM

    @pl.when(n == 0)
    def _():
        wk_ref[...] = jnp.zeros(wk_ref.shape, BF16)
        wv_ref[...] = jnp.zeros(wv_ref.shape, BF16)

    lane = lax.broadcasted_iota(jnp.int32, (1, LANES), 1)
    lane_h = _mod_pow2(lane, hd)
    lo = lane < hd
    ang = pos_ref[...] * inv_ref[...]
    cos = jnp.cos(ang)
    sin = jnp.sin(ang)

    def rope(x):
        x_up = pltpu.roll(x, LANES - half, axis=1)
        x_dn = pltpu.roll(x, half, axis=1)
        return x * cos + jnp.where(lane_h < half, -x_up, x_dn) * sin

    for g in range(2):
        for part in range(2):
            base = part * 2 * blk
            wk_ref[g, base:base + blk, :] = wk_ref[g, base + blk:base + 2 * blk, :]
            wv_ref[g, base:base + blk, :] = wv_ref[g, base + blk:base + 2 * blk, :]
    k_r = rope(k_ref[...])
    v_c = v_ref[...]
    k_sw = pltpu.roll(k_r, hd, axis=1)
    v_sw = pltpu.roll(v_c, hd, axis=1)
    zero = jnp.zeros_like(k_r)
    for g in range(2):
        k_lo, k_hi = (k_r, k_sw) if g == 0 else (k_sw, k_r)
        v_lo, v_hi = (v_c, v_sw) if g == 0 else (v_sw, v_c)
        wk_ref[g, blk:2 * blk, :] = jnp.where(lo, k_lo, zero).astype(BF16)
        wk_ref[g, 3 * blk:4 * blk, :] = jnp.where(lo, zero, k_hi).astype(BF16)
        wv_ref[g, blk:2 * blk, :] = jnp.where(lo, v_lo, zero).astype(BF16)
        wv_ref[g, 3 * blk:4 * blk, :] = jnp.where(lo, zero, v_hi).astype(BF16)

    r = lax.broadcasted_iota(jnp.int32, (blk, 2 * blk), 0)
    c = lax.broadcasted_iota(jnp.int32, (blk, 2 * blk), 1)
    valid = (c > r) & (c <= r + WINDOW) & ((n > 0) | (c >= blk))
    neg_inf = jnp.float32(-jnp.inf)
    qscale = hd ** -0.5
    pairs_per_group = n_pairs // 2
    for p in range(n_pairs):
        g = p // pairs_per_group
        cols = slice(p * LANES, (p + 1) * LANES)
        q_pair = (rope(q_ref[:, cols]) * qscale).astype(BF16)
        s = lax.dot_general(q_pair, wk_ref[g], (((1,), (1,)), ((), ())),
                            preferred_element_type=F32)
        probs = []
        for hh in range(2):
            sink = sink_ref[2 * p + hh]
            sh = jnp.where(valid, s[:, hh * 2 * blk:(hh + 1) * 2 * blk], neg_inf)
            m = jnp.maximum(jnp.max(sh, axis=-1, keepdims=True), sink)
            e = jnp.exp(sh - m)
            denom = jnp.sum(e, axis=-1, keepdims=True) + jnp.exp(sink - m)
            probs.append((e / denom).astype(BF16))
        pcat = jnp.concatenate(probs, axis=1)
        o = jnp.dot(pcat, wv_ref[g], preferred_element_type=F32)
        o_ref[:, cols] = (o * _silu(g_ref[:, cols])).astype(o_ref.dtype)


def _swa(h, pos_f, sinks, bsz, seq, cols):
    col_q, col_k, col_v, col_g = cols
    n_heads = sinks.shape[0]
    bw = n_heads * SWA_HEAD_DIM
    n_pairs = bw // LANES
    blk = SWA_BLOCK
    nblk = seq // blk
    blocks = (2 * _nbytes((blk, bw), F32) + 2 * _nbytes((blk, LANES), F32) + _nbytes((blk, LANES), F32)
              + _nbytes((blk, bw), BF16) + 2 * _nbytes((2, 4 * blk, LANES), BF16))
    return pl.pallas_call(
        functools.partial(_swa_kernel, n_pairs=n_pairs),
        out_shape=jax.ShapeDtypeStruct((bsz * seq, bw), BF16),
        grid=(bsz, nblk),
        in_specs=[pl.BlockSpec(memory_space=pltpu.SMEM),
                  pl.BlockSpec((blk, bw), lambda b, i: (b * nblk + i, col_q // bw)),
                  pl.BlockSpec((blk, LANES), lambda b, i: (b * nblk + i, col_k // LANES)),
                  pl.BlockSpec((blk, LANES), lambda b, i: (b * nblk + i, col_v // LANES)),
                  pl.BlockSpec((blk, bw), lambda b, i: (b * nblk + i, col_g // bw)),
                  pl.BlockSpec((blk, 1), lambda b, i: (b * nblk + i, 0)),
                  pl.BlockSpec((1, LANES), lambda b, i: (0, 0))],
        out_specs=pl.BlockSpec((blk, bw), lambda b, i: (b * nblk + i, 0)),
        scratch_shapes=[pltpu.VMEM((2, 4 * blk, LANES), BF16),
                        pltpu.VMEM((2, 4 * blk, LANES), BF16)],
        compiler_params=pltpu.CompilerParams(
            dimension_semantics=("parallel", "arbitrary"), vmem_limit_bytes=_vmem_limit(blocks)),
        name="swa_mixer",
    )(sinks, h, h, h, h, pos_f, _rope_lane_table())


def _mem_kernel(q_ref, g_ref, mk_ref, mv_ref, o_ref, *, hd):
    scale = hd ** -0.5
    for h in range(XA_HEADS):
        cols = slice(h * hd, (h + 1) * hd)
        qh = q_ref[:, cols].astype(BF16)
        s = lax.dot_general(qh, mk_ref[:, cols], (((1,), (1,)), ((), ())),
                            preferred_element_type=F32) * scale
        m = jnp.max(s, axis=-1, keepdims=True)
        e = jnp.exp(s - m)
        p = e / jnp.sum(e, axis=-1, keepdims=True)
        o = jnp.dot(p.astype(BF16), mv_ref[:, cols], preferred_element_type=F32)
        o_ref[:, cols] = (o * _silu(g_ref[:, cols])).astype(o_ref.dtype)


def _mem_attn(h, mkv, bsz, seq, mem_len, cols, ts):
    col_q, col_g = cols
    bw = mkv.shape[1] // 2
    hd = bw // XA_HEADS
    nblk = seq // ts
    blocks = 2 * _nbytes((ts, bw), F32) + 2 * _nbytes((mem_len, bw), BF16) + _nbytes((ts, bw), BF16)
    return pl.pallas_call(
        functools.partial(_mem_kernel, hd=hd),
        out_shape=jax.ShapeDtypeStruct((bsz * seq, bw), BF16),
        grid=(bsz, nblk),
        in_specs=[pl.BlockSpec((ts, bw), lambda b, i: (b * nblk + i, col_q // bw)),
                  pl.BlockSpec((ts, bw), lambda b, i: (b * nblk + i, col_g // bw)),
                  pl.BlockSpec((mem_len, bw), lambda b, i: (b, 0)),
                  pl.BlockSpec((mem_len, bw), lambda b, i: (b, 1))],
        out_specs=pl.BlockSpec((ts, bw), lambda b, i: (b * nblk + i, 0)),
        compiler_params=pltpu.CompilerParams(
            dimension_semantics=("parallel", "parallel"), vmem_limit_bytes=_vmem_limit(blocks)),
        name="mem_attn",
    )(h, h, mkv, mkv)


STAGE_TN = 2 * LANES


def _stage_kernel(tab_ref, main_ref, next_ref, o_ref):
    j = pl.program_id(0)
    shift = tab_ref[2, j]
    valid = tab_ref[3, j]
    n_tiles = STAGE_TN // LANES

    @pl.when(shift == 0)
    def _():
        col = lax.broadcasted_iota(jnp.int32, (1, STAGE_TN), 1)
        o_ref[...] = jnp.where(col < valid, main_ref[...], 0.0).astype(o_ref.dtype)

    @pl.when(shift != 0)
    def _():
        lane = lax.broadcasted_iota(jnp.int32, (1, LANES), 1)
        tiles = [main_ref[:, t * LANES:(t + 1) * LANES] for t in range(n_tiles)] + [next_ref[...]]
        rolled = [pltpu.roll(t, LANES - GLA_RANK, axis=1) for t in tiles]
        for t in range(n_tiles):
            o_ref[:, t * LANES:(t + 1) * LANES] = jnp.where(
                lane < LANES - GLA_RANK, rolled[t], rolled[t + 1]).astype(o_ref.dtype)


def _stage_columns(w, layer, src_col):
    _, k, n_src = w.shape
    n_dst = len(src_col)
    assert n_dst % STAGE_TN == 0
    tab = np.zeros((4, n_dst // STAGE_TN), np.int32)
    for j in range(n_dst // STAGE_TN):
        blk = src_col[j * STAGE_TN:(j + 1) * STAGE_TN]
        valid = int(np.sum(blk >= 0))
        start = int(blk[0])
        assert valid > 0 and np.all(blk[:valid] == start + np.arange(valid)) and np.all(blk[valid:] < 0)
        shift = start % LANES
        base = start - shift
        assert base % STAGE_TN == 0 and shift in (0, GLA_RANK) and (shift == 0 or valid == STAGE_TN)
        assert base + shift + valid <= n_src
        nxt = (base + STAGE_TN) // LANES
        tab[:, j] = (base // STAGE_TN, min(nxt, (n_src - 1) // LANES), shift, valid)
    blocks = _nbytes((k, STAGE_TN + LANES), w.dtype) + _nbytes((k, STAGE_TN), BF16)
    return pl.pallas_call(
        _stage_kernel,
        out_shape=jax.ShapeDtypeStruct((k, n_dst), BF16),
        grid_spec=pltpu.PrefetchScalarGridSpec(
            num_scalar_prefetch=1,
            grid=(n_dst // STAGE_TN,),
            in_specs=[pl.BlockSpec((None, k, STAGE_TN), lambda j, tab: (layer, 0, tab[0, j])),
                      pl.BlockSpec((None, k, LANES), lambda j, tab: (layer, 0, tab[1, j]))],
            out_specs=pl.BlockSpec((k, STAGE_TN), lambda j, tab: (0, j)),
        ),
        compiler_params=pltpu.CompilerParams(
            dimension_semantics=("parallel",), vmem_limit_bytes=_vmem_limit(blocks)),
        name="stage_weights",
    )(jnp.asarray(tab), w, w)


def _main_layout(d_model):
    bw = d_model // 4
    dkt, dvt = bw // 2, bw
    kvw = 2 * SWA_HEAD_DIM
    names = ["pool_u", "pool_g", "gq", "gk", "gv", "gg", "glr", "sq", "sk", "sv", "sg", "xq", "xg"]
    widths = [bw, bw, dkt, dkt, dvt, dvt, GLA_RANK, bw, kvw, kvw, bw, bw, bw]
    src = {}
    off = 0
    for nme, w in zip(names, widths):
        src[nme] = (off, w)
        off += w
    order = ["pool_u", "pool_g", "gv", "gg", "sq", "sg", "xq", "xg", "gq", "gk", "sk", "sv", "glr"]
    dst = {}
    cols = []
    off = 0
    for nme in order:
        o, w = src[nme]
        assert off % (w if nme != "glr" else 2 * LANES) == 0
        dst[nme] = off
        cols.append(o + np.arange(w))
        off += w
    lr_block = 2 * LANES
    padded = dst["glr"] + lr_block
    cols.append(np.full(padded - off, -1))
    return np.concatenate(cols), dst, off, lr_block


def _hybrid_layer(l, x, xb, mem_b, pos_f, w_in, w_pool, pool_scale, w_gla_up, b_gla, gla_norm,
                  sinks, w_mem_kv, w_branch, w_out, ln_g, ln_b, alpha, emit_bf16, bsz, seq):
    m, d = x.shape
    main_cols, dst, n_main, lr_block = _main_layout(d)
    d_in = w_in.shape[-1]

    w_main = _stage_columns(w_in, l, main_cols)
    w_mg = _stage_columns(w_in, l, np.arange(n_main, d_in))
    nb, bw, _ = w_branch.shape[1:]
    w_br = _stage_columns(w_branch.reshape(w_branch.shape[0], nb * bw, d), l, np.arange(d))
    w_br = w_br.reshape(nb, bw, d)
    w_o = _stage_columns(w_out, l, np.arange(d))
    w_kv = _stage_columns(w_mem_kv, l, np.arange(w_mem_kv.shape[-1]))
    w_up_pad = jnp.concatenate(
        [w_gla_up[l], jnp.zeros((lr_block - GLA_RANK, w_gla_up.shape[-1]), w_gla_up.dtype)], axis=0)

    h = _matmul(xb, w_main, F32, tm=1024, tn=512, name="in_proj")
    mkv = _matmul(mem_b, w_kv, BF16, tm=mem_b.shape[0], tn=512, name="mem_kv")

    o_pool = _pool(h, w_pool[l].astype(BF16), pool_scale[l], bsz, seq,
                   dst["pool_u"], dst["pool_g"], ts=512)
    o_gla = _gla(h, w_up_pad, b_gla[l], gla_norm[l], bsz, seq,
                 (dst["gq"], dst["gk"], dst["gv"], dst["gg"], dst["glr"]), rows=512)
    o_swa = _swa(h, pos_f, sinks[l], bsz, seq, (dst["sq"], dst["sk"], dst["sv"], dst["sg"]))
    o_mem = _mem_attn(h, mkv, bsz, seq, mem_b.shape[0] // bsz, (dst["xq"], dst["xg"]), ts=512)

    y = _merge(xb, (o_pool, o_gla, o_swa, o_mem), w_mg, w_br, tm=512, tn=256)
    z = _matmul(y, w_o, F32, tm=1024, tn=512, name="out_proj", resid=x, alpha=alpha)
    return _layer_norm(z, ln_g[l], ln_b[l], tm=256, emit_bf16=emit_bf16)


def kernel(x, mem, positions, w_in, w_pool, pool_scale, w_gla_up, b_gla, gla_norm, sinks,
           w_mem_kv, w_branch, w_out, ln_g, ln_b):
    bsz, seq, d = x.shape
    depth = w_in.shape[0]
    alpha = (2 * depth) ** 0.25
    xf = x.reshape(bsz * seq, d)
    xb = xf.astype(BF16)
    mem_b = mem.reshape(bsz * mem.shape[1], d).astype(BF16)
    pos_f = positions.astype(F32).reshape(bsz * seq, 1)
    for l in range(depth):
        xf, xb = _hybrid_layer(
            l, xf, xb, mem_b, pos_f, w_in, w_pool, pool_scale, w_gla_up, b_gla,
            gla_norm, sinks, w_mem_kv, w_branch, w_out, ln_g, ln_b,
            alpha, l + 1 < depth, bsz, seq)
    return xf.reshape(bsz, seq, d)
```

```python
import functools
import math

import jax
import jax.numpy as jnp
import numpy as np
from jax import lax
from jax.experimental import pallas as pl
from jax.experimental.pallas import tpu as pltpu

F32 = jnp.float32
BF16 = jnp.bfloat16

N_BRANCH = 4
POOL_WINDOWS = (2, 4, 8, 16)
GLA_HEADS = 4
GLA_RANK = 16
GLA_TAU = 16.0
GLA_CHUNK = 64
GLA_SUB = 16
SWA_HEAD_DIM = 64
SWA_GROUP = 8
WINDOW = 128
SWA_BLOCK = 128
ROPE_THETA = 500000.0
ROPE_DIM = SWA_HEAD_DIM // 4
XA_HEADS = 4
LN_EPS = 1e-5

LANES = 128
SUBLANES = 8
V7X_VMEM_BYTES = 64 * 1024 * 1024
VMEM_CAP = V7X_VMEM_BYTES - 8 * 1024 * 1024
VMEM_TEMP_BYTES = 12 * 1024 * 1024


def _vmem_limit(streamed_bytes, resident_bytes=0):
    return int(min(VMEM_CAP, 2 * streamed_bytes + resident_bytes + VMEM_TEMP_BYTES))


def _nbytes(shape, dtype):
    return int(np.prod(shape)) * jnp.dtype(dtype).itemsize


def _silu(g):
    return g * jax.nn.sigmoid(g)


def _div_pow2(x, n):
    assert n & (n - 1) == 0
    return x >> (n.bit_length() - 1)


def _mod_pow2(x, n):
    assert n & (n - 1) == 0
    return x & (n - 1)


NT_DIMS = (((1,), (1,)), ((), ()))


def _mm_kernel(a_ref, w_ref, o_ref):
    w = w_ref[...].astype(BF16)
    o_ref[...] = jnp.dot(a_ref[...], w, preferred_element_type=F32).astype(o_ref.dtype)


def _mm_resid_kernel(a_ref, w_ref, x_ref, o_ref, *, alpha):
    w = w_ref[...].astype(BF16)
    o_ref[...] = alpha * x_ref[...] + jnp.dot(a_ref[...], w, preferred_element_type=F32)


def _matmul(a, w, layer, out_dtype, tm, tn, name, resid=None, alpha=None):
    m, k = a.shape
    n = w.shape[-1]
    assert m % tm == 0 and n % tn == 0
    resident = _nbytes((tm, k), a.dtype)
    streamed = _nbytes((k, tn), w.dtype) + _nbytes((tm, tn), out_dtype)
    in_specs = [pl.BlockSpec((tm, k), lambda i, j: (i, 0), pipeline_mode=pl.Buffered(1)),
                pl.BlockSpec((None, k, tn), lambda i, j: (layer, 0, j))]
    args = [a, w]
    if resid is None:
        kern = _mm_kernel
    else:
        kern = functools.partial(_mm_resid_kernel, alpha=alpha)
        in_specs.append(pl.BlockSpec((tm, tn), lambda i, j: (i, j)))
        args.append(resid)
        streamed += _nbytes((tm, tn), resid.dtype)
    return pl.pallas_call(
        kern,
        out_shape=jax.ShapeDtypeStruct((m, n), out_dtype),
        grid=(m // tm, n // tn),
        in_specs=in_specs,
        out_specs=pl.BlockSpec((tm, tn), lambda i, j: (i, j)),
        compiler_params=pltpu.CompilerParams(
            dimension_semantics=("parallel", "arbitrary"),
            vmem_limit_bytes=_vmem_limit(streamed, resident)),
        name=name,
    )(*args)


def _in_proj_kernel(tab_ref, x_ref, w_ref, o_ref):
    w = w_ref[0].astype(BF16)
    o_ref[...] = lax.dot_general(x_ref[...], w, NT_DIMS, preferred_element_type=F32)


def _in_proj(xb, w_in_t, layer, col_starts, tm, tn):
    m, k = xb.shape
    nblk = len(col_starts)
    assert m % tm == 0 and all(int(s) % SUBLANES == 0 for s in col_starts)
    resident = _nbytes((tm, k), xb.dtype)
    streamed = _nbytes((tn, k), w_in_t.dtype) + _nbytes((tm, tn), F32)
    return pl.pallas_call(
        _in_proj_kernel,
        out_shape=jax.ShapeDtypeStruct((m, nblk * tn), F32),
        grid_spec=pltpu.PrefetchScalarGridSpec(
            num_scalar_prefetch=1,
            grid=(m // tm, nblk),
            in_specs=[pl.BlockSpec((tm, k), lambda i, j, tab: (i, 0), pipeline_mode=pl.Buffered(1)),
                      pl.BlockSpec((pl.Element(1), pl.Element(tn), pl.Element(k)),
                                   lambda i, j, tab: (layer, pl.multiple_of(tab[j], SUBLANES), 0))],
            out_specs=pl.BlockSpec((tm, tn), lambda i, j, tab: (i, j))),
        compiler_params=pltpu.CompilerParams(
            dimension_semantics=("parallel", "arbitrary"),
            vmem_limit_bytes=_vmem_limit(streamed, resident)),
        name="in_proj",
    )(jnp.asarray(np.asarray(col_starts, np.int32)), xb, w_in_t)


def _cast_rows_kernel(w_ref, o_ref):
    o_ref[...] = w_ref[0].astype(o_ref.dtype)


def _stage_rows(w_t, layer, row0, n, tr):
    k = w_t.shape[-1]
    assert n % tr == 0 and row0 % SUBLANES == 0
    streamed = _nbytes((tr, k), w_t.dtype) + _nbytes((tr, k), BF16)
    return pl.pallas_call(
        _cast_rows_kernel,
        out_shape=jax.ShapeDtypeStruct((n, k), BF16),
        grid=(n // tr,),
        in_specs=[pl.BlockSpec((pl.Element(1), pl.Element(tr), pl.Element(k)),
                               lambda j: (layer, pl.multiple_of(row0 + j * tr, SUBLANES), 0))],
        out_specs=pl.BlockSpec((tr, k), lambda j: (j, 0)),
        compiler_params=pltpu.CompilerParams(
            dimension_semantics=("parallel",), vmem_limit_bytes=_vmem_limit(streamed)),
        name="stage_merge_gates",
    )(w_t)


def _merge_kernel(x_ref, o0, o1, o2, o3, g0, g1, g2, g3, w0, w1, w2, w3, y_ref):
    x = x_ref[...]
    acc = None
    for o_ref, g_ref, w_ref in ((o0, g0, w0), (o1, g1, w1), (o2, g2, w2), (o3, g3, w3)):
        gate = lax.dot_general(x, g_ref[...], NT_DIMS, preferred_element_type=F32)
        proj = jnp.dot(o_ref[...], w_ref[...].astype(BF16), preferred_element_type=F32)
        term = jax.nn.sigmoid(gate) * proj
        acc = term if acc is None else acc + term
    y_ref[...] = acc.astype(y_ref.dtype)


def _merge(xb, branches, w_mg_t, w_branch, layer, tm, tn):
    m, d = xb.shape
    bw = branches[0].shape[1]
    nj = d // tn
    in_specs = [pl.BlockSpec((tm, d), lambda i, j: (i, 0), pipeline_mode=pl.Buffered(1))]
    in_specs += [pl.BlockSpec((tm, bw), lambda i, j: (i, 0), pipeline_mode=pl.Buffered(1))
                 for _ in range(N_BRANCH)]
    in_specs += [pl.BlockSpec((tn, d), functools.partial(lambda i, j, b: (b * nj + j, 0), b=b))
                 for b in range(N_BRANCH)]
    in_specs += [pl.BlockSpec((None, None, bw, tn),
                              functools.partial(lambda i, j, b: (layer, b, 0, j), b=b))
                 for b in range(N_BRANCH)]
    resident = _nbytes((tm, d), BF16) + N_BRANCH * _nbytes((tm, bw), BF16)
    streamed = (N_BRANCH * (_nbytes((tn, d), BF16) + _nbytes((bw, tn), w_branch.dtype))
                + _nbytes((tm, tn), BF16))
    return pl.pallas_call(
        _merge_kernel,
        out_shape=jax.ShapeDtypeStruct((m, d), BF16),
        grid=(m // tm, nj),
        in_specs=in_specs,
        out_specs=pl.BlockSpec((tm, tn), lambda i, j: (i, j)),
        compiler_params=pltpu.CompilerParams(
            dimension_semantics=("parallel", "arbitrary"),
            vmem_limit_bytes=_vmem_limit(streamed, resident)),
        name="merge",
    )(xb, *branches, w_mg_t, w_mg_t, w_mg_t, w_mg_t, w_branch, w_branch, w_branch, w_branch)


def _ln_kernel(z_ref, g_ref, b_ref, o_ref, ob_ref=None):
    z = z_ref[...]
    mu = jnp.mean(z, axis=-1, keepdims=True)
    zc = z - mu
    var = jnp.mean(zc * zc, axis=-1, keepdims=True)
    y = zc * lax.rsqrt(var + LN_EPS) * g_ref[...] + b_ref[...]
    o_ref[...] = y
    if ob_ref is not None:
        ob_ref[...] = y.astype(BF16)


def _layer_norm(z, g, b, tm, emit_bf16):
    m, d = z.shape
    out_shape = [jax.ShapeDtypeStruct((m, d), F32)]
    out_specs = [pl.BlockSpec((tm, d), lambda i: (i, 0))]
    if emit_bf16:
        out_shape.append(jax.ShapeDtypeStruct((m, d), BF16))
        out_specs.append(pl.BlockSpec((tm, d), lambda i: (i, 0)))
    blocks = 2 * _nbytes((tm, d), F32) + _nbytes((tm, d), BF16)
    res = pl.pallas_call(
        _ln_kernel,
        out_shape=out_shape,
        grid=(m // tm,),
        in_specs=[pl.BlockSpec((tm, d), lambda i: (i, 0)),
                  pl.BlockSpec((1, d), lambda i: (0, 0)),
                  pl.BlockSpec((1, d), lambda i: (0, 0))],
        out_specs=out_specs,
        compiler_params=pltpu.CompilerParams(
            dimension_semantics=("parallel",), vmem_limit_bytes=_vmem_limit(blocks)),
        name="layer_norm",
    )(z, g.reshape(1, d), b.reshape(1, d))
    return res if emit_bf16 else (res[0], None)


POOL_HIST = 16


def _pool_kernel(u_ref, g_ref, wp_ref, sc_ref, o_ref, buf, *, ts, group):
    i = pl.program_id(1)

    @pl.when(i == 0)
    def _():
        buf[0:POOL_HIST, :] = jnp.zeros((POOL_HIST, buf.shape[1]), F32)

    buf[POOL_HIST:POOL_HIST + ts, :] = u_ref[...]
    t = i * ts + lax.broadcasted_iota(jnp.int32, (ts, 1), 0)
    outs = []
    for gi, w in enumerate(POOL_WINDOWS):
        cols = slice(gi * group, (gi + 1) * group)
        cur = buf[POOL_HIST:POOL_HIST + ts, cols]
        acc = cur
        for lag in range(1, w):
            acc = acc + buf[POOL_HIST - lag:POOL_HIST - lag + ts, cols]
        cnt = jnp.minimum(t + 1, w).astype(F32)
        pooled = acc / cnt - cur
        outs.append(jnp.dot(pooled.astype(BF16), wp_ref[gi], preferred_element_type=F32))
    o = jnp.concatenate(outs, axis=1) * sc_ref[...] * _silu(g_ref[...])
    o_ref[...] = o.astype(o_ref.dtype)
    buf[0:POOL_HIST, :] = buf[ts:ts + POOL_HIST, :]


def _pool(h, w_pool_b, pool_scale, bsz, seq, col_u, col_g, ts):
    bw = pool_scale.shape[-1]
    group = bw // len(POOL_WINDOWS)
    nblk = seq // ts
    blocks = 2 * _nbytes((ts, bw), F32) + _nbytes((ts, bw), BF16) + _nbytes((ts + POOL_HIST, bw), F32)
    return pl.pallas_call(
        functools.partial(_pool_kernel, ts=ts, group=group),
        out_shape=jax.ShapeDtypeStruct((bsz * seq, bw), BF16),
        grid=(bsz, nblk),
        in_specs=[pl.BlockSpec((ts, bw), lambda b, i: (b * nblk + i, col_u // bw)),
                  pl.BlockSpec((ts, bw), lambda b, i: (b * nblk + i, col_g // bw)),
                  pl.BlockSpec(w_pool_b.shape, lambda b, i: (0, 0, 0)),
                  pl.BlockSpec((1, bw), lambda b, i: (0, 0))],
        out_specs=pl.BlockSpec((ts, bw), lambda b, i: (b * nblk + i, 0)),
        scratch_shapes=[pltpu.VMEM((ts + POOL_HIST, bw), F32)],
        compiler_params=pltpu.CompilerParams(
            dimension_semantics=("parallel", "arbitrary"), vmem_limit_bytes=_vmem_limit(blocks)),
        name="pool_mixer",
    )(h, h, w_pool_b, pool_scale.reshape(1, bw))


def _split_bf16(x, parts):
    out = []
    r = x
    for _ in range(parts):
        p = r.astype(BF16)
        out.append(p)
        r = r - p.astype(F32)
    return out


def _gla_kernel(q_ref, k_ref, v_ref, gg_ref, lr_ref, wup_ref, bg_ref, nrm_ref, o_ref,
                state_ref, b_ref, *, rows, dk, dv):
    i = pl.program_id(1)
    C = GLA_CHUNK
    nsub = C // GLA_SUB

    @pl.when(i == 0)
    def _():
        state_ref[...] = jnp.zeros(state_ref.shape, F32)

    lr_h, lr_m = _split_bf16(lr_ref[...], 2)
    w_h, w_m = _split_bf16(wup_ref[...], 2)
    z = (jnp.dot(lr_h, w_h, preferred_element_type=F32)
         + jnp.dot(lr_h, w_m, preferred_element_type=F32)
         + jnp.dot(lr_m, w_h, preferred_element_type=F32)) + bg_ref[...]
    log_a = -(jnp.maximum(-z, 0.0) + jnp.log1p(jnp.exp(-jnp.abs(z)))) / GLA_TAU

    rr = lax.broadcasted_iota(jnp.int32, (rows, rows), 0)
    cc = lax.broadcasted_iota(jnp.int32, (rows, rows), 1)
    tri = jnp.where((_div_pow2(rr, C) == _div_pow2(cc, C)) & (cc <= rr), 1.0, 0.0).astype(BF16)
    bsum = None
    for part in _split_bf16(log_a, 3):
        t = jnp.dot(tri, part, preferred_element_type=F32)
        bsum = t if bsum is None else bsum + t
    b_ref[...] = bsum

    row = lax.broadcasted_iota(jnp.int32, (C, 1), 0)
    col = lax.broadcasted_iota(jnp.int32, (1, C), 1)
    row_sub = _div_pow2(row, GLA_SUB)
    col_sub = _div_pow2(col, GLA_SUB)
    lane_c = lax.broadcasted_iota(jnp.int32, (GLA_SUB, C), 1)
    scale = dk ** -0.5

    def chunk_body(c, carry):
        r0 = pl.multiple_of(c * C, C)
        for h in range(GLA_HEADS):
            kc = slice(h * dk, (h + 1) * dk)
            vc = slice(h * dv, (h + 1) * dv)
            qs = q_ref[pl.ds(r0, C), kc] * scale
            k = k_ref[pl.ds(r0, C), kc]
            v = v_ref[pl.ds(r0, C), vc].astype(BF16)
            b = b_ref[pl.ds(r0, C), kc]
            b_last = b[C - 1:C, :]
            st = state_ref[h]

            q_in = (qs * jnp.exp(b)).astype(BF16)
            o = lax.dot_general(q_in, st.astype(BF16), (((1,), (1,)), ((), ())),
                                preferred_element_type=F32)

            b_heads = [b[a * GLA_SUB:a * GLA_SUB + 1, :] for a in range(nsub)]
            r_rows = jnp.concatenate(
                [jnp.broadcast_to(bh, (GLA_SUB, dk)) for bh in b_heads], axis=0)
            q_t = qs * jnp.exp(b - r_rows)
            lhs = jnp.concatenate(
                [jnp.where(row_sub == a, q_t, 0.0) for a in range(nsub)], axis=1).astype(BF16)
            rhs = jnp.concatenate(
                [k * jnp.exp(jnp.minimum(b_heads[a] - b, 0.0)) for a in range(nsub)],
                axis=1).astype(BF16)
            a_off = lax.dot_general(lhs, rhs, (((1,), (1,)), ((), ())),
                                    preferred_element_type=F32)

            diag_blocks = []
            for a in range(nsub):
                rs = slice(a * GLA_SUB, (a + 1) * GLA_SUB)
                qa = qs[rs, :]
                ba = b[rs, :]
                blk = jnp.zeros((GLA_SUB, C), F32)
                for jj in range(GLA_SUB):
                    j = a * GLA_SUB + jj
                    kj = k[j:j + 1, :]
                    bj = b[j:j + 1, :]
                    tj = qa * kj * jnp.exp(jnp.minimum(ba - bj, 0.0))
                    cj = jnp.sum(tj, axis=1, keepdims=True)
                    blk = jnp.where(lane_c == j, cj, blk)
                diag_blocks.append(blk)
            a_diag = jnp.concatenate(diag_blocks, axis=0)
            attn = jnp.where(row_sub > col_sub, a_off,
                             jnp.where((row_sub == col_sub) & (row >= col), a_diag, 0.0))
            o = o + jnp.dot(attn.astype(BF16), v, preferred_element_type=F32)

            k_dec = (k * jnp.exp(b_last - b)).astype(BF16)
            upd = lax.dot_general(v, k_dec, (((0,), (0,)), ((), ())),
                                  preferred_element_type=F32)
            state_ref[h] = st * jnp.exp(b_last) + upd

            ms = jnp.mean(o * o, axis=-1, keepdims=True)
            on = o * lax.rsqrt(ms + LN_EPS) * nrm_ref[:, vc]
            o_ref[pl.ds(r0, C), vc] = (on * _silu(gg_ref[pl.ds(r0, C), vc])).astype(o_ref.dtype)
        return carry

    lax.fori_loop(0, rows // C, chunk_body, 0)


def _gla(h, w_up_pad, b_gla, gla_norm, bsz, seq, cols, rows):
    col_q, col_k, col_v, col_g, col_lr = cols
    dkt = b_gla.shape[-1]
    dvt = gla_norm.shape[-1]
    dk, dv = dkt // GLA_HEADS, dvt // GLA_HEADS
    lrw = w_up_pad.shape[0]
    nblk = seq // rows
    blocks = (2 * _nbytes((rows, dkt), F32) + 2 * _nbytes((rows, dvt), F32)
              + _nbytes((rows, lrw), F32) + _nbytes((rows, dvt), BF16)
              + _nbytes((rows, dkt), F32) + _nbytes((GLA_HEADS, dv, dk), F32))
    return pl.pallas_call(
        functools.partial(_gla_kernel, rows=rows, dk=dk, dv=dv),
        out_shape=jax.ShapeDtypeStruct((bsz * seq, dvt), BF16),
        grid=(bsz, nblk),
        in_specs=[pl.BlockSpec((rows, dkt), lambda b, i: (b * nblk + i, col_q // dkt)),
                  pl.BlockSpec((rows, dkt), lambda b, i: (b * nblk + i, col_k // dkt)),
                  pl.BlockSpec((rows, dvt), lambda b, i: (b * nblk + i, col_v // dvt)),
                  pl.BlockSpec((rows, dvt), lambda b, i: (b * nblk + i, col_g // dvt)),
                  pl.BlockSpec((rows, lrw), lambda b, i: (b * nblk + i, col_lr // lrw)),
                  pl.BlockSpec((lrw, dkt), lambda b, i: (0, 0)),
                  pl.BlockSpec((1, dkt), lambda b, i: (0, 0)),
                  pl.BlockSpec((1, dvt), lambda b, i: (0, 0))],
        out_specs=pl.BlockSpec((rows, dvt), lambda b, i: (b * nblk + i, 0)),
        scratch_shapes=[pltpu.VMEM((GLA_HEADS, dv, dk), F32),
                        pltpu.VMEM((rows, dkt), F32)],
        compiler_params=pltpu.CompilerParams(
            dimension_semantics=("parallel", "arbitrary"), vmem_limit_bytes=_vmem_limit(blocks)),
        name="gla_mixer",
    )(h, h, h, h, h, w_up_pad, b_gla.reshape(1, dkt), gla_norm.reshape(1, dvt))


def _rope_lane_table():
    half = ROPE_DIM // 2
    inv_freq = ROPE_THETA ** (-np.arange(0, ROPE_DIM, 2, dtype=np.float32) / ROPE_DIM)
    lane = np.arange(LANES) % SWA_HEAD_DIM
    tab = np.where(lane < ROPE_DIM, inv_freq[lane % half], 0.0).astype(np.float32)
    return jnp.asarray(tab.reshape(1, LANES))


def _swa_kernel(sink_ref, q_ref, k_ref, v_ref, g_ref, pos_ref, inv_ref, o_ref, wk_ref, wv_ref,
                *, n_pairs):
    n = pl.program_id(1)
    blk = SWA_BLOCK
    half = ROPE_DIM // 2
    hd = SWA_HEAD_DIM

    @pl.when(n == 0)
    def _():
        wk_ref[...] = jnp.zeros(wk_ref.shape, BF16)
        wv_ref[...] = jnp.zeros(wv_ref.shape, BF16)

    lane = lax.broadcasted_iota(jnp.int32, (1, LANES), 1)
    lane_h = _mod_pow2(lane, hd)
    lo = lane < hd
    ang = pos_ref[...] * inv_ref[...]
    cos = jnp.cos(ang)
    sin = jnp.sin(ang)

    def rope(x):
        x_up = pltpu.roll(x, LANES - half, axis=1)
        x_dn = pltpu.roll(x, half, axis=1)
        return x * cos + jnp.where(lane_h < half, -x_up, x_dn) * sin

    for g in range(2):
        for part in range(2):
            base = part * 2 * blk
            wk_ref[g, base:base + blk, :] = wk_ref[g, base + blk:base + 2 * blk, :]
            wv_ref[g, base:base + blk, :] = wv_ref[g, base + blk:base + 2 * blk, :]
    k_r = rope(k_ref[...])
    v_c = v_ref[...]
    k_sw = pltpu.roll(k_r, hd, axis=1)
    v_sw = pltpu.roll(v_c, hd, axis=1)
    zero = jnp.zeros_like(k_r)
    for g in range(2):
        k_lo, k_hi = (k_r, k_sw) if g == 0 else (k_sw, k_r)
        v_lo, v_hi = (v_c, v_sw) if g == 0 else (v_sw, v_c)
        wk_ref[g, blk:2 * blk, :] = jnp.where(lo, k_lo, zero).astype(BF16)
        wk_ref[g, 3 * blk:4 * blk, :] = jnp.where(lo, zero, k_hi).astype(BF16)
        wv_ref[g, blk:2 * blk, :] = jnp.where(lo, v_lo, zero).astype(BF16)
        wv_ref[g, 3 * blk:4 * blk, :] = jnp.where(lo, zero, v_hi).astype(BF16)

    r = lax.broadcasted_iota(jnp.int32, (blk, 2 * blk), 0)
    c = lax.broadcasted_iota(jnp.int32, (blk, 2 * blk), 1)
    valid = (c > r) & (c <= r + WINDOW) & ((n > 0) | (c >= blk))
    neg_inf = jnp.float32(-jnp.inf)
    qscale = hd ** -0.5
    pairs_per_group = n_pairs // 2
    for p in range(n_pairs):
        g = p // pairs_per_group
        cols = slice(p * LANES, (p + 1) * LANES)
        q_pair = (rope(q_ref[:, cols]) * qscale).astype(BF16)
        s = lax.dot_general(q_pair, wk_ref[g], (((1,), (1,)), ((), ())),
                            preferred_element_type=F32)
        probs = []
        for hh in range(2):
            sink = sink_ref[2 * p + hh]
            sh = jnp.where(valid, s[:, hh * 2 * blk:(hh + 1) * 2 * blk], neg_inf)
            m = jnp.maximum(jnp.max(sh, axis=-1, keepdims=True), sink)
            e = jnp.exp(sh - m)
            denom = jnp.sum(e, axis=-1, keepdims=True) + jnp.exp(sink - m)
            probs.append((e / denom).astype(BF16))
        pcat = jnp.concatenate(probs, axis=1)
        o = jnp.dot(pcat, wv_ref[g], preferred_element_type=F32)
        o_ref[:, cols] = (o * _silu(g_ref[:, cols])).astype(o_ref.dtype)


def _swa(h, pos_f, sinks, bsz, seq, cols):
    col_q, col_k, col_v, col_g = cols
    n_heads = sinks.shape[0]
    bw = n_heads * SWA_HEAD_DIM
    n_pairs = bw // LANES
    blk = SWA_BLOCK
    nblk = seq // blk
    blocks = (2 * _nbytes((blk, bw), F32) + 2 * _nbytes((blk, LANES), F32) + _nbytes((blk, LANES), F32)
              + _nbytes((blk, bw), BF16) + 2 * _nbytes((2, 4 * blk, LANES), BF16))
    return pl.pallas_call(
        functools.partial(_swa_kernel, n_pairs=n_pairs),
        out_shape=jax.ShapeDtypeStruct((bsz * seq, bw), BF16),
        grid=(bsz, nblk),
        in_specs=[pl.BlockSpec(memory_space=pltpu.SMEM),
                  pl.BlockSpec((blk, bw), lambda b, i: (b * nblk + i, col_q // bw)),
                  pl.BlockSpec((blk, LANES), lambda b, i: (b * nblk + i, col_k // LANES)),
                  pl.BlockSpec((blk, LANES), lambda b, i: (b * nblk + i, col_v // LANES)),
                  pl.BlockSpec((blk, bw), lambda b, i: (b * nblk + i, col_g // bw)),
                  pl.BlockSpec((blk, 1), lambda b, i: (b * nblk + i, 0)),
                  pl.BlockSpec((1, LANES), lambda b, i: (0, 0))],
        out_specs=pl.BlockSpec((blk, bw), lambda b, i: (b * nblk + i, 0)),
        scratch_shapes=[pltpu.VMEM((2, 4 * blk, LANES), BF16),
                        pltpu.VMEM((2, 4 * blk, LANES), BF16)],
        compiler_params=pltpu.CompilerParams(
            dimension_semantics=("parallel", "arbitrary"), vmem_limit_bytes=_vmem_limit(blocks)),
        name="swa_mixer",
    )(sinks, h, h, h, h, pos_f, _rope_lane_table())


def _mem_kernel(q_ref, g_ref, mk_ref, mv_ref, o_ref, *, hd):
    scale = hd ** -0.5
    for h in range(XA_HEADS):
        cols = slice(h * hd, (h + 1) * hd)
        qh = q_ref[:, cols].astype(BF16)
        s = lax.dot_general(qh, mk_ref[:, cols], (((1,), (1,)), ((), ())),
                            preferred_element_type=F32) * scale
        m = jnp.max(s, axis=-1, keepdims=True)
        e = jnp.exp(s - m)
        p = e / jnp.sum(e, axis=-1, keepdims=True)
        o = jnp.dot(p.astype(BF16), mv_ref[:, cols], preferred_element_type=F32)
        o_ref[:, cols] = (o * _silu(g_ref[:, cols])).astype(o_ref.dtype)


def _mem_attn(h, mkv, bsz, seq, mem_len, cols, ts):
    col_q, col_g = cols
    bw = mkv.shape[1] // 2
    hd = bw // XA_HEADS
    nblk = seq // ts
    blocks = 2 * _nbytes((ts, bw), F32) + 2 * _nbytes((mem_len, bw), BF16) + _nbytes((ts, bw), BF16)
    return pl.pallas_call(
        functools.partial(_mem_kernel, hd=hd),
        out_shape=jax.ShapeDtypeStruct((bsz * seq, bw), BF16),
        grid=(bsz, nblk),
        in_specs=[pl.BlockSpec((ts, bw), lambda b, i: (b * nblk + i, col_q // bw)),
                  pl.BlockSpec((ts, bw), lambda b, i: (b * nblk + i, col_g // bw)),
                  pl.BlockSpec((mem_len, bw), lambda b, i: (b, 0)),
                  pl.BlockSpec((mem_len, bw), lambda b, i: (b, 1))],
        out_specs=pl.BlockSpec((ts, bw), lambda b, i: (b * nblk + i, 0)),
        compiler_params=pltpu.CompilerParams(
            dimension_semantics=("parallel", "parallel"), vmem_limit_bytes=_vmem_limit(blocks)),
        name="mem_attn",
    )(h, h, mkv, mkv)


IN_TN = 2 * LANES


def _main_layout(d_model):
    bw = d_model // 4
    dkt, dvt = bw // 2, bw
    kvw = 2 * SWA_HEAD_DIM
    names = ["pool_u", "pool_g", "gq", "gk", "gv", "gg", "glr", "sq", "sk", "sv", "sg", "xq", "xg"]
    widths = [bw, bw, dkt, dkt, dvt, dvt, GLA_RANK, bw, kvw, kvw, bw, bw, bw]
    src = {}
    off = 0
    for nme, w in zip(names, widths):
        src[nme] = (off, w)
        off += w
    n_main = off
    order = ["pool_u", "pool_g", "gv", "gg", "sq", "sg", "xq", "xg", "gq", "gk", "sk", "sv", "glr"]
    dst = {}
    cols = []
    off = 0
    for nme in order:
        o, w = src[nme]
        w_dst = w if nme != "glr" else IN_TN
        assert off % w_dst == 0
        dst[nme] = off
        cols.append(o + np.arange(w_dst))
        off += w_dst
    cols = np.concatenate(cols)
    assert off % IN_TN == 0 and cols.max() < n_main
    starts = cols[::IN_TN]
    assert np.all(cols.reshape(-1, IN_TN) == starts[:, None] + np.arange(IN_TN))
    return starts, dst, n_main


def _hybrid_layer(l, x, xb, mem_b, pos_f, w_in_t, w_pool, pool_scale, w_gla_up, b_gla, gla_norm,
                  sinks, w_mem_kv, w_branch, w_out, ln_g, ln_b, alpha, emit_bf16, bsz, seq):
    m, d = x.shape
    col_starts, dst, n_main = _main_layout(d)
    w_up_pad = jnp.concatenate(
        [w_gla_up[l], jnp.zeros((IN_TN - GLA_RANK, w_gla_up.shape[-1]), w_gla_up.dtype)], axis=0)

    h = _in_proj(xb, w_in_t, l, col_starts, tm=2048, tn=IN_TN)
    mkv = _matmul(mem_b, w_mem_kv, l, BF16, tm=mem_b.shape[0], tn=512, name="mem_kv")

    o_pool = _pool(h, w_pool[l].astype(BF16), pool_scale[l], bsz, seq,
                   dst["pool_u"], dst["pool_g"], ts=512)
    o_gla = _gla(h, w_up_pad, b_gla[l], gla_norm[l], bsz, seq,
                 (dst["gq"], dst["gk"], dst["gv"], dst["gg"], dst["glr"]), rows=512)
    o_swa = _swa(h, pos_f, sinks[l], bsz, seq, (dst["sq"], dst["sk"], dst["sv"], dst["sg"]))
    o_mem = _mem_attn(h, mkv, bsz, seq, mem_b.shape[0] // bsz, (dst["xq"], dst["xg"]), ts=512)

    w_mg_t = _stage_rows(w_in_t, l, n_main, N_BRANCH * d, tr=512)
    y = _merge(xb, (o_pool, o_gla, o_swa, o_mem), w_mg_t, w_branch, l, tm=1024, tn=256)
    z = _matmul(y, w_out, l, F32, tm=2048, tn=256, name="out_proj", resid=x, alpha=alpha)
    return _layer_norm(z, ln_g[l], ln_b[l], tm=256, emit_bf16=emit_bf16)


def kernel(x, mem, positions, w_in, w_pool, pool_scale, w_gla_up, b_gla, gla_norm, sinks,
           w_mem_kv, w_branch, w_out, ln_g, ln_b):
    bsz, seq, d = x.shape
    depth = w_in.shape[0]
    alpha = (2 * depth) ** 0.25
    xf = x.reshape(bsz * seq, d)
    xb = xf.astype(BF16)
    mem_b = mem.reshape(bsz * mem.shape[1], d).astype(BF16)
    pos_f = positions.astype(F32).reshape(bsz * seq, 1)
    w_in_t = jnp.swapaxes(w_in, 1, 2)
    for l in range(depth):
        xf, xb = _hybrid_layer(
            l, xf, xb, mem_b, pos_f, w_in_t, w_pool, pool_scale, w_gla_up, b_gla,
            gla_norm, sinks, w_mem_kv, w_branch, w_out, ln_g, ln_b,
            alpha, l + 1 < depth, bsz, seq)
    return xf.reshape(bsz, seq, d)
```

```python
import functools
import math

import jax
import jax.numpy as jnp
import numpy as np
from jax import lax
from jax.experimental import pallas as pl
from jax.experimental.pallas import tpu as pltpu

F32 = jnp.float32
BF16 = jnp.bfloat16

N_BRANCH = 4
POOL_WINDOWS = (2, 4, 8, 16)
GLA_HEADS = 4
GLA_RANK = 16
GLA_TAU = 16.0
GLA_CHUNK = 64
GLA_SUB = 16
SWA_HEAD_DIM = 64
SWA_GROUP = 8
WINDOW = 128
SWA_BLOCK = 128
ROPE_THETA = 500000.0
ROPE_DIM = SWA_HEAD_DIM // 4
XA_HEADS = 4
LN_EPS = 1e-5

LANES = 128
SUBLANES = 8
V7X_VMEM_BYTES = 64 * 1024 * 1024
VMEM_CAP = V7X_VMEM_BYTES - 8 * 1024 * 1024
VMEM_TEMP_BYTES = 12 * 1024 * 1024


def _vmem_limit(streamed_bytes, resident_bytes=0):
    return int(min(VMEM_CAP, 2 * streamed_bytes + resident_bytes + VMEM_TEMP_BYTES))


def _nbytes(shape, dtype):
    return int(np.prod(shape)) * jnp.dtype(dtype).itemsize


def _silu(g):
    return g * jax.nn.sigmoid(g)


def _div_pow2(x, n):
    assert n & (n - 1) == 0
    return x >> (n.bit_length() - 1)


def _mod_pow2(x, n):
    assert n & (n - 1) == 0
    return x & (n - 1)


NT_DIMS = (((1,), (1,)), ((), ()))


def _mm_kernel(a_ref, w_ref, o_ref):
    w = w_ref[...].astype(BF16)
    o_ref[...] = jnp.dot(a_ref[...], w, preferred_element_type=F32).astype(o_ref.dtype)


def _mm_resid_kernel(a_ref, w_ref, x_ref, o_ref, *, alpha):
    w = w_ref[...].astype(BF16)
    o_ref[...] = alpha * x_ref[...] + jnp.dot(a_ref[...], w, preferred_element_type=F32)


def _matmul(a, w, layer, out_dtype, tm, tn, name, resid=None, alpha=None):
    m, k = a.shape
    n = w.shape[-1]
    assert m % tm == 0 and n % tn == 0
    resident = _nbytes((tm, k), a.dtype)
    streamed = _nbytes((k, tn), w.dtype) + _nbytes((tm, tn), out_dtype)
    in_specs = [pl.BlockSpec((tm, k), lambda i, j: (i, 0), pipeline_mode=pl.Buffered(1)),
                pl.BlockSpec((None, k, tn), lambda i, j: (layer, 0, j))]
    args = [a, w]
    if resid is None:
        kern = _mm_kernel
    else:
        kern = functools.partial(_mm_resid_kernel, alpha=alpha)
        in_specs.append(pl.BlockSpec((tm, tn), lambda i, j: (i, j)))
        args.append(resid)
        streamed += _nbytes((tm, tn), resid.dtype)
    return pl.pallas_call(
        kern,
        out_shape=jax.ShapeDtypeStruct((m, n), out_dtype),
        grid=(m // tm, n // tn),
        in_specs=in_specs,
        out_specs=pl.BlockSpec((tm, tn), lambda i, j: (i, j)),
        compiler_params=pltpu.CompilerParams(
            dimension_semantics=("parallel", "arbitrary"),
            vmem_limit_bytes=_vmem_limit(streamed, resident)),
        name=name,
    )(*args)


def _in_proj_kernel(tab_ref, x_ref, w_ref, g_ref, o_ref, gs_ref):
    w = w_ref[0].astype(BF16)
    o_ref[...] = lax.dot_general(x_ref[...], w, NT_DIMS, preferred_element_type=F32)
    gs_ref[...] = g_ref[0].astype(gs_ref.dtype)


def _in_proj(xb, w_in_t, layer, col_starts, gate_row0, gate_rows, tm, tn):
    m, k = xb.shape
    nblk = len(col_starts)
    assert m % tm == 0 and all(int(s) % SUBLANES == 0 for s in col_starts)
    n_steps = (m // tm) * nblk
    sr = LANES
    n_slabs = gate_rows // sr
    assert gate_rows % sr == 0 and n_slabs <= n_steps and gate_row0 % SUBLANES == 0
    resident = _nbytes((tm, k), xb.dtype)
    streamed = (_nbytes((tn, k), w_in_t.dtype) + _nbytes((tm, tn), F32)
                + _nbytes((sr, k), w_in_t.dtype) + _nbytes((sr, k), BF16))

    def slab(i, j):
        return jnp.minimum(i * nblk + j, n_slabs - 1)

    return pl.pallas_call(
        _in_proj_kernel,
        out_shape=[jax.ShapeDtypeStruct((m, nblk * tn), F32),
                   jax.ShapeDtypeStruct((gate_rows, k), BF16)],
        grid_spec=pltpu.PrefetchScalarGridSpec(
            num_scalar_prefetch=1,
            grid=(m // tm, nblk),
            in_specs=[pl.BlockSpec((tm, k), lambda i, j, tab: (i, 0), pipeline_mode=pl.Buffered(1)),
                      pl.BlockSpec((pl.Element(1), pl.Element(tn), pl.Element(k)),
                                   lambda i, j, tab: (layer, pl.multiple_of(tab[j], SUBLANES), 0)),
                      pl.BlockSpec((pl.Element(1), pl.Element(sr), pl.Element(k)),
                                   lambda i, j, tab: (
                                       layer, pl.multiple_of(gate_row0 + slab(i, j) * sr, SUBLANES), 0))],
            out_specs=[pl.BlockSpec((tm, tn), lambda i, j, tab: (i, j)),
                       pl.BlockSpec((sr, k), lambda i, j, tab: (slab(i, j), 0))]),
        compiler_params=pltpu.CompilerParams(
            dimension_semantics=("arbitrary", "arbitrary"),
            vmem_limit_bytes=_vmem_limit(streamed, resident)),
        name="in_proj",
    )(jnp.asarray(np.asarray(col_starts, np.int32)), xb, w_in_t, w_in_t)


def _merge_kernel(x_ref, o0, o1, o2, o3, g0, g1, g2, g3, w0, w1, w2, w3, y_ref):
    x = x_ref[...]
    acc = None
    for o_ref, g_ref, w_ref in ((o0, g0, w0), (o1, g1, w1), (o2, g2, w2), (o3, g3, w3)):
        gate = lax.dot_general(x, g_ref[...], NT_DIMS, preferred_element_type=F32)
        proj = jnp.dot(o_ref[...], w_ref[...].astype(BF16), preferred_element_type=F32)
        term = jax.nn.sigmoid(gate) * proj
        acc = term if acc is None else acc + term
    y_ref[...] = acc.astype(y_ref.dtype)


def _merge(xb, branches, w_mg_t, w_branch, layer, tm, tn):
    m, d = xb.shape
    bw = branches[0].shape[1]
    nj = d // tn
    in_specs = [pl.BlockSpec((tm, d), lambda i, j: (i, 0), pipeline_mode=pl.Buffered(1))]
    in_specs += [pl.BlockSpec((tm, bw), lambda i, j: (i, 0), pipeline_mode=pl.Buffered(1))
                 for _ in range(N_BRANCH)]
    in_specs += [pl.BlockSpec((tn, d), functools.partial(lambda i, j, b: (b * nj + j, 0), b=b))
                 for b in range(N_BRANCH)]
    in_specs += [pl.BlockSpec((None, None, bw, tn),
                              functools.partial(lambda i, j, b: (layer, b, 0, j), b=b))
                 for b in range(N_BRANCH)]
    resident = _nbytes((tm, d), BF16) + N_BRANCH * _nbytes((tm, bw), BF16)
    streamed = (N_BRANCH * (_nbytes((tn, d), BF16) + _nbytes((bw, tn), w_branch.dtype))
                + _nbytes((tm, tn), BF16))
    return pl.pallas_call(
        _merge_kernel,
        out_shape=jax.ShapeDtypeStruct((m, d), BF16),
        grid=(m // tm, nj),
        in_specs=in_specs,
        out_specs=pl.BlockSpec((tm, tn), lambda i, j: (i, j)),
        compiler_params=pltpu.CompilerParams(
            dimension_semantics=("parallel", "arbitrary"),
            vmem_limit_bytes=_vmem_limit(streamed, resident)),
        name="merge",
    )(xb, *branches, w_mg_t, w_mg_t, w_mg_t, w_mg_t, w_branch, w_branch, w_branch, w_branch)


def _ln_kernel(z_ref, g_ref, b_ref, o_ref, ob_ref=None):
    z = z_ref[...]
    mu = jnp.mean(z, axis=-1, keepdims=True)
    zc = z - mu
    var = jnp.mean(zc * zc, axis=-1, keepdims=True)
    y = zc * lax.rsqrt(var + LN_EPS) * g_ref[...] + b_ref[...]
    o_ref[...] = y
    if ob_ref is not None:
        ob_ref[...] = y.astype(BF16)


def _layer_norm(z, g, b, tm, emit_bf16):
    m, d = z.shape
    out_shape = [jax.ShapeDtypeStruct((m, d), F32)]
    out_specs = [pl.BlockSpec((tm, d), lambda i: (i, 0))]
    if emit_bf16:
        out_shape.append(jax.ShapeDtypeStruct((m, d), BF16))
        out_specs.append(pl.BlockSpec((tm, d), lambda i: (i, 0)))
    blocks = 2 * _nbytes((tm, d), F32) + _nbytes((tm, d), BF16)
    res = pl.pallas_call(
        _ln_kernel,
        out_shape=out_shape,
        grid=(m // tm,),
        in_specs=[pl.BlockSpec((tm, d), lambda i: (i, 0)),
                  pl.BlockSpec((1, d), lambda i: (0, 0)),
                  pl.BlockSpec((1, d), lambda i: (0, 0))],
        out_specs=out_specs,
        compiler_params=pltpu.CompilerParams(
            dimension_semantics=("parallel",), vmem_limit_bytes=_vmem_limit(blocks)),
        name="layer_norm",
    )(z, g.reshape(1, d), b.reshape(1, d))
    return res if emit_bf16 else (res[0], None)


POOL_HIST = 16


def _pool_kernel(u_ref, g_ref, wp_ref, sc_ref, o_ref, buf, *, ts, group):
    i = pl.program_id(1)

    @pl.when(i == 0)
    def _():
        buf[0:POOL_HIST, :] = jnp.zeros((POOL_HIST, buf.shape[1]), F32)

    buf[POOL_HIST:POOL_HIST + ts, :] = u_ref[...]
    t = i * ts + lax.broadcasted_iota(jnp.int32, (ts, 1), 0)
    outs = []
    for gi, w in enumerate(POOL_WINDOWS):
        cols = slice(gi * group, (gi + 1) * group)
        cur = buf[POOL_HIST:POOL_HIST + ts, cols]
        acc = cur
        for lag in range(1, w):
            acc = acc + buf[POOL_HIST - lag:POOL_HIST - lag + ts, cols]
        cnt = jnp.minimum(t + 1, w).astype(F32)
        pooled = acc / cnt - cur
        outs.append(jnp.dot(pooled.astype(BF16), wp_ref[gi], preferred_element_type=F32))
    o = jnp.concatenate(outs, axis=1) * sc_ref[...] * _silu(g_ref[...])
    o_ref[...] = o.astype(o_ref.dtype)
    buf[0:POOL_HIST, :] = buf[ts:ts + POOL_HIST, :]


def _pool(h, w_pool_b, pool_scale, bsz, seq, col_u, col_g, ts):
    bw = pool_scale.shape[-1]
    group = bw // len(POOL_WINDOWS)
    nblk = seq // ts
    blocks = 2 * _nbytes((ts, bw), F32) + _nbytes((ts, bw), BF16) + _nbytes((ts + POOL_HIST, bw), F32)
    return pl.pallas_call(
        functools.partial(_pool_kernel, ts=ts, group=group),
        out_shape=jax.ShapeDtypeStruct((bsz * seq, bw), BF16),
        grid=(bsz, nblk),
        in_specs=[pl.BlockSpec((ts, bw), lambda b, i: (b * nblk + i, col_u // bw)),
                  pl.BlockSpec((ts, bw), lambda b, i: (b * nblk + i, col_g // bw)),
                  pl.BlockSpec(w_pool_b.shape, lambda b, i: (0, 0, 0)),
                  pl.BlockSpec((1, bw), lambda b, i: (0, 0))],
        out_specs=pl.BlockSpec((ts, bw), lambda b, i: (b * nblk + i, 0)),
        scratch_shapes=[pltpu.VMEM((ts + POOL_HIST, bw), F32)],
        compiler_params=pltpu.CompilerParams(
            dimension_semantics=("parallel", "arbitrary"), vmem_limit_bytes=_vmem_limit(blocks)),
        name="pool_mixer",
    )(h, h, w_pool_b, pool_scale.reshape(1, bw))


def _split_bf16(x, parts):
    out = []
    r = x
    for _ in range(parts):
        p = r.astype(BF16)
        out.append(p)
        r = r - p.astype(F32)
    return out


def _gla_kernel(q_ref, k_ref, v_ref, gg_ref, lr_ref, wup_ref, bg_ref, nrm_ref, o_ref,
                state_ref, b_ref, *, rows, dk, dv):
    i = pl.program_id(1)
    C = GLA_CHUNK
    nsub = C // GLA_SUB

    @pl.when(i == 0)
    def _():
        state_ref[...] = jnp.zeros(state_ref.shape, F32)

    lr_h, lr_m = _split_bf16(lr_ref[...], 2)
    w_h, w_m = _split_bf16(wup_ref[...], 2)
    z = (jnp.dot(lr_h, w_h, preferred_element_type=F32)
         + jnp.dot(lr_h, w_m, preferred_element_type=F32)
         + jnp.dot(lr_m, w_h, preferred_element_type=F32)) + bg_ref[...]
    log_a = -(jnp.maximum(-z, 0.0) + jnp.log1p(jnp.exp(-jnp.abs(z)))) / GLA_TAU

    rr = lax.broadcasted_iota(jnp.int32, (rows, rows), 0)
    cc = lax.broadcasted_iota(jnp.int32, (rows, rows), 1)
    tri = jnp.where((_div_pow2(rr, C) == _div_pow2(cc, C)) & (cc <= rr), 1.0, 0.0).astype(BF16)
    bsum = None
    for part in _split_bf16(log_a, 3):
        t = jnp.dot(tri, part, preferred_element_type=F32)
        bsum = t if bsum is None else bsum + t
    b_ref[...] = bsum

    row = lax.broadcasted_iota(jnp.int32, (C, 1), 0)
    col = lax.broadcasted_iota(jnp.int32, (1, C), 1)
    row_sub = _div_pow2(row, GLA_SUB)
    col_sub = _div_pow2(col, GLA_SUB)
    lane_c = lax.broadcasted_iota(jnp.int32, (GLA_SUB, C), 1)
    scale = dk ** -0.5

    def chunk_body(c, carry):
        r0 = pl.multiple_of(c * C, C)
        for h in range(GLA_HEADS):
            kc = slice(h * dk, (h + 1) * dk)
            vc = slice(h * dv, (h + 1) * dv)
            qs = q_ref[pl.ds(r0, C), kc] * scale
            k = k_ref[pl.ds(r0, C), kc]
            v = v_ref[pl.ds(r0, C), vc].astype(BF16)
            b = b_ref[pl.ds(r0, C), kc]
            b_last = b[C - 1:C, :]
            st = state_ref[h]

            q_in = (qs * jnp.exp(b)).astype(BF16)
            o = lax.dot_general(q_in, st.astype(BF16), (((1,), (1,)), ((), ())),
                                preferred_element_type=F32)

            b_heads = [b[a * GLA_SUB:a * GLA_SUB + 1, :] for a in range(nsub)]
            r_rows = jnp.concatenate(
                [jnp.broadcast_to(bh, (GLA_SUB, dk)) for bh in b_heads], axis=0)
            q_t = qs * jnp.exp(b - r_rows)
            lhs = jnp.concatenate(
                [jnp.where(row_sub == a, q_t, 0.0) for a in range(nsub)], axis=1).astype(BF16)
            rhs = jnp.concatenate(
                [k * jnp.exp(jnp.minimum(b_heads[a] - b, 0.0)) for a in range(nsub)],
                axis=1).astype(BF16)
            a_off = lax.dot_general(lhs, rhs, (((1,), (1,)), ((), ())),
                                    preferred_element_type=F32)

            diag_blocks = []
            for a in range(nsub):
                rs = slice(a * GLA_SUB, (a + 1) * GLA_SUB)
                qa = qs[rs, :]
                ba = b[rs, :]
                blk = jnp.zeros((GLA_SUB, C), F32)
                for jj in range(GLA_SUB):
                    j = a * GLA_SUB + jj
                    kj = k[j:j + 1, :]
                    bj = b[j:j + 1, :]
                    tj = qa * kj * jnp.exp(jnp.minimum(ba - bj, 0.0))
                    cj = jnp.sum(tj, axis=1, keepdims=True)
                    blk = jnp.where(lane_c == j, cj, blk)
                diag_blocks.append(blk)
            a_diag = jnp.concatenate(diag_blocks, axis=0)
            attn = jnp.where(row_sub > col_sub, a_off,
                             jnp.where((row_sub == col_sub) & (row >= col), a_diag, 0.0))
            o = o + jnp.dot(attn.astype(BF16), v, preferred_element_type=F32)

            k_dec = (k * jnp.exp(b_last - b)).astype(BF16)
            upd = lax.dot_general(v, k_dec, (((0,), (0,)), ((), ())),
                                  preferred_element_type=F32)
            state_ref[h] = st * jnp.exp(b_last) + upd

            ms = jnp.mean(o * o, axis=-1, keepdims=True)
            on = o * lax.rsqrt(ms + LN_EPS) * nrm_ref[:, vc]
            o_ref[pl.ds(r0, C), vc] = (on * _silu(gg_ref[pl.ds(r0, C), vc])).astype(o_ref.dtype)
        return carry

    lax.fori_loop(0, rows // C, chunk_body, 0)


def _gla(h, w_up_pad, b_gla, gla_norm, bsz, seq, cols, rows):
    col_q, col_k, col_v, col_g, col_lr = cols
    dkt = b_gla.shape[-1]
    dvt = gla_norm.shape[-1]
    dk, dv = dkt // GLA_HEADS, dvt // GLA_HEADS
    lrw = w_up_pad.shape[0]
    nblk = seq // rows
    blocks = (2 * _nbytes((rows, dkt), F32) + 2 * _nbytes((rows, dvt), F32)
              + _nbytes((rows, lrw), F32) + _nbytes((rows, dvt), BF16)
              + _nbytes((rows, dkt), F32) + _nbytes((GLA_HEADS, dv, dk), F32))
    return pl.pallas_call(
        functools.partial(_gla_kernel, rows=rows, dk=dk, dv=dv),
        out_shape=jax.ShapeDtypeStruct((bsz * seq, dvt), BF16),
        grid=(bsz, nblk),
        in_specs=[pl.BlockSpec((rows, dkt), lambda b, i: (b * nblk + i, col_q // dkt)),
                  pl.BlockSpec((rows, dkt), lambda b, i: (b * nblk + i, col_k // dkt)),
                  pl.BlockSpec((rows, dvt), lambda b, i: (b * nblk + i, col_v // dvt)),
                  pl.BlockSpec((rows, dvt), lambda b, i: (b * nblk + i, col_g // dvt)),
                  pl.BlockSpec((rows, lrw), lambda b, i: (b * nblk + i, col_lr // lrw)),
                  pl.BlockSpec((lrw, dkt), lambda b, i: (0, 0)),
                  pl.BlockSpec((1, dkt), lambda b, i: (0, 0)),
                  pl.BlockSpec((1, dvt), lambda b, i: (0, 0))],
        out_specs=pl.BlockSpec((rows, dvt), lambda b, i: (b * nblk + i, 0)),
        scratch_shapes=[pltpu.VMEM((GLA_HEADS, dv, dk), F32),
                        pltpu.VMEM((rows, dkt), F32)],
        compiler_params=pltpu.CompilerParams(
            dimension_semantics=("parallel", "arbitrary"), vmem_limit_bytes=_vmem_limit(blocks)),
        name="gla_mixer",
    )(h, h, h, h, h, w_up_pad, b_gla.reshape(1, dkt), gla_norm.reshape(1, dvt))


def _rope_lane_table():
    half = ROPE_DIM // 2
    inv_freq = ROPE_THETA ** (-np.arange(0, ROPE_DIM, 2, dtype=np.float32) / ROPE_DIM)
    lane = np.arange(LANES) % SWA_HEAD_DIM
    tab = np.where(lane < ROPE_DIM, inv_freq[lane % half], 0.0).astype(np.float32)
    return jnp.asarray(tab.reshape(1, LANES))


SWA_ROW_CHUNK = 32


def _swa_kernel(sink_ref, q_ref, k_ref, v_ref, g_ref, pos_ref, inv_ref, o_ref,
                wk_ref, wv_ref, qr_ref, s_ref, p_ref, es_ref, bias_ref, *, n_pairs):
    n = pl.program_id(1)
    blk = SWA_BLOCK
    half = ROPE_DIM // 2
    hd = SWA_HEAD_DIM
    ppg = n_pairs // 2
    rc_rows = SWA_ROW_CHUNK

    @pl.when(n == 0)
    def _():
        wk_ref[...] = jnp.zeros(wk_ref.shape, BF16)
        row = lax.broadcasted_iota(jnp.int32, (4 * blk, 2 * LANES), 0)
        col = lax.broadcasted_iota(jnp.int32, (4 * blk, 2 * LANES), 1)
        ones = ((col >= LANES) & ((col >= LANES + hd) == (row >= 2 * blk))).astype(BF16)
        for g in range(2):
            wv_ref[g] = ones

    lane = lax.broadcasted_iota(jnp.int32, (1, LANES), 1)
    lane_h = _mod_pow2(lane, hd)
    lo = lane < hd
    ang = pos_ref[...] * inv_ref[...]
    cos = jnp.cos(ang)
    sin = jnp.sin(ang)

    def rope(x):
        x_up = pltpu.roll(x, LANES - half, axis=1)
        x_dn = pltpu.roll(x, half, axis=1)
        return x * cos + jnp.where(lane_h < half, -x_up, x_dn) * sin

    for g in range(2):
        for part in range(2):
            base = part * 2 * blk
            wk_ref[g, base:base + blk, :] = wk_ref[g, base + blk:base + 2 * blk, :]
            wv_ref[g, base:base + blk, 0:LANES] = wv_ref[g, base + blk:base + 2 * blk, 0:LANES]
    k_r = rope(k_ref[...])
    v_c = v_ref[...]
    k_sw = pltpu.roll(k_r, hd, axis=1)
    v_sw = pltpu.roll(v_c, hd, axis=1)
    zero = jnp.zeros_like(k_r)
    for g in range(2):
        k_lo, k_hi = (k_r, k_sw) if g == 0 else (k_sw, k_r)
        v_lo, v_hi = (v_c, v_sw) if g == 0 else (v_sw, v_c)
        wk_ref[g, blk:2 * blk, :] = jnp.where(lo, k_lo, zero).astype(BF16)
        wk_ref[g, 3 * blk:4 * blk, :] = jnp.where(lo, zero, k_hi).astype(BF16)
        wv_ref[g, blk:2 * blk, 0:LANES] = jnp.where(lo, v_lo, zero).astype(BF16)
        wv_ref[g, 3 * blk:4 * blk, 0:LANES] = jnp.where(lo, zero, v_hi).astype(BF16)

    r = lax.broadcasted_iota(jnp.int32, (blk, 2 * blk), 0)
    c = lax.broadcasted_iota(jnp.int32, (blk, 2 * blk), 1)
    valid = (c > r) & (c <= r + WINDOW) & ((n > 0) | (c >= blk))
    bias_ref[...] = jnp.where(valid, 0.0, -jnp.inf).astype(F32)

    qscale = hd ** -0.5
    for p in range(n_pairs):
        g, pp = divmod(p, ppg)
        qr_ref[g, pp * blk:(pp + 1) * blk, :] = (
            rope(q_ref[:, p * LANES:(p + 1) * LANES]) * qscale).astype(BF16)

    for g in range(2):
        s_ref[g] = lax.dot_general(qr_ref[g], wk_ref[g], NT_DIMS, preferred_element_type=F32)

    for g in range(2):
        for pp in range(ppg):
            for hh in range(2):
                sink = sink_ref[2 * (g * ppg + pp) + hh]
                kc = slice(hh * 2 * blk, (hh + 1) * 2 * blk)
                for rc in range(blk // rc_rows):
                    rows = slice(pp * blk + rc * rc_rows, pp * blk + (rc + 1) * rc_rows)
                    sh = s_ref[g, rows, kc] + bias_ref[rc * rc_rows:(rc + 1) * rc_rows, :]
                    m = jnp.maximum(jnp.max(sh, axis=-1, keepdims=True), sink)
                    p_ref[g, rows, kc] = jnp.exp(sh - m).astype(BF16)
                    es_ref[g, rows, hh * hd:(hh + 1) * hd] = jnp.broadcast_to(
                        jnp.exp(sink - m), (rc_rows, hd))

    for g in range(2):
        o2 = jnp.dot(p_ref[g], wv_ref[g], preferred_element_type=F32)
        for pp in range(ppg):
            rows = slice(pp * blk, (pp + 1) * blk)
            cols = slice((g * ppg + pp) * LANES, (g * ppg + pp + 1) * LANES)
            den = o2[rows, LANES:] + es_ref[g, rows, :]
            o_ref[:, cols] = (o2[rows, :LANES] / den * _silu(g_ref[:, cols])).astype(o_ref.dtype)


def _swa(h, pos_f, sinks, bsz, seq, cols):
    col_q, col_k, col_v, col_g = cols
    n_heads = sinks.shape[0]
    bw = n_heads * SWA_HEAD_DIM
    n_pairs = bw // LANES
    blk = SWA_BLOCK
    nblk = seq // blk
    scratch = [pltpu.VMEM((2, 4 * blk, LANES), BF16),
               pltpu.VMEM((2, 4 * blk, 2 * LANES), BF16),
               pltpu.VMEM((2, 4 * blk, LANES), BF16),
               pltpu.VMEM((2, 4 * blk, 4 * blk), F32),
               pltpu.VMEM((2, 4 * blk, 4 * blk), BF16),
               pltpu.VMEM((2, 4 * blk, LANES), F32),
               pltpu.VMEM((blk, 2 * blk), F32)]
    blocks = (2 * _nbytes((blk, bw), F32) + 3 * _nbytes((blk, LANES), F32) + _nbytes((blk, bw), BF16))
    resident = sum(_nbytes(sc.shape, sc.dtype) for sc in scratch)
    return pl.pallas_call(
        functools.partial(_swa_kernel, n_pairs=n_pairs),
        out_shape=jax.ShapeDtypeStruct((bsz * seq, bw), BF16),
        grid=(bsz, nblk),
        in_specs=[pl.BlockSpec(memory_space=pltpu.SMEM),
                  pl.BlockSpec((blk, bw), lambda b, i: (b * nblk + i, col_q // bw)),
                  pl.BlockSpec((blk, LANES), lambda b, i: (b * nblk + i, col_k // LANES)),
                  pl.BlockSpec((blk, LANES), lambda b, i: (b * nblk + i, col_v // LANES)),
                  pl.BlockSpec((blk, bw), lambda b, i: (b * nblk + i, col_g // bw)),
                  pl.BlockSpec((blk, 1), lambda b, i: (b * nblk + i, 0)),
                  pl.BlockSpec((1, LANES), lambda b, i: (0, 0))],
        out_specs=pl.BlockSpec((blk, bw), lambda b, i: (b * nblk + i, 0)),
        scratch_shapes=scratch,
        compiler_params=pltpu.CompilerParams(
            dimension_semantics=("parallel", "arbitrary"),
            vmem_limit_bytes=_vmem_limit(blocks, resident)),
        name="swa_mixer",
    )(sinks, h, h, h, h, pos_f, _rope_lane_table())


def _mem_kernel(q_ref, g_ref, mk_ref, mv_ref, o_ref, *, hd):
    scale = hd ** -0.5
    for h in range(XA_HEADS):
        cols = slice(h * hd, (h + 1) * hd)
        qh = q_ref[:, cols].astype(BF16)
        s = lax.dot_general(qh, mk_ref[:, cols], (((1,), (1,)), ((), ())),
                            preferred_element_type=F32) * scale
        m = jnp.max(s, axis=-1, keepdims=True)
        e = jnp.exp(s - m)
        p = e / jnp.sum(e, axis=-1, keepdims=True)
        o = jnp.dot(p.astype(BF16), mv_ref[:, cols], preferred_element_type=F32)
        o_ref[:, cols] = (o * _silu(g_ref[:, cols])).astype(o_ref.dtype)


def _mem_attn(h, mkv, bsz, seq, mem_len, cols, ts):
    col_q, col_g = cols
    bw = mkv.shape[1] // 2
    hd = bw // XA_HEADS
    nblk = seq // ts
    blocks = 2 * _nbytes((ts, bw), F32) + 2 * _nbytes((mem_len, bw), BF16) + _nbytes((ts, bw), BF16)
    return pl.pallas_call(
        functools.partial(_mem_kernel, hd=hd),
        out_shape=jax.ShapeDtypeStruct((bsz * seq, bw), BF16),
        grid=(bsz, nblk),
        in_specs=[pl.BlockSpec((ts, bw), lambda b, i: (b * nblk + i, col_q // bw)),
                  pl.BlockSpec((ts, bw), lambda b, i: (b * nblk + i, col_g // bw)),
                  pl.BlockSpec((mem_len, bw), lambda b, i: (b, 0)),
                  pl.BlockSpec((mem_len, bw), lambda b, i: (b, 1))],
        out_specs=pl.BlockSpec((ts, bw), lambda b, i: (b * nblk + i, 0)),
        compiler_params=pltpu.CompilerParams(
            dimension_semantics=("parallel", "parallel"), vmem_limit_bytes=_vmem_limit(blocks)),
        name="mem_attn",
    )(h, h, mkv, mkv)


IN_TN = 2 * LANES


def _main_layout(d_model):
    bw = d_model // 4
    dkt, dvt = bw // 2, bw
    kvw = 2 * SWA_HEAD_DIM
    names = ["pool_u", "pool_g", "gq", "gk", "gv", "gg", "glr", "sq", "sk", "sv", "sg", "xq", "xg"]
    widths = [bw, bw, dkt, dkt, dvt, dvt, GLA_RANK, bw, kvw, kvw, bw, bw, bw]
    src = {}
    off = 0
    for nme, w in zip(names, widths):
        src[nme] = (off, w)
        off += w
    n_main = off
    order = ["pool_u", "pool_g", "gv", "gg", "sq", "sg", "xq", "xg", "gq", "gk", "sk", "sv", "glr"]
    dst = {}
    cols = []
    off = 0
    for nme in order:
        o, w = src[nme]
        w_dst = w if nme != "glr" else IN_TN
        assert off % w_dst == 0
        dst[nme] = off
        cols.append(o + np.arange(w_dst))
        off += w_dst
    cols = np.concatenate(cols)
    assert off % IN_TN == 0 and cols.max() < n_main
    starts = cols[::IN_TN]
    assert np.all(cols.reshape(-1, IN_TN) == starts[:, None] + np.arange(IN_TN))
    return starts, dst, n_main


def _hybrid_layer(l, x, xb, mem_b, pos_f, w_in_t, w_pool, pool_scale, w_gla_up, b_gla, gla_norm,
                  sinks, w_mem_kv, w_branch, w_out, ln_g, ln_b, alpha, emit_bf16, bsz, seq):
    m, d = x.shape
    col_starts, dst, n_main = _main_layout(d)
    w_up_pad = jnp.concatenate(
        [w_gla_up[l], jnp.zeros((IN_TN - GLA_RANK, w_gla_up.shape[-1]), w_gla_up.dtype)], axis=0)

    h, w_mg_t = _in_proj(xb, w_in_t, l, col_starts, n_main, N_BRANCH * d, tm=2048, tn=IN_TN)
    mkv = _matmul(mem_b, w_mem_kv, l, BF16, tm=mem_b.shape[0], tn=512, name="mem_kv")

    o_pool = _pool(h, w_pool[l].astype(BF16), pool_scale[l], bsz, seq,
                   dst["pool_u"], dst["pool_g"], ts=512)
    o_gla = _gla(h, w_up_pad, b_gla[l], gla_norm[l], bsz, seq,
                 (dst["gq"], dst["gk"], dst["gv"], dst["gg"], dst["glr"]), rows=512)
    o_swa = _swa(h, pos_f, sinks[l], bsz, seq, (dst["sq"], dst["sk"], dst["sv"], dst["sg"]))
    o_mem = _mem_attn(h, mkv, bsz, seq, mem_b.shape[0] // bsz, (dst["xq"], dst["xg"]), ts=512)

    y = _merge(xb, (o_pool, o_gla, o_swa, o_mem), w_mg_t, w_branch, l, tm=1024, tn=256)
    z = _matmul(y, w_out, l, F32, tm=2048, tn=256, name="out_proj", resid=x, alpha=alpha)
    return _layer_norm(z, ln_g[l], ln_b[l], tm=256, emit_bf16=emit_bf16)


def kernel(x, mem, positions, w_in, w_pool, pool_scale, w_gla_up, b_gla, gla_norm, sinks,
           w_mem_kv, w_branch, w_out, ln_g, ln_b):
    bsz, seq, d = x.shape
    depth = w_in.shape[0]
    alpha = (2 * depth) ** 0.25
    xf = x.reshape(bsz * seq, d)
    xb = xf.astype(BF16)
    mem_b = mem.reshape(bsz * mem.shape[1], d).astype(BF16)
    pos_f = positions.astype(F32).reshape(bsz * seq, 1)
    w_in_t = jnp.swapaxes(w_in, 1, 2)
    for l in range(depth):
        xf, xb = _hybrid_layer(
            l, xf, xb, mem_b, pos_f, w_in_t, w_pool, pool_scale, w_gla_up, b_gla,
            gla_norm, sinks, w_mem_kv, w_branch, w_out, ln_g, ln_b,
            alpha, l + 1 < depth, bsz, seq)
    return xf.reshape(bsz, seq, d)
```

```python
import functools
import math

import jax
import jax.numpy as jnp
import numpy as np
from jax import lax
from jax.experimental import pallas as pl
from jax.experimental.pallas import tpu as pltpu

F32 = jnp.float32
BF16 = jnp.bfloat16

N_BRANCH = 4
POOL_WINDOWS = (2, 4, 8, 16)
GLA_HEADS = 4
GLA_RANK = 16
GLA_TAU = 16.0
GLA_CHUNK = 64
GLA_SUB = 16
SWA_HEAD_DIM = 64
SWA_GROUP = 8
WINDOW = 128
SWA_BLOCK = 128
ROPE_THETA = 500000.0
ROPE_DIM = SWA_HEAD_DIM // 4
XA_HEADS = 4
LN_EPS = 1e-5

LANES = 128
SUBLANES = 8
V7X_VMEM_BYTES = 64 * 1024 * 1024
VMEM_CAP = V7X_VMEM_BYTES - 8 * 1024 * 1024
VMEM_TEMP_BYTES = 12 * 1024 * 1024


def _vmem_limit(streamed_bytes, resident_bytes=0):
    return int(min(VMEM_CAP, 2 * streamed_bytes + resident_bytes + VMEM_TEMP_BYTES))


def _nbytes(shape, dtype):
    return int(np.prod(shape)) * jnp.dtype(dtype).itemsize


def _silu(g):
    return g * jax.nn.sigmoid(g)


def _div_pow2(x, n):
    assert n & (n - 1) == 0
    return x >> (n.bit_length() - 1)


def _mod_pow2(x, n):
    assert n & (n - 1) == 0
    return x & (n - 1)


NT_DIMS = (((1,), (1,)), ((), ()))


def _mm_kernel(a_ref, w_ref, o_ref):
    w = w_ref[...].astype(BF16)
    o_ref[...] = jnp.dot(a_ref[...], w, preferred_element_type=F32).astype(o_ref.dtype)


def _mm_resid_kernel(a_ref, w_ref, x_ref, o_ref, *, alpha):
    w = w_ref[...].astype(BF16)
    o_ref[...] = alpha * x_ref[...] + jnp.dot(a_ref[...], w, preferred_element_type=F32)


def _matmul(a, w, layer, out_dtype, tm, tn, name, resid=None, alpha=None):
    m, k = a.shape
    n = w.shape[-1]
    assert m % tm == 0 and n % tn == 0
    resident = _nbytes((tm, k), a.dtype)
    streamed = _nbytes((k, tn), w.dtype) + _nbytes((tm, tn), out_dtype)
    in_specs = [pl.BlockSpec((tm, k), lambda i, j: (i, 0), pipeline_mode=pl.Buffered(1)),
                pl.BlockSpec((None, k, tn), lambda i, j: (layer, 0, j))]
    args = [a, w]
    if resid is None:
        kern = _mm_kernel
    else:
        kern = functools.partial(_mm_resid_kernel, alpha=alpha)
        in_specs.append(pl.BlockSpec((tm, tn), lambda i, j: (i, j)))
        args.append(resid)
        streamed += _nbytes((tm, tn), resid.dtype)
    return pl.pallas_call(
        kern,
        out_shape=jax.ShapeDtypeStruct((m, n), out_dtype),
        grid=(m // tm, n // tn),
        in_specs=in_specs,
        out_specs=pl.BlockSpec((tm, tn), lambda i, j: (i, j)),
        compiler_params=pltpu.CompilerParams(
            dimension_semantics=("parallel", "arbitrary"),
            vmem_limit_bytes=_vmem_limit(streamed, resident)),
        name=name,
    )(*args)


def _in_proj_kernel(tab_ref, x_ref, w_ref, g_ref, o_ref, gs_ref):
    w = w_ref[0].astype(BF16)
    o_ref[...] = lax.dot_general(x_ref[...], w, NT_DIMS, preferred_element_type=F32)
    gs_ref[...] = g_ref[0].astype(gs_ref.dtype)


def _in_proj(xb, w_in_t, layer, col_starts, gate_row0, gate_rows, tm, tn):
    m, k = xb.shape
    nblk = len(col_starts)
    assert m % tm == 0 and all(int(s) % SUBLANES == 0 for s in col_starts)
    n_steps = (m // tm) * nblk
    sr = LANES
    n_slabs = gate_rows // sr
    assert gate_rows % sr == 0 and n_slabs <= n_steps and gate_row0 % SUBLANES == 0
    resident = _nbytes((tm, k), xb.dtype)
    streamed = (_nbytes((tn, k), w_in_t.dtype) + _nbytes((tm, tn), F32)
                + _nbytes((sr, k), w_in_t.dtype) + _nbytes((sr, k), BF16))

    def slab(i, j):
        return jnp.minimum(i * nblk + j, n_slabs - 1)

    return pl.pallas_call(
        _in_proj_kernel,
        out_shape=[jax.ShapeDtypeStruct((m, nblk * tn), F32),
                   jax.ShapeDtypeStruct((gate_rows, k), BF16)],
        grid_spec=pltpu.PrefetchScalarGridSpec(
            num_scalar_prefetch=1,
            grid=(m // tm, nblk),
            in_specs=[pl.BlockSpec((tm, k), lambda i, j, tab: (i, 0), pipeline_mode=pl.Buffered(1)),
                      pl.BlockSpec((pl.Element(1), pl.Element(tn), pl.Element(k)),
                                   lambda i, j, tab: (layer, pl.multiple_of(tab[j], SUBLANES), 0)),
                      pl.BlockSpec((pl.Element(1), pl.Element(sr), pl.Element(k)),
                                   lambda i, j, tab: (
                                       layer, pl.multiple_of(gate_row0 + slab(i, j) * sr, SUBLANES), 0))],
            out_specs=[pl.BlockSpec((tm, tn), lambda i, j, tab: (i, j)),
                       pl.BlockSpec((sr, k), lambda i, j, tab: (slab(i, j), 0))]),
        compiler_params=pltpu.CompilerParams(
            dimension_semantics=("arbitrary", "arbitrary"),
            vmem_limit_bytes=_vmem_limit(streamed, resident)),
        name="in_proj",
    )(jnp.asarray(np.asarray(col_starts, np.int32)), xb, w_in_t, w_in_t)


def _merge_kernel(x_ref, o0, o1, o2, o3, g0, g1, g2, g3, w0, w1, w2, w3, y_ref):
    x = x_ref[...]
    acc = None
    for o_ref, g_ref, w_ref in ((o0, g0, w0), (o1, g1, w1), (o2, g2, w2), (o3, g3, w3)):
        gate = lax.dot_general(x, g_ref[...], NT_DIMS, preferred_element_type=F32)
        proj = jnp.dot(o_ref[...], w_ref[...].astype(BF16), preferred_element_type=F32)
        term = jax.nn.sigmoid(gate) * proj
        acc = term if acc is None else acc + term
    y_ref[...] = acc.astype(y_ref.dtype)


def _merge(xb, branches, w_mg_t, w_branch, layer, tm, tn):
    m, d = xb.shape
    bw = branches[0].shape[1]
    nj = d // tn
    in_specs = [pl.BlockSpec((tm, d), lambda i, j: (i, 0), pipeline_mode=pl.Buffered(1))]
    in_specs += [pl.BlockSpec((tm, bw), lambda i, j: (i, 0), pipeline_mode=pl.Buffered(1))
                 for _ in range(N_BRANCH)]
    in_specs += [pl.BlockSpec((tn, d), functools.partial(lambda i, j, b: (b * nj + j, 0), b=b))
                 for b in range(N_BRANCH)]
    in_specs += [pl.BlockSpec((None, None, bw, tn),
                              functools.partial(lambda i, j, b: (layer, b, 0, j), b=b))
                 for b in range(N_BRANCH)]
    resident = _nbytes((tm, d), BF16) + N_BRANCH * _nbytes((tm, bw), BF16)
    streamed = (N_BRANCH * (_nbytes((tn, d), BF16) + _nbytes((bw, tn), w_branch.dtype))
                + _nbytes((tm, tn), BF16))
    return pl.pallas_call(
        _merge_kernel,
        out_shape=jax.ShapeDtypeStruct((m, d), BF16),
        grid=(m // tm, nj),
        in_specs=in_specs,
        out_specs=pl.BlockSpec((tm, tn), lambda i, j: (i, j)),
        compiler_params=pltpu.CompilerParams(
            dimension_semantics=("parallel", "arbitrary"),
            vmem_limit_bytes=_vmem_limit(streamed, resident)),
        name="merge",
    )(xb, *branches, w_mg_t, w_mg_t, w_mg_t, w_mg_t, w_branch, w_branch, w_branch, w_branch)


def _ln_kernel(z_ref, g_ref, b_ref, o_ref, ob_ref=None):
    z = z_ref[...]
    mu = jnp.mean(z, axis=-1, keepdims=True)
    zc = z - mu
    var = jnp.mean(zc * zc, axis=-1, keepdims=True)
    y = zc * lax.rsqrt(var + LN_EPS) * g_ref[...] + b_ref[...]
    o_ref[...] = y
    if ob_ref is not None:
        ob_ref[...] = y.astype(BF16)


def _layer_norm(z, g, b, tm, emit_bf16):
    m, d = z.shape
    out_shape = [jax.ShapeDtypeStruct((m, d), F32)]
    out_specs = [pl.BlockSpec((tm, d), lambda i: (i, 0))]
    if emit_bf16:
        out_shape.append(jax.ShapeDtypeStruct((m, d), BF16))
        out_specs.append(pl.BlockSpec((tm, d), lambda i: (i, 0)))
    blocks = 2 * _nbytes((tm, d), F32) + _nbytes((tm, d), BF16)
    res = pl.pallas_call(
        _ln_kernel,
        out_shape=out_shape,
        grid=(m // tm,),
        in_specs=[pl.BlockSpec((tm, d), lambda i: (i, 0)),
                  pl.BlockSpec((1, d), lambda i: (0, 0)),
                  pl.BlockSpec((1, d), lambda i: (0, 0))],
        out_specs=out_specs,
        compiler_params=pltpu.CompilerParams(
            dimension_semantics=("parallel",), vmem_limit_bytes=_vmem_limit(blocks)),
        name="layer_norm",
    )(z, g.reshape(1, d), b.reshape(1, d))
    return res if emit_bf16 else (res[0], None)


POOL_HIST = 16


def _pool_kernel(u_ref, g_ref, wp_ref, sc_ref, o_ref, buf, *, ts, group):
    i = pl.program_id(1)

    @pl.when(i == 0)
    def _():
        buf[0:POOL_HIST, :] = jnp.zeros((POOL_HIST, buf.shape[1]), F32)

    buf[POOL_HIST:POOL_HIST + ts, :] = u_ref[...]
    t = i * ts + lax.broadcasted_iota(jnp.int32, (ts, 1), 0)
    outs = []
    for gi, w in enumerate(POOL_WINDOWS):
        cols = slice(gi * group, (gi + 1) * group)
        cur = buf[POOL_HIST:POOL_HIST + ts, cols]
        acc = cur
        for lag in range(1, w):
            acc = acc + buf[POOL_HIST - lag:POOL_HIST - lag + ts, cols]
        cnt = jnp.minimum(t + 1, w).astype(F32)
        pooled = acc / cnt - cur
        outs.append(jnp.dot(pooled.astype(BF16), wp_ref[gi], preferred_element_type=F32))
    o = jnp.concatenate(outs, axis=1) * sc_ref[...] * _silu(g_ref[...])
    o_ref[...] = o.astype(o_ref.dtype)
    buf[0:POOL_HIST, :] = buf[ts:ts + POOL_HIST, :]


def _pool(h, w_pool_b, pool_scale, bsz, seq, col_u, col_g, ts):
    bw = pool_scale.shape[-1]
    group = bw // len(POOL_WINDOWS)
    nblk = seq // ts
    blocks = 2 * _nbytes((ts, bw), F32) + _nbytes((ts, bw), BF16) + _nbytes((ts + POOL_HIST, bw), F32)
    return pl.pallas_call(
        functools.partial(_pool_kernel, ts=ts, group=group),
        out_shape=jax.ShapeDtypeStruct((bsz * seq, bw), BF16),
        grid=(bsz, nblk),
        in_specs=[pl.BlockSpec((ts, bw), lambda b, i: (b * nblk + i, col_u // bw)),
                  pl.BlockSpec((ts, bw), lambda b, i: (b * nblk + i, col_g // bw)),
                  pl.BlockSpec(w_pool_b.shape, lambda b, i: (0, 0, 0)),
                  pl.BlockSpec((1, bw), lambda b, i: (0, 0))],
        out_specs=pl.BlockSpec((ts, bw), lambda b, i: (b * nblk + i, 0)),
        scratch_shapes=[pltpu.VMEM((ts + POOL_HIST, bw), F32)],
        compiler_params=pltpu.CompilerParams(
            dimension_semantics=("parallel", "arbitrary"), vmem_limit_bytes=_vmem_limit(blocks)),
        name="pool_mixer",
    )(h, h, w_pool_b, pool_scale.reshape(1, bw))


def _split_bf16(x, parts):
    out = []
    r = x
    for _ in range(parts):
        p = r.astype(BF16)
        out.append(p)
        r = r - p.astype(F32)
    return out


LOG2E = math.log2(math.e)


def _gla_cumsum_operator(rows):
    r = np.arange(rows)
    tri = ((r[:, None] // GLA_CHUNK == r[None, :] // GLA_CHUNK) & (r[None, :] <= r[:, None]))
    return jnp.asarray(tri, BF16)


def _gla_kernel(q_ref, k_ref, v_ref, gg_ref, lr_ref, wup_ref, bg_ref, nrm_ref, tri_ref,
                o_ref, state_ref, b_ref, *, rows, dk, dv):
    i = pl.program_id(1)
    C = GLA_CHUNK
    nsub = C // GLA_SUB

    @pl.when(i == 0)
    def _():
        state_ref[...] = jnp.zeros(state_ref.shape, F32)

    lr_h, lr_m = _split_bf16(lr_ref[...], 2)
    w_h, w_m = _split_bf16(wup_ref[...], 2)
    z = (jnp.dot(lr_h, w_h, preferred_element_type=F32)
         + jnp.dot(lr_h, w_m, preferred_element_type=F32)
         + jnp.dot(lr_m, w_h, preferred_element_type=F32)) + bg_ref[...]
    log2_a = (jnp.minimum(z, 0.0) - jnp.log(1.0 + jnp.exp(-jnp.abs(z)))) * (LOG2E / GLA_TAU)

    bsum = None
    for part in _split_bf16(log2_a, 3):
        t = jnp.dot(tri_ref[...], part, preferred_element_type=F32)
        bsum = t if bsum is None else bsum + t
    b_ref[...] = bsum

    row = lax.broadcasted_iota(jnp.int32, (C, 1), 0)
    col = lax.broadcasted_iota(jnp.int32, (1, C), 1)
    row_sub = _div_pow2(row, GLA_SUB)
    col_sub = _div_pow2(col, GLA_SUB)
    lane_c = lax.broadcasted_iota(jnp.int32, (GLA_SUB, C), 1)
    scale = dk ** -0.5

    def sub_heads(x):
        return jnp.concatenate(
            [jnp.broadcast_to(x[a * GLA_SUB:a * GLA_SUB + 1, :], (GLA_SUB, dk))
             for a in range(nsub)], axis=0)

    def chunk_body(c, carry):
        r0 = pl.multiple_of(c * C, C)
        for h in range(GLA_HEADS):
            kc = slice(h * dk, (h + 1) * dk)
            vc = slice(h * dv, (h + 1) * dv)
            qs = q_ref[pl.ds(r0, C), kc] * scale
            k = k_ref[pl.ds(r0, C), kc]
            v = v_ref[pl.ds(r0, C), vc].astype(BF16)
            b = b_ref[pl.ds(r0, C), kc]
            b_last = b[C - 1:C, :]
            st = state_ref[h]

            q_in = (qs * jnp.exp2(b)).astype(BF16)
            o = lax.dot_general(q_in, st.astype(BF16), NT_DIMS, preferred_element_type=F32)

            q_t = qs * jnp.exp2(b - sub_heads(b))
            lhs = jnp.concatenate(
                [jnp.where(row_sub == a, q_t, 0.0) for a in range(nsub)], axis=1).astype(BF16)
            rhs = jnp.concatenate(
                [k * jnp.exp2(jnp.minimum(b[a * GLA_SUB:a * GLA_SUB + 1, :] - b, 0.0))
                 for a in range(nsub)], axis=1).astype(BF16)
            a_off = lax.dot_general(lhs, rhs, NT_DIMS, preferred_element_type=F32)

            diag_blocks = []
            for a in range(nsub):
                rs = slice(a * GLA_SUB, (a + 1) * GLA_SUB)
                qa = qs[rs, :]
                ba = b[rs, :]
                blk = jnp.zeros((GLA_SUB, C), F32)
                for jj in range(GLA_SUB):
                    j = a * GLA_SUB + jj
                    tj = qa * k[j:j + 1, :] * jnp.exp2(ba - b[j:j + 1, :])
                    cj = jnp.sum(tj, axis=1, keepdims=True)
                    blk = jnp.where(lane_c == j, cj, blk)
                diag_blocks.append(blk)
            a_diag = jnp.concatenate(diag_blocks, axis=0)
            attn = jnp.where(row_sub > col_sub, a_off,
                             jnp.where((row_sub == col_sub) & (row >= col), a_diag, 0.0))
            o = o + jnp.dot(attn.astype(BF16), v, preferred_element_type=F32)

            k_dec = (k * jnp.exp2(b_last - b)).astype(BF16)
            upd = lax.dot_general(v, k_dec, (((0,), (0,)), ((), ())),
                                  preferred_element_type=F32)
            state_ref[h] = st * jnp.exp2(b_last) + upd

            ms = jnp.mean(o * o, axis=-1, keepdims=True)
            on = o * lax.rsqrt(ms + LN_EPS) * nrm_ref[:, vc]
            o_ref[pl.ds(r0, C), vc] = (on * _silu(gg_ref[pl.ds(r0, C), vc])).astype(o_ref.dtype)
        return carry

    lax.fori_loop(0, rows // C, chunk_body, 0)


def _gla(h, w_up_pad, b_gla, gla_norm, bsz, seq, cols, rows):
    col_q, col_k, col_v, col_g, col_lr = cols
    dkt = b_gla.shape[-1]
    dvt = gla_norm.shape[-1]
    dk, dv = dkt // GLA_HEADS, dvt // GLA_HEADS
    lrw = w_up_pad.shape[0]
    nblk = seq // rows
    tri = _gla_cumsum_operator(rows)
    blocks = (2 * _nbytes((rows, dkt), F32) + 2 * _nbytes((rows, dvt), F32)
              + _nbytes((rows, lrw), F32) + _nbytes((rows, dvt), BF16))
    resident = (_nbytes((rows, dkt), F32) + _nbytes((GLA_HEADS, dv, dk), F32)
                + _nbytes(tri.shape, BF16))
    const = lambda b, i: (0, 0)
    return pl.pallas_call(
        functools.partial(_gla_kernel, rows=rows, dk=dk, dv=dv),
        out_shape=jax.ShapeDtypeStruct((bsz * seq, dvt), BF16),
        grid=(bsz, nblk),
        in_specs=[pl.BlockSpec((rows, dkt), lambda b, i: (b * nblk + i, col_q // dkt)),
                  pl.BlockSpec((rows, dkt), lambda b, i: (b * nblk + i, col_k // dkt)),
                  pl.BlockSpec((rows, dvt), lambda b, i: (b * nblk + i, col_v // dvt)),
                  pl.BlockSpec((rows, dvt), lambda b, i: (b * nblk + i, col_g // dvt)),
                  pl.BlockSpec((rows, lrw), lambda b, i: (b * nblk + i, col_lr // lrw)),
                  pl.BlockSpec((lrw, dkt), const),
                  pl.BlockSpec((1, dkt), const),
                  pl.BlockSpec((1, dvt), const),
                  pl.BlockSpec(tri.shape, const)],
        out_specs=pl.BlockSpec((rows, dvt), lambda b, i: (b * nblk + i, 0)),
        scratch_shapes=[pltpu.VMEM((GLA_HEADS, dv, dk), F32),
                        pltpu.VMEM((rows, dkt), F32)],
        compiler_params=pltpu.CompilerParams(
            dimension_semantics=("parallel", "arbitrary"),
            vmem_limit_bytes=_vmem_limit(blocks, resident)),
        name="gla_mixer",
    )(h, h, h, h, h, w_up_pad, b_gla.reshape(1, dkt), gla_norm.reshape(1, dvt), tri)


def _rope_lane_table():
    half = ROPE_DIM // 2
    inv_freq = ROPE_THETA ** (-np.arange(0, ROPE_DIM, 2, dtype=np.float32) / ROPE_DIM)
    lane = np.arange(LANES) % SWA_HEAD_DIM
    tab = np.where(lane < ROPE_DIM, inv_freq[lane % half], 0.0).astype(np.float32)
    return jnp.asarray(tab.reshape(1, LANES))


SWA_ROW_CHUNK = 32


def _swa_kernel(sink_ref, q_ref, k_ref, v_ref, g_ref, pos_ref, inv_ref, o_ref,
                wk_ref, wv_ref, qr_ref, s_ref, p_ref, es_ref, bias_ref, *, n_pairs):
    n = pl.program_id(1)
    blk = SWA_BLOCK
    half = ROPE_DIM // 2
    hd = SWA_HEAD_DIM
    ppg = n_pairs // 2
    rc_rows = SWA_ROW_CHUNK

    @pl.when(n == 0)
    def _():
        wk_ref[...] = jnp.zeros(wk_ref.shape, BF16)
        row = lax.broadcasted_iota(jnp.int32, (4 * blk, 2 * LANES), 0)
        col = lax.broadcasted_iota(jnp.int32, (4 * blk, 2 * LANES), 1)
        ones = ((col >= LANES) & ((col >= LANES + hd) == (row >= 2 * blk))).astype(BF16)
        for g in range(2):
            wv_ref[g] = ones

    lane = lax.broadcasted_iota(jnp.int32, (1, LANES), 1)
    lane_h = _mod_pow2(lane, hd)
    lo = lane < hd
    ang = pos_ref[...] * inv_ref[...]
    cos = jnp.cos(ang)
    sin = jnp.sin(ang)

    def rope(x):
        x_up = pltpu.roll(x, LANES - half, axis=1)
        x_dn = pltpu.roll(x, half, axis=1)
        return x * cos + jnp.where(lane_h < half, -x_up, x_dn) * sin

    for g in range(2):
        for part in range(2):
            base = part * 2 * blk
            wk_ref[g, base:base + blk, :] = wk_ref[g, base + blk:base + 2 * blk, :]
            wv_ref[g, base:base + blk, 0:LANES] = wv_ref[g, base + blk:base + 2 * blk, 0:LANES]
    k_r = rope(k_ref[...])
    v_c = v_ref[...]
    k_sw = pltpu.roll(k_r, hd, axis=1)
    v_sw = pltpu.roll(v_c, hd, axis=1)
    zero = jnp.zeros_like(k_r)
    for g in range(2):
        k_lo, k_hi = (k_r, k_sw) if g == 0 else (k_sw, k_r)
        v_lo, v_hi = (v_c, v_sw) if g == 0 else (v_sw, v_c)
        wk_ref[g, blk:2 * blk, :] = jnp.where(lo, k_lo, zero).astype(BF16)
        wk_ref[g, 3 * blk:4 * blk, :] = jnp.where(lo, zero, k_hi).astype(BF16)
        wv_ref[g, blk:2 * blk, 0:LANES] = jnp.where(lo, v_lo, zero).astype(BF16)
        wv_ref[g, 3 * blk:4 * blk, 0:LANES] = jnp.where(lo, zero, v_hi).astype(BF16)

    r = lax.broadcasted_iota(jnp.int32, (blk, 2 * blk), 0)
    c = lax.broadcasted_iota(jnp.int32, (blk, 2 * blk), 1)
    valid = (c > r) & (c <= r + WINDOW) & ((n > 0) | (c >= blk))
    bias_ref[...] = jnp.where(valid, 0.0, -jnp.inf).astype(F32)

    qscale = hd ** -0.5
    for p in range(n_pairs):
        g, pp = divmod(p, ppg)
        qr_ref[g, pp * blk:(pp + 1) * blk, :] = (
            rope(q_ref[:, p * LANES:(p + 1) * LANES]) * qscale).astype(BF16)

    for g in range(2):
        s_ref[g] = lax.dot_general(qr_ref[g], wk_ref[g], NT_DIMS, preferred_element_type=F32)

    for g in range(2):
        for pp in range(ppg):
            for hh in range(2):
                sink = sink_ref[2 * (g * ppg + pp) + hh]
                kc = slice(hh * 2 * blk, (hh + 1) * 2 * blk)
                for rc in range(blk // rc_rows):
                    rows = slice(pp * blk + rc * rc_rows, pp * blk + (rc + 1) * rc_rows)
                    sh = s_ref[g, rows, kc] + bias_ref[rc * rc_rows:(rc + 1) * rc_rows, :]
                    m = jnp.maximum(jnp.max(sh, axis=-1, keepdims=True), sink)
                    p_ref[g, rows, kc] = jnp.exp(sh - m).astype(BF16)
                    es_ref[g, rows, hh * hd:(hh + 1) * hd] = jnp.broadcast_to(
                        jnp.exp(sink - m), (rc_rows, hd))

    for g in range(2):
        o2 = jnp.dot(p_ref[g], wv_ref[g], preferred_element_type=F32)
        for pp in range(ppg):
            rows = slice(pp * blk, (pp + 1) * blk)
            cols = slice((g * ppg + pp) * LANES, (g * ppg + pp + 1) * LANES)
            den = o2[rows, LANES:] + es_ref[g, rows, :]
            o_ref[:, cols] = (o2[rows, :LANES] / den * _silu(g_ref[:, cols])).astype(o_ref.dtype)


def _swa(h, pos_f, sinks, bsz, seq, cols):
    col_q, col_k, col_v, col_g = cols
    n_heads = sinks.shape[0]
    bw = n_heads * SWA_HEAD_DIM
    n_pairs = bw // LANES
    blk = SWA_BLOCK
    nblk = seq // blk
    scratch = [pltpu.VMEM((2, 4 * blk, LANES), BF16),
               pltpu.VMEM((2, 4 * blk, 2 * LANES), BF16),
               pltpu.VMEM((2, 4 * blk, LANES), BF16),
               pltpu.VMEM((2, 4 * blk, 4 * blk), F32),
               pltpu.VMEM((2, 4 * blk, 4 * blk), BF16),
               pltpu.VMEM((2, 4 * blk, LANES), F32),
               pltpu.VMEM((blk, 2 * blk), F32)]
    blocks = (2 * _nbytes((blk, bw), F32) + 3 * _nbytes((blk, LANES), F32) + _nbytes((blk, bw), BF16))
    resident = sum(_nbytes(sc.shape, sc.dtype) for sc in scratch)
    return pl.pallas_call(
        functools.partial(_swa_kernel, n_pairs=n_pairs),
        out_shape=jax.ShapeDtypeStruct((bsz * seq, bw), BF16),
        grid=(bsz, nblk),
        in_specs=[pl.BlockSpec(memory_space=pltpu.SMEM),
                  pl.BlockSpec((blk, bw), lambda b, i: (b * nblk + i, col_q // bw)),
                  pl.BlockSpec((blk, LANES), lambda b, i: (b * nblk + i, col_k // LANES)),
                  pl.BlockSpec((blk, LANES), lambda b, i: (b * nblk + i, col_v // LANES)),
                  pl.BlockSpec((blk, bw), lambda b, i: (b * nblk + i, col_g // bw)),
                  pl.BlockSpec((blk, 1), lambda b, i: (b * nblk + i, 0)),
                  pl.BlockSpec((1, LANES), lambda b, i: (0, 0))],
        out_specs=pl.BlockSpec((blk, bw), lambda b, i: (b * nblk + i, 0)),
        scratch_shapes=scratch,
        compiler_params=pltpu.CompilerParams(
            dimension_semantics=("parallel", "arbitrary"),
            vmem_limit_bytes=_vmem_limit(blocks, resident)),
        name="swa_mixer",
    )(sinks, h, h, h, h, pos_f, _rope_lane_table())


def _mem_kernel(q_ref, g_ref, mk_ref, mv_ref, o_ref, *, hd):
    scale = hd ** -0.5
    for h in range(XA_HEADS):
        cols = slice(h * hd, (h + 1) * hd)
        qh = q_ref[:, cols].astype(BF16)
        s = lax.dot_general(qh, mk_ref[:, cols], (((1,), (1,)), ((), ())),
                            preferred_element_type=F32) * scale
        m = jnp.max(s, axis=-1, keepdims=True)
        e = jnp.exp(s - m)
        p = e / jnp.sum(e, axis=-1, keepdims=True)
        o = jnp.dot(p.astype(BF16), mv_ref[:, cols], preferred_element_type=F32)
        o_ref[:, cols] = (o * _silu(g_ref[:, cols])).astype(o_ref.dtype)


def _mem_attn(h, mkv, bsz, seq, mem_len, cols, ts):
    col_q, col_g = cols
    bw = mkv.shape[1] // 2
    hd = bw // XA_HEADS
    nblk = seq // ts
    blocks = 2 * _nbytes((ts, bw), F32) + 2 * _nbytes((mem_len, bw), BF16) + _nbytes((ts, bw), BF16)
    return pl.pallas_call(
        functools.partial(_mem_kernel, hd=hd),
        out_shape=jax.ShapeDtypeStruct((bsz * seq, bw), BF16),
        grid=(bsz, nblk),
        in_specs=[pl.BlockSpec((ts, bw), lambda b, i: (b * nblk + i, col_q // bw)),
                  pl.BlockSpec((ts, bw), lambda b, i: (b * nblk + i, col_g // bw)),
                  pl.BlockSpec((mem_len, bw), lambda b, i: (b, 0)),
                  pl.BlockSpec((mem_len, bw), lambda b, i: (b, 1))],
        out_specs=pl.BlockSpec((ts, bw), lambda b, i: (b * nblk + i, 0)),
        compiler_params=pltpu.CompilerParams(
            dimension_semantics=("parallel", "parallel"), vmem_limit_bytes=_vmem_limit(blocks)),
        name="mem_attn",
    )(h, h, mkv, mkv)


IN_TN = 2 * LANES


def _main_layout(d_model):
    bw = d_model // 4
    dkt, dvt = bw // 2, bw
    kvw = 2 * SWA_HEAD_DIM
    names = ["pool_u", "pool_g", "gq", "gk", "gv", "gg", "glr", "sq", "sk", "sv", "sg", "xq", "xg"]
    widths = [bw, bw, dkt, dkt, dvt, dvt, GLA_RANK, bw, kvw, kvw, bw, bw, bw]
    src = {}
    off = 0
    for nme, w in zip(names, widths):
        src[nme] = (off, w)
        off += w
    n_main = off
    order = ["pool_u", "pool_g", "gv", "gg", "sq", "sg", "xq", "xg", "gq", "gk", "sk", "sv", "glr"]
    dst = {}
    cols = []
    off = 0
    for nme in order:
        o, w = src[nme]
        w_dst = w if nme != "glr" else IN_TN
        assert off % w_dst == 0
        dst[nme] = off
        cols.append(o + np.arange(w_dst))
        off += w_dst
    cols = np.concatenate(cols)
    assert off % IN_TN == 0 and cols.max() < n_main
    starts = cols[::IN_TN]
    assert np.all(cols.reshape(-1, IN_TN) == starts[:, None] + np.arange(IN_TN))
    return starts, dst, n_main


def _hybrid_layer(l, x, xb, mem_b, pos_f, w_in_t, w_pool, pool_scale, w_gla_up, b_gla, gla_norm,
                  sinks, w_mem_kv, w_branch, w_out, ln_g, ln_b, alpha, emit_bf16, bsz, seq):
    m, d = x.shape
    col_starts, dst, n_main = _main_layout(d)
    w_up_pad = jnp.concatenate(
        [w_gla_up[l], jnp.zeros((IN_TN - GLA_RANK, w_gla_up.shape[-1]), w_gla_up.dtype)], axis=0)

    h, w_mg_t = _in_proj(xb, w_in_t, l, col_starts, n_main, N_BRANCH * d, tm=2048, tn=IN_TN)
    mkv = _matmul(mem_b, w_mem_kv, l, BF16, tm=mem_b.shape[0], tn=512, name="mem_kv")

    o_pool = _pool(h, w_pool[l].astype(BF16), pool_scale[l], bsz, seq,
                   dst["pool_u"], dst["pool_g"], ts=512)
    o_gla = _gla(h, w_up_pad, b_gla[l], gla_norm[l], bsz, seq,
                 (dst["gq"], dst["gk"], dst["gv"], dst["gg"], dst["glr"]), rows=512)
    o_swa = _swa(h, pos_f, sinks[l], bsz, seq, (dst["sq"], dst["sk"], dst["sv"], dst["sg"]))
    o_mem = _mem_attn(h, mkv, bsz, seq, mem_b.shape[0] // bsz, (dst["xq"], dst["xg"]), ts=512)

    y = _merge(xb, (o_pool, o_gla, o_swa, o_mem), w_mg_t, w_branch, l, tm=1024, tn=256)
    z = _matmul(y, w_out, l, F32, tm=2048, tn=256, name="out_proj", resid=x, alpha=alpha)
    return _layer_norm(z, ln_g[l], ln_b[l], tm=256, emit_bf16=emit_bf16)


def kernel(x, mem, positions, w_in, w_pool, pool_scale, w_gla_up, b_gla, gla_norm, sinks,
           w_mem_kv, w_branch, w_out, ln_g, ln_b):
    bsz, seq, d = x.shape
    depth = w_in.shape[0]
    alpha = (2 * depth) ** 0.25
    xf = x.reshape(bsz * seq, d)
    xb = xf.astype(BF16)
    mem_b = mem.reshape(bsz * mem.shape[1], d).astype(BF16)
    pos_f = positions.astype(F32).reshape(bsz * seq, 1)
    w_in_t = jnp.swapaxes(w_in, 1, 2)
    for l in range(depth):
        xf, xb = _hybrid_layer(
            l, xf, xb, mem_b, pos_f, w_in_t, w_pool, pool_scale, w_gla_up, b_gla,
            gla_norm, sinks, w_mem_kv, w_branch, w_out, ln_g, ln_b,
            alpha, l + 1 < depth, bsz, seq)
    return xf.reshape(bsz, seq, d)
```

```python
import functools
import math

import jax
import jax.numpy as jnp
import numpy as np
from jax import lax
from jax.experimental import pallas as pl
from jax.experimental.pallas import tpu as pltpu

F32 = jnp.float32
BF16 = jnp.bfloat16

N_BRANCH = 4
POOL_WINDOWS = (2, 4, 8, 16)
GLA_HEADS = 4
GLA_RANK = 16
GLA_TAU = 16.0
GLA_CHUNK = 64
GLA_SUB = 16
SWA_HEAD_DIM = 64
SWA_GROUP = 8
WINDOW = 128
SWA_BLOCK = 128
ROPE_THETA = 500000.0
ROPE_DIM = SWA_HEAD_DIM // 4
XA_HEADS = 4
LN_EPS = 1e-5

LANES = 128
SUBLANES = 8
V7X_VMEM_BYTES = 64 * 1024 * 1024
VMEM_CAP = V7X_VMEM_BYTES - 8 * 1024 * 1024
VMEM_TEMP_BYTES = 12 * 1024 * 1024


def _vmem_limit(streamed_bytes, resident_bytes=0):
    return int(min(VMEM_CAP, 2 * streamed_bytes + resident_bytes + VMEM_TEMP_BYTES))


def _nbytes(shape, dtype):
    return int(np.prod(shape)) * jnp.dtype(dtype).itemsize


def _silu(g):
    return g * jax.nn.sigmoid(g)


def _div_pow2(x, n):
    assert n & (n - 1) == 0
    return x >> (n.bit_length() - 1)


def _mod_pow2(x, n):
    assert n & (n - 1) == 0
    return x & (n - 1)


NT_DIMS = (((1,), (1,)), ((), ()))


def _mm_kernel(a_ref, w_ref, o_ref):
    w = w_ref[...].astype(BF16)
    o_ref[...] = jnp.dot(a_ref[...], w, preferred_element_type=F32).astype(o_ref.dtype)


def _mm_resid_kernel(a_ref, w_ref, x_ref, o_ref, *, alpha):
    w = w_ref[...].astype(BF16)
    o_ref[...] = alpha * x_ref[...] + jnp.dot(a_ref[...], w, preferred_element_type=F32)


def _matmul(a, w, layer, out_dtype, tm, tn, name, resid=None, alpha=None):
    m, k = a.shape
    n = w.shape[-1]
    assert m % tm == 0 and n % tn == 0
    resident = _nbytes((tm, k), a.dtype)
    streamed = _nbytes((k, tn), w.dtype) + _nbytes((tm, tn), out_dtype)
    in_specs = [pl.BlockSpec((tm, k), lambda i, j: (i, 0), pipeline_mode=pl.Buffered(1)),
                pl.BlockSpec((None, k, tn), lambda i, j: (layer, 0, j))]
    args = [a, w]
    if resid is None:
        kern = _mm_kernel
    else:
        kern = functools.partial(_mm_resid_kernel, alpha=alpha)
        in_specs.append(pl.BlockSpec((tm, tn), lambda i, j: (i, j)))
        args.append(resid)
        streamed += _nbytes((tm, tn), resid.dtype)
    return pl.pallas_call(
        kern,
        out_shape=jax.ShapeDtypeStruct((m, n), out_dtype),
        grid=(m // tm, n // tn),
        in_specs=in_specs,
        out_specs=pl.BlockSpec((tm, tn), lambda i, j: (i, j)),
        compiler_params=pltpu.CompilerParams(
            dimension_semantics=("parallel", "arbitrary"),
            vmem_limit_bytes=_vmem_limit(streamed, resident)),
        name=name,
    )(*args)


def _in_proj_kernel(tab_ref, x_ref, w_ref, g_ref, o_ref, gs_ref):
    w = w_ref[0].astype(BF16)
    o_ref[...] = lax.dot_general(x_ref[...], w, NT_DIMS, preferred_element_type=F32)
    gs_ref[...] = g_ref[0].astype(gs_ref.dtype)


def _in_proj(xb, w_in_t, layer, col_starts, gate_row0, gate_tn, tm, tn):
    m, k = xb.shape
    nblk = len(col_starts)
    assert m % tm == 0 and all(int(s) % SUBLANES == 0 for s in col_starts)
    n_steps = (m // tm) * nblk
    sr = LANES
    gate_rows = N_BRANCH * k
    n_slabs = gate_rows // sr
    per_blk, per_branch = gate_tn // sr, k // sr
    assert gate_tn % sr == 0 and k % gate_tn == 0 and n_slabs <= n_steps and gate_row0 % SUBLANES == 0
    resident = _nbytes((tm, k), xb.dtype)
    streamed = (_nbytes((tn, k), w_in_t.dtype) + _nbytes((tm, tn), F32)
                + _nbytes((sr, k), w_in_t.dtype) + _nbytes((sr, k), BF16))

    def slab(i, j):
        return jnp.minimum(i * nblk + j, n_slabs - 1)

    def slab_dst(s):
        b, within = s // per_branch, s % per_branch
        return (within // per_blk) * (N_BRANCH * per_blk) + b * per_blk + within % per_blk

    return pl.pallas_call(
        _in_proj_kernel,
        out_shape=[jax.ShapeDtypeStruct((m, nblk * tn), F32),
                   jax.ShapeDtypeStruct((gate_rows, k), BF16)],
        grid_spec=pltpu.PrefetchScalarGridSpec(
            num_scalar_prefetch=1,
            grid=(m // tm, nblk),
            in_specs=[pl.BlockSpec((tm, k), lambda i, j, tab: (i, 0), pipeline_mode=pl.Buffered(1)),
                      pl.BlockSpec((pl.Element(1), pl.Element(tn), pl.Element(k)),
                                   lambda i, j, tab: (layer, pl.multiple_of(tab[j], SUBLANES), 0)),
                      pl.BlockSpec((pl.Element(1), pl.Element(sr), pl.Element(k)),
                                   lambda i, j, tab: (
                                       layer, pl.multiple_of(gate_row0 + slab(i, j) * sr, SUBLANES), 0))],
            out_specs=[pl.BlockSpec((tm, tn), lambda i, j, tab: (i, j)),
                       pl.BlockSpec((sr, k), lambda i, j, tab: (slab_dst(slab(i, j)), 0))]),
        compiler_params=pltpu.CompilerParams(
            dimension_semantics=("arbitrary", "arbitrary"),
            vmem_limit_bytes=_vmem_limit(streamed, resident)),
        name="in_proj",
    )(jnp.asarray(np.asarray(col_starts, np.int32)), xb, w_in_t, w_in_t)


def _merge_kernel(x_ref, o0, o1, o2, o3, g_ref, w_ref, y_ref):
    x = x_ref[...]
    tn = y_ref.shape[1]
    acc = None
    for b, o_ref in enumerate((o0, o1, o2, o3)):
        gate = lax.dot_general(x, g_ref[b * tn:(b + 1) * tn, :], NT_DIMS, preferred_element_type=F32)
        proj = jnp.dot(o_ref[...], w_ref[b].astype(BF16), preferred_element_type=F32)
        term = jax.nn.sigmoid(gate) * proj
        acc = term if acc is None else acc + term
    y_ref[...] = acc.astype(y_ref.dtype)


def _merge(xb, branches, w_mg_t, w_branch, layer, tm, tn):
    m, d = xb.shape
    bw = branches[0].shape[1]
    nj = d // tn
    in_specs = [pl.BlockSpec((tm, d), lambda i, j: (i, 0), pipeline_mode=pl.Buffered(1))]
    in_specs += [pl.BlockSpec((tm, bw), lambda i, j: (i, 0), pipeline_mode=pl.Buffered(1))
                 for _ in range(N_BRANCH)]
    in_specs += [pl.BlockSpec((N_BRANCH * tn, d), lambda i, j: (j, 0)),
                 pl.BlockSpec((None, N_BRANCH, bw, tn), lambda i, j: (layer, 0, 0, j))]
    resident = _nbytes((tm, d), BF16) + N_BRANCH * _nbytes((tm, bw), BF16)
    streamed = (N_BRANCH * (_nbytes((tn, d), BF16) + _nbytes((bw, tn), w_branch.dtype))
                + _nbytes((tm, tn), BF16))
    return pl.pallas_call(
        _merge_kernel,
        out_shape=jax.ShapeDtypeStruct((m, d), BF16),
        grid=(m // tm, nj),
        in_specs=in_specs,
        out_specs=pl.BlockSpec((tm, tn), lambda i, j: (i, j)),
        compiler_params=pltpu.CompilerParams(
            dimension_semantics=("parallel", "arbitrary"),
            vmem_limit_bytes=_vmem_limit(streamed, resident)),
        name="merge",
    )(xb, *branches, w_mg_t, w_branch)


def _ln_kernel(z_ref, g_ref, b_ref, o_ref, ob_ref=None):
    z = z_ref[...]
    mu = jnp.mean(z, axis=-1, keepdims=True)
    zc = z - mu
    var = jnp.mean(zc * zc, axis=-1, keepdims=True)
    y = zc * lax.rsqrt(var + LN_EPS) * g_ref[...] + b_ref[...]
    o_ref[...] = y
    if ob_ref is not None:
        ob_ref[...] = y.astype(BF16)


def _layer_norm(z, g, b, tm, emit_bf16):
    m, d = z.shape
    out_shape = [jax.ShapeDtypeStruct((m, d), F32)]
    out_specs = [pl.BlockSpec((tm, d), lambda i: (i, 0))]
    if emit_bf16:
        out_shape.append(jax.ShapeDtypeStruct((m, d), BF16))
        out_specs.append(pl.BlockSpec((tm, d), lambda i: (i, 0)))
    blocks = 2 * _nbytes((tm, d), F32) + _nbytes((tm, d), BF16)
    res = pl.pallas_call(
        _ln_kernel,
        out_shape=out_shape,
        grid=(m // tm,),
        in_specs=[pl.BlockSpec((tm, d), lambda i: (i, 0)),
                  pl.BlockSpec((1, d), lambda i: (0, 0)),
                  pl.BlockSpec((1, d), lambda i: (0, 0))],
        out_specs=out_specs,
        compiler_params=pltpu.CompilerParams(
            dimension_semantics=("parallel",), vmem_limit_bytes=_vmem_limit(blocks)),
        name="layer_norm",
    )(z, g.reshape(1, d), b.reshape(1, d))
    return res if emit_bf16 else (res[0], None)


POOL_HIST = 16


def _pool_kernel(u_ref, g_ref, wp_ref, sc_ref, o_ref, buf, *, ts, group):
    i = pl.program_id(1)

    @pl.when(i == 0)
    def _():
        buf[0:POOL_HIST, :] = jnp.zeros((POOL_HIST, buf.shape[1]), F32)

    buf[POOL_HIST:POOL_HIST + ts, :] = u_ref[...]
    t = i * ts + lax.broadcasted_iota(jnp.int32, (ts, 1), 0)
    outs = []
    for gi, w in enumerate(POOL_WINDOWS):
        cols = slice(gi * group, (gi + 1) * group)
        cur = buf[POOL_HIST:POOL_HIST + ts, cols]
        acc = cur
        for lag in range(1, w):
            acc = acc + buf[POOL_HIST - lag:POOL_HIST - lag + ts, cols]
        cnt = jnp.minimum(t + 1, w).astype(F32)
        pooled = acc / cnt - cur
        outs.append(jnp.dot(pooled.astype(BF16), wp_ref[gi], preferred_element_type=F32))
    o = jnp.concatenate(outs, axis=1) * sc_ref[...] * _silu(g_ref[...])
    o_ref[...] = o.astype(o_ref.dtype)
    buf[0:POOL_HIST, :] = buf[ts:ts + POOL_HIST, :]


def _pool(h, w_pool_b, pool_scale, bsz, seq, col_u, col_g, ts):
    bw = pool_scale.shape[-1]
    group = bw // len(POOL_WINDOWS)
    nblk = seq // ts
    blocks = 2 * _nbytes((ts, bw), F32) + _nbytes((ts, bw), BF16) + _nbytes((ts + POOL_HIST, bw), F32)
    return pl.pallas_call(
        functools.partial(_pool_kernel, ts=ts, group=group),
        out_shape=jax.ShapeDtypeStruct((bsz * seq, bw), BF16),
        grid=(bsz, nblk),
        in_specs=[pl.BlockSpec((ts, bw), lambda b, i: (b * nblk + i, col_u // bw)),
                  pl.BlockSpec((ts, bw), lambda b, i: (b * nblk + i, col_g // bw)),
                  pl.BlockSpec(w_pool_b.shape, lambda b, i: (0, 0, 0)),
                  pl.BlockSpec((1, bw), lambda b, i: (0, 0))],
        out_specs=pl.BlockSpec((ts, bw), lambda b, i: (b * nblk + i, 0)),
        scratch_shapes=[pltpu.VMEM((ts + POOL_HIST, bw), F32)],
        compiler_params=pltpu.CompilerParams(
            dimension_semantics=("parallel", "arbitrary"), vmem_limit_bytes=_vmem_limit(blocks)),
        name="pool_mixer",
    )(h, h, w_pool_b, pool_scale.reshape(1, bw))


def _split_bf16(x, parts):
    out = []
    r = x
    for _ in range(parts):
        p = r.astype(BF16)
        out.append(p)
        r = r - p.astype(F32)
    return out


LOG2E = math.log2(math.e)


def _gla_cumsum_operator(rows):
    r = np.arange(rows)
    tri = ((r[:, None] // GLA_CHUNK == r[None, :] // GLA_CHUNK) & (r[None, :] <= r[:, None]))
    return jnp.asarray(tri, BF16)


def _gla_kernel(q_ref, k_ref, v_ref, gg_ref, lr_ref, wup_ref, bg_ref, nrm_ref, tri_ref,
                o_ref, state_ref, b_ref, *, rows, dk, dv):
    i = pl.program_id(1)
    C = GLA_CHUNK
    nsub = C // GLA_SUB

    @pl.when(i == 0)
    def _():
        state_ref[...] = jnp.zeros(state_ref.shape, F32)

    lr_h, lr_m = _split_bf16(lr_ref[...], 2)
    w_h, w_m = _split_bf16(wup_ref[...], 2)
    z = (jnp.dot(lr_h, w_h, preferred_element_type=F32)
         + jnp.dot(lr_h, w_m, preferred_element_type=F32)
         + jnp.dot(lr_m, w_h, preferred_element_type=F32)) + bg_ref[...]
    log2_a = (jnp.minimum(z, 0.0) - jnp.log(1.0 + jnp.exp(-jnp.abs(z)))) * (LOG2E / GLA_TAU)

    bsum = None
    for part in _split_bf16(log2_a, 3):
        t = jnp.dot(tri_ref[...], part, preferred_element_type=F32)
        bsum = t if bsum is None else bsum + t
    b_ref[...] = bsum

    row = lax.broadcasted_iota(jnp.int32, (C, 1), 0)
    col = lax.broadcasted_iota(jnp.int32, (1, C), 1)
    row_sub = _div_pow2(row, GLA_SUB)
    col_sub = _div_pow2(col, GLA_SUB)
    lane_c = lax.broadcasted_iota(jnp.int32, (GLA_SUB, C), 1)
    scale = dk ** -0.5

    def sub_heads(x):
        return jnp.concatenate(
            [jnp.broadcast_to(x[a * GLA_SUB:a * GLA_SUB + 1, :], (GLA_SUB, dk))
             for a in range(nsub)], axis=0)

    def chunk_body(c, carry):
        r0 = pl.multiple_of(c * C, C)
        for h in range(GLA_HEADS):
            kc = slice(h * dk, (h + 1) * dk)
            vc = slice(h * dv, (h + 1) * dv)
            qs = q_ref[pl.ds(r0, C), kc] * scale
            k = k_ref[pl.ds(r0, C), kc]
            v = v_ref[pl.ds(r0, C), vc].astype(BF16)
            b = b_ref[pl.ds(r0, C), kc]
            b_last = b[C - 1:C, :]
            st = state_ref[h]

            q_in = (qs * jnp.exp2(b)).astype(BF16)
            o = lax.dot_general(q_in, st.astype(BF16), NT_DIMS, preferred_element_type=F32)

            q_t = qs * jnp.exp2(b - sub_heads(b))
            lhs = jnp.concatenate(
                [jnp.where(row_sub == a, q_t, 0.0) for a in range(nsub)], axis=1).astype(BF16)
            rhs = jnp.concatenate(
                [k * jnp.exp2(jnp.minimum(b[a * GLA_SUB:a * GLA_SUB + 1, :] - b, 0.0))
                 for a in range(nsub)], axis=1).astype(BF16)
            a_off = lax.dot_general(lhs, rhs, NT_DIMS, preferred_element_type=F32)

            diag_blocks = []
            for a in range(nsub):
                rs = slice(a * GLA_SUB, (a + 1) * GLA_SUB)
                qa = qs[rs, :]
                ba = b[rs, :]
                blk = jnp.zeros((GLA_SUB, C), F32)
                for jj in range(GLA_SUB):
                    j = a * GLA_SUB + jj
                    tj = qa * k[j:j + 1, :] * jnp.exp2(ba - b[j:j + 1, :])
                    cj = jnp.sum(tj, axis=1, keepdims=True)
                    blk = jnp.where(lane_c == j, cj, blk)
                diag_blocks.append(blk)
            a_diag = jnp.concatenate(diag_blocks, axis=0)
            attn = jnp.where(row_sub > col_sub, a_off,
                             jnp.where((row_sub == col_sub) & (row >= col), a_diag, 0.0))
            o = o + jnp.dot(attn.astype(BF16), v, preferred_element_type=F32)

            k_dec = (k * jnp.exp2(b_last - b)).astype(BF16)
            upd = lax.dot_general(v, k_dec, (((0,), (0,)), ((), ())),
                                  preferred_element_type=F32)
            state_ref[h] = st * jnp.exp2(b_last) + upd

            ms = jnp.mean(o * o, axis=-1, keepdims=True)
            on = o * lax.rsqrt(ms + LN_EPS) * nrm_ref[:, vc]
            o_ref[pl.ds(r0, C), vc] = (on * _silu(gg_ref[pl.ds(r0, C), vc])).astype(o_ref.dtype)
        return carry

    lax.fori_loop(0, rows // C, chunk_body, 0)


def _gla(h, w_up_pad, b_gla, gla_norm, bsz, seq, cols, rows):
    col_q, col_k, col_v, col_g, col_lr = cols
    dkt = b_gla.shape[-1]
    dvt = gla_norm.shape[-1]
    dk, dv = dkt // GLA_HEADS, dvt // GLA_HEADS
    lrw = w_up_pad.shape[0]
    nblk = seq // rows
    tri = _gla_cumsum_operator(rows)
    blocks = (2 * _nbytes((rows, dkt), F32) + 2 * _nbytes((rows, dvt), F32)
              + _nbytes((rows, lrw), F32) + _nbytes((rows, dvt), BF16))
    resident = (_nbytes((rows, dkt), F32) + _nbytes((GLA_HEADS, dv, dk), F32)
                + _nbytes(tri.shape, BF16))
    const = lambda b, i: (0, 0)
    return pl.pallas_call(
        functools.partial(_gla_kernel, rows=rows, dk=dk, dv=dv),
        out_shape=jax.ShapeDtypeStruct((bsz * seq, dvt), BF16),
        grid=(bsz, nblk),
        in_specs=[pl.BlockSpec((rows, dkt), lambda b, i: (b * nblk + i, col_q // dkt)),
                  pl.BlockSpec((rows, dkt), lambda b, i: (b * nblk + i, col_k // dkt)),
                  pl.BlockSpec((rows, dvt), lambda b, i: (b * nblk + i, col_v // dvt)),
                  pl.BlockSpec((rows, dvt), lambda b, i: (b * nblk + i, col_g // dvt)),
                  pl.BlockSpec((rows, lrw), lambda b, i: (b * nblk + i, col_lr // lrw)),
                  pl.BlockSpec((lrw, dkt), const),
                  pl.BlockSpec((1, dkt), const),
                  pl.BlockSpec((1, dvt), const),
                  pl.BlockSpec(tri.shape, const)],
        out_specs=pl.BlockSpec((rows, dvt), lambda b, i: (b * nblk + i, 0)),
        scratch_shapes=[pltpu.VMEM((GLA_HEADS, dv, dk), F32),
                        pltpu.VMEM((rows, dkt), F32)],
        compiler_params=pltpu.CompilerParams(
            dimension_semantics=("parallel", "arbitrary"),
            vmem_limit_bytes=_vmem_limit(blocks, resident)),
        name="gla_mixer",
    )(h, h, h, h, h, w_up_pad, b_gla.reshape(1, dkt), gla_norm.reshape(1, dvt), tri)


def _rope_lane_table():
    half = ROPE_DIM // 2
    inv_freq = ROPE_THETA ** (-np.arange(0, ROPE_DIM, 2, dtype=np.float32) / ROPE_DIM)
    lane = np.arange(LANES) % SWA_HEAD_DIM
    tab = np.where(lane < ROPE_DIM, inv_freq[lane % half], 0.0).astype(np.float32)
    return jnp.asarray(tab.reshape(1, LANES))


SWA_ROW_CHUNK = 32


def _swa_kernel(sink_ref, q_ref, k_ref, v_ref, g_ref, pos_ref, inv_ref, o_ref,
                wk_ref, wv_ref, qr_ref, s_ref, p_ref, es_ref, bias_ref, *, n_pairs):
    n = pl.program_id(1)
    blk = SWA_BLOCK
    half = ROPE_DIM // 2
    hd = SWA_HEAD_DIM
    ppg = n_pairs // 2
    rc_rows = SWA_ROW_CHUNK

    @pl.when(n == 0)
    def _():
        wk_ref[...] = jnp.zeros(wk_ref.shape, BF16)
        row = lax.broadcasted_iota(jnp.int32, (4 * blk, 2 * LANES), 0)
        col = lax.broadcasted_iota(jnp.int32, (4 * blk, 2 * LANES), 1)
        ones = ((col >= LANES) & ((col >= LANES + hd) == (row >= 2 * blk))).astype(BF16)
        for g in range(2):
            wv_ref[g] = ones

    lane = lax.broadcasted_iota(jnp.int32, (1, LANES), 1)
    lane_h = _mod_pow2(lane, hd)
    lo = lane < hd
    ang = pos_ref[...] * inv_ref[...]
    cos = jnp.cos(ang)
    sin = jnp.sin(ang)

    def rope(x):
        x_up = pltpu.roll(x, LANES - half, axis=1)
        x_dn = pltpu.roll(x, half, axis=1)
        return x * cos + jnp.where(lane_h < half, -x_up, x_dn) * sin

    for g in range(2):
        for part in range(2):
            base = part * 2 * blk
            wk_ref[g, base:base + blk, :] = wk_ref[g, base + blk:base + 2 * blk, :]
            wv_ref[g, base:base + blk, 0:LANES] = wv_ref[g, base + blk:base + 2 * blk, 0:LANES]
    k_r = rope(k_ref[...])
    v_c = v_ref[...]
    k_sw = pltpu.roll(k_r, hd, axis=1)
    v_sw = pltpu.roll(v_c, hd, axis=1)
    zero = jnp.zeros_like(k_r)
    for g in range(2):
        k_lo, k_hi = (k_r, k_sw) if g == 0 else (k_sw, k_r)
        v_lo, v_hi = (v_c, v_sw) if g == 0 else (v_sw, v_c)
        wk_ref[g, blk:2 * blk, :] = jnp.where(lo, k_lo, zero).astype(BF16)
        wk_ref[g, 3 * blk:4 * blk, :] = jnp.where(lo, zero, k_hi).astype(BF16)
        wv_ref[g, blk:2 * blk, 0:LANES] = jnp.where(lo, v_lo, zero).astype(BF16)
        wv_ref[g, 3 * blk:4 * blk, 0:LANES] = jnp.where(lo, zero, v_hi).astype(BF16)

    r = lax.broadcasted_iota(jnp.int32, (blk, 2 * blk), 0)
    c = lax.broadcasted_iota(jnp.int32, (blk, 2 * blk), 1)
    valid = (c > r) & (c <= r + WINDOW) & ((n > 0) | (c >= blk))
    bias_ref[...] = jnp.where(valid, 0.0, -jnp.inf).astype(F32)

    qscale = hd ** -0.5
    for p in range(n_pairs):
        g, pp = divmod(p, ppg)
        qr_ref[g, pp * blk:(pp + 1) * blk, :] = (
            rope(q_ref[:, p * LANES:(p + 1) * LANES]) * qscale).astype(BF16)

    for g in range(2):
        s_ref[g] = lax.dot_general(qr_ref[g], wk_ref[g], NT_DIMS, preferred_element_type=F32)

    for g in range(2):
        for pp in range(ppg):
            for hh in range(2):
                sink = sink_ref[2 * (g * ppg + pp) + hh]
                kc = slice(hh * 2 * blk, (hh + 1) * 2 * blk)
                for rc in range(blk // rc_rows):
                    rows = slice(pp * blk + rc * rc_rows, pp * blk + (rc + 1) * rc_rows)
                    sh = s_ref[g, rows, kc] + bias_ref[rc * rc_rows:(rc + 1) * rc_rows, :]
                    m = jnp.maximum(jnp.max(sh, axis=-1, keepdims=True), sink)
                    p_ref[g, rows, kc] = jnp.exp(sh - m).astype(BF16)
                    es_ref[g, rows, hh * hd:(hh + 1) * hd] = jnp.broadcast_to(
                        jnp.exp(sink - m), (rc_rows, hd))

    for g in range(2):
        o2 = jnp.dot(p_ref[g], wv_ref[g], preferred_element_type=F32)
        for pp in range(ppg):
            rows = slice(pp * blk, (pp + 1) * blk)
            cols = slice((g * ppg + pp) * LANES, (g * ppg + pp + 1) * LANES)
            den = o2[rows, LANES:] + es_ref[g, rows, :]
            o_ref[:, cols] = (o2[rows, :LANES] / den * _silu(g_ref[:, cols])).astype(o_ref.dtype)


def _swa(h, pos_f, sinks, bsz, seq, cols):
    col_q, col_k, col_v, col_g = cols
    n_heads = sinks.shape[0]
    bw = n_heads * SWA_HEAD_DIM
    n_pairs = bw // LANES
    blk = SWA_BLOCK
    nblk = seq // blk
    scratch = [pltpu.VMEM((2, 4 * blk, LANES), BF16),
               pltpu.VMEM((2, 4 * blk, 2 * LANES), BF16),
               pltpu.VMEM((2, 4 * blk, LANES), BF16),
               pltpu.VMEM((2, 4 * blk, 4 * blk), F32),
               pltpu.VMEM((2, 4 * blk, 4 * blk), BF16),
               pltpu.VMEM((2, 4 * blk, LANES), F32),
               pltpu.VMEM((blk, 2 * blk), F32)]
    blocks = (2 * _nbytes((blk, bw), F32) + 3 * _nbytes((blk, LANES), F32) + _nbytes((blk, bw), BF16))
    resident = sum(_nbytes(sc.shape, sc.dtype) for sc in scratch)
    return pl.pallas_call(
        functools.partial(_swa_kernel, n_pairs=n_pairs),
        out_shape=jax.ShapeDtypeStruct((bsz * seq, bw), BF16),
        grid=(bsz, nblk),
        in_specs=[pl.BlockSpec(memory_space=pltpu.SMEM),
                  pl.BlockSpec((blk, bw), lambda b, i: (b * nblk + i, col_q // bw)),
                  pl.BlockSpec((blk, LANES), lambda b, i: (b * nblk + i, col_k // LANES)),
                  pl.BlockSpec((blk, LANES), lambda b, i: (b * nblk + i, col_v // LANES)),
                  pl.BlockSpec((blk, bw), lambda b, i: (b * nblk + i, col_g // bw)),
                  pl.BlockSpec((blk, 1), lambda b, i: (b * nblk + i, 0)),
                  pl.BlockSpec((1, LANES), lambda b, i: (0, 0))],
        out_specs=pl.BlockSpec((blk, bw), lambda b, i: (b * nblk + i, 0)),
        scratch_shapes=scratch,
        compiler_params=pltpu.CompilerParams(
            dimension_semantics=("parallel", "arbitrary"),
            vmem_limit_bytes=_vmem_limit(blocks, resident)),
        name="swa_mixer",
    )(sinks, h, h, h, h, pos_f, _rope_lane_table())


def _mem_kernel(q_ref, g_ref, mk_ref, mv_ref, o_ref, *, hd):
    scale = hd ** -0.5
    for h in range(XA_HEADS):
        cols = slice(h * hd, (h + 1) * hd)
        qh = q_ref[:, cols].astype(BF16)
        s = lax.dot_general(qh, mk_ref[:, cols], (((1,), (1,)), ((), ())),
                            preferred_element_type=F32) * scale
        m = jnp.max(s, axis=-1, keepdims=True)
        e = jnp.exp(s - m)
        p = e / jnp.sum(e, axis=-1, keepdims=True)
        o = jnp.dot(p.astype(BF16), mv_ref[:, cols], preferred_element_type=F32)
        o_ref[:, cols] = (o * _silu(g_ref[:, cols])).astype(o_ref.dtype)


def _mem_attn(h, mkv, bsz, seq, mem_len, cols, ts):
    col_q, col_g = cols
    bw = mkv.shape[1] // 2
    hd = bw // XA_HEADS
    nblk = seq // ts
    blocks = 2 * _nbytes((ts, bw), F32) + 2 * _nbytes((mem_len, bw), BF16) + _nbytes((ts, bw), BF16)
    return pl.pallas_call(
        functools.partial(_mem_kernel, hd=hd),
        out_shape=jax.ShapeDtypeStruct((bsz * seq, bw), BF16),
        grid=(bsz, nblk),
        in_specs=[pl.BlockSpec((ts, bw), lambda b, i: (b * nblk + i, col_q // bw)),
                  pl.BlockSpec((ts, bw), lambda b, i: (b * nblk + i, col_g // bw)),
                  pl.BlockSpec((mem_len, bw), lambda b, i: (b, 0)),
                  pl.BlockSpec((mem_len, bw), lambda b, i: (b, 1))],
        out_specs=pl.BlockSpec((ts, bw), lambda b, i: (b * nblk + i, 0)),
        compiler_params=pltpu.CompilerParams(
            dimension_semantics=("parallel", "parallel"), vmem_limit_bytes=_vmem_limit(blocks)),
        name="mem_attn",
    )(h, h, mkv, mkv)


IN_TN = 2 * LANES
MERGE_TN = 2 * LANES


def _main_layout(d_model):
    bw = d_model // 4
    dkt, dvt = bw // 2, bw
    kvw = 2 * SWA_HEAD_DIM
    names = ["pool_u", "pool_g", "gq", "gk", "gv", "gg", "glr", "sq", "sk", "sv", "sg", "xq", "xg"]
    widths = [bw, bw, dkt, dkt, dvt, dvt, GLA_RANK, bw, kvw, kvw, bw, bw, bw]
    src = {}
    off = 0
    for nme, w in zip(names, widths):
        src[nme] = (off, w)
        off += w
    n_main = off
    order = ["pool_u", "pool_g", "gv", "gg", "sq", "sg", "xq", "xg", "gq", "gk", "sk", "sv", "glr"]
    dst = {}
    cols = []
    off = 0
    for nme in order:
        o, w = src[nme]
        w_dst = w if nme != "glr" else IN_TN
        assert off % w_dst == 0
        dst[nme] = off
        cols.append(o + np.arange(w_dst))
        off += w_dst
    cols = np.concatenate(cols)
    assert off % IN_TN == 0 and cols.max() < n_main
    starts = cols[::IN_TN]
    assert np.all(cols.reshape(-1, IN_TN) == starts[:, None] + np.arange(IN_TN))
    return starts, dst, n_main


def _hybrid_layer(l, x, xb, mem_b, pos_f, w_in_t, w_pool, pool_scale, w_gla_up, b_gla, gla_norm,
                  sinks, w_mem_kv, w_branch, w_out, ln_g, ln_b, alpha, emit_bf16, bsz, seq):
    m, d = x.shape
    col_starts, dst, n_main = _main_layout(d)
    w_up_pad = jnp.concatenate(
        [w_gla_up[l], jnp.zeros((IN_TN - GLA_RANK, w_gla_up.shape[-1]), w_gla_up.dtype)], axis=0)

    h, w_mg_t = _in_proj(xb, w_in_t, l, col_starts, n_main, MERGE_TN, tm=2048, tn=IN_TN)
    mkv = _matmul(mem_b, w_mem_kv, l, BF16, tm=mem_b.shape[0], tn=512, name="mem_kv")

    o_pool = _pool(h, w_pool[l].astype(BF16), pool_scale[l], bsz, seq,
                   dst["pool_u"], dst["pool_g"], ts=512)
    o_gla = _gla(h, w_up_pad, b_gla[l], gla_norm[l], bsz, seq,
                 (dst["gq"], dst["gk"], dst["gv"], dst["gg"], dst["glr"]), rows=512)
    o_swa = _swa(h, pos_f, sinks[l], bsz, seq, (dst["sq"], dst["sk"], dst["sv"], dst["sg"]))
    o_mem = _mem_attn(h, mkv, bsz, seq, mem_b.shape[0] // bsz, (dst["xq"], dst["xg"]), ts=512)

    y = _merge(xb, (o_pool, o_gla, o_swa, o_mem), w_mg_t, w_branch, l, tm=1024, tn=MERGE_TN)
    z = _matmul(y, w_out, l, F32, tm=2048, tn=256, name="out_proj", resid=x, alpha=alpha)
    return _layer_norm(z, ln_g[l], ln_b[l], tm=512, emit_bf16=emit_bf16)


def kernel(x, mem, positions, w_in, w_pool, pool_scale, w_gla_up, b_gla, gla_norm, sinks,
           w_mem_kv, w_branch, w_out, ln_g, ln_b):
    bsz, seq, d = x.shape
    depth = w_in.shape[0]
    alpha = (2 * depth) ** 0.25
    xf = x.reshape(bsz * seq, d)
    xb = xf.astype(BF16)
    mem_b = mem.reshape(bsz * mem.shape[1], d).astype(BF16)
    pos_f = positions.astype(F32).reshape(bsz * seq, 1)
    w_in_t = jnp.swapaxes(w_in, 1, 2)
    for l in range(depth):
        xf, xb = _hybrid_layer(
            l, xf, xb, mem_b, pos_f, w_in_t, w_pool, pool_scale, w_gla_up, b_gla,
            gla_norm, sinks, w_mem_kv, w_branch, w_out, ln_g, ln_b,
            alpha, l + 1 < depth, bsz, seq)
    return xf.reshape(bsz, seq, d)
```

```python
import functools
import math

import jax
import jax.numpy as jnp
import numpy as np
from jax import lax
from jax.experimental import pallas as pl
from jax.experimental.pallas import tpu as pltpu

F32 = jnp.float32
BF16 = jnp.bfloat16

N_BRANCH = 4
POOL_WINDOWS = (2, 4, 8, 16)
GLA_HEADS = 4
GLA_RANK = 16
GLA_TAU = 16.0
GLA_CHUNK = 64
GLA_SUB = 16
GLA_SAFE_LOG2 = 24.0
GLA_SAFE_KEY = 2.0 ** 100
SWA_HEAD_DIM = 64
SWA_GROUP = 8
WINDOW = 128
SWA_BLOCK = 128
ROPE_THETA = 500000.0
ROPE_DIM = SWA_HEAD_DIM // 4
XA_HEADS = 4
LN_EPS = 1e-5

LANES = 128
SUBLANES = 8
V7X_VMEM_BYTES = 64 * 1024 * 1024
VMEM_CAP = V7X_VMEM_BYTES - 8 * 1024 * 1024
VMEM_TEMP_BYTES = 12 * 1024 * 1024


def _vmem_limit(streamed_bytes, resident_bytes=0):
    return int(min(VMEM_CAP, 2 * streamed_bytes + resident_bytes + VMEM_TEMP_BYTES))


def _nbytes(shape, dtype):
    return int(np.prod(shape)) * jnp.dtype(dtype).itemsize


def _silu(g):
    return g * jax.nn.sigmoid(g)


def _div_pow2(x, n):
    assert n & (n - 1) == 0
    return x >> (n.bit_length() - 1)


def _mod_pow2(x, n):
    assert n & (n - 1) == 0
    return x & (n - 1)


NT_DIMS = (((1,), (1,)), ((), ()))


def _mm_kernel(a_ref, w_ref, o_ref):
    w = w_ref[...].astype(BF16)
    o_ref[...] = jnp.dot(a_ref[...], w, preferred_element_type=F32).astype(o_ref.dtype)


def _mm_resid_kernel(a_ref, w_ref, x_ref, o_ref, *, alpha):
    w = w_ref[...].astype(BF16)
    o_ref[...] = alpha * x_ref[...] + jnp.dot(a_ref[...], w, preferred_element_type=F32)


def _matmul(a, w, layer, out_dtype, tm, tn, name, resid=None, alpha=None):
    m, k = a.shape
    n = w.shape[-1]
    assert m % tm == 0 and n % tn == 0
    resident = _nbytes((tm, k), a.dtype)
    streamed = _nbytes((k, tn), w.dtype) + _nbytes((tm, tn), out_dtype)
    in_specs = [pl.BlockSpec((tm, k), lambda i, j: (i, 0), pipeline_mode=pl.Buffered(1)),
                pl.BlockSpec((None, k, tn), lambda i, j: (layer, 0, j))]
    args = [a, w]
    if resid is None:
        kern = _mm_kernel
    else:
        kern = functools.partial(_mm_resid_kernel, alpha=alpha)
        in_specs.append(pl.BlockSpec((tm, tn), lambda i, j: (i, j)))
        args.append(resid)
        streamed += _nbytes((tm, tn), resid.dtype)
    return pl.pallas_call(
        kern,
        out_shape=jax.ShapeDtypeStruct((m, n), out_dtype),
        grid=(m // tm, n // tn),
        in_specs=in_specs,
        out_specs=pl.BlockSpec((tm, tn), lambda i, j: (i, j)),
        compiler_params=pltpu.CompilerParams(
            dimension_semantics=("parallel", "arbitrary"),
            vmem_limit_bytes=_vmem_limit(streamed, resident)),
        name=name,
    )(*args)


def _in_proj_kernel(tab_ref, x_ref, w_ref, g_ref, o_ref, gs_ref):
    w = w_ref[0].astype(BF16)
    o_ref[...] = lax.dot_general(x_ref[...], w, NT_DIMS, preferred_element_type=F32)
    gs_ref[...] = g_ref[0].astype(gs_ref.dtype)


def _in_proj(xb, w_in_t, layer, col_starts, gate_row0, gate_tn, tm, tn):
    m, k = xb.shape
    nblk = len(col_starts)
    assert m % tm == 0 and all(int(s) % SUBLANES == 0 for s in col_starts)
    n_steps = (m // tm) * nblk
    sr = LANES
    gate_rows = N_BRANCH * k
    n_slabs = gate_rows // sr
    per_blk, per_branch = gate_tn // sr, k // sr
    assert gate_tn % sr == 0 and k % gate_tn == 0 and n_slabs <= n_steps and gate_row0 % SUBLANES == 0
    resident = _nbytes((tm, k), xb.dtype)
    streamed = (_nbytes((tn, k), w_in_t.dtype) + _nbytes((tm, tn), F32)
                + _nbytes((sr, k), w_in_t.dtype) + _nbytes((sr, k), BF16))

    def slab(i, j):
        return jnp.minimum(i * nblk + j, n_slabs - 1)

    def slab_dst(s):
        b, within = s // per_branch, s % per_branch
        return (within // per_blk) * (N_BRANCH * per_blk) + b * per_blk + within % per_blk

    return pl.pallas_call(
        _in_proj_kernel,
        out_shape=[jax.ShapeDtypeStruct((m, nblk * tn), F32),
                   jax.ShapeDtypeStruct((gate_rows, k), BF16)],
        grid_spec=pltpu.PrefetchScalarGridSpec(
            num_scalar_prefetch=1,
            grid=(m // tm, nblk),
            in_specs=[pl.BlockSpec((tm, k), lambda i, j, tab: (i, 0), pipeline_mode=pl.Buffered(1)),
                      pl.BlockSpec((pl.Element(1), pl.Element(tn), pl.Element(k)),
                                   lambda i, j, tab: (layer, pl.multiple_of(tab[j], SUBLANES), 0)),
                      pl.BlockSpec((pl.Element(1), pl.Element(sr), pl.Element(k)),
                                   lambda i, j, tab: (
                                       layer, pl.multiple_of(gate_row0 + slab(i, j) * sr, SUBLANES), 0))],
            out_specs=[pl.BlockSpec((tm, tn), lambda i, j, tab: (i, j)),
                       pl.BlockSpec((sr, k), lambda i, j, tab: (slab_dst(slab(i, j)), 0))]),
        compiler_params=pltpu.CompilerParams(
            dimension_semantics=("arbitrary", "arbitrary"),
            vmem_limit_bytes=_vmem_limit(streamed, resident)),
        name="in_proj",
    )(jnp.asarray(np.asarray(col_starts, np.int32)), xb, w_in_t, w_in_t)


def _merge_kernel(x_ref, o0, o1, o2, o3, g_ref, w_ref, y_ref):
    x = x_ref[...]
    tn = y_ref.shape[1]
    acc = None
    for b, o_ref in enumerate((o0, o1, o2, o3)):
        gate = lax.dot_general(x, g_ref[b * tn:(b + 1) * tn, :], NT_DIMS, preferred_element_type=F32)
        proj = jnp.dot(o_ref[...], w_ref[b].astype(BF16), preferred_element_type=F32)
        term = jax.nn.sigmoid(gate) * proj
        acc = term if acc is None else acc + term
    y_ref[...] = acc.astype(y_ref.dtype)


def _merge(xb, branches, w_mg_t, w_branch, layer, tm, tn):
    m, d = xb.shape
    bw = branches[0].shape[1]
    nj = d // tn
    in_specs = [pl.BlockSpec((tm, d), lambda i, j: (i, 0), pipeline_mode=pl.Buffered(1))]
    in_specs += [pl.BlockSpec((tm, bw), lambda i, j: (i, 0), pipeline_mode=pl.Buffered(1))
                 for _ in range(N_BRANCH)]
    in_specs += [pl.BlockSpec((N_BRANCH * tn, d), lambda i, j: (j, 0)),
                 pl.BlockSpec((None, N_BRANCH, bw, tn), lambda i, j: (layer, 0, 0, j))]
    resident = _nbytes((tm, d), BF16) + N_BRANCH * _nbytes((tm, bw), BF16)
    streamed = (N_BRANCH * (_nbytes((tn, d), BF16) + _nbytes((bw, tn), w_branch.dtype))
                + _nbytes((tm, tn), BF16))
    return pl.pallas_call(
        _merge_kernel,
        out_shape=jax.ShapeDtypeStruct((m, d), BF16),
        grid=(m // tm, nj),
        in_specs=in_specs,
        out_specs=pl.BlockSpec((tm, tn), lambda i, j: (i, j)),
        compiler_params=pltpu.CompilerParams(
            dimension_semantics=("parallel", "arbitrary"),
            vmem_limit_bytes=_vmem_limit(streamed, resident)),
        name="merge",
    )(xb, *branches, w_mg_t, w_branch)


def _ln_kernel(z_ref, g_ref, b_ref, o_ref, ob_ref=None):
    z = z_ref[...]
    mu = jnp.mean(z, axis=-1, keepdims=True)
    zc = z - mu
    var = jnp.mean(zc * zc, axis=-1, keepdims=True)
    y = zc * lax.rsqrt(var + LN_EPS) * g_ref[...] + b_ref[...]
    o_ref[...] = y
    if ob_ref is not None:
        ob_ref[...] = y.astype(BF16)


def _layer_norm(z, g, b, tm, emit_bf16):
    m, d = z.shape
    out_shape = [jax.ShapeDtypeStruct((m, d), F32)]
    out_specs = [pl.BlockSpec((tm, d), lambda i: (i, 0))]
    if emit_bf16:
        out_shape.append(jax.ShapeDtypeStruct((m, d), BF16))
        out_specs.append(pl.BlockSpec((tm, d), lambda i: (i, 0)))
    blocks = 2 * _nbytes((tm, d), F32) + _nbytes((tm, d), BF16)
    res = pl.pallas_call(
        _ln_kernel,
        out_shape=out_shape,
        grid=(m // tm,),
        in_specs=[pl.BlockSpec((tm, d), lambda i: (i, 0)),
                  pl.BlockSpec((1, d), lambda i: (0, 0)),
                  pl.BlockSpec((1, d), lambda i: (0, 0))],
        out_specs=out_specs,
        compiler_params=pltpu.CompilerParams(
            dimension_semantics=("parallel",), vmem_limit_bytes=_vmem_limit(blocks)),
        name="layer_norm",
    )(z, g.reshape(1, d), b.reshape(1, d))
    return res if emit_bf16 else (res[0], None)


POOL_HIST = 16


def _pool_kernel(u_ref, g_ref, wp_ref, sc_ref, o_ref, buf, *, ts, group):
    i = pl.program_id(1)

    @pl.when(i == 0)
    def _():
        buf[0:POOL_HIST, :] = jnp.zeros((POOL_HIST, buf.shape[1]), F32)

    buf[POOL_HIST:POOL_HIST + ts, :] = u_ref[...]
    t = i * ts + lax.broadcasted_iota(jnp.int32, (ts, 1), 0)
    outs = []
    for gi, w in enumerate(POOL_WINDOWS):
        cols = slice(gi * group, (gi + 1) * group)
        cur = buf[POOL_HIST:POOL_HIST + ts, cols]
        acc = cur
        for lag in range(1, w):
            acc = acc + buf[POOL_HIST - lag:POOL_HIST - lag + ts, cols]
        cnt = jnp.minimum(t + 1, w).astype(F32)
        pooled = acc / cnt - cur
        outs.append(jnp.dot(pooled.astype(BF16), wp_ref[gi], preferred_element_type=F32))
    o = jnp.concatenate(outs, axis=1) * sc_ref[...] * _silu(g_ref[...])
    o_ref[...] = o.astype(o_ref.dtype)
    buf[0:POOL_HIST, :] = buf[ts:ts + POOL_HIST, :]


def _pool(h, w_pool_b, pool_scale, bsz, seq, col_u, col_g, ts):
    bw = pool_scale.shape[-1]
    group = bw // len(POOL_WINDOWS)
    nblk = seq // ts
    blocks = 2 * _nbytes((ts, bw), F32) + _nbytes((ts, bw), BF16) + _nbytes((ts + POOL_HIST, bw), F32)
    return pl.pallas_call(
        functools.partial(_pool_kernel, ts=ts, group=group),
        out_shape=jax.ShapeDtypeStruct((bsz * seq, bw), BF16),
        grid=(bsz, nblk),
        in_specs=[pl.BlockSpec((ts, bw), lambda b, i: (b * nblk + i, col_u // bw)),
                  pl.BlockSpec((ts, bw), lambda b, i: (b * nblk + i, col_g // bw)),
                  pl.BlockSpec(w_pool_b.shape, lambda b, i: (0, 0, 0)),
                  pl.BlockSpec((1, bw), lambda b, i: (0, 0))],
        out_specs=pl.BlockSpec((ts, bw), lambda b, i: (b * nblk + i, 0)),
        scratch_shapes=[pltpu.VMEM((ts + POOL_HIST, bw), F32)],
        compiler_params=pltpu.CompilerParams(
            dimension_semantics=("parallel", "arbitrary"), vmem_limit_bytes=_vmem_limit(blocks)),
        name="pool_mixer",
    )(h, h, w_pool_b, pool_scale.reshape(1, bw))


def _split_bf16(x, parts):
    out = []
    r = x
    for _ in range(parts):
        p = r.astype(BF16)
        out.append(p)
        r = r - p.astype(F32)
    return out


LOG2E = math.log2(math.e)


def _gla_cumsum_operator(rows):
    r = np.arange(rows)
    tri = ((r[:, None] // GLA_CHUNK == r[None, :] // GLA_CHUNK) & (r[None, :] <= r[:, None]))
    return jnp.asarray(tri, BF16)


def _gla_kernel(q_ref, k_ref, v_ref, gg_ref, lr_ref, wup_ref, bg_ref, nrm_ref, tri_ref,
                o_ref, state_ref, b_ref, *, rows, dk, dv):
    i = pl.program_id(1)
    C = GLA_CHUNK
    nsub = C // GLA_SUB

    @pl.when(i == 0)
    def _():
        state_ref[...] = jnp.zeros(state_ref.shape, F32)

    lr_h, lr_m = _split_bf16(lr_ref[...], 2)
    w_h, w_m = _split_bf16(wup_ref[...], 2)
    z = (jnp.dot(lr_h, w_h, preferred_element_type=F32)
         + jnp.dot(lr_h, w_m, preferred_element_type=F32)
         + jnp.dot(lr_m, w_h, preferred_element_type=F32)) + bg_ref[...]
    log2_a = (jnp.minimum(z, 0.0) - jnp.log(1.0 + jnp.exp(-jnp.abs(z)))) * (LOG2E / GLA_TAU)

    bsum = None
    for part in _split_bf16(log2_a, 3):
        t = jnp.dot(tri_ref[...], part, preferred_element_type=F32)
        bsum = t if bsum is None else bsum + t
    b_ref[...] = bsum

    row = lax.broadcasted_iota(jnp.int32, (C, 1), 0)
    col = lax.broadcasted_iota(jnp.int32, (1, C), 1)
    row_sub = _div_pow2(row, GLA_SUB)
    col_sub = _div_pow2(col, GLA_SUB)
    lane_c = lax.broadcasted_iota(jnp.int32, (GLA_SUB, C), 1)
    scale = dk ** -0.5

    def sub_heads(x):
        return jnp.concatenate(
            [jnp.broadcast_to(x[a * GLA_SUB:a * GLA_SUB + 1, :], (GLA_SUB, dk))
             for a in range(nsub)], axis=0)

    def chunk_body(c, carry, *, bounded):
        r0 = pl.multiple_of(c * C, C)
        for h in range(GLA_HEADS):
            kc = slice(h * dk, (h + 1) * dk)
            vc = slice(h * dv, (h + 1) * dv)
            qs = q_ref[pl.ds(r0, C), kc] * scale
            k = k_ref[pl.ds(r0, C), kc]
            v = v_ref[pl.ds(r0, C), vc].astype(BF16)
            b = b_ref[pl.ds(r0, C), kc]
            b_last = b[C - 1:C, :]
            st = state_ref[h]

            q_in = (qs * jnp.exp2(b)).astype(BF16)
            o = lax.dot_general(q_in, st.astype(BF16), NT_DIMS, preferred_element_type=F32)

            q_t = qs * jnp.exp2(b - sub_heads(b))
            lhs = jnp.concatenate(
                [jnp.where(row_sub == a, q_t, 0.0) for a in range(nsub)], axis=1).astype(BF16)
            key_exp = [b[a * GLA_SUB:a * GLA_SUB + 1, :] - b for a in range(nsub)]
            if not bounded:
                key_exp = [jnp.minimum(e, 0.0) for e in key_exp]
            rhs = jnp.concatenate([k * jnp.exp2(e) for e in key_exp], axis=1).astype(BF16)
            a_fac = lax.dot_general(lhs, rhs, NT_DIMS, preferred_element_type=F32)

            if bounded:
                attn = jnp.where(row >= col, a_fac, 0.0)
            else:
                diag_blocks = []
                for a in range(nsub):
                    rs = slice(a * GLA_SUB, (a + 1) * GLA_SUB)
                    qa = qs[rs, :]
                    ba = b[rs, :]
                    blk = jnp.zeros((GLA_SUB, C), F32)
                    for jj in range(GLA_SUB):
                        j = a * GLA_SUB + jj
                        tj = qa * k[j:j + 1, :] * jnp.exp2(ba - b[j:j + 1, :])
                        cj = jnp.sum(tj, axis=1, keepdims=True)
                        blk = jnp.where(lane_c == j, cj, blk)
                    diag_blocks.append(blk)
                a_diag = jnp.concatenate(diag_blocks, axis=0)
                attn = jnp.where(row_sub > col_sub, a_fac,
                                 jnp.where((row_sub == col_sub) & (row >= col), a_diag, 0.0))
            o = o + jnp.dot(attn.astype(BF16), v, preferred_element_type=F32)

            k_dec = (k * jnp.exp2(b_last - b)).astype(BF16)
            upd = lax.dot_general(v, k_dec, (((0,), (0,)), ((), ())),
                                  preferred_element_type=F32)
            state_ref[h] = st * jnp.exp2(b_last) + upd

            ms = jnp.mean(o * o, axis=-1, keepdims=True)
            on = o * lax.rsqrt(ms + LN_EPS) * nrm_ref[:, vc]
            o_ref[pl.ds(r0, C), vc] = (on * _silu(gg_ref[pl.ds(r0, C), vc])).astype(o_ref.dtype)
        return carry

    bounded = (-jnp.min(bsum) < GLA_SAFE_LOG2) & (jnp.max(jnp.abs(k_ref[...])) < GLA_SAFE_KEY)

    @pl.when(bounded)
    def _():
        lax.fori_loop(0, rows // C, functools.partial(chunk_body, bounded=True), 0)

    @pl.when(jnp.logical_not(bounded))
    def _():
        lax.fori_loop(0, rows // C, functools.partial(chunk_body, bounded=False), 0)


def _gla(h, w_up_pad, b_gla, gla_norm, bsz, seq, cols, rows):
    col_q, col_k, col_v, col_g, col_lr = cols
    dkt = b_gla.shape[-1]
    dvt = gla_norm.shape[-1]
    dk, dv = dkt // GLA_HEADS, dvt // GLA_HEADS
    lrw = w_up_pad.shape[0]
    nblk = seq // rows
    tri = _gla_cumsum_operator(rows)
    blocks = (2 * _nbytes((rows, dkt), F32) + 2 * _nbytes((rows, dvt), F32)
              + _nbytes((rows, lrw), F32) + _nbytes((rows, dvt), BF16))
    resident = (_nbytes((rows, dkt), F32) + _nbytes((GLA_HEADS, dv, dk), F32)
                + _nbytes(tri.shape, BF16))
    const = lambda b, i: (0, 0)
    return pl.pallas_call(
        functools.partial(_gla_kernel, rows=rows, dk=dk, dv=dv),
        out_shape=jax.ShapeDtypeStruct((bsz * seq, dvt), BF16),
        grid=(bsz, nblk),
        in_specs=[pl.BlockSpec((rows, dkt), lambda b, i: (b * nblk + i, col_q // dkt)),
                  pl.BlockSpec((rows, dkt), lambda b, i: (b * nblk + i, col_k // dkt)),
                  pl.BlockSpec((rows, dvt), lambda b, i: (b * nblk + i, col_v // dvt)),
                  pl.BlockSpec((rows, dvt), lambda b, i: (b * nblk + i, col_g // dvt)),
                  pl.BlockSpec((rows, lrw), lambda b, i: (b * nblk + i, col_lr // lrw)),
                  pl.BlockSpec((lrw, dkt), const),
                  pl.BlockSpec((1, dkt), const),
                  pl.BlockSpec((1, dvt), const),
                  pl.BlockSpec(tri.shape, const)],
        out_specs=pl.BlockSpec((rows, dvt), lambda b, i: (b * nblk + i, 0)),
        scratch_shapes=[pltpu.VMEM((GLA_HEADS, dv, dk), F32),
                        pltpu.VMEM((rows, dkt), F32)],
        compiler_params=pltpu.CompilerParams(
            dimension_semantics=("parallel", "arbitrary"),
            vmem_limit_bytes=_vmem_limit(blocks, resident)),
        name="gla_mixer",
    )(h, h, h, h, h, w_up_pad, b_gla.reshape(1, dkt), gla_norm.reshape(1, dvt), tri)


def _rope_lane_table():
    half = ROPE_DIM // 2
    inv_freq = ROPE_THETA ** (-np.arange(0, ROPE_DIM, 2, dtype=np.float32) / ROPE_DIM)
    lane = np.arange(LANES) % SWA_HEAD_DIM
    tab = np.where(lane < ROPE_DIM, inv_freq[lane % half], 0.0).astype(np.float32)
    return jnp.asarray(tab.reshape(1, LANES))


SWA_ROW_CHUNK = 32


def _swa_kernel(sink_ref, q_ref, k_ref, v_ref, g_ref, pos_ref, inv_ref, o_ref,
                wk_ref, wv_ref, qr_ref, s_ref, p_ref, es_ref, bias_ref, *, n_pairs):
    n = pl.program_id(1)
    blk = SWA_BLOCK
    half = ROPE_DIM // 2
    hd = SWA_HEAD_DIM
    ppg = n_pairs // 2
    rc_rows = SWA_ROW_CHUNK

    @pl.when(n == 0)
    def _():
        wk_ref[...] = jnp.zeros(wk_ref.shape, BF16)
        row = lax.broadcasted_iota(jnp.int32, (4 * blk, 2 * LANES), 0)
        col = lax.broadcasted_iota(jnp.int32, (4 * blk, 2 * LANES), 1)
        ones = ((col >= LANES) & ((col >= LANES + hd) == (row >= 2 * blk))).astype(BF16)
        for g in range(2):
            wv_ref[g] = ones

    lane = lax.broadcasted_iota(jnp.int32, (1, LANES), 1)
    lane_h = _mod_pow2(lane, hd)
    lo = lane < hd
    ang = pos_ref[...] * inv_ref[...]
    cos = jnp.cos(ang)
    sin = jnp.sin(ang)

    def rope(x):
        x_up = pltpu.roll(x, LANES - half, axis=1)
        x_dn = pltpu.roll(x, half, axis=1)
        return x * cos + jnp.where(lane_h < half, -x_up, x_dn) * sin

    for g in range(2):
        for part in range(2):
            base = part * 2 * blk
            wk_ref[g, base:base + blk, :] = wk_ref[g, base + blk:base + 2 * blk, :]
            wv_ref[g, base:base + blk, 0:LANES] = wv_ref[g, base + blk:base + 2 * blk, 0:LANES]
    k_r = rope(k_ref[...])
    v_c = v_ref[...]
    k_sw = pltpu.roll(k_r, hd, axis=1)
    v_sw = pltpu.roll(v_c, hd, axis=1)
    zero = jnp.zeros_like(k_r)
    for g in range(2):
        k_lo, k_hi = (k_r, k_sw) if g == 0 else (k_sw, k_r)
        v_lo, v_hi = (v_c, v_sw) if g == 0 else (v_sw, v_c)
        wk_ref[g, blk:2 * blk, :] = jnp.where(lo, k_lo, zero).astype(BF16)
        wk_ref[g, 3 * blk:4 * blk, :] = jnp.where(lo, zero, k_hi).astype(BF16)
        wv_ref[g, blk:2 * blk, 0:LANES] = jnp.where(lo, v_lo, zero).astype(BF16)
        wv_ref[g, 3 * blk:4 * blk, 0:LANES] = jnp.where(lo, zero, v_hi).astype(BF16)

    r = lax.broadcasted_iota(jnp.int32, (blk, 2 * blk), 0)
    c = lax.broadcasted_iota(jnp.int32, (blk, 2 * blk), 1)
    valid = (c > r) & (c <= r + WINDOW) & ((n > 0) | (c >= blk))
    bias_ref[...] = jnp.where(valid, 0.0, -jnp.inf).astype(F32)

    qscale = hd ** -0.5
    for p in range(n_pairs):
        g, pp = divmod(p, ppg)
        qr_ref[g, pp * blk:(pp + 1) * blk, :] = (
            rope(q_ref[:, p * LANES:(p + 1) * LANES]) * qscale).astype(BF16)

    for g in range(2):
        s_ref[g] = lax.dot_general(qr_ref[g], wk_ref[g], NT_DIMS, preferred_element_type=F32)

    for g in range(2):
        for pp in range(ppg):
            for hh in range(2):
                sink = sink_ref[2 * (g * ppg + pp) + hh]
                kc = slice(hh * 2 * blk, (hh + 1) * 2 * blk)
                for rc in range(blk // rc_rows):
                    rows = slice(pp * blk + rc * rc_rows, pp * blk + (rc + 1) * rc_rows)
                    sh = s_ref[g, rows, kc] + bias_ref[rc * rc_rows:(rc + 1) * rc_rows, :]
                    m = jnp.maximum(jnp.max(sh, axis=-1, keepdims=True), sink)
                    p_ref[g, rows, kc] = jnp.exp(sh - m).astype(BF16)
                    es_ref[g, rows, hh * hd:(hh + 1) * hd] = jnp.broadcast_to(
                        jnp.exp(sink - m), (rc_rows, hd))

    for g in range(2):
        o2 = jnp.dot(p_ref[g], wv_ref[g], preferred_element_type=F32)
        for pp in range(ppg):
            rows = slice(pp * blk, (pp + 1) * blk)
            cols = slice((g * ppg + pp) * LANES, (g * ppg + pp + 1) * LANES)
            den = o2[rows, LANES:] + es_ref[g, rows, :]
            o_ref[:, cols] = (o2[rows, :LANES] / den * _silu(g_ref[:, cols])).astype(o_ref.dtype)


def _swa(h, pos_f, sinks, bsz, seq, cols):
    col_q, col_k, col_v, col_g = cols
    n_heads = sinks.shape[0]
    bw = n_heads * SWA_HEAD_DIM
    n_pairs = bw // LANES
    blk = SWA_BLOCK
    nblk = seq // blk
    scratch = [pltpu.VMEM((2, 4 * blk, LANES), BF16),
               pltpu.VMEM((2, 4 * blk, 2 * LANES), BF16),
               pltpu.VMEM((2, 4 * blk, LANES), BF16),
               pltpu.VMEM((2, 4 * blk, 4 * blk), F32),
               pltpu.VMEM((2, 4 * blk, 4 * blk), BF16),
               pltpu.VMEM((2, 4 * blk, LANES), F32),
               pltpu.VMEM((blk, 2 * blk), F32)]
    blocks = (2 * _nbytes((blk, bw), F32) + 3 * _nbytes((blk, LANES), F32) + _nbytes((blk, bw), BF16))
    resident = sum(_nbytes(sc.shape, sc.dtype) for sc in scratch)
    return pl.pallas_call(
        functools.partial(_swa_kernel, n_pairs=n_pairs),
        out_shape=jax.ShapeDtypeStruct((bsz * seq, bw), BF16),
        grid=(bsz, nblk),
        in_specs=[pl.BlockSpec(memory_space=pltpu.SMEM),
                  pl.BlockSpec((blk, bw), lambda b, i: (b * nblk + i, col_q // bw)),
                  pl.BlockSpec((blk, LANES), lambda b, i: (b * nblk + i, col_k // LANES)),
                  pl.BlockSpec((blk, LANES), lambda b, i: (b * nblk + i, col_v // LANES)),
                  pl.BlockSpec((blk, bw), lambda b, i: (b * nblk + i, col_g // bw)),
                  pl.BlockSpec((blk, 1), lambda b, i: (b * nblk + i, 0)),
                  pl.BlockSpec((1, LANES), lambda b, i: (0, 0))],
        out_specs=pl.BlockSpec((blk, bw), lambda b, i: (b * nblk + i, 0)),
        scratch_shapes=scratch,
        compiler_params=pltpu.CompilerParams(
            dimension_semantics=("parallel", "arbitrary"),
            vmem_limit_bytes=_vmem_limit(blocks, resident)),
        name="swa_mixer",
    )(sinks, h, h, h, h, pos_f, _rope_lane_table())


def _mem_kernel(q_ref, g_ref, mk_ref, mv_ref, o_ref, *, hd):
    scale = hd ** -0.5
    for h in range(XA_HEADS):
        cols = slice(h * hd, (h + 1) * hd)
        qh = q_ref[:, cols].astype(BF16)
        s = lax.dot_general(qh, mk_ref[:, cols], (((1,), (1,)), ((), ())),
                            preferred_element_type=F32) * scale
        m = jnp.max(s, axis=-1, keepdims=True)
        e = jnp.exp(s - m)
        p = e / jnp.sum(e, axis=-1, keepdims=True)
        o = jnp.dot(p.astype(BF16), mv_ref[:, cols], preferred_element_type=F32)
        o_ref[:, cols] = (o * _silu(g_ref[:, cols])).astype(o_ref.dtype)


def _mem_attn(h, mkv, bsz, seq, mem_len, cols, ts):
    col_q, col_g = cols
    bw = mkv.shape[1] // 2
    hd = bw // XA_HEADS
    nblk = seq // ts
    blocks = 2 * _nbytes((ts, bw), F32) + 2 * _nbytes((mem_len, bw), BF16) + _nbytes((ts, bw), BF16)
    return pl.pallas_call(
        functools.partial(_mem_kernel, hd=hd),
        out_shape=jax.ShapeDtypeStruct((bsz * seq, bw), BF16),
        grid=(bsz, nblk),
        in_specs=[pl.BlockSpec((ts, bw), lambda b, i: (b * nblk + i, col_q // bw)),
                  pl.BlockSpec((ts, bw), lambda b, i: (b * nblk + i, col_g // bw)),
                  pl.BlockSpec((mem_len, bw), lambda b, i: (b, 0)),
                  pl.BlockSpec((mem_len, bw), lambda b, i: (b, 1))],
        out_specs=pl.BlockSpec((ts, bw), lambda b, i: (b * nblk + i, 0)),
        compiler_params=pltpu.CompilerParams(
            dimension_semantics=("parallel", "parallel"), vmem_limit_bytes=_vmem_limit(blocks)),
        name="mem_attn",
    )(h, h, mkv, mkv)


IN_TN = 2 * LANES
MERGE_TN = 2 * LANES


def _main_layout(d_model):
    bw = d_model // 4
    dkt, dvt = bw // 2, bw
    kvw = 2 * SWA_HEAD_DIM
    names = ["pool_u", "pool_g", "gq", "gk", "gv", "gg", "glr", "sq", "sk", "sv", "sg", "xq", "xg"]
    widths = [bw, bw, dkt, dkt, dvt, dvt, GLA_RANK, bw, kvw, kvw, bw, bw, bw]
    src = {}
    off = 0
    for nme, w in zip(names, widths):
        src[nme] = (off, w)
        off += w
    n_main = off
    order = ["pool_u", "pool_g", "gv", "gg", "sq", "sg", "xq", "xg", "gq", "gk", "sk", "sv", "glr"]
    dst = {}
    cols = []
    off = 0
    for nme in order:
        o, w = src[nme]
        w_dst = w if nme != "glr" else IN_TN
        assert off % w_dst == 0
        dst[nme] = off
        cols.append(o + np.arange(w_dst))
        off += w_dst
    cols = np.concatenate(cols)
    assert off % IN_TN == 0 and cols.max() < n_main
    starts = cols[::IN_TN]
    assert np.all(cols.reshape(-1, IN_TN) == starts[:, None] + np.arange(IN_TN))
    return starts, dst, n_main


def _hybrid_layer(l, x, xb, mem_b, pos_f, w_in_t, w_pool, pool_scale, w_gla_up, b_gla, gla_norm,
                  sinks, w_mem_kv, w_branch, w_out, ln_g, ln_b, alpha, emit_bf16, bsz, seq):
    m, d = x.shape
    col_starts, dst, n_main = _main_layout(d)
    w_up_pad = jnp.concatenate(
        [w_gla_up[l], jnp.zeros((IN_TN - GLA_RANK, w_gla_up.shape[-1]), w_gla_up.dtype)], axis=0)

    h, w_mg_t = _in_proj(xb, w_in_t, l, col_starts, n_main, MERGE_TN, tm=2048, tn=IN_TN)
    mkv = _matmul(mem_b, w_mem_kv, l, BF16, tm=mem_b.shape[0], tn=512, name="mem_kv")

    o_pool = _pool(h, w_pool[l].astype(BF16), pool_scale[l], bsz, seq,
                   dst["pool_u"], dst["pool_g"], ts=512)
    o_gla = _gla(h, w_up_pad, b_gla[l], gla_norm[l], bsz, seq,
                 (dst["gq"], dst["gk"], dst["gv"], dst["gg"], dst["glr"]), rows=512)
    o_swa = _swa(h, pos_f, sinks[l], bsz, seq, (dst["sq"], dst["sk"], dst["sv"], dst["sg"]))
    o_mem = _mem_attn(h, mkv, bsz, seq, mem_b.shape[0] // bsz, (dst["xq"], dst["xg"]), ts=512)

    y = _merge(xb, (o_pool, o_gla, o_swa, o_mem), w_mg_t, w_branch, l, tm=1024, tn=MERGE_TN)
    z = _matmul(y, w_out, l, F32, tm=2048, tn=256, name="out_proj", resid=x, alpha=alpha)
    return _layer_norm(z, ln_g[l], ln_b[l], tm=512, emit_bf16=emit_bf16)


def kernel(x, mem, positions, w_in, w_pool, pool_scale, w_gla_up, b_gla, gla_norm, sinks,
           w_mem_kv, w_branch, w_out, ln_g, ln_b):
    bsz, seq, d = x.shape
    depth = w_in.shape[0]
    alpha = (2 * depth) ** 0.25
    xf = x.reshape(bsz * seq, d)
    xb = xf.astype(BF16)
    mem_b = mem.reshape(bsz * mem.shape[1], d).astype(BF16)
    pos_f = positions.astype(F32).reshape(bsz * seq, 1)
    w_in_t = jnp.swapaxes(w_in, 1, 2)
    for l in range(depth):
        xf, xb = _hybrid_layer(
            l, xf, xb, mem_b, pos_f, w_in_t, w_pool, pool_scale, w_gla_up, b_gla,
            gla_norm, sinks, w_mem_kv, w_branch, w_out, ln_g, ln_b,
            alpha, l + 1 < depth, bsz, seq)
    return xf.reshape(bsz, seq, d)
```

```python
import functools
import math

import jax
import jax.numpy as jnp
import numpy as np
from jax import lax
from jax.experimental import pallas as pl
from jax.experimental.pallas import tpu as pltpu

F32 = jnp.float32
BF16 = jnp.bfloat16

N_BRANCH = 4
POOL_WINDOWS = (2, 4, 8, 16)
GLA_HEADS = 4
GLA_RANK = 16
GLA_TAU = 16.0
GLA_CHUNK = 64
GLA_SUB = 16
GLA_SAFE_LOG2 = 24.0
GLA_SAFE_KEY = 2.0 ** 100
SWA_HEAD_DIM = 64
SWA_GROUP = 8
WINDOW = 128
SWA_BLOCK = 128
ROPE_THETA = 500000.0
ROPE_DIM = SWA_HEAD_DIM // 4
XA_HEADS = 4
LN_EPS = 1e-5

LANES = 128
SUBLANES = 8
V7X_VMEM_BYTES = 64 * 1024 * 1024
VMEM_CAP = V7X_VMEM_BYTES - 8 * 1024 * 1024
VMEM_TEMP_BYTES = 12 * 1024 * 1024


def _vmem_limit(streamed_bytes, resident_bytes=0):
    return int(min(VMEM_CAP, 2 * streamed_bytes + resident_bytes + VMEM_TEMP_BYTES))


def _nbytes(shape, dtype):
    return int(np.prod(shape)) * jnp.dtype(dtype).itemsize


def _silu(g):
    return g * jax.nn.sigmoid(g)


def _div_pow2(x, n):
    assert n & (n - 1) == 0
    return x >> (n.bit_length() - 1)


def _mod_pow2(x, n):
    assert n & (n - 1) == 0
    return x & (n - 1)


NT_DIMS = (((1,), (1,)), ((), ()))


def _mm_kernel(a_ref, w_ref, o_ref):
    w = w_ref[...].astype(BF16)
    o_ref[...] = jnp.dot(a_ref[...], w, preferred_element_type=F32).astype(o_ref.dtype)


def _mm_resid_kernel(a_ref, w_ref, x_ref, o_ref, *, alpha):
    w = w_ref[...].astype(BF16)
    o_ref[...] = alpha * x_ref[...] + jnp.dot(a_ref[...], w, preferred_element_type=F32)


def _matmul(a, w, layer, out_dtype, tm, tn, name, resid=None, alpha=None):
    m, k = a.shape
    n = w.shape[-1]
    assert m % tm == 0 and n % tn == 0
    resident = _nbytes((tm, k), a.dtype)
    streamed = _nbytes((k, tn), w.dtype) + _nbytes((tm, tn), out_dtype)
    in_specs = [pl.BlockSpec((tm, k), lambda i, j: (i, 0), pipeline_mode=pl.Buffered(1)),
                pl.BlockSpec((None, k, tn), lambda i, j: (layer, 0, j))]
    args = [a, w]
    if resid is None:
        kern = _mm_kernel
    else:
        kern = functools.partial(_mm_resid_kernel, alpha=alpha)
        in_specs.append(pl.BlockSpec((tm, tn), lambda i, j: (i, j)))
        args.append(resid)
        streamed += _nbytes((tm, tn), resid.dtype)
    return pl.pallas_call(
        kern,
        out_shape=jax.ShapeDtypeStruct((m, n), out_dtype),
        grid=(m // tm, n // tn),
        in_specs=in_specs,
        out_specs=pl.BlockSpec((tm, tn), lambda i, j: (i, j)),
        compiler_params=pltpu.CompilerParams(
            dimension_semantics=("parallel", "arbitrary"),
            vmem_limit_bytes=_vmem_limit(streamed, resident)),
        name=name,
    )(*args)


def _in_proj_kernel(tab_ref, x_ref, w_ref, g_ref, o_ref, gs_ref):
    w = w_ref[0].astype(BF16)
    o_ref[...] = lax.dot_general(x_ref[...], w, NT_DIMS, preferred_element_type=F32)
    gs_ref[...] = g_ref[0].astype(gs_ref.dtype)


def _in_proj(xb, w_in_t, layer, col_starts, gate_row0, gate_tn, tm, tn):
    m, k = xb.shape
    nblk = len(col_starts)
    assert m % tm == 0 and all(int(s) % SUBLANES == 0 for s in col_starts)
    n_steps = (m // tm) * nblk
    sr = LANES
    gate_rows = N_BRANCH * k
    n_slabs = gate_rows // sr
    per_blk, per_branch = gate_tn // sr, k // sr
    assert gate_tn % sr == 0 and k % gate_tn == 0 and n_slabs <= n_steps and gate_row0 % SUBLANES == 0
    resident = _nbytes((tm, k), xb.dtype)
    streamed = (_nbytes((tn, k), w_in_t.dtype) + _nbytes((tm, tn), F32)
                + _nbytes((sr, k), w_in_t.dtype) + _nbytes((sr, k), BF16))

    def slab(i, j):
        return jnp.minimum(i * nblk + j, n_slabs - 1)

    def slab_dst(s):
        b, within = s // per_branch, s % per_branch
        return (within // per_blk) * (N_BRANCH * per_blk) + b * per_blk + within % per_blk

    return pl.pallas_call(
        _in_proj_kernel,
        out_shape=[jax.ShapeDtypeStruct((m, nblk * tn), F32),
                   jax.ShapeDtypeStruct((gate_rows, k), BF16)],
        grid_spec=pltpu.PrefetchScalarGridSpec(
            num_scalar_prefetch=1,
            grid=(m // tm, nblk),
            in_specs=[pl.BlockSpec((tm, k), lambda i, j, tab: (i, 0), pipeline_mode=pl.Buffered(1)),
                      pl.BlockSpec((pl.Element(1), pl.Element(tn), pl.Element(k)),
                                   lambda i, j, tab: (layer, pl.multiple_of(tab[j], SUBLANES), 0)),
                      pl.BlockSpec((pl.Element(1), pl.Element(sr), pl.Element(k)),
                                   lambda i, j, tab: (
                                       layer, pl.multiple_of(gate_row0 + slab(i, j) * sr, SUBLANES), 0))],
            out_specs=[pl.BlockSpec((tm, tn), lambda i, j, tab: (i, j)),
                       pl.BlockSpec((sr, k), lambda i, j, tab: (slab_dst(slab(i, j)), 0))]),
        compiler_params=pltpu.CompilerParams(
            dimension_semantics=("arbitrary", "arbitrary"),
            vmem_limit_bytes=_vmem_limit(streamed, resident)),
        name="in_proj",
    )(jnp.asarray(np.asarray(col_starts, np.int32)), xb, w_in_t, w_in_t)


def _merge_kernel(x_ref, o0, o1, o2, o3, g_ref, w_ref, y_ref):
    x = x_ref[...]
    tn = y_ref.shape[1]
    acc = None
    for b, o_ref in enumerate((o0, o1, o2, o3)):
        gate = lax.dot_general(x, g_ref[b * tn:(b + 1) * tn, :], NT_DIMS, preferred_element_type=F32)
        proj = jnp.dot(o_ref[...], w_ref[b].astype(BF16), preferred_element_type=F32)
        term = jax.nn.sigmoid(gate) * proj
        acc = term if acc is None else acc + term
    y_ref[...] = acc.astype(y_ref.dtype)


def _merge(xb, branches, w_mg_t, w_branch, layer, tm, tn):
    m, d = xb.shape
    bw = branches[0].shape[1]
    nj = d // tn
    in_specs = [pl.BlockSpec((tm, d), lambda i, j: (i, 0), pipeline_mode=pl.Buffered(1))]
    in_specs += [pl.BlockSpec((tm, bw), lambda i, j: (i, 0), pipeline_mode=pl.Buffered(1))
                 for _ in range(N_BRANCH)]
    in_specs += [pl.BlockSpec((N_BRANCH * tn, d), lambda i, j: (j, 0)),
                 pl.BlockSpec((None, N_BRANCH, bw, tn), lambda i, j: (layer, 0, 0, j))]
    resident = _nbytes((tm, d), BF16) + N_BRANCH * _nbytes((tm, bw), BF16)
    streamed = (N_BRANCH * (_nbytes((tn, d), BF16) + _nbytes((bw, tn), w_branch.dtype))
                + _nbytes((tm, tn), BF16))
    return pl.pallas_call(
        _merge_kernel,
        out_shape=jax.ShapeDtypeStruct((m, d), BF16),
        grid=(m // tm, nj),
        in_specs=in_specs,
        out_specs=pl.BlockSpec((tm, tn), lambda i, j: (i, j)),
        compiler_params=pltpu.CompilerParams(
            dimension_semantics=("parallel", "arbitrary"),
            vmem_limit_bytes=_vmem_limit(streamed, resident)),
        name="merge",
    )(xb, *branches, w_mg_t, w_branch)


def _ln_kernel(z_ref, g_ref, b_ref, o_ref, ob_ref=None):
    z = z_ref[...]
    mu = jnp.mean(z, axis=-1, keepdims=True)
    zc = z - mu
    var = jnp.mean(zc * zc, axis=-1, keepdims=True)
    y = zc * lax.rsqrt(var + LN_EPS) * g_ref[...] + b_ref[...]
    o_ref[...] = y
    if ob_ref is not None:
        ob_ref[...] = y.astype(BF16)


def _layer_norm(z, g, b, tm, emit_bf16):
    m, d = z.shape
    out_shape = [jax.ShapeDtypeStruct((m, d), F32)]
    out_specs = [pl.BlockSpec((tm, d), lambda i: (i, 0))]
    if emit_bf16:
        out_shape.append(jax.ShapeDtypeStruct((m, d), BF16))
        out_specs.append(pl.BlockSpec((tm, d), lambda i: (i, 0)))
    blocks = 2 * _nbytes((tm, d), F32) + _nbytes((tm, d), BF16)
    res = pl.pallas_call(
        _ln_kernel,
        out_shape=out_shape,
        grid=(m // tm,),
        in_specs=[pl.BlockSpec((tm, d), lambda i: (i, 0)),
                  pl.BlockSpec((1, d), lambda i: (0, 0)),
                  pl.BlockSpec((1, d), lambda i: (0, 0))],
        out_specs=out_specs,
        compiler_params=pltpu.CompilerParams(
            dimension_semantics=("parallel",), vmem_limit_bytes=_vmem_limit(blocks)),
        name="layer_norm",
    )(z, g.reshape(1, d), b.reshape(1, d))
    return res if emit_bf16 else (res[0], None)


POOL_HIST = max(POOL_WINDOWS)
POOL_PAD = 2 * POOL_HIST


def _pool_kernel(u_ref, g_ref, wp_ref, sc_ref, o_ref, buf, lvl_a, lvl_b, *, ts, group):
    i = pl.program_id(1)
    ext = POOL_PAD + ts

    @pl.when(i == 0)
    def _():
        buf[0:POOL_PAD, :] = jnp.zeros((POOL_PAD, buf.shape[1]), F32)

    buf[POOL_PAD:ext, :] = u_ref[...]
    src = buf
    level_refs = []
    for lvl, dst in enumerate((lvl_a, lvl_b, lvl_a, lvl_b)[:len(POOL_WINDOWS)], start=1):
        lag = 2 ** (lvl - 1)
        r0 = SUBLANES * lvl
        c0 = (lvl - 1) * group
        dst[r0:ext, c0:] = src[r0:ext, c0:] + src[r0 - lag:ext - lag, c0:]
        level_refs.append(dst)
        src = dst

    t = i * ts + lax.broadcasted_iota(jnp.int32, (ts, 1), 0)
    outs = []
    for gi, w in enumerate(POOL_WINDOWS):
        assert w == 2 ** (gi + 1)
        cols = slice(gi * group, (gi + 1) * group)
        acc = level_refs[gi][POOL_PAD:ext, cols]
        cnt = jnp.minimum(t + 1, w).astype(F32)
        pooled = acc / cnt - buf[POOL_PAD:ext, cols]
        outs.append(jnp.dot(pooled.astype(BF16), wp_ref[gi], preferred_element_type=F32))
    o = jnp.concatenate(outs, axis=1) * sc_ref[...] * _silu(g_ref[...])
    o_ref[...] = o.astype(o_ref.dtype)
    buf[POOL_PAD - POOL_HIST:POOL_PAD, :] = buf[ext - POOL_HIST:ext, :]


def _pool(h, w_pool_b, pool_scale, bsz, seq, col_u, col_g, ts):
    bw = pool_scale.shape[-1]
    group = bw // len(POOL_WINDOWS)
    nblk = seq // ts
    blocks = 2 * _nbytes((ts, bw), F32) + _nbytes((ts, bw), BF16)
    scratch = [pltpu.VMEM((ts + POOL_PAD, bw), F32) for _ in range(3)]
    resident = sum(_nbytes(sc.shape, sc.dtype) for sc in scratch)
    return pl.pallas_call(
        functools.partial(_pool_kernel, ts=ts, group=group),
        out_shape=jax.ShapeDtypeStruct((bsz * seq, bw), BF16),
        grid=(bsz, nblk),
        in_specs=[pl.BlockSpec((ts, bw), lambda b, i: (b * nblk + i, col_u // bw)),
                  pl.BlockSpec((ts, bw), lambda b, i: (b * nblk + i, col_g // bw)),
                  pl.BlockSpec(w_pool_b.shape, lambda b, i: (0, 0, 0)),
                  pl.BlockSpec((1, bw), lambda b, i: (0, 0))],
        out_specs=pl.BlockSpec((ts, bw), lambda b, i: (b * nblk + i, 0)),
        scratch_shapes=scratch,
        compiler_params=pltpu.CompilerParams(
            dimension_semantics=("parallel", "arbitrary"),
            vmem_limit_bytes=_vmem_limit(blocks, resident)),
        name="pool_mixer",
    )(h, h, w_pool_b, pool_scale.reshape(1, bw))


def _split_bf16(x, parts):
    out = []
    r = x
    for _ in range(parts):
        p = r.astype(BF16)
        out.append(p)
        r = r - p.astype(F32)
    return out


LOG2E = math.log2(math.e)


def _gla_cumsum_operator(rows):
    r = np.arange(rows)
    tri = ((r[:, None] // GLA_CHUNK == r[None, :] // GLA_CHUNK) & (r[None, :] <= r[:, None]))
    return jnp.asarray(tri, BF16)


def _gla_kernel(q_ref, k_ref, v_ref, gg_ref, lr_ref, wup_ref, bg_ref, nrm_ref, tri_ref,
                o_ref, state_ref, b_ref, *, rows, dk, dv):
    i = pl.program_id(1)
    C = GLA_CHUNK
    nsub = C // GLA_SUB

    @pl.when(i == 0)
    def _():
        state_ref[...] = jnp.zeros(state_ref.shape, F32)

    lr_h, lr_m = _split_bf16(lr_ref[...], 2)
    w_h, w_m = _split_bf16(wup_ref[...], 2)
    z = (jnp.dot(lr_h, w_h, preferred_element_type=F32)
         + jnp.dot(lr_h, w_m, preferred_element_type=F32)
         + jnp.dot(lr_m, w_h, preferred_element_type=F32)) + bg_ref[...]
    log2_a = (jnp.minimum(z, 0.0) - jnp.log(1.0 + jnp.exp(-jnp.abs(z)))) * (LOG2E / GLA_TAU)

    bsum = None
    for part in _split_bf16(log2_a, 3):
        t = jnp.dot(tri_ref[...], part, preferred_element_type=F32)
        bsum = t if bsum is None else bsum + t
    b_ref[...] = bsum

    row = lax.broadcasted_iota(jnp.int32, (C, 1), 0)
    col = lax.broadcasted_iota(jnp.int32, (1, C), 1)
    row_sub = _div_pow2(row, GLA_SUB)
    col_sub = _div_pow2(col, GLA_SUB)
    lane_c = lax.broadcasted_iota(jnp.int32, (GLA_SUB, C), 1)
    scale = dk ** -0.5

    def sub_heads(x):
        return jnp.concatenate(
            [jnp.broadcast_to(x[a * GLA_SUB:a * GLA_SUB + 1, :], (GLA_SUB, dk))
             for a in range(nsub)], axis=0)

    def chunk_body(c, carry, *, bounded):
        r0 = pl.multiple_of(c * C, C)
        for h in range(GLA_HEADS):
            kc = slice(h * dk, (h + 1) * dk)
            vc = slice(h * dv, (h + 1) * dv)
            qs = q_ref[pl.ds(r0, C), kc] * scale
            k = k_ref[pl.ds(r0, C), kc]
            v = v_ref[pl.ds(r0, C), vc].astype(BF16)
            b = b_ref[pl.ds(r0, C), kc]
            b_last = b[C - 1:C, :]
            st = state_ref[h]

            q_in = (qs * jnp.exp2(b)).astype(BF16)
            o = lax.dot_general(q_in, st.astype(BF16), NT_DIMS, preferred_element_type=F32)

            q_t = qs * jnp.exp2(b - sub_heads(b))
            lhs = jnp.concatenate(
                [jnp.where(row_sub == a, q_t, 0.0) for a in range(nsub)], axis=1).astype(BF16)
            key_exp = [b[a * GLA_SUB:a * GLA_SUB + 1, :] - b for a in range(nsub)]
            if not bounded:
                key_exp = [jnp.minimum(e, 0.0) for e in key_exp]
            rhs = jnp.concatenate([k * jnp.exp2(e) for e in key_exp], axis=1).astype(BF16)
            a_fac = lax.dot_general(lhs, rhs, NT_DIMS, preferred_element_type=F32)

            if bounded:
                attn = jnp.where(row >= col, a_fac, 0.0)
            else:
                diag_blocks = []
                for a in range(nsub):
                    rs = slice(a * GLA_SUB, (a + 1) * GLA_SUB)
                    qa = qs[rs, :]
                    ba = b[rs, :]
                    blk = jnp.zeros((GLA_SUB, C), F32)
                    for jj in range(GLA_SUB):
                        j = a * GLA_SUB + jj
                        tj = qa * k[j:j + 1, :] * jnp.exp2(ba - b[j:j + 1, :])
                        cj = jnp.sum(tj, axis=1, keepdims=True)
                        blk = jnp.where(lane_c == j, cj, blk)
                    diag_blocks.append(blk)
                a_diag = jnp.concatenate(diag_blocks, axis=0)
                attn = jnp.where(row_sub > col_sub, a_fac,
                                 jnp.where((row_sub == col_sub) & (row >= col), a_diag, 0.0))
            o = o + jnp.dot(attn.astype(BF16), v, preferred_element_type=F32)

            k_dec = (k * jnp.exp2(b_last - b)).astype(BF16)
            upd = lax.dot_general(v, k_dec, (((0,), (0,)), ((), ())),
                                  preferred_element_type=F32)
            state_ref[h] = st * jnp.exp2(b_last) + upd

            ms = jnp.mean(o * o, axis=-1, keepdims=True)
            on = o * lax.rsqrt(ms + LN_EPS) * nrm_ref[:, vc]
            o_ref[pl.ds(r0, C), vc] = (on * _silu(gg_ref[pl.ds(r0, C), vc])).astype(o_ref.dtype)
        return carry

    bounded = (-jnp.min(bsum) < GLA_SAFE_LOG2) & (jnp.max(jnp.abs(k_ref[...])) < GLA_SAFE_KEY)

    @pl.when(bounded)
    def _():
        lax.fori_loop(0, rows // C, functools.partial(chunk_body, bounded=True), 0)

    @pl.when(jnp.logical_not(bounded))
    def _():
        lax.fori_loop(0, rows // C, functools.partial(chunk_body, bounded=False), 0)


def _gla(h, w_up_pad, b_gla, gla_norm, bsz, seq, cols, rows):
    col_q, col_k, col_v, col_g, col_lr = cols
    dkt = b_gla.shape[-1]
    dvt = gla_norm.shape[-1]
    dk, dv = dkt // GLA_HEADS, dvt // GLA_HEADS
    lrw = w_up_pad.shape[0]
    nblk = seq // rows
    tri = _gla_cumsum_operator(rows)
    blocks = (2 * _nbytes((rows, dkt), F32) + 2 * _nbytes((rows, dvt), F32)
              + _nbytes((rows, lrw), F32) + _nbytes((rows, dvt), BF16))
    resident = (_nbytes((rows, dkt), F32) + _nbytes((GLA_HEADS, dv, dk), F32)
                + _nbytes(tri.shape, BF16))
    const = lambda b, i: (0, 0)
    return pl.pallas_call(
        functools.partial(_gla_kernel, rows=rows, dk=dk, dv=dv),
        out_shape=jax.ShapeDtypeStruct((bsz * seq, dvt), BF16),
        grid=(bsz, nblk),
        in_specs=[pl.BlockSpec((rows, dkt), lambda b, i: (b * nblk + i, col_q // dkt)),
                  pl.BlockSpec((rows, dkt), lambda b, i: (b * nblk + i, col_k // dkt)),
                  pl.BlockSpec((rows, dvt), lambda b, i: (b * nblk + i, col_v // dvt)),
                  pl.BlockSpec((rows, dvt), lambda b, i: (b * nblk + i, col_g // dvt)),
                  pl.BlockSpec((rows, lrw), lambda b, i: (b * nblk + i, col_lr // lrw)),
                  pl.BlockSpec((lrw, dkt), const),
                  pl.BlockSpec((1, dkt), const),
                  pl.BlockSpec((1, dvt), const),
                  pl.BlockSpec(tri.shape, const)],
        out_specs=pl.BlockSpec((rows, dvt), lambda b, i: (b * nblk + i, 0)),
        scratch_shapes=[pltpu.VMEM((GLA_HEADS, dv, dk), F32),
                        pltpu.VMEM((rows, dkt), F32)],
        compiler_params=pltpu.CompilerParams(
            dimension_semantics=("parallel", "arbitrary"),
            vmem_limit_bytes=_vmem_limit(blocks, resident)),
        name="gla_mixer",
    )(h, h, h, h, h, w_up_pad, b_gla.reshape(1, dkt), gla_norm.reshape(1, dvt), tri)


def _rope_lane_table():
    half = ROPE_DIM // 2
    inv_freq = ROPE_THETA ** (-np.arange(0, ROPE_DIM, 2, dtype=np.float32) / ROPE_DIM)
    lane = np.arange(LANES) % SWA_HEAD_DIM
    tab = np.where(lane < ROPE_DIM, inv_freq[lane % half], 0.0).astype(np.float32)
    return jnp.asarray(tab.reshape(1, LANES))


SWA_ROW_CHUNK = 32
SWA_BLOCKS_PER_STEP = 2


def _swa_kernel(sink_ref, q_ref, k_ref, v_ref, g_ref, pos_ref, inv_ref, o_ref, *scratch,
                n_pairs, n_sub):
    step = pl.program_id(1)
    for sb in range(n_sub):
        rows = pl.ds(sb * SWA_BLOCK, SWA_BLOCK)
        _swa_block(step * n_sub + sb, sink_ref, q_ref.at[rows], k_ref.at[rows], v_ref.at[rows],
                   g_ref.at[rows], pos_ref.at[rows], inv_ref, o_ref.at[rows], *scratch,
                   n_pairs=n_pairs)


def _swa_block(n, sink_ref, q_ref, k_ref, v_ref, g_ref, pos_ref, inv_ref, o_ref,
               wk_ref, wv_ref, qr_ref, s_ref, p_ref, es_ref, bias_ref, *, n_pairs):
    blk = SWA_BLOCK
    half = ROPE_DIM // 2
    hd = SWA_HEAD_DIM
    ppg = n_pairs // 2
    rc_rows = SWA_ROW_CHUNK

    @pl.when(n == 0)
    def _():
        wk_ref[...] = jnp.zeros(wk_ref.shape, BF16)
        row = lax.broadcasted_iota(jnp.int32, (4 * blk, 2 * LANES), 0)
        col = lax.broadcasted_iota(jnp.int32, (4 * blk, 2 * LANES), 1)
        ones = ((col >= LANES) & ((col >= LANES + hd) == (row >= 2 * blk))).astype(BF16)
        for g in range(2):
            wv_ref[g] = ones

    lane = lax.broadcasted_iota(jnp.int32, (1, LANES), 1)
    lane_h = _mod_pow2(lane, hd)
    lo = lane < hd
    ang = pos_ref[...] * inv_ref[...]
    cos = jnp.cos(ang)
    sin = jnp.sin(ang)

    def rope(x):
        x_up = pltpu.roll(x, LANES - half, axis=1)
        x_dn = pltpu.roll(x, half, axis=1)
        return x * cos + jnp.where(lane_h < half, -x_up, x_dn) * sin

    for g in range(2):
        for part in range(2):
            base = part * 2 * blk
            wk_ref[g, base:base + blk, :] = wk_ref[g, base + blk:base + 2 * blk, :]
            wv_ref[g, base:base + blk, 0:LANES] = wv_ref[g, base + blk:base + 2 * blk, 0:LANES]
    k_r = rope(k_ref[...])
    v_c = v_ref[...]
    k_sw = pltpu.roll(k_r, hd, axis=1)
    v_sw = pltpu.roll(v_c, hd, axis=1)
    zero = jnp.zeros_like(k_r)
    for g in range(2):
        k_lo, k_hi = (k_r, k_sw) if g == 0 else (k_sw, k_r)
        v_lo, v_hi = (v_c, v_sw) if g == 0 else (v_sw, v_c)
        wk_ref[g, blk:2 * blk, :] = jnp.where(lo, k_lo, zero).astype(BF16)
        wk_ref[g, 3 * blk:4 * blk, :] = jnp.where(lo, zero, k_hi).astype(BF16)
        wv_ref[g, blk:2 * blk, 0:LANES] = jnp.where(lo, v_lo, zero).astype(BF16)
        wv_ref[g, 3 * blk:4 * blk, 0:LANES] = jnp.where(lo, zero, v_hi).astype(BF16)

    r = lax.broadcasted_iota(jnp.int32, (blk, 2 * blk), 0)
    c = lax.broadcasted_iota(jnp.int32, (blk, 2 * blk), 1)
    valid = (c > r) & (c <= r + WINDOW) & ((n > 0) | (c >= blk))
    bias_ref[...] = jnp.where(valid, 0.0, -jnp.inf).astype(F32)

    qscale = hd ** -0.5
    for p in range(n_pairs):
        g, pp = divmod(p, ppg)
        qr_ref[g, pp * blk:(pp + 1) * blk, :] = (
            rope(q_ref[:, p * LANES:(p + 1) * LANES]) * qscale).astype(BF16)

    for g in range(2):
        s_ref[g] = lax.dot_general(qr_ref[g], wk_ref[g], NT_DIMS, preferred_element_type=F32)

    for g in range(2):
        for pp in range(ppg):
            for hh in range(2):
                sink = sink_ref[2 * (g * ppg + pp) + hh]
                kc = slice(hh * 2 * blk, (hh + 1) * 2 * blk)
                for rc in range(blk // rc_rows):
                    rows = slice(pp * blk + rc * rc_rows, pp * blk + (rc + 1) * rc_rows)
                    sh = s_ref[g, rows, kc] + bias_ref[rc * rc_rows:(rc + 1) * rc_rows, :]
                    m = jnp.maximum(jnp.max(sh, axis=-1, keepdims=True), sink)
                    p_ref[g, rows, kc] = jnp.exp(sh - m).astype(BF16)
                    es_ref[g, rows, hh * hd:(hh + 1) * hd] = jnp.broadcast_to(
                        jnp.exp(sink - m), (rc_rows, hd))

    for g in range(2):
        o2 = jnp.dot(p_ref[g], wv_ref[g], preferred_element_type=F32)
        for pp in range(ppg):
            rows = slice(pp * blk, (pp + 1) * blk)
            cols = slice((g * ppg + pp) * LANES, (g * ppg + pp + 1) * LANES)
            den = o2[rows, LANES:] + es_ref[g, rows, :]
            o_ref[:, cols] = (o2[rows, :LANES] / den * _silu(g_ref[:, cols])).astype(o_ref.dtype)


def _swa(h, pos_f, sinks, bsz, seq, cols):
    col_q, col_k, col_v, col_g = cols
    n_heads = sinks.shape[0]
    bw = n_heads * SWA_HEAD_DIM
    n_pairs = bw // LANES
    n_sub = SWA_BLOCKS_PER_STEP
    blk = n_sub * SWA_BLOCK
    nblk = seq // blk
    scratch = [pltpu.VMEM((2, 4 * SWA_BLOCK, LANES), BF16),
               pltpu.VMEM((2, 4 * SWA_BLOCK, 2 * LANES), BF16),
               pltpu.VMEM((2, 4 * SWA_BLOCK, LANES), BF16),
               pltpu.VMEM((2, 4 * SWA_BLOCK, 4 * SWA_BLOCK), F32),
               pltpu.VMEM((2, 4 * SWA_BLOCK, 4 * SWA_BLOCK), BF16),
               pltpu.VMEM((2, 4 * SWA_BLOCK, LANES), F32),
               pltpu.VMEM((SWA_BLOCK, 2 * SWA_BLOCK), F32)]
    blocks = (2 * _nbytes((blk, bw), F32) + 3 * _nbytes((blk, LANES), F32) + _nbytes((blk, bw), BF16))
    resident = sum(_nbytes(sc.shape, sc.dtype) for sc in scratch)
    return pl.pallas_call(
        functools.partial(_swa_kernel, n_pairs=n_pairs, n_sub=n_sub),
        out_shape=jax.ShapeDtypeStruct((bsz * seq, bw), BF16),
        grid=(bsz, nblk),
        in_specs=[pl.BlockSpec(memory_space=pltpu.SMEM),
                  pl.BlockSpec((blk, bw), lambda b, i: (b * nblk + i, col_q // bw)),
                  pl.BlockSpec((blk, LANES), lambda b, i: (b * nblk + i, col_k // LANES)),
                  pl.BlockSpec((blk, LANES), lambda b, i: (b * nblk + i, col_v // LANES)),
                  pl.BlockSpec((blk, bw), lambda b, i: (b * nblk + i, col_g // bw)),
                  pl.BlockSpec((blk, 1), lambda b, i: (b * nblk + i, 0)),
                  pl.BlockSpec((1, LANES), lambda b, i: (0, 0))],
        out_specs=pl.BlockSpec((blk, bw), lambda b, i: (b * nblk + i, 0)),
        scratch_shapes=scratch,
        compiler_params=pltpu.CompilerParams(
            dimension_semantics=("parallel", "arbitrary"),
            vmem_limit_bytes=_vmem_limit(blocks, resident)),
        name="swa_mixer",
    )(sinks, h, h, h, h, pos_f, _rope_lane_table())


def _mem_kernel(q_ref, g_ref, mk_ref, mv_ref, o_ref, *, hd):
    scale = hd ** -0.5
    for h in range(XA_HEADS):
        cols = slice(h * hd, (h + 1) * hd)
        qh = q_ref[:, cols].astype(BF16)
        s = lax.dot_general(qh, mk_ref[:, cols], (((1,), (1,)), ((), ())),
                            preferred_element_type=F32) * scale
        m = jnp.max(s, axis=-1, keepdims=True)
        e = jnp.exp(s - m)
        p = e / jnp.sum(e, axis=-1, keepdims=True)
        o = jnp.dot(p.astype(BF16), mv_ref[:, cols], preferred_element_type=F32)
        o_ref[:, cols] = (o * _silu(g_ref[:, cols])).astype(o_ref.dtype)


def _mem_attn(h, mkv, bsz, seq, mem_len, cols, ts):
    col_q, col_g = cols
    bw = mkv.shape[1] // 2
    hd = bw // XA_HEADS
    nblk = seq // ts
    blocks = 2 * _nbytes((ts, bw), F32) + 2 * _nbytes((mem_len, bw), BF16) + _nbytes((ts, bw), BF16)
    return pl.pallas_call(
        functools.partial(_mem_kernel, hd=hd),
        out_shape=jax.ShapeDtypeStruct((bsz * seq, bw), BF16),
        grid=(bsz, nblk),
        in_specs=[pl.BlockSpec((ts, bw), lambda b, i: (b * nblk + i, col_q // bw)),
                  pl.BlockSpec((ts, bw), lambda b, i: (b * nblk + i, col_g // bw)),
                  pl.BlockSpec((mem_len, bw), lambda b, i: (b, 0)),
                  pl.BlockSpec((mem_len, bw), lambda b, i: (b, 1))],
        out_specs=pl.BlockSpec((ts, bw), lambda b, i: (b * nblk + i, 0)),
        compiler_params=pltpu.CompilerParams(
            dimension_semantics=("parallel", "parallel"), vmem_limit_bytes=_vmem_limit(blocks)),
        name="mem_attn",
    )(h, h, mkv, mkv)


IN_TN = 2 * LANES
MERGE_TN = 2 * LANES


def _main_layout(d_model):
    bw = d_model // 4
    dkt, dvt = bw // 2, bw
    kvw = 2 * SWA_HEAD_DIM
    names = ["pool_u", "pool_g", "gq", "gk", "gv", "gg", "glr", "sq", "sk", "sv", "sg", "xq", "xg"]
    widths = [bw, bw, dkt, dkt, dvt, dvt, GLA_RANK, bw, kvw, kvw, bw, bw, bw]
    src = {}
    off = 0
    for nme, w in zip(names, widths):
        src[nme] = (off, w)
        off += w
    n_main = off
    order = ["pool_u", "pool_g", "gv", "gg", "sq", "sg", "xq", "xg", "gq", "gk", "sk", "sv", "glr"]
    dst = {}
    cols = []
    off = 0
    for nme in order:
        o, w = src[nme]
        w_dst = w if nme != "glr" else IN_TN
        assert off % w_dst == 0
        dst[nme] = off
        cols.append(o + np.arange(w_dst))
        off += w_dst
    cols = np.concatenate(cols)
    assert off % IN_TN == 0 and cols.max() < n_main
    starts = cols[::IN_TN]
    assert np.all(cols.reshape(-1, IN_TN) == starts[:, None] + np.arange(IN_TN))
    return starts, dst, n_main


def _hybrid_layer(l, x, xb, mem_b, pos_f, w_in_t, w_pool, pool_scale, w_gla_up, b_gla, gla_norm,
                  sinks, w_mem_kv, w_branch, w_out, ln_g, ln_b, alpha, emit_bf16, bsz, seq):
    m, d = x.shape
    col_starts, dst, n_main = _main_layout(d)
    w_up_pad = jnp.concatenate(
        [w_gla_up[l], jnp.zeros((IN_TN - GLA_RANK, w_gla_up.shape[-1]), w_gla_up.dtype)], axis=0)

    h, w_mg_t = _in_proj(xb, w_in_t, l, col_starts, n_main, MERGE_TN, tm=2048, tn=IN_TN)
    mkv = _matmul(mem_b, w_mem_kv, l, BF16, tm=mem_b.shape[0], tn=512, name="mem_kv")

    o_pool = _pool(h, w_pool[l].astype(BF16), pool_scale[l], bsz, seq,
                   dst["pool_u"], dst["pool_g"], ts=512)
    o_gla = _gla(h, w_up_pad, b_gla[l], gla_norm[l], bsz, seq,
                 (dst["gq"], dst["gk"], dst["gv"], dst["gg"], dst["glr"]), rows=512)
    o_swa = _swa(h, pos_f, sinks[l], bsz, seq, (dst["sq"], dst["sk"], dst["sv"], dst["sg"]))
    o_mem = _mem_attn(h, mkv, bsz, seq, mem_b.shape[0] // bsz, (dst["xq"], dst["xg"]), ts=512)

    y = _merge(xb, (o_pool, o_gla, o_swa, o_mem), w_mg_t, w_branch, l, tm=1024, tn=MERGE_TN)
    z = _matmul(y, w_out, l, F32, tm=2048, tn=256, name="out_proj", resid=x, alpha=alpha)
    return _layer_norm(z, ln_g[l], ln_b[l], tm=512, emit_bf16=emit_bf16)


def kernel(x, mem, positions, w_in, w_pool, pool_scale, w_gla_up, b_gla, gla_norm, sinks,
           w_mem_kv, w_branch, w_out, ln_g, ln_b):
    bsz, seq, d = x.shape
    depth = w_in.shape[0]
    alpha = (2 * depth) ** 0.25
    xf = x.reshape(bsz * seq, d)
    xb = xf.astype(BF16)
    mem_b = mem.reshape(bsz * mem.shape[1], d).astype(BF16)
    pos_f = positions.astype(F32).reshape(bsz * seq, 1)
    w_in_t = jnp.swapaxes(w_in, 1, 2)
    for l in range(depth):
        xf, xb = _hybrid_layer(
            l, xf, xb, mem_b, pos_f, w_in_t, w_pool, pool_scale, w_gla_up, b_gla,
            gla_norm, sinks, w_mem_kv, w_branch, w_out, ln_g, ln_b,
            alpha, l + 1 < depth, bsz, seq)
    return xf.reshape(bsz, seq, d)
```

```python
import functools
import math

import jax
import jax.numpy as jnp
import numpy as np
from jax import lax
from jax.experimental import pallas as pl
from jax.experimental.pallas import tpu as pltpu

F32 = jnp.float32
BF16 = jnp.bfloat16

N_BRANCH = 4
POOL_WINDOWS = (2, 4, 8, 16)
GLA_HEADS = 4
GLA_RANK = 16
GLA_TAU = 16.0
GLA_CHUNK = 64
GLA_SUB = 16
GLA_SAFE_LOG2 = 24.0
GLA_SAFE_KEY = 2.0 ** 100
SWA_HEAD_DIM = 64
SWA_GROUP = 8
WINDOW = 128
SWA_BLOCK = 128
ROPE_THETA = 500000.0
ROPE_DIM = SWA_HEAD_DIM // 4
XA_HEADS = 4
LN_EPS = 1e-5

LANES = 128
SUBLANES = 8
V7X_VMEM_BYTES = 64 * 1024 * 1024
VMEM_CAP = V7X_VMEM_BYTES - 8 * 1024 * 1024
VMEM_TEMP_BYTES = 12 * 1024 * 1024


def _vmem_limit(streamed_bytes, resident_bytes=0):
    return int(min(VMEM_CAP, 2 * streamed_bytes + resident_bytes + VMEM_TEMP_BYTES))


def _nbytes(shape, dtype):
    return int(np.prod(shape)) * jnp.dtype(dtype).itemsize


def _silu(g):
    return g * jax.nn.sigmoid(g)


def _div_pow2(x, n):
    assert n & (n - 1) == 0
    return x >> (n.bit_length() - 1)


def _mod_pow2(x, n):
    assert n & (n - 1) == 0
    return x & (n - 1)


NT_DIMS = (((1,), (1,)), ((), ()))


def _mm_kernel(a_ref, w_ref, o_ref):
    w = w_ref[...].astype(BF16)
    o_ref[...] = jnp.dot(a_ref[...], w, preferred_element_type=F32).astype(o_ref.dtype)


def _mm_resid_kernel(a_ref, w_ref, x_ref, o_ref, *, alpha):
    w = w_ref[...].astype(BF16)
    o_ref[...] = alpha * x_ref[...] + jnp.dot(a_ref[...], w, preferred_element_type=F32)


def _matmul(a, w, layer, out_dtype, tm, tn, name, resid=None, alpha=None):
    m, k = a.shape
    n = w.shape[-1]
    assert m % tm == 0 and n % tn == 0
    resident = _nbytes((tm, k), a.dtype)
    streamed = _nbytes((k, tn), w.dtype) + _nbytes((tm, tn), out_dtype)
    in_specs = [pl.BlockSpec((tm, k), lambda i, j: (i, 0), pipeline_mode=pl.Buffered(1)),
                pl.BlockSpec((None, k, tn), lambda i, j: (layer, 0, j))]
    args = [a, w]
    if resid is None:
        kern = _mm_kernel
    else:
        kern = functools.partial(_mm_resid_kernel, alpha=alpha)
        in_specs.append(pl.BlockSpec((tm, tn), lambda i, j: (i, j)))
        args.append(resid)
        streamed += _nbytes((tm, tn), resid.dtype)
    return pl.pallas_call(
        kern,
        out_shape=jax.ShapeDtypeStruct((m, n), out_dtype),
        grid=(m // tm, n // tn),
        in_specs=in_specs,
        out_specs=pl.BlockSpec((tm, tn), lambda i, j: (i, j)),
        compiler_params=pltpu.CompilerParams(
            dimension_semantics=("parallel", "arbitrary"),
            vmem_limit_bytes=_vmem_limit(streamed, resident)),
        name=name,
    )(*args)


def _in_proj_kernel(tab_ref, x_ref, w_ref, g_ref, o_ref, gs_ref):
    w = w_ref[0].astype(BF16)
    o_ref[...] = lax.dot_general(x_ref[...], w, NT_DIMS, preferred_element_type=F32)
    gs_ref[...] = g_ref[0].astype(gs_ref.dtype)


def _in_proj(xb, w_in_t, layer, col_starts, gate_row0, gate_tn, tm, tn):
    m, k = xb.shape
    nblk = len(col_starts)
    assert m % tm == 0 and all(int(s) % SUBLANES == 0 for s in col_starts)
    n_steps = (m // tm) * nblk
    sr = LANES
    gate_rows = N_BRANCH * k
    n_slabs = gate_rows // sr
    per_blk, per_branch = gate_tn // sr, k // sr
    assert gate_tn % sr == 0 and k % gate_tn == 0 and n_slabs <= n_steps and gate_row0 % SUBLANES == 0
    resident = _nbytes((tm, k), xb.dtype)
    streamed = (_nbytes((tn, k), w_in_t.dtype) + _nbytes((tm, tn), F32)
                + _nbytes((sr, k), w_in_t.dtype) + _nbytes((sr, k), BF16))

    def slab(i, j):
        return jnp.minimum(i * nblk + j, n_slabs - 1)

    def slab_dst(s):
        b, within = s // per_branch, s % per_branch
        return (within // per_blk) * (N_BRANCH * per_blk) + b * per_blk + within % per_blk

    return pl.pallas_call(
        _in_proj_kernel,
        out_shape=[jax.ShapeDtypeStruct((m, nblk * tn), F32),
                   jax.ShapeDtypeStruct((gate_rows, k), BF16)],
        grid_spec=pltpu.PrefetchScalarGridSpec(
            num_scalar_prefetch=1,
            grid=(m // tm, nblk),
            in_specs=[pl.BlockSpec((tm, k), lambda i, j, tab: (i, 0), pipeline_mode=pl.Buffered(1)),
                      pl.BlockSpec((pl.Element(1), pl.Element(tn), pl.Element(k)),
                                   lambda i, j, tab: (layer, pl.multiple_of(tab[j], SUBLANES), 0)),
                      pl.BlockSpec((pl.Element(1), pl.Element(sr), pl.Element(k)),
                                   lambda i, j, tab: (
                                       layer, pl.multiple_of(gate_row0 + slab(i, j) * sr, SUBLANES), 0))],
            out_specs=[pl.BlockSpec((tm, tn), lambda i, j, tab: (i, j)),
                       pl.BlockSpec((sr, k), lambda i, j, tab: (slab_dst(slab(i, j)), 0))]),
        compiler_params=pltpu.CompilerParams(
            dimension_semantics=("arbitrary", "arbitrary"),
            vmem_limit_bytes=_vmem_limit(streamed, resident)),
        name="in_proj",
    )(jnp.asarray(np.asarray(col_starts, np.int32)), xb, w_in_t, w_in_t)


def _merge_kernel(x_ref, o0, o1, o2, o3, g_ref, w_ref, wo_ref, y_ref, wos_ref):
    wos_ref[...] = wo_ref[...].astype(wos_ref.dtype)
    x = x_ref[...]
    tn = y_ref.shape[1]
    acc = None
    for b, o_ref in enumerate((o0, o1, o2, o3)):
        gate = lax.dot_general(x, g_ref[b * tn:(b + 1) * tn, :], NT_DIMS, preferred_element_type=F32)
        proj = jnp.dot(o_ref[...], w_ref[b].astype(BF16), preferred_element_type=F32)
        term = jax.nn.sigmoid(gate) * proj
        acc = term if acc is None else acc + term
    y_ref[...] = acc.astype(y_ref.dtype)


def _merge(xb, branches, w_mg_t, w_branch, w_out, layer, tm, tn):
    m, d = xb.shape
    bw = branches[0].shape[1]
    nj = d // tn
    n_steps = (m // tm) * nj
    sr = d // n_steps
    assert d % n_steps == 0 and sr % (2 * SUBLANES) == 0
    in_specs = [pl.BlockSpec((tm, d), lambda i, j: (i, 0), pipeline_mode=pl.Buffered(1))]
    in_specs += [pl.BlockSpec((tm, bw), lambda i, j: (i, 0), pipeline_mode=pl.Buffered(1))
                 for _ in range(N_BRANCH)]
    in_specs += [pl.BlockSpec((N_BRANCH * tn, d), lambda i, j: (j, 0)),
                 pl.BlockSpec((None, N_BRANCH, bw, tn), lambda i, j: (layer, 0, 0, j)),
                 pl.BlockSpec((None, sr, d), lambda i, j: (layer, i * nj + j, 0))]
    resident = _nbytes((tm, d), BF16) + N_BRANCH * _nbytes((tm, bw), BF16)
    streamed = (N_BRANCH * (_nbytes((tn, d), BF16) + _nbytes((bw, tn), w_branch.dtype))
                + _nbytes((tm, tn), BF16) + _nbytes((sr, d), w_out.dtype) + _nbytes((sr, d), BF16))
    return pl.pallas_call(
        _merge_kernel,
        out_shape=[jax.ShapeDtypeStruct((m, d), BF16), jax.ShapeDtypeStruct((d, d), BF16)],
        grid=(m // tm, nj),
        in_specs=in_specs,
        out_specs=[pl.BlockSpec((tm, tn), lambda i, j: (i, j)),
                   pl.BlockSpec((sr, d), lambda i, j: (i * nj + j, 0))],
        compiler_params=pltpu.CompilerParams(
            dimension_semantics=("parallel", "arbitrary"),
            vmem_limit_bytes=_vmem_limit(streamed, resident)),
        name="merge",
    )(xb, *branches, w_mg_t, w_branch, w_out)


def _out_ln_kernel(y_ref, w_ref, x_ref, g_ref, b_ref, *refs, alpha, nj, tn, emit_bf16):
    if emit_bf16:
        o_ref, ob_ref, z_ref, mu_ref, rs_ref = refs
    else:
        o_ref, z_ref, mu_ref, rs_ref = refs
    j = pl.program_id(1)

    @pl.when(j < nj)
    def _():
        col = pl.multiple_of(j * tn, tn)
        z_ref[:, pl.ds(col, tn)] = alpha * x_ref[...] + jnp.dot(
            y_ref[...], w_ref[...], preferred_element_type=F32)

    @pl.when(j == nj)
    def _():
        z = z_ref[...]
        mu = jnp.mean(z, axis=-1, keepdims=True)
        zc = z - mu
        var = jnp.mean(zc * zc, axis=-1, keepdims=True)
        mu_ref[...] = mu
        rs_ref[...] = lax.rsqrt(var + LN_EPS)

    @pl.when(j >= nj)
    def _():
        tc = o_ref.shape[1]
        col = pl.multiple_of((j - nj) * tc, tc)
        out = (z_ref[:, pl.ds(col, tc)] - mu_ref[...]) * rs_ref[...] * g_ref[...] + b_ref[...]
        o_ref[...] = out
        if emit_bf16:
            ob_ref[...] = out.astype(BF16)


def _out_proj_ln(y, w_out_b, x, g, b, alpha, tm, tn, tc, emit_bf16):
    m, d = x.shape
    nj, nc = d // tn, d // tc
    assert m % tm == 0 and d % tn == 0 and d % tc == 0
    last = nj - 1
    out_shape = [jax.ShapeDtypeStruct((m, d), F32)]
    out_specs = [pl.BlockSpec((tm, tc), lambda i, j: (i, jnp.maximum(j - nj, 0)))]
    streamed = (_nbytes((d, tn), BF16) + _nbytes((tm, tn), F32) + _nbytes((tm, tc), F32)
                + 2 * _nbytes((1, tc), F32))
    if emit_bf16:
        out_shape.append(jax.ShapeDtypeStruct((m, d), BF16))
        out_specs.append(pl.BlockSpec((tm, tc), lambda i, j: (i, jnp.maximum(j - nj, 0))))
        streamed += _nbytes((tm, tc), BF16)
    resident = _nbytes((tm, d), BF16) + _nbytes((tm, d), F32) + 2 * _nbytes((tm, LANES), F32)
    res = pl.pallas_call(
        functools.partial(_out_ln_kernel, alpha=alpha, nj=nj, tn=tn, emit_bf16=emit_bf16),
        out_shape=out_shape,
        grid=(m // tm, nj + nc),
        in_specs=[pl.BlockSpec((tm, d), lambda i, j: (i, 0), pipeline_mode=pl.Buffered(1)),
                  pl.BlockSpec((d, tn), lambda i, j: (0, jnp.minimum(j, last))),
                  pl.BlockSpec((tm, tn), lambda i, j: (i, jnp.minimum(j, last))),
                  pl.BlockSpec((1, tc), lambda i, j: (0, jnp.maximum(j - nj, 0))),
                  pl.BlockSpec((1, tc), lambda i, j: (0, jnp.maximum(j - nj, 0)))],
        out_specs=out_specs,
        scratch_shapes=[pltpu.VMEM((tm, d), F32), pltpu.VMEM((tm, 1), F32), pltpu.VMEM((tm, 1), F32)],
        compiler_params=pltpu.CompilerParams(
            dimension_semantics=("parallel", "arbitrary"),
            vmem_limit_bytes=_vmem_limit(streamed, resident)),
        name="out_proj_ln",
    )(y, w_out_b, x, g.reshape(1, d), b.reshape(1, d))
    return res if emit_bf16 else (res[0], None)


POOL_HIST = max(POOL_WINDOWS)
POOL_PAD = 2 * POOL_HIST


def _pool_kernel(u_ref, g_ref, wp_ref, sc_ref, o_ref, buf, lvl_a, lvl_b, *, ts, group):
    i = pl.program_id(1)
    ext = POOL_PAD + ts

    @pl.when(i == 0)
    def _():
        buf[0:POOL_PAD, :] = jnp.zeros((POOL_PAD, buf.shape[1]), F32)

    buf[POOL_PAD:ext, :] = u_ref[...]
    src = buf
    level_refs = []
    for lvl, dst in enumerate((lvl_a, lvl_b, lvl_a, lvl_b)[:len(POOL_WINDOWS)], start=1):
        lag = 2 ** (lvl - 1)
        r0 = SUBLANES * lvl
        c0 = (lvl - 1) * group
        dst[r0:ext, c0:] = src[r0:ext, c0:] + src[r0 - lag:ext - lag, c0:]
        level_refs.append(dst)
        src = dst

    t = i * ts + lax.broadcasted_iota(jnp.int32, (ts, 1), 0)
    outs = []
    for gi, w in enumerate(POOL_WINDOWS):
        assert w == 2 ** (gi + 1)
        cols = slice(gi * group, (gi + 1) * group)
        acc = level_refs[gi][POOL_PAD:ext, cols]
        cnt = jnp.minimum(t + 1, w).astype(F32)
        pooled = acc / cnt - buf[POOL_PAD:ext, cols]
        outs.append(jnp.dot(pooled.astype(BF16), wp_ref[gi], preferred_element_type=F32))
    o = jnp.concatenate(outs, axis=1) * sc_ref[...] * _silu(g_ref[...])
    o_ref[...] = o.astype(o_ref.dtype)
    buf[POOL_PAD - POOL_HIST:POOL_PAD, :] = buf[ext - POOL_HIST:ext, :]


def _pool(h, w_pool_b, pool_scale, bsz, seq, col_u, col_g, ts):
    bw = pool_scale.shape[-1]
    group = bw // len(POOL_WINDOWS)
    nblk = seq // ts
    blocks = 2 * _nbytes((ts, bw), F32) + _nbytes((ts, bw), BF16)
    scratch = [pltpu.VMEM((ts + POOL_PAD, bw), F32) for _ in range(3)]
    resident = sum(_nbytes(sc.shape, sc.dtype) for sc in scratch)
    return pl.pallas_call(
        functools.partial(_pool_kernel, ts=ts, group=group),
        out_shape=jax.ShapeDtypeStruct((bsz * seq, bw), BF16),
        grid=(bsz, nblk),
        in_specs=[pl.BlockSpec((ts, bw), lambda b, i: (b * nblk + i, col_u // bw)),
                  pl.BlockSpec((ts, bw), lambda b, i: (b * nblk + i, col_g // bw)),
                  pl.BlockSpec(w_pool_b.shape, lambda b, i: (0, 0, 0)),
                  pl.BlockSpec((1, bw), lambda b, i: (0, 0))],
        out_specs=pl.BlockSpec((ts, bw), lambda b, i: (b * nblk + i, 0)),
        scratch_shapes=scratch,
        compiler_params=pltpu.CompilerParams(
            dimension_semantics=("parallel", "arbitrary"),
            vmem_limit_bytes=_vmem_limit(blocks, resident)),
        name="pool_mixer",
    )(h, h, w_pool_b, pool_scale.reshape(1, bw))


def _split_bf16(x, parts):
    out = []
    r = x
    for _ in range(parts):
        p = r.astype(BF16)
        out.append(p)
        r = r - p.astype(F32)
    return out


LOG2E = math.log2(math.e)


def _gla_cumsum_operator(rows):
    r = np.arange(rows)
    tri = ((r[:, None] // GLA_CHUNK == r[None, :] // GLA_CHUNK) & (r[None, :] <= r[:, None]))
    return jnp.asarray(tri, BF16)


def _gla_kernel(q_ref, k_ref, v_ref, gg_ref, lr_ref, wup_ref, bg_ref, nrm_ref, tri_ref,
                o_ref, state_ref, b_ref, *, rows, dk, dv):
    i = pl.program_id(1)
    C = GLA_CHUNK
    nsub = C // GLA_SUB

    @pl.when(i == 0)
    def _():
        state_ref[...] = jnp.zeros(state_ref.shape, F32)

    lr_h, lr_m = _split_bf16(lr_ref[...], 2)
    w_h, w_m = _split_bf16(wup_ref[...], 2)
    z = (jnp.dot(lr_h, w_h, preferred_element_type=F32)
         + jnp.dot(lr_h, w_m, preferred_element_type=F32)
         + jnp.dot(lr_m, w_h, preferred_element_type=F32)) + bg_ref[...]
    log2_a = (jnp.minimum(z, 0.0) - jnp.log(1.0 + jnp.exp(-jnp.abs(z)))) * (LOG2E / GLA_TAU)

    bsum = None
    for part in _split_bf16(log2_a, 3):
        t = jnp.dot(tri_ref[...], part, preferred_element_type=F32)
        bsum = t if bsum is None else bsum + t
    b_ref[...] = bsum

    row = lax.broadcasted_iota(jnp.int32, (C, 1), 0)
    col = lax.broadcasted_iota(jnp.int32, (1, C), 1)
    row_sub = _div_pow2(row, GLA_SUB)
    col_sub = _div_pow2(col, GLA_SUB)
    lane_c = lax.broadcasted_iota(jnp.int32, (GLA_SUB, C), 1)
    scale = dk ** -0.5

    def sub_heads(x):
        return jnp.concatenate(
            [jnp.broadcast_to(x[a * GLA_SUB:a * GLA_SUB + 1, :], (GLA_SUB, dk))
             for a in range(nsub)], axis=0)

    def chunk_body(c, carry, *, bounded):
        r0 = pl.multiple_of(c * C, C)
        for h in range(GLA_HEADS):
            kc = slice(h * dk, (h + 1) * dk)
            vc = slice(h * dv, (h + 1) * dv)
            qs = q_ref[pl.ds(r0, C), kc] * scale
            k = k_ref[pl.ds(r0, C), kc]
            v = v_ref[pl.ds(r0, C), vc].astype(BF16)
            b = b_ref[pl.ds(r0, C), kc]
            b_last = b[C - 1:C, :]
            st = state_ref[h]

            q_in = (qs * jnp.exp2(b)).astype(BF16)
            o = lax.dot_general(q_in, st.astype(BF16), NT_DIMS, preferred_element_type=F32)

            q_t = qs * jnp.exp2(b - sub_heads(b))
            lhs = jnp.concatenate(
                [jnp.where(row_sub == a, q_t, 0.0) for a in range(nsub)], axis=1).astype(BF16)
            key_exp = [b[a * GLA_SUB:a * GLA_SUB + 1, :] - b for a in range(nsub)]
            if not bounded:
                key_exp = [jnp.minimum(e, 0.0) for e in key_exp]
            rhs = jnp.concatenate([k * jnp.exp2(e) for e in key_exp], axis=1).astype(BF16)
            a_fac = lax.dot_general(lhs, rhs, NT_DIMS, preferred_element_type=F32)

            if bounded:
                attn = jnp.where(row >= col, a_fac, 0.0)
            else:
                diag_blocks = []
                for a in range(nsub):
                    rs = slice(a * GLA_SUB, (a + 1) * GLA_SUB)
                    qa = qs[rs, :]
                    ba = b[rs, :]
                    blk = jnp.zeros((GLA_SUB, C), F32)
                    for jj in range(GLA_SUB):
                        j = a * GLA_SUB + jj
                        tj = qa * k[j:j + 1, :] * jnp.exp2(ba - b[j:j + 1, :])
                        cj = jnp.sum(tj, axis=1, keepdims=True)
                        blk = jnp.where(lane_c == j, cj, blk)
                    diag_blocks.append(blk)
                a_diag = jnp.concatenate(diag_blocks, axis=0)
                attn = jnp.where(row_sub > col_sub, a_fac,
                                 jnp.where((row_sub == col_sub) & (row >= col), a_diag, 0.0))
            o = o + jnp.dot(attn.astype(BF16), v, preferred_element_type=F32)

            k_dec = (k * jnp.exp2(b_last - b)).astype(BF16)
            upd = lax.dot_general(v, k_dec, (((0,), (0,)), ((), ())),
                                  preferred_element_type=F32)
            state_ref[h] = st * jnp.exp2(b_last) + upd

            ms = jnp.mean(o * o, axis=-1, keepdims=True)
            on = o * lax.rsqrt(ms + LN_EPS) * nrm_ref[:, vc]
            o_ref[pl.ds(r0, C), vc] = (on * _silu(gg_ref[pl.ds(r0, C), vc])).astype(o_ref.dtype)
        return carry

    bounded = (-jnp.min(bsum) < GLA_SAFE_LOG2) & (jnp.max(jnp.abs(k_ref[...])) < GLA_SAFE_KEY)

    @pl.when(bounded)
    def _():
        lax.fori_loop(0, rows // C, functools.partial(chunk_body, bounded=True), 0)

    @pl.when(jnp.logical_not(bounded))
    def _():
        lax.fori_loop(0, rows // C, functools.partial(chunk_body, bounded=False), 0)


def _gla(h, w_up_pad, b_gla, gla_norm, bsz, seq, cols, rows):
    col_q, col_k, col_v, col_g, col_lr = cols
    dkt = b_gla.shape[-1]
    dvt = gla_norm.shape[-1]
    dk, dv = dkt // GLA_HEADS, dvt // GLA_HEADS
    lrw = w_up_pad.shape[0]
    nblk = seq // rows
    tri = _gla_cumsum_operator(rows)
    blocks = (2 * _nbytes((rows, dkt), F32) + 2 * _nbytes((rows, dvt), F32)
              + _nbytes((rows, lrw), F32) + _nbytes((rows, dvt), BF16))
    resident = (_nbytes((rows, dkt), F32) + _nbytes((GLA_HEADS, dv, dk), F32)
                + _nbytes(tri.shape, BF16))
    const = lambda b, i: (0, 0)
    return pl.pallas_call(
        functools.partial(_gla_kernel, rows=rows, dk=dk, dv=dv),
        out_shape=jax.ShapeDtypeStruct((bsz * seq, dvt), BF16),
        grid=(bsz, nblk),
        in_specs=[pl.BlockSpec((rows, dkt), lambda b, i: (b * nblk + i, col_q // dkt)),
                  pl.BlockSpec((rows, dkt), lambda b, i: (b * nblk + i, col_k // dkt)),
                  pl.BlockSpec((rows, dvt), lambda b, i: (b * nblk + i, col_v // dvt)),
                  pl.BlockSpec((rows, dvt), lambda b, i: (b * nblk + i, col_g // dvt)),
                  pl.BlockSpec((rows, lrw), lambda b, i: (b * nblk + i, col_lr // lrw)),
                  pl.BlockSpec((lrw, dkt), const),
                  pl.BlockSpec((1, dkt), const),
                  pl.BlockSpec((1, dvt), const),
                  pl.BlockSpec(tri.shape, const)],
        out_specs=pl.BlockSpec((rows, dvt), lambda b, i: (b * nblk + i, 0)),
        scratch_shapes=[pltpu.VMEM((GLA_HEADS, dv, dk), F32),
                        pltpu.VMEM((rows, dkt), F32)],
        compiler_params=pltpu.CompilerParams(
            dimension_semantics=("parallel", "arbitrary"),
            vmem_limit_bytes=_vmem_limit(blocks, resident)),
        name="gla_mixer",
    )(h, h, h, h, h, w_up_pad, b_gla.reshape(1, dkt), gla_norm.reshape(1, dvt), tri)


def _rope_lane_table():
    half = ROPE_DIM // 2
    inv_freq = ROPE_THETA ** (-np.arange(0, ROPE_DIM, 2, dtype=np.float32) / ROPE_DIM)
    lane = np.arange(LANES) % SWA_HEAD_DIM
    tab = np.where(lane < ROPE_DIM, inv_freq[lane % half], 0.0).astype(np.float32)
    return jnp.asarray(tab.reshape(1, LANES))


SWA_ROW_CHUNK = 32
SWA_BLOCKS_PER_STEP = 2


def _swa_kernel(sink_ref, q_ref, k_ref, v_ref, g_ref, pos_ref, inv_ref, o_ref, *scratch,
                n_pairs, n_sub):
    step = pl.program_id(1)
    for sb in range(n_sub):
        rows = pl.ds(sb * SWA_BLOCK, SWA_BLOCK)
        _swa_block(step * n_sub + sb, sink_ref, q_ref.at[rows], k_ref.at[rows], v_ref.at[rows],
                   g_ref.at[rows], pos_ref.at[rows], inv_ref, o_ref.at[rows], *scratch,
                   n_pairs=n_pairs)


def _swa_block(n, sink_ref, q_ref, k_ref, v_ref, g_ref, pos_ref, inv_ref, o_ref,
               wk_ref, wv_ref, qr_ref, s_ref, p_ref, es_ref, bias_ref, *, n_pairs):
    blk = SWA_BLOCK
    half = ROPE_DIM // 2
    hd = SWA_HEAD_DIM
    ppg = n_pairs // 2
    rc_rows = SWA_ROW_CHUNK

    @pl.when(n == 0)
    def _():
        wk_ref[...] = jnp.zeros(wk_ref.shape, BF16)
        row = lax.broadcasted_iota(jnp.int32, (4 * blk, 2 * LANES), 0)
        col = lax.broadcasted_iota(jnp.int32, (4 * blk, 2 * LANES), 1)
        ones = ((col >= LANES) & ((col >= LANES + hd) == (row >= 2 * blk))).astype(BF16)
        for g in range(2):
            wv_ref[g] = ones

    lane = lax.broadcasted_iota(jnp.int32, (1, LANES), 1)
    lane_h = _mod_pow2(lane, hd)
    lo = lane < hd
    ang = pos_ref[...] * inv_ref[...]
    cos = jnp.cos(ang)
    sin = jnp.sin(ang)

    def rope(x):
        x_up = pltpu.roll(x, LANES - half, axis=1)
        x_dn = pltpu.roll(x, half, axis=1)
        return x * cos + jnp.where(lane_h < half, -x_up, x_dn) * sin

    for g in range(2):
        for part in range(2):
            base = part * 2 * blk
            wk_ref[g, base:base + blk, :] = wk_ref[g, base + blk:base + 2 * blk, :]
            wv_ref[g, base:base + blk, 0:LANES] = wv_ref[g, base + blk:base + 2 * blk, 0:LANES]
    k_r = rope(k_ref[...])
    v_c = v_ref[...]
    k_sw = pltpu.roll(k_r, hd, axis=1)
    v_sw = pltpu.roll(v_c, hd, axis=1)
    zero = jnp.zeros_like(k_r)
    for g in range(2):
        k_lo, k_hi = (k_r, k_sw) if g == 0 else (k_sw, k_r)
        v_lo, v_hi = (v_c, v_sw) if g == 0 else (v_sw, v_c)
        wk_ref[g, blk:2 * blk, :] = jnp.where(lo, k_lo, zero).astype(BF16)
        wk_ref[g, 3 * blk:4 * blk, :] = jnp.where(lo, zero, k_hi).astype(BF16)
        wv_ref[g, blk:2 * blk, 0:LANES] = jnp.where(lo, v_lo, zero).astype(BF16)
        wv_ref[g, 3 * blk:4 * blk, 0:LANES] = jnp.where(lo, zero, v_hi).astype(BF16)

    r = lax.broadcasted_iota(jnp.int32, (blk, 2 * blk), 0)
    c = lax.broadcasted_iota(jnp.int32, (blk, 2 * blk), 1)
    valid = (c > r) & (c <= r + WINDOW) & ((n > 0) | (c >= blk))
    bias_ref[...] = jnp.where(valid, 0.0, -jnp.inf).astype(F32)

    qscale = hd ** -0.5
    for p in range(n_pairs):
        g, pp = divmod(p, ppg)
        qr_ref[g, pp * blk:(pp + 1) * blk, :] = (
            rope(q_ref[:, p * LANES:(p + 1) * LANES]) * qscale).astype(BF16)

    for g in range(2):
        s_ref[g] = lax.dot_general(qr_ref[g], wk_ref[g], NT_DIMS, preferred_element_type=F32)

    for g in range(2):
        for pp in range(ppg):
            for hh in range(2):
                sink = sink_ref[2 * (g * ppg + pp) + hh]
                kc = slice(hh * 2 * blk, (hh + 1) * 2 * blk)
                for rc in range(blk // rc_rows):
                    rows = slice(pp * blk + rc * rc_rows, pp * blk + (rc + 1) * rc_rows)
                    sh = s_ref[g, rows, kc] + bias_ref[rc * rc_rows:(rc + 1) * rc_rows, :]
                    m = jnp.maximum(jnp.max(sh, axis=-1, keepdims=True), sink)
                    p_ref[g, rows, kc] = jnp.exp(sh - m).astype(BF16)
                    es_ref[g, rows, hh * hd:(hh + 1) * hd] = jnp.broadcast_to(
                        jnp.exp(sink - m), (rc_rows, hd))

    for g in range(2):
        o2 = jnp.dot(p_ref[g], wv_ref[g], preferred_element_type=F32)
        for pp in range(ppg):
            rows = slice(pp * blk, (pp + 1) * blk)
            cols = slice((g * ppg + pp) * LANES, (g * ppg + pp + 1) * LANES)
            den = o2[rows, LANES:] + es_ref[g, rows, :]
            o_ref[:, cols] = (o2[rows, :LANES] / den * _silu(g_ref[:, cols])).astype(o_ref.dtype)


def _swa(h, pos_f, sinks, bsz, seq, cols):
    col_q, col_k, col_v, col_g = cols
    n_heads = sinks.shape[0]
    bw = n_heads * SWA_HEAD_DIM
    n_pairs = bw // LANES
    n_sub = SWA_BLOCKS_PER_STEP
    blk = n_sub * SWA_BLOCK
    nblk = seq // blk
    scratch = [pltpu.VMEM((2, 4 * SWA_BLOCK, LANES), BF16),
               pltpu.VMEM((2, 4 * SWA_BLOCK, 2 * LANES), BF16),
               pltpu.VMEM((2, 4 * SWA_BLOCK, LANES), BF16),
               pltpu.VMEM((2, 4 * SWA_BLOCK, 4 * SWA_BLOCK), F32),
               pltpu.VMEM((2, 4 * SWA_BLOCK, 4 * SWA_BLOCK), BF16),
               pltpu.VMEM((2, 4 * SWA_BLOCK, LANES), F32),
               pltpu.VMEM((SWA_BLOCK, 2 * SWA_BLOCK), F32)]
    blocks = (2 * _nbytes((blk, bw), F32) + 3 * _nbytes((blk, LANES), F32) + _nbytes((blk, bw), BF16))
    resident = sum(_nbytes(sc.shape, sc.dtype) for sc in scratch)
    return pl.pallas_call(
        functools.partial(_swa_kernel, n_pairs=n_pairs, n_sub=n_sub),
        out_shape=jax.ShapeDtypeStruct((bsz * seq, bw), BF16),
        grid=(bsz, nblk),
        in_specs=[pl.BlockSpec(memory_space=pltpu.SMEM),
                  pl.BlockSpec((blk, bw), lambda b, i: (b * nblk + i, col_q // bw)),
                  pl.BlockSpec((blk, LANES), lambda b, i: (b * nblk + i, col_k // LANES)),
                  pl.BlockSpec((blk, LANES), lambda b, i: (b * nblk + i, col_v // LANES)),
                  pl.BlockSpec((blk, bw), lambda b, i: (b * nblk + i, col_g // bw)),
                  pl.BlockSpec((blk, 1), lambda b, i: (b * nblk + i, 0)),
                  pl.BlockSpec((1, LANES), lambda b, i: (0, 0))],
        out_specs=pl.BlockSpec((blk, bw), lambda b, i: (b * nblk + i, 0)),
        scratch_shapes=scratch,
        compiler_params=pltpu.CompilerParams(
            dimension_semantics=("parallel", "arbitrary"),
            vmem_limit_bytes=_vmem_limit(blocks, resident)),
        name="swa_mixer",
    )(sinks, h, h, h, h, pos_f, _rope_lane_table())


def _mem_kernel(q_ref, g_ref, mk_ref, mv_ref, o_ref, *, hd):
    scale = hd ** -0.5
    for h in range(XA_HEADS):
        cols = slice(h * hd, (h + 1) * hd)
        qh = q_ref[:, cols].astype(BF16)
        s = lax.dot_general(qh, mk_ref[:, cols], (((1,), (1,)), ((), ())),
                            preferred_element_type=F32) * scale
        m = jnp.max(s, axis=-1, keepdims=True)
        e = jnp.exp(s - m)
        p = e / jnp.sum(e, axis=-1, keepdims=True)
        o = jnp.dot(p.astype(BF16), mv_ref[:, cols], preferred_element_type=F32)
        o_ref[:, cols] = (o * _silu(g_ref[:, cols])).astype(o_ref.dtype)


def _mem_attn(h, mkv, bsz, seq, mem_len, cols, ts):
    col_q, col_g = cols
    bw = mkv.shape[1] // 2
    hd = bw // XA_HEADS
    nblk = seq // ts
    blocks = 2 * _nbytes((ts, bw), F32) + 2 * _nbytes((mem_len, bw), BF16) + _nbytes((ts, bw), BF16)
    return pl.pallas_call(
        functools.partial(_mem_kernel, hd=hd),
        out_shape=jax.ShapeDtypeStruct((bsz * seq, bw), BF16),
        grid=(bsz, nblk),
        in_specs=[pl.BlockSpec((ts, bw), lambda b, i: (b * nblk + i, col_q // bw)),
                  pl.BlockSpec((ts, bw), lambda b, i: (b * nblk + i, col_g // bw)),
                  pl.BlockSpec((mem_len, bw), lambda b, i: (b, 0)),
                  pl.BlockSpec((mem_len, bw), lambda b, i: (b, 1))],
        out_specs=pl.BlockSpec((ts, bw), lambda b, i: (b * nblk + i, 0)),
        compiler_params=pltpu.CompilerParams(
            dimension_semantics=("parallel", "parallel"), vmem_limit_bytes=_vmem_limit(blocks)),
        name="mem_attn",
    )(h, h, mkv, mkv)


IN_TN = 2 * LANES
MERGE_TN = 2 * LANES


def _main_layout(d_model):
    bw = d_model // 4
    dkt, dvt = bw // 2, bw
    kvw = 2 * SWA_HEAD_DIM
    names = ["pool_u", "pool_g", "gq", "gk", "gv", "gg", "glr", "sq", "sk", "sv", "sg", "xq", "xg"]
    widths = [bw, bw, dkt, dkt, dvt, dvt, GLA_RANK, bw, kvw, kvw, bw, bw, bw]
    src = {}
    off = 0
    for nme, w in zip(names, widths):
        src[nme] = (off, w)
        off += w
    n_main = off
    order = ["pool_u", "pool_g", "gv", "gg", "sq", "sg", "xq", "xg", "gq", "gk", "sk", "sv", "glr"]
    dst = {}
    cols = []
    off = 0
    for nme in order:
        o, w = src[nme]
        w_dst = w if nme != "glr" else IN_TN
        assert off % w_dst == 0
        dst[nme] = off
        cols.append(o + np.arange(w_dst))
        off += w_dst
    cols = np.concatenate(cols)
    assert off % IN_TN == 0 and cols.max() < n_main
    starts = cols[::IN_TN]
    assert np.all(cols.reshape(-1, IN_TN) == starts[:, None] + np.arange(IN_TN))
    return starts, dst, n_main


def _hybrid_layer(l, x, xb, mem_b, pos_f, w_in_t, w_pool, pool_scale, w_gla_up, b_gla, gla_norm,
                  sinks, w_mem_kv, w_branch, w_out, ln_g, ln_b, alpha, emit_bf16, bsz, seq):
    m, d = x.shape
    col_starts, dst, n_main = _main_layout(d)
    w_up_pad = jnp.concatenate(
        [w_gla_up[l], jnp.zeros((IN_TN - GLA_RANK, w_gla_up.shape[-1]), w_gla_up.dtype)], axis=0)

    h, w_mg_t = _in_proj(xb, w_in_t, l, col_starts, n_main, MERGE_TN, tm=2048, tn=IN_TN)
    mkv = _matmul(mem_b, w_mem_kv, l, BF16, tm=mem_b.shape[0], tn=512, name="mem_kv")

    o_pool = _pool(h, w_pool[l].astype(BF16), pool_scale[l], bsz, seq,
                   dst["pool_u"], dst["pool_g"], ts=512)
    o_gla = _gla(h, w_up_pad, b_gla[l], gla_norm[l], bsz, seq,
                 (dst["gq"], dst["gk"], dst["gv"], dst["gg"], dst["glr"]), rows=512)
    o_swa = _swa(h, pos_f, sinks[l], bsz, seq, (dst["sq"], dst["sk"], dst["sv"], dst["sg"]))
    o_mem = _mem_attn(h, mkv, bsz, seq, mem_b.shape[0] // bsz, (dst["xq"], dst["xg"]), ts=512)

    y, w_out_b = _merge(xb, (o_pool, o_gla, o_swa, o_mem), w_mg_t, w_branch, w_out, l,
                        tm=1024, tn=MERGE_TN)
    return _out_proj_ln(y, w_out_b, x, ln_g[l], ln_b[l], alpha, tm=1024, tn=256, tc=1024,
                        emit_bf16=emit_bf16)


def kernel(x, mem, positions, w_in, w_pool, pool_scale, w_gla_up, b_gla, gla_norm, sinks,
           w_mem_kv, w_branch, w_out, ln_g, ln_b):
    bsz, seq, d = x.shape
    depth = w_in.shape[0]
    alpha = (2 * depth) ** 0.25
    xf = x.reshape(bsz * seq, d)
    xb = xf.astype(BF16)
    mem_b = mem.reshape(bsz * mem.shape[1], d).astype(BF16)
    pos_f = positions.astype(F32).reshape(bsz * seq, 1)
    w_in_t = jnp.swapaxes(w_in, 1, 2)
    for l in range(depth):
        xf, xb = _hybrid_layer(
            l, xf, xb, mem_b, pos_f, w_in_t, w_pool, pool_scale, w_gla_up, b_gla,
            gla_norm, sinks, w_mem_kv, w_branch, w_out, ln_g, ln_b,
            alpha, l + 1 < depth, bsz, seq)
    return xf.reshape(bsz, seq, d)
```

```python
import functools
import math

import jax
import jax.numpy as jnp
import numpy as np
from jax import lax
from jax.experimental import pallas as pl
from jax.experimental.pallas import tpu as pltpu

F32 = jnp.float32
BF16 = jnp.bfloat16

N_BRANCH = 4
POOL_WINDOWS = (2, 4, 8, 16)
GLA_HEADS = 4
GLA_RANK = 16
GLA_TAU = 16.0
GLA_CHUNK = 64
GLA_SUB = 16
GLA_SAFE_LOG2 = 24.0
GLA_SAFE_KEY = 2.0 ** 100
SWA_HEAD_DIM = 64
SWA_GROUP = 8
WINDOW = 128
SWA_BLOCK = 128
ROPE_THETA = 500000.0
ROPE_DIM = SWA_HEAD_DIM // 4
XA_HEADS = 4
LN_EPS = 1e-5

LANES = 128
SUBLANES = 8
V7X_VMEM_BYTES = 64 * 1024 * 1024
VMEM_CAP = V7X_VMEM_BYTES - 8 * 1024 * 1024
VMEM_TEMP_BYTES = 12 * 1024 * 1024


def _vmem_limit(streamed_bytes, resident_bytes=0):
    return int(min(VMEM_CAP, 2 * streamed_bytes + resident_bytes + VMEM_TEMP_BYTES))


def _nbytes(shape, dtype):
    return int(np.prod(shape)) * jnp.dtype(dtype).itemsize


def _silu(g):
    return g * jax.nn.sigmoid(g)


def _div_pow2(x, n):
    assert n & (n - 1) == 0
    return x >> (n.bit_length() - 1)


def _mod_pow2(x, n):
    assert n & (n - 1) == 0
    return x & (n - 1)


NT_DIMS = (((1,), (1,)), ((), ()))


def _mm_kernel(a_ref, w_ref, o_ref):
    w = w_ref[...].astype(BF16)
    o_ref[...] = jnp.dot(a_ref[...], w, preferred_element_type=F32).astype(o_ref.dtype)


def _mm_resid_kernel(a_ref, w_ref, x_ref, o_ref, *, alpha):
    w = w_ref[...].astype(BF16)
    o_ref[...] = alpha * x_ref[...] + jnp.dot(a_ref[...], w, preferred_element_type=F32)


def _matmul(a, w, layer, out_dtype, tm, tn, name, resid=None, alpha=None):
    m, k = a.shape
    n = w.shape[-1]
    assert m % tm == 0 and n % tn == 0
    resident = _nbytes((tm, k), a.dtype)
    streamed = _nbytes((k, tn), w.dtype) + _nbytes((tm, tn), out_dtype)
    in_specs = [pl.BlockSpec((tm, k), lambda i, j: (i, 0), pipeline_mode=pl.Buffered(1)),
                pl.BlockSpec((None, k, tn), lambda i, j: (layer, 0, j))]
    args = [a, w]
    if resid is None:
        kern = _mm_kernel
    else:
        kern = functools.partial(_mm_resid_kernel, alpha=alpha)
        in_specs.append(pl.BlockSpec((tm, tn), lambda i, j: (i, j)))
        args.append(resid)
        streamed += _nbytes((tm, tn), resid.dtype)
    return pl.pallas_call(
        kern,
        out_shape=jax.ShapeDtypeStruct((m, n), out_dtype),
        grid=(m // tm, n // tn),
        in_specs=in_specs,
        out_specs=pl.BlockSpec((tm, tn), lambda i, j: (i, j)),
        compiler_params=pltpu.CompilerParams(
            dimension_semantics=("parallel", "arbitrary"),
            vmem_limit_bytes=_vmem_limit(streamed, resident)),
        name=name,
    )(*args)


def _in_proj_kernel(tab_ref, x_ref, w_ref, g_ref, o_ref, gs_ref):
    w = w_ref[0].astype(BF16)
    o_ref[...] = lax.dot_general(x_ref[...], w, NT_DIMS, preferred_element_type=F32)
    gs_ref[...] = g_ref[0].astype(gs_ref.dtype)


def _in_proj(xb, w_in_t, layer, col_starts, gate_row0, gate_tn, tm, tn):
    m, k = xb.shape
    nblk = len(col_starts)
    assert m % tm == 0 and all(int(s) % SUBLANES == 0 for s in col_starts)
    n_steps = (m // tm) * nblk
    sr = LANES
    gate_rows = N_BRANCH * k
    n_slabs = gate_rows // sr
    per_blk, per_branch = gate_tn // sr, k // sr
    assert gate_tn % sr == 0 and k % gate_tn == 0 and n_slabs <= n_steps and gate_row0 % SUBLANES == 0
    resident = _nbytes((tm, k), xb.dtype)
    streamed = (_nbytes((tn, k), w_in_t.dtype) + _nbytes((tm, tn), F32)
                + _nbytes((sr, k), w_in_t.dtype) + _nbytes((sr, k), BF16))

    def slab(i, j):
        return jnp.minimum(i * nblk + j, n_slabs - 1)

    def slab_dst(s):
        b, within = s // per_branch, s % per_branch
        return (within // per_blk) * (N_BRANCH * per_blk) + b * per_blk + within % per_blk

    return pl.pallas_call(
        _in_proj_kernel,
        out_shape=[jax.ShapeDtypeStruct((m, nblk * tn), F32),
                   jax.ShapeDtypeStruct((gate_rows, k), BF16)],
        grid_spec=pltpu.PrefetchScalarGridSpec(
            num_scalar_prefetch=1,
            grid=(m // tm, nblk),
            in_specs=[pl.BlockSpec((tm, k), lambda i, j, tab: (i, 0), pipeline_mode=pl.Buffered(1)),
                      pl.BlockSpec((pl.Element(1), pl.Element(tn), pl.Element(k)),
                                   lambda i, j, tab: (layer, pl.multiple_of(tab[j], SUBLANES), 0)),
                      pl.BlockSpec((pl.Element(1), pl.Element(sr), pl.Element(k)),
                                   lambda i, j, tab: (
                                       layer, pl.multiple_of(gate_row0 + slab(i, j) * sr, SUBLANES), 0))],
            out_specs=[pl.BlockSpec((tm, tn), lambda i, j, tab: (i, j)),
                       pl.BlockSpec((sr, k), lambda i, j, tab: (slab_dst(slab(i, j)), 0))]),
        compiler_params=pltpu.CompilerParams(
            dimension_semantics=("arbitrary", "arbitrary"),
            vmem_limit_bytes=_vmem_limit(streamed, resident)),
        name="in_proj",
    )(jnp.asarray(np.asarray(col_starts, np.int32)), xb, w_in_t, w_in_t)


def _merge_kernel(x_ref, o0, o1, o2, o3, g_ref, w_ref, y_ref):
    x = x_ref[...]
    tn = y_ref.shape[1]
    acc = None
    for b, o_ref in enumerate((o0, o1, o2, o3)):
        gate = lax.dot_general(x, g_ref[b * tn:(b + 1) * tn, :], NT_DIMS, preferred_element_type=F32)
        proj = jnp.dot(o_ref[...], w_ref[b].astype(BF16), preferred_element_type=F32)
        term = jax.nn.sigmoid(gate) * proj
        acc = term if acc is None else acc + term
    y_ref[...] = acc.astype(y_ref.dtype)


def _merge(xb, branches, w_mg_t, w_branch, layer, tm, tn):
    m, d = xb.shape
    bw = branches[0].shape[1]
    nj = d // tn
    in_specs = [pl.BlockSpec((tm, d), lambda i, j: (i, 0), pipeline_mode=pl.Buffered(1))]
    in_specs += [pl.BlockSpec((tm, bw), lambda i, j: (i, 0), pipeline_mode=pl.Buffered(1))
                 for _ in range(N_BRANCH)]
    in_specs += [pl.BlockSpec((N_BRANCH * tn, d), lambda i, j: (j, 0)),
                 pl.BlockSpec((None, N_BRANCH, bw, tn), lambda i, j: (layer, 0, 0, j))]
    resident = _nbytes((tm, d), BF16) + N_BRANCH * _nbytes((tm, bw), BF16)
    streamed = (N_BRANCH * (_nbytes((tn, d), BF16) + _nbytes((bw, tn), w_branch.dtype))
                + _nbytes((tm, tn), BF16))
    return pl.pallas_call(
        _merge_kernel,
        out_shape=jax.ShapeDtypeStruct((m, d), BF16),
        grid=(m // tm, nj),
        in_specs=in_specs,
        out_specs=pl.BlockSpec((tm, tn), lambda i, j: (i, j)),
        compiler_params=pltpu.CompilerParams(
            dimension_semantics=("parallel", "arbitrary"),
            vmem_limit_bytes=_vmem_limit(streamed, resident)),
        name="merge",
    )(xb, *branches, w_mg_t, w_branch)


def _ln_kernel(z_ref, g_ref, b_ref, o_ref, ob_ref=None):
    z = z_ref[...]
    mu = jnp.mean(z, axis=-1, keepdims=True)
    zc = z - mu
    var = jnp.mean(zc * zc, axis=-1, keepdims=True)
    y = zc * lax.rsqrt(var + LN_EPS) * g_ref[...] + b_ref[...]
    o_ref[...] = y
    if ob_ref is not None:
        ob_ref[...] = y.astype(BF16)


def _layer_norm(z, g, b, tm, emit_bf16):
    m, d = z.shape
    out_shape = [jax.ShapeDtypeStruct((m, d), F32)]
    out_specs = [pl.BlockSpec((tm, d), lambda i: (i, 0))]
    if emit_bf16:
        out_shape.append(jax.ShapeDtypeStruct((m, d), BF16))
        out_specs.append(pl.BlockSpec((tm, d), lambda i: (i, 0)))
    blocks = 2 * _nbytes((tm, d), F32) + _nbytes((tm, d), BF16)
    res = pl.pallas_call(
        _ln_kernel,
        out_shape=out_shape,
        grid=(m // tm,),
        in_specs=[pl.BlockSpec((tm, d), lambda i: (i, 0)),
                  pl.BlockSpec((1, d), lambda i: (0, 0)),
                  pl.BlockSpec((1, d), lambda i: (0, 0))],
        out_specs=out_specs,
        compiler_params=pltpu.CompilerParams(
            dimension_semantics=("parallel",), vmem_limit_bytes=_vmem_limit(blocks)),
        name="layer_norm",
    )(z, g.reshape(1, d), b.reshape(1, d))
    return res if emit_bf16 else (res[0], None)


POOL_HIST = max(POOL_WINDOWS)
POOL_PAD = 2 * POOL_HIST


def _pool_kernel(u_ref, g_ref, wp_ref, sc_ref, o_ref, buf, lvl_a, lvl_b, *, ts, group):
    i = pl.program_id(1)
    ext = POOL_PAD + ts

    @pl.when(i == 0)
    def _():
        buf[0:POOL_PAD, :] = jnp.zeros((POOL_PAD, buf.shape[1]), F32)

    buf[POOL_PAD:ext, :] = u_ref[...]
    src = buf
    level_refs = []
    for lvl, dst in enumerate((lvl_a, lvl_b, lvl_a, lvl_b)[:len(POOL_WINDOWS)], start=1):
        lag = 2 ** (lvl - 1)
        r0 = SUBLANES * lvl
        c0 = (lvl - 1) * group
        dst[r0:ext, c0:] = src[r0:ext, c0:] + src[r0 - lag:ext - lag, c0:]
        level_refs.append(dst)
        src = dst

    t = i * ts + lax.broadcasted_iota(jnp.int32, (ts, 1), 0)
    outs = []
    for gi, w in enumerate(POOL_WINDOWS):
        assert w == 2 ** (gi + 1)
        cols = slice(gi * group, (gi + 1) * group)
        acc = level_refs[gi][POOL_PAD:ext, cols]
        cnt = jnp.minimum(t + 1, w).astype(F32)
        pooled = acc / cnt - buf[POOL_PAD:ext, cols]
        outs.append(jnp.dot(pooled.astype(BF16), wp_ref[gi], preferred_element_type=F32))
    o = jnp.concatenate(outs, axis=1) * sc_ref[...] * _silu(g_ref[...])
    o_ref[...] = o.astype(o_ref.dtype)
    buf[POOL_PAD - POOL_HIST:POOL_PAD, :] = buf[ext - POOL_HIST:ext, :]


def _pool(h, w_pool_b, pool_scale, bsz, seq, col_u, col_g, ts):
    bw = pool_scale.shape[-1]
    group = bw // len(POOL_WINDOWS)
    nblk = seq // ts
    blocks = 2 * _nbytes((ts, bw), F32) + _nbytes((ts, bw), BF16)
    scratch = [pltpu.VMEM((ts + POOL_PAD, bw), F32) for _ in range(3)]
    resident = sum(_nbytes(sc.shape, sc.dtype) for sc in scratch)
    return pl.pallas_call(
        functools.partial(_pool_kernel, ts=ts, group=group),
        out_shape=jax.ShapeDtypeStruct((bsz * seq, bw), BF16),
        grid=(bsz, nblk),
        in_specs=[pl.BlockSpec((ts, bw), lambda b, i: (b * nblk + i, col_u // bw)),
                  pl.BlockSpec((ts, bw), lambda b, i: (b * nblk + i, col_g // bw)),
                  pl.BlockSpec(w_pool_b.shape, lambda b, i: (0, 0, 0)),
                  pl.BlockSpec((1, bw), lambda b, i: (0, 0))],
        out_specs=pl.BlockSpec((ts, bw), lambda b, i: (b * nblk + i, 0)),
        scratch_shapes=scratch,
        compiler_params=pltpu.CompilerParams(
            dimension_semantics=("parallel", "arbitrary"),
            vmem_limit_bytes=_vmem_limit(blocks, resident)),
        name="pool_mixer",
    )(h, h, w_pool_b, pool_scale.reshape(1, bw))


def _split_bf16(x, parts):
    out = []
    r = x
    for _ in range(parts):
        p = r.astype(BF16)
        out.append(p)
        r = r - p.astype(F32)
    return out


LOG2E = math.log2(math.e)


def _gla_cumsum_operator(rows):
    r = np.arange(rows)
    tri = ((r[:, None] // GLA_CHUNK == r[None, :] // GLA_CHUNK) & (r[None, :] <= r[:, None]))
    return jnp.asarray(tri, BF16)


def _gla_kernel(q_ref, k_ref, v_ref, gg_ref, lr_ref, wup_ref, bg_ref, nrm_ref, tri_ref,
                o_ref, state_ref, b_ref, *, rows, dk, dv):
    i = pl.program_id(1)
    C = GLA_CHUNK
    nsub = C // GLA_SUB

    @pl.when(i == 0)
    def _():
        state_ref[...] = jnp.zeros(state_ref.shape, F32)

    lr_h, lr_m = _split_bf16(lr_ref[...], 2)
    w_h, w_m = _split_bf16(wup_ref[...], 2)
    z = (jnp.dot(lr_h, w_h, preferred_element_type=F32)
         + jnp.dot(lr_h, w_m, preferred_element_type=F32)
         + jnp.dot(lr_m, w_h, preferred_element_type=F32)) + bg_ref[...]
    log2_a = (jnp.minimum(z, 0.0) - jnp.log(1.0 + jnp.exp(-jnp.abs(z)))) * (LOG2E / GLA_TAU)

    bsum = None
    for part in _split_bf16(log2_a, 3):
        t = jnp.dot(tri_ref[...], part, preferred_element_type=F32)
        bsum = t if bsum is None else bsum + t
    b_ref[...] = bsum

    row = lax.broadcasted_iota(jnp.int32, (C, 1), 0)
    col = lax.broadcasted_iota(jnp.int32, (1, C), 1)
    row_sub = _div_pow2(row, GLA_SUB)
    col_sub = _div_pow2(col, GLA_SUB)
    lane_c = lax.broadcasted_iota(jnp.int32, (GLA_SUB, C), 1)
    scale = dk ** -0.5

    def sub_heads(x):
        return jnp.concatenate(
            [jnp.broadcast_to(x[a * GLA_SUB:a * GLA_SUB + 1, :], (GLA_SUB, dk))
             for a in range(nsub)], axis=0)

    def chunk_body(c, carry, *, bounded):
        r0 = pl.multiple_of(c * C, C)
        for h in range(GLA_HEADS):
            kc = slice(h * dk, (h + 1) * dk)
            vc = slice(h * dv, (h + 1) * dv)
            qs = q_ref[pl.ds(r0, C), kc] * scale
            k = k_ref[pl.ds(r0, C), kc]
            v = v_ref[pl.ds(r0, C), vc].astype(BF16)
            b = b_ref[pl.ds(r0, C), kc]
            b_last = b[C - 1:C, :]
            st = state_ref[h]

            q_in = (qs * jnp.exp2(b)).astype(BF16)
            o = lax.dot_general(q_in, st.astype(BF16), NT_DIMS, preferred_element_type=F32)

            q_t = qs * jnp.exp2(b - sub_heads(b))
            lhs = jnp.concatenate(
                [jnp.where(row_sub == a, q_t, 0.0) for a in range(nsub)], axis=1).astype(BF16)
            key_exp = [b[a * GLA_SUB:a * GLA_SUB + 1, :] - b for a in range(nsub)]
            if not bounded:
                key_exp = [jnp.minimum(e, 0.0) for e in key_exp]
            rhs = jnp.concatenate([k * jnp.exp2(e) for e in key_exp], axis=1).astype(BF16)
            a_fac = lax.dot_general(lhs, rhs, NT_DIMS, preferred_element_type=F32)

            if bounded:
                attn = jnp.where(row >= col, a_fac, 0.0)
            else:
                diag_blocks = []
                for a in range(nsub):
                    rs = slice(a * GLA_SUB, (a + 1) * GLA_SUB)
                    qa = qs[rs, :]
                    ba = b[rs, :]
                    blk = jnp.zeros((GLA_SUB, C), F32)
                    for jj in range(GLA_SUB):
                        j = a * GLA_SUB + jj
                        tj = qa * k[j:j + 1, :] * jnp.exp2(ba - b[j:j + 1, :])
                        cj = jnp.sum(tj, axis=1, keepdims=True)
                        blk = jnp.where(lane_c == j, cj, blk)
                    diag_blocks.append(blk)
                a_diag = jnp.concatenate(diag_blocks, axis=0)
                attn = jnp.where(row_sub > col_sub, a_fac,
                                 jnp.where((row_sub == col_sub) & (row >= col), a_diag, 0.0))
            o = o + jnp.dot(attn.astype(BF16), v, preferred_element_type=F32)

            k_dec = (k * jnp.exp2(b_last - b)).astype(BF16)
            upd = lax.dot_general(v, k_dec, (((0,), (0,)), ((), ())),
                                  preferred_element_type=F32)
            state_ref[h] = st * jnp.exp2(b_last) + upd

            ms = jnp.mean(o * o, axis=-1, keepdims=True)
            on = o * lax.rsqrt(ms + LN_EPS) * nrm_ref[:, vc]
            o_ref[pl.ds(r0, C), vc] = (on * _silu(gg_ref[pl.ds(r0, C), vc])).astype(o_ref.dtype)
        return carry

    bounded = (-jnp.min(bsum) < GLA_SAFE_LOG2) & (jnp.max(jnp.abs(k_ref[...])) < GLA_SAFE_KEY)

    @pl.when(bounded)
    def _():
        lax.fori_loop(0, rows // C, functools.partial(chunk_body, bounded=True), 0, unroll=2)

    @pl.when(jnp.logical_not(bounded))
    def _():
        lax.fori_loop(0, rows // C, functools.partial(chunk_body, bounded=False), 0)


def _gla(h, w_up_pad, b_gla, gla_norm, bsz, seq, cols, rows):
    col_q, col_k, col_v, col_g, col_lr = cols
    dkt = b_gla.shape[-1]
    dvt = gla_norm.shape[-1]
    dk, dv = dkt // GLA_HEADS, dvt // GLA_HEADS
    lrw = w_up_pad.shape[0]
    nblk = seq // rows
    tri = _gla_cumsum_operator(rows)
    blocks = (2 * _nbytes((rows, dkt), F32) + 2 * _nbytes((rows, dvt), F32)
              + _nbytes((rows, lrw), F32) + _nbytes((rows, dvt), BF16))
    resident = (_nbytes((rows, dkt), F32) + _nbytes((GLA_HEADS, dv, dk), F32)
                + _nbytes(tri.shape, BF16))
    const = lambda b, i: (0, 0)
    return pl.pallas_call(
        functools.partial(_gla_kernel, rows=rows, dk=dk, dv=dv),
        out_shape=jax.ShapeDtypeStruct((bsz * seq, dvt), BF16),
        grid=(bsz, nblk),
        in_specs=[pl.BlockSpec((rows, dkt), lambda b, i: (b * nblk + i, col_q // dkt)),
                  pl.BlockSpec((rows, dkt), lambda b, i: (b * nblk + i, col_k // dkt)),
                  pl.BlockSpec((rows, dvt), lambda b, i: (b * nblk + i, col_v // dvt)),
                  pl.BlockSpec((rows, dvt), lambda b, i: (b * nblk + i, col_g // dvt)),
                  pl.BlockSpec((rows, lrw), lambda b, i: (b * nblk + i, col_lr // lrw)),
                  pl.BlockSpec((lrw, dkt), const),
                  pl.BlockSpec((1, dkt), const),
                  pl.BlockSpec((1, dvt), const),
                  pl.BlockSpec(tri.shape, const)],
        out_specs=pl.BlockSpec((rows, dvt), lambda b, i: (b * nblk + i, 0)),
        scratch_shapes=[pltpu.VMEM((GLA_HEADS, dv, dk), F32),
                        pltpu.VMEM((rows, dkt), F32)],
        compiler_params=pltpu.CompilerParams(
            dimension_semantics=("parallel", "arbitrary"),
            vmem_limit_bytes=_vmem_limit(blocks, resident)),
        name="gla_mixer",
    )(h, h, h, h, h, w_up_pad, b_gla.reshape(1, dkt), gla_norm.reshape(1, dvt), tri)


def _rope_lane_table():
    half = ROPE_DIM // 2
    inv_freq = ROPE_THETA ** (-np.arange(0, ROPE_DIM, 2, dtype=np.float32) / ROPE_DIM)
    lane = np.arange(LANES) % SWA_HEAD_DIM
    tab = np.where(lane < ROPE_DIM, inv_freq[lane % half], 0.0).astype(np.float32)
    return jnp.asarray(tab.reshape(1, LANES))


SWA_ROW_CHUNK = 32
SWA_BLOCKS_PER_STEP = 2


def _swa_kernel(sink_ref, q_ref, k_ref, v_ref, g_ref, pos_ref, inv_ref, o_ref, *scratch,
                n_pairs, n_sub):
    step = pl.program_id(1)
    for sb in range(n_sub):
        rows = pl.ds(sb * SWA_BLOCK, SWA_BLOCK)
        _swa_block(step * n_sub + sb, sink_ref, q_ref.at[rows], k_ref.at[rows], v_ref.at[rows],
                   g_ref.at[rows], pos_ref.at[rows], inv_ref, o_ref.at[rows], *scratch,
                   n_pairs=n_pairs)


def _swa_block(n, sink_ref, q_ref, k_ref, v_ref, g_ref, pos_ref, inv_ref, o_ref,
               wk_ref, wv_ref, qr_ref, s_ref, p_ref, es_ref, bias_ref, *, n_pairs):
    blk = SWA_BLOCK
    half = ROPE_DIM // 2
    hd = SWA_HEAD_DIM
    ppg = n_pairs // 2
    rc_rows = SWA_ROW_CHUNK

    @pl.when(n == 0)
    def _():
        wk_ref[...] = jnp.zeros(wk_ref.shape, BF16)
        row = lax.broadcasted_iota(jnp.int32, (4 * blk, 2 * LANES), 0)
        col = lax.broadcasted_iota(jnp.int32, (4 * blk, 2 * LANES), 1)
        ones = ((col >= LANES) & ((col >= LANES + hd) == (row >= 2 * blk))).astype(BF16)
        for g in range(2):
            wv_ref[g] = ones

    lane = lax.broadcasted_iota(jnp.int32, (1, LANES), 1)
    lane_h = _mod_pow2(lane, hd)
    lo = lane < hd
    ang = pos_ref[...] * inv_ref[...]
    cos = jnp.cos(ang)
    sin = jnp.sin(ang)

    def rope(x):
        x_up = pltpu.roll(x, LANES - half, axis=1)
        x_dn = pltpu.roll(x, half, axis=1)
        return x * cos + jnp.where(lane_h < half, -x_up, x_dn) * sin

    for g in range(2):
        for part in range(2):
            base = part * 2 * blk
            wk_ref[g, base:base + blk, :] = wk_ref[g, base + blk:base + 2 * blk, :]
            wv_ref[g, base:base + blk, 0:LANES] = wv_ref[g, base + blk:base + 2 * blk, 0:LANES]
    k_r = rope(k_ref[...])
    v_c = v_ref[...]
    k_sw = pltpu.roll(k_r, hd, axis=1)
    v_sw = pltpu.roll(v_c, hd, axis=1)
    zero = jnp.zeros_like(k_r)
    for g in range(2):
        k_lo, k_hi = (k_r, k_sw) if g == 0 else (k_sw, k_r)
        v_lo, v_hi = (v_c, v_sw) if g == 0 else (v_sw, v_c)
        wk_ref[g, blk:2 * blk, :] = jnp.where(lo, k_lo, zero).astype(BF16)
        wk_ref[g, 3 * blk:4 * blk, :] = jnp.where(lo, zero, k_hi).astype(BF16)
        wv_ref[g, blk:2 * blk, 0:LANES] = jnp.where(lo, v_lo, zero).astype(BF16)
        wv_ref[g, 3 * blk:4 * blk, 0:LANES] = jnp.where(lo, zero, v_hi).astype(BF16)

    r = lax.broadcasted_iota(jnp.int32, (blk, 2 * blk), 0)
    c = lax.broadcasted_iota(jnp.int32, (blk, 2 * blk), 1)
    valid = (c > r) & (c <= r + WINDOW) & ((n > 0) | (c >= blk))
    bias_ref[...] = jnp.where(valid, 0.0, -jnp.inf).astype(F32)

    qscale = hd ** -0.5
    for p in range(n_pairs):
        g, pp = divmod(p, ppg)
        qr_ref[g, pp * blk:(pp + 1) * blk, :] = (
            rope(q_ref[:, p * LANES:(p + 1) * LANES]) * qscale).astype(BF16)

    for g in range(2):
        s_ref[g] = lax.dot_general(qr_ref[g], wk_ref[g], NT_DIMS, preferred_element_type=F32)

    for g in range(2):
        for pp in range(ppg):
            for hh in range(2):
                sink = sink_ref[2 * (g * ppg + pp) + hh]
                kc = slice(hh * 2 * blk, (hh + 1) * 2 * blk)
                for rc in range(blk // rc_rows):
                    rows = slice(pp * blk + rc * rc_rows, pp * blk + (rc + 1) * rc_rows)
                    sh = s_ref[g, rows, kc] + bias_ref[rc * rc_rows:(rc + 1) * rc_rows, :]
                    m = jnp.maximum(jnp.max(sh, axis=-1, keepdims=True), sink)
                    p_ref[g, rows, kc] = jnp.exp(sh - m).astype(BF16)
                    es_ref[g, rows, hh * hd:(hh + 1) * hd] = jnp.broadcast_to(
                        jnp.exp(sink - m), (rc_rows, hd))

    for g in range(2):
        o2 = jnp.dot(p_ref[g], wv_ref[g], preferred_element_type=F32)
        for pp in range(ppg):
            rows = slice(pp * blk, (pp + 1) * blk)
            cols = slice((g * ppg + pp) * LANES, (g * ppg + pp + 1) * LANES)
            den = o2[rows, LANES:] + es_ref[g, rows, :]
            o_ref[:, cols] = (o2[rows, :LANES] / den * _silu(g_ref[:, cols])).astype(o_ref.dtype)


def _swa(h, pos_f, sinks, bsz, seq, cols):
    col_q, col_k, col_v, col_g = cols
    n_heads = sinks.shape[0]
    bw = n_heads * SWA_HEAD_DIM
    n_pairs = bw // LANES
    n_sub = SWA_BLOCKS_PER_STEP
    blk = n_sub * SWA_BLOCK
    nblk = seq // blk
    scratch = [pltpu.VMEM((2, 4 * SWA_BLOCK, LANES), BF16),
               pltpu.VMEM((2, 4 * SWA_BLOCK, 2 * LANES), BF16),
               pltpu.VMEM((2, 4 * SWA_BLOCK, LANES), BF16),
               pltpu.VMEM((2, 4 * SWA_BLOCK, 4 * SWA_BLOCK), F32),
               pltpu.VMEM((2, 4 * SWA_BLOCK, 4 * SWA_BLOCK), BF16),
               pltpu.VMEM((2, 4 * SWA_BLOCK, LANES), F32),
               pltpu.VMEM((SWA_BLOCK, 2 * SWA_BLOCK), F32)]
    blocks = (2 * _nbytes((blk, bw), F32) + 3 * _nbytes((blk, LANES), F32) + _nbytes((blk, bw), BF16))
    resident = sum(_nbytes(sc.shape, sc.dtype) for sc in scratch)
    return pl.pallas_call(
        functools.partial(_swa_kernel, n_pairs=n_pairs, n_sub=n_sub),
        out_shape=jax.ShapeDtypeStruct((bsz * seq, bw), BF16),
        grid=(bsz, nblk),
        in_specs=[pl.BlockSpec(memory_space=pltpu.SMEM),
                  pl.BlockSpec((blk, bw), lambda b, i: (b * nblk + i, col_q // bw)),
                  pl.BlockSpec((blk, LANES), lambda b, i: (b * nblk + i, col_k // LANES)),
                  pl.BlockSpec((blk, LANES), lambda b, i: (b * nblk + i, col_v // LANES)),
                  pl.BlockSpec((blk, bw), lambda b, i: (b * nblk + i, col_g // bw)),
                  pl.BlockSpec((blk, 1), lambda b, i: (b * nblk + i, 0)),
                  pl.BlockSpec((1, LANES), lambda b, i: (0, 0))],
        out_specs=pl.BlockSpec((blk, bw), lambda b, i: (b * nblk + i, 0)),
        scratch_shapes=scratch,
        compiler_params=pltpu.CompilerParams(
            dimension_semantics=("parallel", "arbitrary"),
            vmem_limit_bytes=_vmem_limit(blocks, resident)),
        name="swa_mixer",
    )(sinks, h, h, h, h, pos_f, _rope_lane_table())


def _mem_kernel(q_ref, g_ref, mk_ref, mv_ref, o_ref, *, hd):
    scale = hd ** -0.5
    for h in range(XA_HEADS):
        cols = slice(h * hd, (h + 1) * hd)
        qh = q_ref[:, cols].astype(BF16)
        s = lax.dot_general(qh, mk_ref[:, cols], (((1,), (1,)), ((), ())),
                            preferred_element_type=F32) * scale
        m = jnp.max(s, axis=-1, keepdims=True)
        e = jnp.exp(s - m)
        p = e / jnp.sum(e, axis=-1, keepdims=True)
        o = jnp.dot(p.astype(BF16), mv_ref[:, cols], preferred_element_type=F32)
        o_ref[:, cols] = (o * _silu(g_ref[:, cols])).astype(o_ref.dtype)


def _mem_attn(h, mkv, bsz, seq, mem_len, cols, ts):
    col_q, col_g = cols
    bw = mkv.shape[1] // 2
    hd = bw // XA_HEADS
    nblk = seq // ts
    blocks = 2 * _nbytes((ts, bw), F32) + 2 * _nbytes((mem_len, bw), BF16) + _nbytes((ts, bw), BF16)
    return pl.pallas_call(
        functools.partial(_mem_kernel, hd=hd),
        out_shape=jax.ShapeDtypeStruct((bsz * seq, bw), BF16),
        grid=(bsz, nblk),
        in_specs=[pl.BlockSpec((ts, bw), lambda b, i: (b * nblk + i, col_q // bw)),
                  pl.BlockSpec((ts, bw), lambda b, i: (b * nblk + i, col_g // bw)),
                  pl.BlockSpec((mem_len, bw), lambda b, i: (b, 0)),
                  pl.BlockSpec((mem_len, bw), lambda b, i: (b, 1))],
        out_specs=pl.BlockSpec((ts, bw), lambda b, i: (b * nblk + i, 0)),
        compiler_params=pltpu.CompilerParams(
            dimension_semantics=("parallel", "parallel"), vmem_limit_bytes=_vmem_limit(blocks)),
        name="mem_attn",
    )(h, h, mkv, mkv)


IN_TN = 2 * LANES
MERGE_TN = 2 * LANES


def _main_layout(d_model):
    bw = d_model // 4
    dkt, dvt = bw // 2, bw
    kvw = 2 * SWA_HEAD_DIM
    names = ["pool_u", "pool_g", "gq", "gk", "gv", "gg", "glr", "sq", "sk", "sv", "sg", "xq", "xg"]
    widths = [bw, bw, dkt, dkt, dvt, dvt, GLA_RANK, bw, kvw, kvw, bw, bw, bw]
    src = {}
    off = 0
    for nme, w in zip(names, widths):
        src[nme] = (off, w)
        off += w
    n_main = off
    order = ["pool_u", "pool_g", "gv", "gg", "sq", "sg", "xq", "xg", "gq", "gk", "sk", "sv", "glr"]
    dst = {}
    cols = []
    off = 0
    for nme in order:
        o, w = src[nme]
        w_dst = w if nme != "glr" else IN_TN
        assert off % w_dst == 0
        dst[nme] = off
        cols.append(o + np.arange(w_dst))
        off += w_dst
    cols = np.concatenate(cols)
    assert off % IN_TN == 0 and cols.max() < n_main
    starts = cols[::IN_TN]
    assert np.all(cols.reshape(-1, IN_TN) == starts[:, None] + np.arange(IN_TN))
    return starts, dst, n_main


def _hybrid_layer(l, x, xb, mem_b, pos_f, w_in_t, w_pool, pool_scale, w_gla_up, b_gla, gla_norm,
                  sinks, w_mem_kv, w_branch, w_out, ln_g, ln_b, alpha, emit_bf16, bsz, seq):
    m, d = x.shape
    col_starts, dst, n_main = _main_layout(d)
    w_up_pad = jnp.concatenate(
        [w_gla_up[l], jnp.zeros((IN_TN - GLA_RANK, w_gla_up.shape[-1]), w_gla_up.dtype)], axis=0)

    h, w_mg_t = _in_proj(xb, w_in_t, l, col_starts, n_main, MERGE_TN, tm=2048, tn=IN_TN)
    mkv = _matmul(mem_b, w_mem_kv, l, BF16, tm=mem_b.shape[0], tn=512, name="mem_kv")

    o_pool = _pool(h, w_pool[l].astype(BF16), pool_scale[l], bsz, seq,
                   dst["pool_u"], dst["pool_g"], ts=512)
    o_gla = _gla(h, w_up_pad, b_gla[l], gla_norm[l], bsz, seq,
                 (dst["gq"], dst["gk"], dst["gv"], dst["gg"], dst["glr"]), rows=512)
    o_swa = _swa(h, pos_f, sinks[l], bsz, seq, (dst["sq"], dst["sk"], dst["sv"], dst["sg"]))
    o_mem = _mem_attn(h, mkv, bsz, seq, mem_b.shape[0] // bsz, (dst["xq"], dst["xg"]), ts=512)

    y = _merge(xb, (o_pool, o_gla, o_swa, o_mem), w_mg_t, w_branch, l, tm=1024, tn=MERGE_TN)
    z = _matmul(y, w_out, l, F32, tm=2048, tn=256, name="out_proj", resid=x, alpha=alpha)
    return _layer_norm(z, ln_g[l], ln_b[l], tm=512, emit_bf16=emit_bf16)


def kernel(x, mem, positions, w_in, w_pool, pool_scale, w_gla_up, b_gla, gla_norm, sinks,
           w_mem_kv, w_branch, w_out, ln_g, ln_b):
    bsz, seq, d = x.shape
    depth = w_in.shape[0]
    alpha = (2 * depth) ** 0.25
    xf = x.reshape(bsz * seq, d)
    xb = xf.astype(BF16)
    mem_b = mem.reshape(bsz * mem.shape[1], d).astype(BF16)
    pos_f = positions.astype(F32).reshape(bsz * seq, 1)
    w_in_t = jnp.swapaxes(w_in, 1, 2)
    for l in range(depth):
        xf, xb = _hybrid_layer(
            l, xf, xb, mem_b, pos_f, w_in_t, w_pool, pool_scale, w_gla_up, b_gla,
            gla_norm, sinks, w_mem_kv, w_branch, w_out, ln_g, ln_b,
            alpha, l + 1 < depth, bsz, seq)
    return xf.reshape(bsz, seq, d)
```

```python
import functools
import math

import jax
import jax.numpy as jnp
import numpy as np
from jax import lax
from jax.experimental import pallas as pl
from jax.experimental.pallas import tpu as pltpu

F32 = jnp.float32
BF16 = jnp.bfloat16

N_BRANCH = 4
POOL_WINDOWS = (2, 4, 8, 16)
GLA_HEADS = 4
GLA_RANK = 16
GLA_TAU = 16.0
GLA_CHUNK = 64
GLA_SUB = 16
GLA_SAFE_LOG2 = 24.0
GLA_SAFE_KEY = 2.0 ** 100
SWA_HEAD_DIM = 64
SWA_GROUP = 8
WINDOW = 128
SWA_BLOCK = 128
ROPE_THETA = 500000.0
ROPE_DIM = SWA_HEAD_DIM // 4
XA_HEADS = 4
LN_EPS = 1e-5

LANES = 128
SUBLANES = 8
V7X_VMEM_BYTES = 64 * 1024 * 1024
VMEM_CAP = V7X_VMEM_BYTES - 8 * 1024 * 1024
VMEM_TEMP_BYTES = 12 * 1024 * 1024


def _vmem_limit(streamed_bytes, resident_bytes=0):
    return int(min(VMEM_CAP, 2 * streamed_bytes + resident_bytes + VMEM_TEMP_BYTES))


def _nbytes(shape, dtype):
    return int(np.prod(shape)) * jnp.dtype(dtype).itemsize


def _silu(g):
    return g * jax.nn.sigmoid(g)


def _div_pow2(x, n):
    assert n & (n - 1) == 0
    return x >> (n.bit_length() - 1)


def _mod_pow2(x, n):
    assert n & (n - 1) == 0
    return x & (n - 1)


NT_DIMS = (((1,), (1,)), ((), ()))


def _mm_kernel(a_ref, w_ref, o_ref):
    w = w_ref[...].astype(BF16)
    o_ref[...] = jnp.dot(a_ref[...], w, preferred_element_type=F32).astype(o_ref.dtype)


def _mm_resid_kernel(a_ref, w_ref, x_ref, o_ref, *, alpha):
    w = w_ref[...].astype(BF16)
    o_ref[...] = alpha * x_ref[...] + jnp.dot(a_ref[...], w, preferred_element_type=F32)


def _matmul(a, w, layer, out_dtype, tm, tn, name, resid=None, alpha=None):
    m, k = a.shape
    n = w.shape[-1]
    assert m % tm == 0 and n % tn == 0
    resident = _nbytes((tm, k), a.dtype)
    streamed = _nbytes((k, tn), w.dtype) + _nbytes((tm, tn), out_dtype)
    in_specs = [pl.BlockSpec((tm, k), lambda i, j: (i, 0), pipeline_mode=pl.Buffered(1)),
                pl.BlockSpec((None, k, tn), lambda i, j: (layer, 0, j))]
    args = [a, w]
    if resid is None:
        kern = _mm_kernel
    else:
        kern = functools.partial(_mm_resid_kernel, alpha=alpha)
        in_specs.append(pl.BlockSpec((tm, tn), lambda i, j: (i, j)))
        args.append(resid)
        streamed += _nbytes((tm, tn), resid.dtype)
    return pl.pallas_call(
        kern,
        out_shape=jax.ShapeDtypeStruct((m, n), out_dtype),
        grid=(m // tm, n // tn),
        in_specs=in_specs,
        out_specs=pl.BlockSpec((tm, tn), lambda i, j: (i, j)),
        compiler_params=pltpu.CompilerParams(
            dimension_semantics=("parallel", "arbitrary"),
            vmem_limit_bytes=_vmem_limit(streamed, resident)),
        name=name,
    )(*args)


def _in_proj_kernel(tab_ref, x_ref, *refs):
    *w_refs, o_ref = refs
    x = x_ref[...]
    tn = w_refs[0].shape[1]
    for t, w_ref in enumerate(w_refs):
        w = w_ref[0].astype(BF16)
        o_ref[:, t * tn:(t + 1) * tn] = lax.dot_general(x, w, NT_DIMS, preferred_element_type=F32)


def _in_proj(xb, w_in_t, layer, col_starts, tm, tn, per_step):
    m, k = xb.shape
    nblk = len(col_starts)
    assert m % tm == 0 and nblk % per_step == 0 and all(int(s) % SUBLANES == 0 for s in col_starts)
    resident = _nbytes((tm, k), xb.dtype)
    streamed = per_step * (_nbytes((tn, k), w_in_t.dtype) + _nbytes((tm, tn), F32))

    def window(t):
        return pl.BlockSpec(
            (pl.Element(1), pl.Element(tn), pl.Element(k)),
            lambda i, j, tab: (layer, pl.multiple_of(tab[j * per_step + t], SUBLANES), 0))

    return pl.pallas_call(
        _in_proj_kernel,
        out_shape=jax.ShapeDtypeStruct((m, nblk * tn), F32),
        grid_spec=pltpu.PrefetchScalarGridSpec(
            num_scalar_prefetch=1,
            grid=(m // tm, nblk // per_step),
            in_specs=[pl.BlockSpec((tm, k), lambda i, j, tab: (i, 0), pipeline_mode=pl.Buffered(1))]
            + [window(t) for t in range(per_step)],
            out_specs=pl.BlockSpec((tm, per_step * tn), lambda i, j, tab: (i, j))),
        compiler_params=pltpu.CompilerParams(
            dimension_semantics=("parallel", "arbitrary"),
            vmem_limit_bytes=_vmem_limit(streamed, resident)),
        name="in_proj",
    )(jnp.asarray(np.asarray(col_starts, np.int32)), xb, *([w_in_t] * per_step))


def _merge_kernel(x_ref, o0, o1, o2, o3, g_ref, w_ref, y_ref):
    x = x_ref[...]
    tn = y_ref.shape[1]
    acc = None
    for b, o_ref in enumerate((o0, o1, o2, o3)):
        gate = lax.dot_general(x, g_ref[b * tn:(b + 1) * tn, :], NT_DIMS, preferred_element_type=F32)
        proj = jnp.dot(o_ref[...], w_ref[b].astype(BF16), preferred_element_type=F32)
        term = jax.nn.sigmoid(gate) * proj
        acc = term if acc is None else acc + term
    y_ref[...] = acc.astype(y_ref.dtype)


def _merge(xb, branches, w_mg_t, w_branch, layer, tm, tn):
    m, d = xb.shape
    bw = branches[0].shape[1]
    nj = d // tn
    in_specs = [pl.BlockSpec((tm, d), lambda i, j: (i, 0), pipeline_mode=pl.Buffered(1))]
    in_specs += [pl.BlockSpec((tm, bw), lambda i, j: (i, 0), pipeline_mode=pl.Buffered(1))
                 for _ in range(N_BRANCH)]
    in_specs += [pl.BlockSpec((None, N_BRANCH * tn, d), lambda i, j: (j, 0, 0)),
                 pl.BlockSpec((None, N_BRANCH, bw, tn), lambda i, j: (layer, 0, 0, j))]
    resident = _nbytes((tm, d), BF16) + N_BRANCH * _nbytes((tm, bw), BF16)
    streamed = (N_BRANCH * (_nbytes((tn, d), BF16) + _nbytes((bw, tn), w_branch.dtype))
                + _nbytes((tm, tn), BF16))
    return pl.pallas_call(
        _merge_kernel,
        out_shape=jax.ShapeDtypeStruct((m, d), BF16),
        grid=(m // tm, nj),
        in_specs=in_specs,
        out_specs=pl.BlockSpec((tm, tn), lambda i, j: (i, j)),
        compiler_params=pltpu.CompilerParams(
            dimension_semantics=("parallel", "arbitrary"),
            vmem_limit_bytes=_vmem_limit(streamed, resident)),
        name="merge",
    )(xb, *branches, w_mg_t, w_branch)


def _ln_kernel(z_ref, g_ref, b_ref, o_ref, ob_ref=None):
    z = z_ref[...]
    mu = jnp.mean(z, axis=-1, keepdims=True)
    zc = z - mu
    var = jnp.mean(zc * zc, axis=-1, keepdims=True)
    y = zc * lax.rsqrt(var + LN_EPS) * g_ref[...] + b_ref[...]
    o_ref[...] = y
    if ob_ref is not None:
        ob_ref[...] = y.astype(BF16)


def _layer_norm(z, g, b, tm, emit_bf16):
    m, d = z.shape
    out_shape = [jax.ShapeDtypeStruct((m, d), F32)]
    out_specs = [pl.BlockSpec((tm, d), lambda i: (i, 0))]
    if emit_bf16:
        out_shape.append(jax.ShapeDtypeStruct((m, d), BF16))
        out_specs.append(pl.BlockSpec((tm, d), lambda i: (i, 0)))
    blocks = 2 * _nbytes((tm, d), F32) + _nbytes((tm, d), BF16)
    res = pl.pallas_call(
        _ln_kernel,
        out_shape=out_shape,
        grid=(m // tm,),
        in_specs=[pl.BlockSpec((tm, d), lambda i: (i, 0)),
                  pl.BlockSpec((1, d), lambda i: (0, 0)),
                  pl.BlockSpec((1, d), lambda i: (0, 0))],
        out_specs=out_specs,
        compiler_params=pltpu.CompilerParams(
            dimension_semantics=("parallel",), vmem_limit_bytes=_vmem_limit(blocks)),
        name="layer_norm",
    )(z, g.reshape(1, d), b.reshape(1, d))
    return res if emit_bf16 else (res[0], None)


POOL_HIST = max(POOL_WINDOWS)
POOL_PAD = 2 * POOL_HIST


def _pool_kernel(u_ref, g_ref, wp_ref, sc_ref, o_ref, buf, lvl_a, lvl_b, *, ts, group):
    i = pl.program_id(1)
    ext = POOL_PAD + ts

    @pl.when(i == 0)
    def _():
        buf[0:POOL_PAD, :] = jnp.zeros((POOL_PAD, buf.shape[1]), F32)

    buf[POOL_PAD:ext, :] = u_ref[...]
    src = buf
    level_refs = []
    for lvl, dst in enumerate((lvl_a, lvl_b, lvl_a, lvl_b)[:len(POOL_WINDOWS)], start=1):
        lag = 2 ** (lvl - 1)
        r0 = SUBLANES * lvl
        c0 = (lvl - 1) * group
        dst[r0:ext, c0:] = src[r0:ext, c0:] + src[r0 - lag:ext - lag, c0:]
        level_refs.append(dst)
        src = dst

    t = i * ts + lax.broadcasted_iota(jnp.int32, (ts, 1), 0)
    outs = []
    for gi, w in enumerate(POOL_WINDOWS):
        assert w == 2 ** (gi + 1)
        cols = slice(gi * group, (gi + 1) * group)
        acc = level_refs[gi][POOL_PAD:ext, cols]
        cnt = jnp.minimum(t + 1, w).astype(F32)
        pooled = acc / cnt - buf[POOL_PAD:ext, cols]
        outs.append(jnp.dot(pooled.astype(BF16), wp_ref[gi], preferred_element_type=F32))
    o = jnp.concatenate(outs, axis=1) * sc_ref[...] * _silu(g_ref[...])
    o_ref[...] = o.astype(o_ref.dtype)
    buf[POOL_PAD - POOL_HIST:POOL_PAD, :] = buf[ext - POOL_HIST:ext, :]


def _pool(h, w_pool_b, pool_scale, bsz, seq, col_u, col_g, ts):
    bw = pool_scale.shape[-1]
    group = bw // len(POOL_WINDOWS)
    nblk = seq // ts
    blocks = 2 * _nbytes((ts, bw), F32) + _nbytes((ts, bw), BF16)
    scratch = [pltpu.VMEM((ts + POOL_PAD, bw), F32) for _ in range(3)]
    resident = sum(_nbytes(sc.shape, sc.dtype) for sc in scratch)
    return pl.pallas_call(
        functools.partial(_pool_kernel, ts=ts, group=group),
        out_shape=jax.ShapeDtypeStruct((bsz * seq, bw), BF16),
        grid=(bsz, nblk),
        in_specs=[pl.BlockSpec((ts, bw), lambda b, i: (b * nblk + i, col_u // bw)),
                  pl.BlockSpec((ts, bw), lambda b, i: (b * nblk + i, col_g // bw)),
                  pl.BlockSpec(w_pool_b.shape, lambda b, i: (0, 0, 0)),
                  pl.BlockSpec((1, bw), lambda b, i: (0, 0))],
        out_specs=pl.BlockSpec((ts, bw), lambda b, i: (b * nblk + i, 0)),
        scratch_shapes=scratch,
        compiler_params=pltpu.CompilerParams(
            dimension_semantics=("parallel", "arbitrary"),
            vmem_limit_bytes=_vmem_limit(blocks, resident)),
        name="pool_mixer",
    )(h, h, w_pool_b, pool_scale.reshape(1, bw))


def _split_bf16(x, parts):
    out = []
    r = x
    for _ in range(parts):
        p = r.astype(BF16)
        out.append(p)
        r = r - p.astype(F32)
    return out


LOG2E = math.log2(math.e)


def _gla_cumsum_operator(rows):
    r = np.arange(rows)
    tri = ((r[:, None] // GLA_CHUNK == r[None, :] // GLA_CHUNK) & (r[None, :] <= r[:, None]))
    return jnp.asarray(tri, BF16)


def _gla_kernel(q_ref, k_ref, v_ref, gg_ref, lr_ref, wup_ref, bg_ref, nrm_ref, tri_ref,
                o_ref, state_ref, b_ref, *, rows, dk, dv):
    i = pl.program_id(1)
    C = GLA_CHUNK
    nsub = C // GLA_SUB

    @pl.when(i == 0)
    def _():
        state_ref[...] = jnp.zeros(state_ref.shape, F32)

    lr_h, lr_m = _split_bf16(lr_ref[...], 2)
    w_h, w_m = _split_bf16(wup_ref[...], 2)
    z = (jnp.dot(lr_h, w_h, preferred_element_type=F32)
         + jnp.dot(lr_h, w_m, preferred_element_type=F32)
         + jnp.dot(lr_m, w_h, preferred_element_type=F32)) + bg_ref[...]
    log2_a = (jnp.minimum(z, 0.0) - jnp.log(1.0 + jnp.exp(-jnp.abs(z)))) * (LOG2E / GLA_TAU)

    bsum = None
    for part in _split_bf16(log2_a, 3):
        t = jnp.dot(tri_ref[...], part, preferred_element_type=F32)
        bsum = t if bsum is None else bsum + t
    b_ref[...] = bsum

    row = lax.broadcasted_iota(jnp.int32, (C, 1), 0)
    col = lax.broadcasted_iota(jnp.int32, (1, C), 1)
    row_sub = _div_pow2(row, GLA_SUB)
    col_sub = _div_pow2(col, GLA_SUB)
    lane_c = lax.broadcasted_iota(jnp.int32, (GLA_SUB, C), 1)
    scale = dk ** -0.5

    def sub_heads(x):
        return jnp.concatenate(
            [jnp.broadcast_to(x[a * GLA_SUB:a * GLA_SUB + 1, :], (GLA_SUB, dk))
             for a in range(nsub)], axis=0)

    def chunk_body(c, carry, *, bounded):
        r0 = pl.multiple_of(c * C, C)
        for h in range(GLA_HEADS):
            kc = slice(h * dk, (h + 1) * dk)
            vc = slice(h * dv, (h + 1) * dv)
            qs = q_ref[pl.ds(r0, C), kc] * scale
            k = k_ref[pl.ds(r0, C), kc]
            v = v_ref[pl.ds(r0, C), vc].astype(BF16)
            b = b_ref[pl.ds(r0, C), kc]
            b_last = b[C - 1:C, :]
            st = state_ref[h]

            q_in = (qs * jnp.exp2(b)).astype(BF16)
            o = lax.dot_general(q_in, st.astype(BF16), NT_DIMS, preferred_element_type=F32)

            q_t = qs * jnp.exp2(b - sub_heads(b))
            lhs = jnp.concatenate(
                [jnp.where(row_sub == a, q_t, 0.0) for a in range(nsub)], axis=1).astype(BF16)
            key_exp = [b[a * GLA_SUB:a * GLA_SUB + 1, :] - b for a in range(nsub)]
            if not bounded:
                key_exp = [jnp.minimum(e, 0.0) for e in key_exp]
            rhs = jnp.concatenate([k * jnp.exp2(e) for e in key_exp], axis=1).astype(BF16)
            a_fac = lax.dot_general(lhs, rhs, NT_DIMS, preferred_element_type=F32)

            if bounded:
                attn = jnp.where(row >= col, a_fac, 0.0)
            else:
                diag_blocks = []
                for a in range(nsub):
                    rs = slice(a * GLA_SUB, (a + 1) * GLA_SUB)
                    qa = qs[rs, :]
                    ba = b[rs, :]
                    blk = jnp.zeros((GLA_SUB, C), F32)
                    for jj in range(GLA_SUB):
                        j = a * GLA_SUB + jj
                        tj = qa * k[j:j + 1, :] * jnp.exp2(ba - b[j:j + 1, :])
                        cj = jnp.sum(tj, axis=1, keepdims=True)
                        blk = jnp.where(lane_c == j, cj, blk)
                    diag_blocks.append(blk)
                a_diag = jnp.concatenate(diag_blocks, axis=0)
                attn = jnp.where(row_sub > col_sub, a_fac,
                                 jnp.where((row_sub == col_sub) & (row >= col), a_diag, 0.0))
            o = o + jnp.dot(attn.astype(BF16), v, preferred_element_type=F32)

            k_dec = (k * jnp.exp2(b_last - b)).astype(BF16)
            upd = lax.dot_general(v, k_dec, (((0,), (0,)), ((), ())),
                                  preferred_element_type=F32)
            state_ref[h] = st * jnp.exp2(b_last) + upd

            ms = jnp.mean(o * o, axis=-1, keepdims=True)
            on = o * lax.rsqrt(ms + LN_EPS) * nrm_ref[:, vc]
            o_ref[pl.ds(r0, C), vc] = (on * _silu(gg_ref[pl.ds(r0, C), vc])).astype(o_ref.dtype)
        return carry

    bounded = (-jnp.min(bsum) < GLA_SAFE_LOG2) & (jnp.max(jnp.abs(k_ref[...])) < GLA_SAFE_KEY)

    @pl.when(bounded)
    def _():
        lax.fori_loop(0, rows // C, functools.partial(chunk_body, bounded=True), 0, unroll=2)

    @pl.when(jnp.logical_not(bounded))
    def _():
        lax.fori_loop(0, rows // C, functools.partial(chunk_body, bounded=False), 0)


def _gla(h, w_up_pad, b_gla, gla_norm, bsz, seq, cols, rows):
    col_q, col_k, col_v, col_g, col_lr = cols
    dkt = b_gla.shape[-1]
    dvt = gla_norm.shape[-1]
    dk, dv = dkt // GLA_HEADS, dvt // GLA_HEADS
    lrw = w_up_pad.shape[0]
    nblk = seq // rows
    tri = _gla_cumsum_operator(rows)
    blocks = (2 * _nbytes((rows, dkt), F32) + 2 * _nbytes((rows, dvt), F32)
              + _nbytes((rows, lrw), F32) + _nbytes((rows, dvt), BF16))
    resident = (_nbytes((rows, dkt), F32) + _nbytes((GLA_HEADS, dv, dk), F32)
                + _nbytes(tri.shape, BF16))
    const = lambda b, i: (0, 0)
    return pl.pallas_call(
        functools.partial(_gla_kernel, rows=rows, dk=dk, dv=dv),
        out_shape=jax.ShapeDtypeStruct((bsz * seq, dvt), BF16),
        grid=(bsz, nblk),
        in_specs=[pl.BlockSpec((rows, dkt), lambda b, i: (b * nblk + i, col_q // dkt)),
                  pl.BlockSpec((rows, dkt), lambda b, i: (b * nblk + i, col_k // dkt)),
                  pl.BlockSpec((rows, dvt), lambda b, i: (b * nblk + i, col_v // dvt)),
                  pl.BlockSpec((rows, dvt), lambda b, i: (b * nblk + i, col_g // dvt)),
                  pl.BlockSpec((rows, lrw), lambda b, i: (b * nblk + i, col_lr // lrw)),
                  pl.BlockSpec((lrw, dkt), const),
                  pl.BlockSpec((1, dkt), const),
                  pl.BlockSpec((1, dvt), const),
                  pl.BlockSpec(tri.shape, const)],
        out_specs=pl.BlockSpec((rows, dvt), lambda b, i: (b * nblk + i, 0)),
        scratch_shapes=[pltpu.VMEM((GLA_HEADS, dv, dk), F32),
                        pltpu.VMEM((rows, dkt), F32)],
        compiler_params=pltpu.CompilerParams(
            dimension_semantics=("parallel", "arbitrary"),
            vmem_limit_bytes=_vmem_limit(blocks, resident)),
        name="gla_mixer",
    )(h, h, h, h, h, w_up_pad, b_gla.reshape(1, dkt), gla_norm.reshape(1, dvt), tri)


def _rope_lane_table():
    half = ROPE_DIM // 2
    inv_freq = ROPE_THETA ** (-np.arange(0, ROPE_DIM, 2, dtype=np.float32) / ROPE_DIM)
    lane = np.arange(LANES) % SWA_HEAD_DIM
    tab = np.where(lane < ROPE_DIM, inv_freq[lane % half], 0.0).astype(np.float32)
    return jnp.asarray(tab.reshape(1, LANES))


SWA_ROW_CHUNK = 32
SWA_BLOCKS_PER_STEP = 2


def _swa_kernel(sink_ref, q_ref, k_ref, v_ref, g_ref, pos_ref, inv_ref, wg_ref, o_ref, wgs_ref,
                *scratch, n_pairs, n_sub):
    for t in range(wgs_ref.shape[0]):
        rows_t = wgs_ref.shape[1]
        wgs_ref[t] = wg_ref[0, t * rows_t:(t + 1) * rows_t, :].astype(wgs_ref.dtype)
    step = pl.program_id(1)
    for sb in range(n_sub):
        rows = pl.ds(sb * SWA_BLOCK, SWA_BLOCK)
        _swa_block(step * n_sub + sb, sink_ref, q_ref.at[rows], k_ref.at[rows], v_ref.at[rows],
                   g_ref.at[rows], pos_ref.at[rows], inv_ref, o_ref.at[rows], *scratch,
                   n_pairs=n_pairs)


def _swa_block(n, sink_ref, q_ref, k_ref, v_ref, g_ref, pos_ref, inv_ref, o_ref,
               wk_ref, wv_ref, qr_ref, s_ref, p_ref, es_ref, bias_ref, *, n_pairs):
    blk = SWA_BLOCK
    half = ROPE_DIM // 2
    hd = SWA_HEAD_DIM
    ppg = n_pairs // 2
    rc_rows = SWA_ROW_CHUNK

    @pl.when(n == 0)
    def _():
        wk_ref[...] = jnp.zeros(wk_ref.shape, BF16)
        row = lax.broadcasted_iota(jnp.int32, (4 * blk, 2 * LANES), 0)
        col = lax.broadcasted_iota(jnp.int32, (4 * blk, 2 * LANES), 1)
        ones = ((col >= LANES) & ((col >= LANES + hd) == (row >= 2 * blk))).astype(BF16)
        for g in range(2):
            wv_ref[g] = ones

    lane = lax.broadcasted_iota(jnp.int32, (1, LANES), 1)
    lane_h = _mod_pow2(lane, hd)
    lo = lane < hd
    ang = pos_ref[...] * inv_ref[...]
    cos = jnp.cos(ang)
    sin = jnp.sin(ang)

    def rope(x):
        x_up = pltpu.roll(x, LANES - half, axis=1)
        x_dn = pltpu.roll(x, half, axis=1)
        return x * cos + jnp.where(lane_h < half, -x_up, x_dn) * sin

    for g in range(2):
        for part in range(2):
            base = part * 2 * blk
            wk_ref[g, base:base + blk, :] = wk_ref[g, base + blk:base + 2 * blk, :]
            wv_ref[g, base:base + blk, 0:LANES] = wv_ref[g, base + blk:base + 2 * blk, 0:LANES]
    k_r = rope(k_ref[...])
    v_c = v_ref[...]
    k_sw = pltpu.roll(k_r, hd, axis=1)
    v_sw = pltpu.roll(v_c, hd, axis=1)
    zero = jnp.zeros_like(k_r)
    for g in range(2):
        k_lo, k_hi = (k_r, k_sw) if g == 0 else (k_sw, k_r)
        v_lo, v_hi = (v_c, v_sw) if g == 0 else (v_sw, v_c)
        wk_ref[g, blk:2 * blk, :] = jnp.where(lo, k_lo, zero).astype(BF16)
        wk_ref[g, 3 * blk:4 * blk, :] = jnp.where(lo, zero, k_hi).astype(BF16)
        wv_ref[g, blk:2 * blk, 0:LANES] = jnp.where(lo, v_lo, zero).astype(BF16)
        wv_ref[g, 3 * blk:4 * blk, 0:LANES] = jnp.where(lo, zero, v_hi).astype(BF16)

    r = lax.broadcasted_iota(jnp.int32, (blk, 2 * blk), 0)
    c = lax.broadcasted_iota(jnp.int32, (blk, 2 * blk), 1)
    valid = (c > r) & (c <= r + WINDOW) & ((n > 0) | (c >= blk))
    bias_ref[...] = jnp.where(valid, 0.0, -jnp.inf).astype(F32)

    qscale = hd ** -0.5
    for p in range(n_pairs):
        g, pp = divmod(p, ppg)
        qr_ref[g, pp * blk:(pp + 1) * blk, :] = (
            rope(q_ref[:, p * LANES:(p + 1) * LANES]) * qscale).astype(BF16)

    for g in range(2):
        s_ref[g] = lax.dot_general(qr_ref[g], wk_ref[g], NT_DIMS, preferred_element_type=F32)

    for g in range(2):
        for pp in range(ppg):
            for hh in range(2):
                sink = sink_ref[2 * (g * ppg + pp) + hh]
                kc = slice(hh * 2 * blk, (hh + 1) * 2 * blk)
                for rc in range(blk // rc_rows):
                    rows = slice(pp * blk + rc * rc_rows, pp * blk + (rc + 1) * rc_rows)
                    sh = s_ref[g, rows, kc] + bias_ref[rc * rc_rows:(rc + 1) * rc_rows, :]
                    m = jnp.maximum(jnp.max(sh, axis=-1, keepdims=True), sink)
                    p_ref[g, rows, kc] = jnp.exp(sh - m).astype(BF16)
                    es_ref[g, rows, hh * hd:(hh + 1) * hd] = jnp.broadcast_to(
                        jnp.exp(sink - m), (rc_rows, hd))

    for g in range(2):
        o2 = jnp.dot(p_ref[g], wv_ref[g], preferred_element_type=F32)
        for pp in range(ppg):
            rows = slice(pp * blk, (pp + 1) * blk)
            cols = slice((g * ppg + pp) * LANES, (g * ppg + pp + 1) * LANES)
            den = o2[rows, LANES:] + es_ref[g, rows, :]
            o_ref[:, cols] = (o2[rows, :LANES] / den * _silu(g_ref[:, cols])).astype(o_ref.dtype)


def _swa(h, pos_f, sinks, bsz, seq, cols, w_in_t, layer, gate_row0, gate_tn):
    col_q, col_k, col_v, col_g = cols
    n_heads = sinks.shape[0]
    bw = n_heads * SWA_HEAD_DIM
    n_pairs = bw // LANES
    n_sub = SWA_BLOCKS_PER_STEP
    blk = n_sub * SWA_BLOCK
    nblk = seq // blk
    scratch = [pltpu.VMEM((2, 4 * SWA_BLOCK, LANES), BF16),
               pltpu.VMEM((2, 4 * SWA_BLOCK, 2 * LANES), BF16),
               pltpu.VMEM((2, 4 * SWA_BLOCK, LANES), BF16),
               pltpu.VMEM((2, 4 * SWA_BLOCK, 4 * SWA_BLOCK), F32),
               pltpu.VMEM((2, 4 * SWA_BLOCK, 4 * SWA_BLOCK), BF16),
               pltpu.VMEM((2, 4 * SWA_BLOCK, LANES), F32),
               pltpu.VMEM((SWA_BLOCK, 2 * SWA_BLOCK), F32)]
    k = w_in_t.shape[-1]
    n_steps = bsz * nblk
    slab = N_BRANCH * k // n_steps
    per_slab = slab // gate_tn
    slabs_per_branch = k // slab
    assert N_BRANCH * k % n_steps == 0 and slab % gate_tn == 0 and k % slab == 0
    assert gate_row0 % SUBLANES == 0
    blocks = (2 * _nbytes((blk, bw), F32) + 3 * _nbytes((blk, LANES), F32) + _nbytes((blk, bw), BF16)
              + _nbytes((slab, k), w_in_t.dtype) + _nbytes((slab, k), BF16))
    resident = sum(_nbytes(sc.shape, sc.dtype) for sc in scratch)

    def slab_src(b, i):
        return pl.multiple_of(gate_row0 + (b * nblk + i) * slab, SUBLANES)

    def slab_dst(b, i):
        s_idx = b * nblk + i
        return (s_idx % slabs_per_branch, s_idx // slabs_per_branch, 0)

    return pl.pallas_call(
        functools.partial(_swa_kernel, n_pairs=n_pairs, n_sub=n_sub),
        out_shape=[jax.ShapeDtypeStruct((bsz * seq, bw), BF16),
                   jax.ShapeDtypeStruct((k // gate_tn, N_BRANCH * gate_tn, k), BF16)],
        grid=(bsz, nblk),
        in_specs=[pl.BlockSpec(memory_space=pltpu.SMEM),
                  pl.BlockSpec((blk, bw), lambda b, i: (b * nblk + i, col_q // bw)),
                  pl.BlockSpec((blk, LANES), lambda b, i: (b * nblk + i, col_k // LANES)),
                  pl.BlockSpec((blk, LANES), lambda b, i: (b * nblk + i, col_v // LANES)),
                  pl.BlockSpec((blk, bw), lambda b, i: (b * nblk + i, col_g // bw)),
                  pl.BlockSpec((blk, 1), lambda b, i: (b * nblk + i, 0)),
                  pl.BlockSpec((1, LANES), lambda b, i: (0, 0)),
                  pl.BlockSpec((pl.Element(1), pl.Element(slab), pl.Element(k)),
                               lambda b, i: (layer, slab_src(b, i), 0))],
        out_specs=[pl.BlockSpec((blk, bw), lambda b, i: (b * nblk + i, 0)),
                   pl.BlockSpec((per_slab, gate_tn, k), slab_dst)],
        scratch_shapes=scratch,
        compiler_params=pltpu.CompilerParams(
            dimension_semantics=("parallel", "arbitrary"),
            vmem_limit_bytes=_vmem_limit(blocks, resident)),
        name="swa_mixer",
    )(sinks, h, h, h, h, pos_f, _rope_lane_table(), w_in_t)


def _mem_kernel(q_ref, g_ref, mk_ref, mv_ref, o_ref, *, hd):
    scale = hd ** -0.5
    for h in range(XA_HEADS):
        cols = slice(h * hd, (h + 1) * hd)
        qh = q_ref[:, cols].astype(BF16)
        s = lax.dot_general(qh, mk_ref[:, cols], (((1,), (1,)), ((), ())),
                            preferred_element_type=F32) * scale
        m = jnp.max(s, axis=-1, keepdims=True)
        e = jnp.exp(s - m)
        p = e / jnp.sum(e, axis=-1, keepdims=True)
        o = jnp.dot(p.astype(BF16), mv_ref[:, cols], preferred_element_type=F32)
        o_ref[:, cols] = (o * _silu(g_ref[:, cols])).astype(o_ref.dtype)


def _mem_attn(h, mkv, bsz, seq, mem_len, cols, ts):
    col_q, col_g = cols
    bw = mkv.shape[1] // 2
    hd = bw // XA_HEADS
    nblk = seq // ts
    blocks = 2 * _nbytes((ts, bw), F32) + 2 * _nbytes((mem_len, bw), BF16) + _nbytes((ts, bw), BF16)
    return pl.pallas_call(
        functools.partial(_mem_kernel, hd=hd),
        out_shape=jax.ShapeDtypeStruct((bsz * seq, bw), BF16),
        grid=(bsz, nblk),
        in_specs=[pl.BlockSpec((ts, bw), lambda b, i: (b * nblk + i, col_q // bw)),
                  pl.BlockSpec((ts, bw), lambda b, i: (b * nblk + i, col_g // bw)),
                  pl.BlockSpec((mem_len, bw), lambda b, i: (b, 0)),
                  pl.BlockSpec((mem_len, bw), lambda b, i: (b, 1))],
        out_specs=pl.BlockSpec((ts, bw), lambda b, i: (b * nblk + i, 0)),
        compiler_params=pltpu.CompilerParams(
            dimension_semantics=("parallel", "parallel"), vmem_limit_bytes=_vmem_limit(blocks)),
        name="mem_attn",
    )(h, h, mkv, mkv)


IN_TN = 2 * LANES
MERGE_TN = 2 * LANES


def _main_layout(d_model):
    bw = d_model // 4
    dkt, dvt = bw // 2, bw
    kvw = 2 * SWA_HEAD_DIM
    names = ["pool_u", "pool_g", "gq", "gk", "gv", "gg", "glr", "sq", "sk", "sv", "sg", "xq", "xg"]
    widths = [bw, bw, dkt, dkt, dvt, dvt, GLA_RANK, bw, kvw, kvw, bw, bw, bw]
    src = {}
    off = 0
    for nme, w in zip(names, widths):
        src[nme] = (off, w)
        off += w
    n_main = off
    order = ["pool_u", "pool_g", "gv", "gg", "sq", "sg", "xq", "xg", "gq", "gk", "sk", "sv", "glr"]
    dst = {}
    cols = []
    off = 0
    for nme in order:
        o, w = src[nme]
        w_dst = w if nme != "glr" else IN_TN
        assert off % w_dst == 0
        dst[nme] = off
        cols.append(o + np.arange(w_dst))
        off += w_dst
    cols = np.concatenate(cols)
    assert off % IN_TN == 0 and cols.max() < n_main
    starts = cols[::IN_TN]
    assert np.all(cols.reshape(-1, IN_TN) == starts[:, None] + np.arange(IN_TN))
    return starts, dst, n_main


def _hybrid_layer(l, x, xb, mem_b, pos_f, w_in_t, w_pool, pool_scale, w_gla_up, b_gla, gla_norm,
                  sinks, w_mem_kv, w_branch, w_out, ln_g, ln_b, alpha, emit_bf16, bsz, seq):
    m, d = x.shape
    col_starts, dst, n_main = _main_layout(d)
    w_up_pad = jnp.concatenate(
        [w_gla_up[l], jnp.zeros((IN_TN - GLA_RANK, w_gla_up.shape[-1]), w_gla_up.dtype)], axis=0)

    h = _in_proj(xb, w_in_t, l, col_starts, tm=2048, tn=IN_TN, per_step=2)
    mkv = _matmul(mem_b, w_mem_kv, l, BF16, tm=mem_b.shape[0], tn=512, name="mem_kv")

    o_pool = _pool(h, w_pool[l].astype(BF16), pool_scale[l], bsz, seq,
                   dst["pool_u"], dst["pool_g"], ts=512)
    o_gla = _gla(h, w_up_pad, b_gla[l], gla_norm[l], bsz, seq,
                 (dst["gq"], dst["gk"], dst["gv"], dst["gg"], dst["glr"]), rows=512)
    o_swa, w_mg_t = _swa(h, pos_f, sinks[l], bsz, seq, (dst["sq"], dst["sk"], dst["sv"], dst["sg"]),
                         w_in_t, l, n_main, MERGE_TN)
    o_mem = _mem_attn(h, mkv, bsz, seq, mem_b.shape[0] // bsz, (dst["xq"], dst["xg"]), ts=512)

    y = _merge(xb, (o_pool, o_gla, o_swa, o_mem), w_mg_t, w_branch, l, tm=1024, tn=MERGE_TN)
    z = _matmul(y, w_out, l, F32, tm=2048, tn=256, name="out_proj", resid=x, alpha=alpha)
    return _layer_norm(z, ln_g[l], ln_b[l], tm=512, emit_bf16=emit_bf16)


def kernel(x, mem, positions, w_in, w_pool, pool_scale, w_gla_up, b_gla, gla_norm, sinks,
           w_mem_kv, w_branch, w_out, ln_g, ln_b):
    bsz, seq, d = x.shape
    depth = w_in.shape[0]
    alpha = (2 * depth) ** 0.25
    xf = x.reshape(bsz * seq, d)
    xb = xf.astype(BF16)
    mem_b = mem.reshape(bsz * mem.shape[1], d).astype(BF16)
    pos_f = positions.astype(F32).reshape(bsz * seq, 1)
    w_in_t = jnp.swapaxes(w_in, 1, 2)
    for l in range(depth):
        xf, xb = _hybrid_layer(
            l, xf, xb, mem_b, pos_f, w_in_t, w_pool, pool_scale, w_gla_up, b_gla,
            gla_norm, sinks, w_mem_kv, w_branch, w_out, ln_g, ln_b,
            alpha, l + 1 < depth, bsz, seq)
    return xf.reshape(bsz, seq, d)
```

```python
import functools
import math

import jax
import jax.numpy as jnp
import numpy as np
from jax import lax
from jax.experimental import pallas as pl
from jax.experimental.pallas import tpu as pltpu

F32 = jnp.float32
BF16 = jnp.bfloat16

N_BRANCH = 4
POOL_WINDOWS = (2, 4, 8, 16)
GLA_HEADS = 4
GLA_RANK = 16
GLA_TAU = 16.0
GLA_CHUNK = 64
GLA_SUB = 16
GLA_SAFE_LOG2 = 24.0
GLA_SAFE_KEY = 2.0 ** 100
SWA_HEAD_DIM = 64
SWA_GROUP = 8
WINDOW = 128
SWA_BLOCK = 128
ROPE_THETA = 500000.0
ROPE_DIM = SWA_HEAD_DIM // 4
XA_HEADS = 4
LN_EPS = 1e-5

LANES = 128
SUBLANES = 8
V7X_VMEM_BYTES = 64 * 1024 * 1024
VMEM_CAP = V7X_VMEM_BYTES - 8 * 1024 * 1024
VMEM_TEMP_BYTES = 12 * 1024 * 1024


def _vmem_limit(streamed_bytes, resident_bytes=0):
    return int(min(VMEM_CAP, 2 * streamed_bytes + resident_bytes + VMEM_TEMP_BYTES))


def _nbytes(shape, dtype):
    return int(np.prod(shape)) * jnp.dtype(dtype).itemsize


def _silu(g):
    return g * jax.nn.sigmoid(g)


def _div_pow2(x, n):
    assert n & (n - 1) == 0
    return x >> (n.bit_length() - 1)


def _mod_pow2(x, n):
    assert n & (n - 1) == 0
    return x & (n - 1)


NT_DIMS = (((1,), (1,)), ((), ()))


def _mm_kernel(a_ref, w_ref, o_ref):
    w = w_ref[...].astype(BF16)
    o_ref[...] = jnp.dot(a_ref[...], w, preferred_element_type=F32).astype(o_ref.dtype)


def _mm_resid_kernel(a_ref, w_ref, x_ref, o_ref, *, alpha):
    w = w_ref[...].astype(BF16)
    o_ref[...] = alpha * x_ref[...] + jnp.dot(a_ref[...], w, preferred_element_type=F32)


def _matmul(a, w, layer, out_dtype, tm, tn, name, resid=None, alpha=None):
    m, k = a.shape
    n = w.shape[-1]
    assert m % tm == 0 and n % tn == 0
    resident = _nbytes((tm, k), a.dtype)
    streamed = _nbytes((k, tn), w.dtype) + _nbytes((tm, tn), out_dtype)
    in_specs = [pl.BlockSpec((tm, k), lambda i, j: (i, 0), pipeline_mode=pl.Buffered(1)),
                pl.BlockSpec((None, k, tn), lambda i, j: (layer, 0, j))]
    args = [a, w]
    if resid is None:
        kern = _mm_kernel
    else:
        kern = functools.partial(_mm_resid_kernel, alpha=alpha)
        in_specs.append(pl.BlockSpec((tm, tn), lambda i, j: (i, j)))
        args.append(resid)
        streamed += _nbytes((tm, tn), resid.dtype)
    return pl.pallas_call(
        kern,
        out_shape=jax.ShapeDtypeStruct((m, n), out_dtype),
        grid=(m // tm, n // tn),
        in_specs=in_specs,
        out_specs=pl.BlockSpec((tm, tn), lambda i, j: (i, j)),
        compiler_params=pltpu.CompilerParams(
            dimension_semantics=("parallel", "arbitrary"),
            vmem_limit_bytes=_vmem_limit(streamed, resident)),
        name=name,
    )(*args)


def _in_proj_kernel(tab_ref, x_ref, *refs):
    *w_refs, o_ref = refs
    x = x_ref[...]
    tn = w_refs[0].shape[1]
    for t, w_ref in enumerate(w_refs):
        w = w_ref[0].astype(BF16)
        o_ref[:, t * tn:(t + 1) * tn] = lax.dot_general(x, w, NT_DIMS, preferred_element_type=F32)


def _in_proj(xb, w_in_t, layer, col_starts, tm, tn, per_step):
    m, k = xb.shape
    nblk = len(col_starts)
    assert m % tm == 0 and nblk % per_step == 0 and all(int(s) % SUBLANES == 0 for s in col_starts)
    resident = _nbytes((tm, k), xb.dtype)
    streamed = per_step * (_nbytes((tn, k), w_in_t.dtype) + _nbytes((tm, tn), F32))

    def window(t):
        return pl.BlockSpec(
            (pl.Element(1), pl.Element(tn), pl.Element(k)),
            lambda i, j, tab: (layer, pl.multiple_of(tab[j * per_step + t], SUBLANES), 0))

    return pl.pallas_call(
        _in_proj_kernel,
        out_shape=jax.ShapeDtypeStruct((m, nblk * tn), F32),
        grid_spec=pltpu.PrefetchScalarGridSpec(
            num_scalar_prefetch=1,
            grid=(m // tm, nblk // per_step),
            in_specs=[pl.BlockSpec((tm, k), lambda i, j, tab: (i, 0), pipeline_mode=pl.Buffered(1))]
            + [window(t) for t in range(per_step)],
            out_specs=pl.BlockSpec((tm, per_step * tn), lambda i, j, tab: (i, j))),
        compiler_params=pltpu.CompilerParams(
            dimension_semantics=("parallel", "arbitrary"),
            vmem_limit_bytes=_vmem_limit(streamed, resident)),
        name="in_proj",
    )(jnp.asarray(np.asarray(col_starts, np.int32)), xb, *([w_in_t] * per_step))


def _gate_stage_plan(w_in_t, layer, row0, n_branches, n_steps, gate_tn, step_of):
    k = w_in_t.shape[-1]
    rows = n_branches * k
    slab = rows // n_steps
    per_slab = slab // gate_tn
    slabs_per_branch = k // slab
    assert rows % n_steps == 0 and slab % gate_tn == 0 and k % slab == 0 and row0 % SUBLANES == 0
    in_spec = pl.BlockSpec(
        (pl.Element(1), pl.Element(slab), pl.Element(k)),
        lambda *ids: (layer, pl.multiple_of(row0 + step_of(*ids) * slab, SUBLANES), 0))
    out_spec = pl.BlockSpec(
        (per_slab, gate_tn, k),
        lambda *ids: (step_of(*ids) % slabs_per_branch, step_of(*ids) // slabs_per_branch, 0))
    out_shape = jax.ShapeDtypeStruct((k // gate_tn, n_branches * gate_tn, k), BF16)
    return in_spec, out_spec, out_shape, _nbytes((slab, k), w_in_t.dtype) + _nbytes((slab, k), BF16)


def _stage_gate_slab(src_ref, dst_ref):
    rows = dst_ref.shape[1]
    for t in range(dst_ref.shape[0]):
        dst_ref[t] = src_ref[0, t * rows:(t + 1) * rows, :].astype(dst_ref.dtype)


def _merge_kernel(x_ref, o0, o1, o2, o3, ga_ref, gb_ref, w_ref, y_ref):
    x = x_ref[...]
    tn = y_ref.shape[1]
    half = N_BRANCH // 2
    acc = None
    for b, o_ref in enumerate((o0, o1, o2, o3)):
        g_ref, bl = (ga_ref, b) if b < half else (gb_ref, b - half)
        gate = lax.dot_general(x, g_ref[bl * tn:(bl + 1) * tn, :], NT_DIMS, preferred_element_type=F32)
        proj = jnp.dot(o_ref[...], w_ref[b].astype(BF16), preferred_element_type=F32)
        term = jax.nn.sigmoid(gate) * proj
        acc = term if acc is None else acc + term
    y_ref[...] = acc.astype(y_ref.dtype)


def _merge(xb, branches, w_mg_a, w_mg_b, w_branch, layer, tm, tn):
    m, d = xb.shape
    bw = branches[0].shape[1]
    nj = d // tn
    in_specs = [pl.BlockSpec((tm, d), lambda i, j: (i, 0), pipeline_mode=pl.Buffered(1))]
    in_specs += [pl.BlockSpec((tm, bw), lambda i, j: (i, 0), pipeline_mode=pl.Buffered(1))
                 for _ in range(N_BRANCH)]
    in_specs += [pl.BlockSpec((None, N_BRANCH // 2 * tn, d), lambda i, j: (j, 0, 0)),
                 pl.BlockSpec((None, N_BRANCH // 2 * tn, d), lambda i, j: (j, 0, 0)),
                 pl.BlockSpec((None, N_BRANCH, bw, tn), lambda i, j: (layer, 0, 0, j))]
    resident = _nbytes((tm, d), BF16) + N_BRANCH * _nbytes((tm, bw), BF16)
    streamed = (N_BRANCH * (_nbytes((tn, d), BF16) + _nbytes((bw, tn), w_branch.dtype))
                + _nbytes((tm, tn), BF16))
    return pl.pallas_call(
        _merge_kernel,
        out_shape=jax.ShapeDtypeStruct((m, d), BF16),
        grid=(m // tm, nj),
        in_specs=in_specs,
        out_specs=pl.BlockSpec((tm, tn), lambda i, j: (i, j)),
        compiler_params=pltpu.CompilerParams(
            dimension_semantics=("parallel", "arbitrary"),
            vmem_limit_bytes=_vmem_limit(streamed, resident)),
        name="merge",
    )(xb, *branches, w_mg_a, w_mg_b, w_branch)


def _ln_kernel(z_ref, g_ref, b_ref, o_ref, ob_ref=None):
    z = z_ref[...]
    mu = jnp.mean(z, axis=-1, keepdims=True)
    zc = z - mu
    var = jnp.mean(zc * zc, axis=-1, keepdims=True)
    y = zc * lax.rsqrt(var + LN_EPS) * g_ref[...] + b_ref[...]
    o_ref[...] = y
    if ob_ref is not None:
        ob_ref[...] = y.astype(BF16)


def _layer_norm(z, g, b, tm, emit_bf16):
    m, d = z.shape
    out_shape = [jax.ShapeDtypeStruct((m, d), F32)]
    out_specs = [pl.BlockSpec((tm, d), lambda i: (i, 0))]
    if emit_bf16:
        out_shape.append(jax.ShapeDtypeStruct((m, d), BF16))
        out_specs.append(pl.BlockSpec((tm, d), lambda i: (i, 0)))
    blocks = 2 * _nbytes((tm, d), F32) + _nbytes((tm, d), BF16)
    res = pl.pallas_call(
        _ln_kernel,
        out_shape=out_shape,
        grid=(m // tm,),
        in_specs=[pl.BlockSpec((tm, d), lambda i: (i, 0)),
                  pl.BlockSpec((1, d), lambda i: (0, 0)),
                  pl.BlockSpec((1, d), lambda i: (0, 0))],
        out_specs=out_specs,
        compiler_params=pltpu.CompilerParams(
            dimension_semantics=("parallel",), vmem_limit_bytes=_vmem_limit(blocks)),
        name="layer_norm",
    )(z, g.reshape(1, d), b.reshape(1, d))
    return res if emit_bf16 else (res[0], None)


POOL_HIST = max(POOL_WINDOWS)
POOL_PAD = 2 * POOL_HIST


def _pool_kernel(u_ref, g_ref, wp_ref, sc_ref, o_ref, buf, lvl_a, lvl_b, *, ts, group):
    i = pl.program_id(1)
    ext = POOL_PAD + ts

    @pl.when(i == 0)
    def _():
        buf[0:POOL_PAD, :] = jnp.zeros((POOL_PAD, buf.shape[1]), F32)

    buf[POOL_PAD:ext, :] = u_ref[...]
    src = buf
    level_refs = []
    for lvl, dst in enumerate((lvl_a, lvl_b, lvl_a, lvl_b)[:len(POOL_WINDOWS)], start=1):
        lag = 2 ** (lvl - 1)
        r0 = SUBLANES * lvl
        c0 = (lvl - 1) * group
        dst[r0:ext, c0:] = src[r0:ext, c0:] + src[r0 - lag:ext - lag, c0:]
        level_refs.append(dst)
        src = dst

    t = i * ts + lax.broadcasted_iota(jnp.int32, (ts, 1), 0)
    outs = []
    for gi, w in enumerate(POOL_WINDOWS):
        assert w == 2 ** (gi + 1)
        cols = slice(gi * group, (gi + 1) * group)
        acc = level_refs[gi][POOL_PAD:ext, cols]
        cnt = jnp.minimum(t + 1, w).astype(F32)
        pooled = acc / cnt - buf[POOL_PAD:ext, cols]
        outs.append(jnp.dot(pooled.astype(BF16), wp_ref[gi], preferred_element_type=F32))
    o = jnp.concatenate(outs, axis=1) * sc_ref[...] * _silu(g_ref[...])
    o_ref[...] = o.astype(o_ref.dtype)
    buf[POOL_PAD - POOL_HIST:POOL_PAD, :] = buf[ext - POOL_HIST:ext, :]


def _pool(h, w_pool_b, pool_scale, bsz, seq, col_u, col_g, ts):
    bw = pool_scale.shape[-1]
    group = bw // len(POOL_WINDOWS)
    nblk = seq // ts
    blocks = 2 * _nbytes((ts, bw), F32) + _nbytes((ts, bw), BF16)
    scratch = [pltpu.VMEM((ts + POOL_PAD, bw), F32) for _ in range(3)]
    resident = sum(_nbytes(sc.shape, sc.dtype) for sc in scratch)
    return pl.pallas_call(
        functools.partial(_pool_kernel, ts=ts, group=group),
        out_shape=jax.ShapeDtypeStruct((bsz * seq, bw), BF16),
        grid=(bsz, nblk),
        in_specs=[pl.BlockSpec((ts, bw), lambda b, i: (b * nblk + i, col_u // bw)),
                  pl.BlockSpec((ts, bw), lambda b, i: (b * nblk + i, col_g // bw)),
                  pl.BlockSpec(w_pool_b.shape, lambda b, i: (0, 0, 0)),
                  pl.BlockSpec((1, bw), lambda b, i: (0, 0))],
        out_specs=pl.BlockSpec((ts, bw), lambda b, i: (b * nblk + i, 0)),
        scratch_shapes=scratch,
        compiler_params=pltpu.CompilerParams(
            dimension_semantics=("parallel", "arbitrary"),
            vmem_limit_bytes=_vmem_limit(blocks, resident)),
        name="pool_mixer",
    )(h, h, w_pool_b, pool_scale.reshape(1, bw))


def _split_bf16(x, parts):
    out = []
    r = x
    for _ in range(parts):
        p = r.astype(BF16)
        out.append(p)
        r = r - p.astype(F32)
    return out


LOG2E = math.log2(math.e)


def _gla_cumsum_operator(rows):
    r = np.arange(rows)
    tri = ((r[:, None] // GLA_CHUNK == r[None, :] // GLA_CHUNK) & (r[None, :] <= r[:, None]))
    return jnp.asarray(tri, BF16)


def _gla_kernel(q_ref, k_ref, v_ref, gg_ref, lr_ref, wup_ref, bg_ref, nrm_ref, tri_ref, wg_ref,
                o_ref, wgs_ref, state_ref, b_ref, *, rows, dk, dv):
    _stage_gate_slab(wg_ref, wgs_ref)
    i = pl.program_id(1)
    C = GLA_CHUNK
    nsub = C // GLA_SUB

    @pl.when(i == 0)
    def _():
        state_ref[...] = jnp.zeros(state_ref.shape, F32)

    lr_h, lr_m = _split_bf16(lr_ref[...], 2)
    w_h, w_m = _split_bf16(wup_ref[...], 2)
    z = (jnp.dot(lr_h, w_h, preferred_element_type=F32)
         + jnp.dot(lr_h, w_m, preferred_element_type=F32)
         + jnp.dot(lr_m, w_h, preferred_element_type=F32)) + bg_ref[...]
    log2_a = (jnp.minimum(z, 0.0) - jnp.log(1.0 + jnp.exp(-jnp.abs(z)))) * (LOG2E / GLA_TAU)

    bsum = None
    for part in _split_bf16(log2_a, 3):
        t = jnp.dot(tri_ref[...], part, preferred_element_type=F32)
        bsum = t if bsum is None else bsum + t
    b_ref[...] = bsum

    row = lax.broadcasted_iota(jnp.int32, (C, 1), 0)
    col = lax.broadcasted_iota(jnp.int32, (1, C), 1)
    row_sub = _div_pow2(row, GLA_SUB)
    col_sub = _div_pow2(col, GLA_SUB)
    lane_c = lax.broadcasted_iota(jnp.int32, (GLA_SUB, C), 1)
    scale = dk ** -0.5

    def sub_heads(x):
        return jnp.concatenate(
            [jnp.broadcast_to(x[a * GLA_SUB:a * GLA_SUB + 1, :], (GLA_SUB, dk))
             for a in range(nsub)], axis=0)

    def chunk_body(c, carry, *, bounded):
        r0 = pl.multiple_of(c * C, C)
        for h in range(GLA_HEADS):
            kc = slice(h * dk, (h + 1) * dk)
            vc = slice(h * dv, (h + 1) * dv)
            qs = q_ref[pl.ds(r0, C), kc] * scale
            k = k_ref[pl.ds(r0, C), kc]
            v = v_ref[pl.ds(r0, C), vc].astype(BF16)
            b = b_ref[pl.ds(r0, C), kc]
            b_last = b[C - 1:C, :]
            st = state_ref[h]

            q_in = (qs * jnp.exp2(b)).astype(BF16)
            o = lax.dot_general(q_in, st.astype(BF16), NT_DIMS, preferred_element_type=F32)

            q_t = qs * jnp.exp2(b - sub_heads(b))
            lhs = jnp.concatenate(
                [jnp.where(row_sub == a, q_t, 0.0) for a in range(nsub)], axis=1).astype(BF16)
            key_exp = [b[a * GLA_SUB:a * GLA_SUB + 1, :] - b for a in range(nsub)]
            if not bounded:
                key_exp = [jnp.minimum(e, 0.0) for e in key_exp]
            rhs = jnp.concatenate([k * jnp.exp2(e) for e in key_exp], axis=1).astype(BF16)
            a_fac = lax.dot_general(lhs, rhs, NT_DIMS, preferred_element_type=F32)

            if bounded:
                attn = jnp.where(row >= col, a_fac, 0.0)
            else:
                diag_blocks = []
                for a in range(nsub):
                    rs = slice(a * GLA_SUB, (a + 1) * GLA_SUB)
                    qa = qs[rs, :]
                    ba = b[rs, :]
                    blk = jnp.zeros((GLA_SUB, C), F32)
                    for jj in range(GLA_SUB):
                        j = a * GLA_SUB + jj
                        tj = qa * k[j:j + 1, :] * jnp.exp2(ba - b[j:j + 1, :])
                        cj = jnp.sum(tj, axis=1, keepdims=True)
                        blk = jnp.where(lane_c == j, cj, blk)
                    diag_blocks.append(blk)
                a_diag = jnp.concatenate(diag_blocks, axis=0)
                attn = jnp.where(row_sub > col_sub, a_fac,
                                 jnp.where((row_sub == col_sub) & (row >= col), a_diag, 0.0))
            o = o + jnp.dot(attn.astype(BF16), v, preferred_element_type=F32)

            k_dec = (k * jnp.exp2(b_last - b)).astype(BF16)
            upd = lax.dot_general(v, k_dec, (((0,), (0,)), ((), ())),
                                  preferred_element_type=F32)
            state_ref[h] = st * jnp.exp2(b_last) + upd

            ms = jnp.mean(o * o, axis=-1, keepdims=True)
            on = o * lax.rsqrt(ms + LN_EPS) * nrm_ref[:, vc]
            o_ref[pl.ds(r0, C), vc] = (on * _silu(gg_ref[pl.ds(r0, C), vc])).astype(o_ref.dtype)
        return carry

    bounded = (-jnp.min(bsum) < GLA_SAFE_LOG2) & (jnp.max(jnp.abs(k_ref[...])) < GLA_SAFE_KEY)

    @pl.when(bounded)
    def _():
        lax.fori_loop(0, rows // C, functools.partial(chunk_body, bounded=True), 0, unroll=2)

    @pl.when(jnp.logical_not(bounded))
    def _():
        lax.fori_loop(0, rows // C, functools.partial(chunk_body, bounded=False), 0)


def _gla(h, w_up_pad, b_gla, gla_norm, bsz, seq, cols, rows,
         w_in_t, layer, gate_row0, gate_branches, gate_tn):
    col_q, col_k, col_v, col_g, col_lr = cols
    dkt = b_gla.shape[-1]
    dvt = gla_norm.shape[-1]
    dk, dv = dkt // GLA_HEADS, dvt // GLA_HEADS
    lrw = w_up_pad.shape[0]
    nblk = seq // rows
    tri = _gla_cumsum_operator(rows)
    g_in, g_out, g_shape, g_bytes = _gate_stage_plan(
        w_in_t, layer, gate_row0, gate_branches, bsz * nblk, gate_tn, lambda b, i: b * nblk + i)
    blocks = (2 * _nbytes((rows, dkt), F32) + 2 * _nbytes((rows, dvt), F32)
              + _nbytes((rows, lrw), F32) + _nbytes((rows, dvt), BF16) + g_bytes)
    resident = (_nbytes((rows, dkt), F32) + _nbytes((GLA_HEADS, dv, dk), F32)
                + _nbytes(tri.shape, BF16))
    const = lambda b, i: (0, 0)
    return pl.pallas_call(
        functools.partial(_gla_kernel, rows=rows, dk=dk, dv=dv),
        out_shape=[jax.ShapeDtypeStruct((bsz * seq, dvt), BF16), g_shape],
        grid=(bsz, nblk),
        in_specs=[pl.BlockSpec((rows, dkt), lambda b, i: (b * nblk + i, col_q // dkt)),
                  pl.BlockSpec((rows, dkt), lambda b, i: (b * nblk + i, col_k // dkt)),
                  pl.BlockSpec((rows, dvt), lambda b, i: (b * nblk + i, col_v // dvt)),
                  pl.BlockSpec((rows, dvt), lambda b, i: (b * nblk + i, col_g // dvt)),
                  pl.BlockSpec((rows, lrw), lambda b, i: (b * nblk + i, col_lr // lrw)),
                  pl.BlockSpec((lrw, dkt), const),
                  pl.BlockSpec((1, dkt), const),
                  pl.BlockSpec((1, dvt), const),
                  pl.BlockSpec(tri.shape, const),
                  g_in],
        out_specs=[pl.BlockSpec((rows, dvt), lambda b, i: (b * nblk + i, 0)), g_out],
        scratch_shapes=[pltpu.VMEM((GLA_HEADS, dv, dk), F32),
                        pltpu.VMEM((rows, dkt), F32)],
        compiler_params=pltpu.CompilerParams(
            dimension_semantics=("parallel", "arbitrary"),
            vmem_limit_bytes=_vmem_limit(blocks, resident)),
        name="gla_mixer",
    )(h, h, h, h, h, w_up_pad, b_gla.reshape(1, dkt), gla_norm.reshape(1, dvt), tri, w_in_t)


def _rope_lane_table():
    half = ROPE_DIM // 2
    inv_freq = ROPE_THETA ** (-np.arange(0, ROPE_DIM, 2, dtype=np.float32) / ROPE_DIM)
    lane = np.arange(LANES) % SWA_HEAD_DIM
    tab = np.where(lane < ROPE_DIM, inv_freq[lane % half], 0.0).astype(np.float32)
    return jnp.asarray(tab.reshape(1, LANES))


SWA_ROW_CHUNK = 32
SWA_BLOCKS_PER_STEP = 2


def _swa_kernel(sink_ref, q_ref, k_ref, v_ref, g_ref, pos_ref, inv_ref, wg_ref, o_ref, wgs_ref,
                *scratch, n_pairs, n_sub):
    _stage_gate_slab(wg_ref, wgs_ref)
    step = pl.program_id(1)
    for sb in range(n_sub):
        rows = pl.ds(sb * SWA_BLOCK, SWA_BLOCK)
        _swa_block(step * n_sub + sb, sink_ref, q_ref.at[rows], k_ref.at[rows], v_ref.at[rows],
                   g_ref.at[rows], pos_ref.at[rows], inv_ref, o_ref.at[rows], *scratch,
                   n_pairs=n_pairs)


def _swa_block(n, sink_ref, q_ref, k_ref, v_ref, g_ref, pos_ref, inv_ref, o_ref,
               wk_ref, wv_ref, qr_ref, s_ref, p_ref, es_ref, bias_ref, *, n_pairs):
    blk = SWA_BLOCK
    half = ROPE_DIM // 2
    hd = SWA_HEAD_DIM
    ppg = n_pairs // 2
    rc_rows = SWA_ROW_CHUNK

    @pl.when(n == 0)
    def _():
        wk_ref[...] = jnp.zeros(wk_ref.shape, BF16)
        row = lax.broadcasted_iota(jnp.int32, (4 * blk, 2 * LANES), 0)
        col = lax.broadcasted_iota(jnp.int32, (4 * blk, 2 * LANES), 1)
        ones = ((col >= LANES) & ((col >= LANES + hd) == (row >= 2 * blk))).astype(BF16)
        for g in range(2):
            wv_ref[g] = ones

    lane = lax.broadcasted_iota(jnp.int32, (1, LANES), 1)
    lane_h = _mod_pow2(lane, hd)
    lo = lane < hd
    ang = pos_ref[...] * inv_ref[...]
    cos = jnp.cos(ang)
    sin = jnp.sin(ang)

    def rope(x):
        x_up = pltpu.roll(x, LANES - half, axis=1)
        x_dn = pltpu.roll(x, half, axis=1)
        return x * cos + jnp.where(lane_h < half, -x_up, x_dn) * sin

    for g in range(2):
        for part in range(2):
            base = part * 2 * blk
            wk_ref[g, base:base + blk, :] = wk_ref[g, base + blk:base + 2 * blk, :]
            wv_ref[g, base:base + blk, 0:LANES] = wv_ref[g, base + blk:base + 2 * blk, 0:LANES]
    k_r = rope(k_ref[...])
    v_c = v_ref[...]
    k_sw = pltpu.roll(k_r, hd, axis=1)
    v_sw = pltpu.roll(v_c, hd, axis=1)
    zero = jnp.zeros_like(k_r)
    for g in range(2):
        k_lo, k_hi = (k_r, k_sw) if g == 0 else (k_sw, k_r)
        v_lo, v_hi = (v_c, v_sw) if g == 0 else (v_sw, v_c)
        wk_ref[g, blk:2 * blk, :] = jnp.where(lo, k_lo, zero).astype(BF16)
        wk_ref[g, 3 * blk:4 * blk, :] = jnp.where(lo, zero, k_hi).astype(BF16)
        wv_ref[g, blk:2 * blk, 0:LANES] = jnp.where(lo, v_lo, zero).astype(BF16)
        wv_ref[g, 3 * blk:4 * blk, 0:LANES] = jnp.where(lo, zero, v_hi).astype(BF16)

    r = lax.broadcasted_iota(jnp.int32, (blk, 2 * blk), 0)
    c = lax.broadcasted_iota(jnp.int32, (blk, 2 * blk), 1)
    valid = (c > r) & (c <= r + WINDOW) & ((n > 0) | (c >= blk))
    bias_ref[...] = jnp.where(valid, 0.0, -jnp.inf).astype(F32)

    qscale = hd ** -0.5
    for p in range(n_pairs):
        g, pp = divmod(p, ppg)
        qr_ref[g, pp * blk:(pp + 1) * blk, :] = (
            rope(q_ref[:, p * LANES:(p + 1) * LANES]) * qscale).astype(BF16)

    for g in range(2):
        s_ref[g] = lax.dot_general(qr_ref[g], wk_ref[g], NT_DIMS, preferred_element_type=F32)

    for g in range(2):
        for pp in range(ppg):
            for hh in range(2):
                sink = sink_ref[2 * (g * ppg + pp) + hh]
                kc = slice(hh * 2 * blk, (hh + 1) * 2 * blk)
                for rc in range(blk // rc_rows):
                    rows = slice(pp * blk + rc * rc_rows, pp * blk + (rc + 1) * rc_rows)
                    sh = s_ref[g, rows, kc] + bias_ref[rc * rc_rows:(rc + 1) * rc_rows, :]
                    m = jnp.maximum(jnp.max(sh, axis=-1, keepdims=True), sink)
                    p_ref[g, rows, kc] = jnp.exp(sh - m).astype(BF16)
                    es_ref[g, rows, hh * hd:(hh + 1) * hd] = jnp.broadcast_to(
                        jnp.exp(sink - m), (rc_rows, hd))

    for g in range(2):
        o2 = jnp.dot(p_ref[g], wv_ref[g], preferred_element_type=F32)
        for pp in range(ppg):
            rows = slice(pp * blk, (pp + 1) * blk)
            cols = slice((g * ppg + pp) * LANES, (g * ppg + pp + 1) * LANES)
            den = o2[rows, LANES:] + es_ref[g, rows, :]
            o_ref[:, cols] = (o2[rows, :LANES] / den * _silu(g_ref[:, cols])).astype(o_ref.dtype)


def _swa(h, pos_f, sinks, bsz, seq, cols, w_in_t, layer, gate_row0, gate_branches, gate_tn):
    col_q, col_k, col_v, col_g = cols
    n_heads = sinks.shape[0]
    bw = n_heads * SWA_HEAD_DIM
    n_pairs = bw // LANES
    n_sub = SWA_BLOCKS_PER_STEP
    blk = n_sub * SWA_BLOCK
    nblk = seq // blk
    scratch = [pltpu.VMEM((2, 4 * SWA_BLOCK, LANES), BF16),
               pltpu.VMEM((2, 4 * SWA_BLOCK, 2 * LANES), BF16),
               pltpu.VMEM((2, 4 * SWA_BLOCK, LANES), BF16),
               pltpu.VMEM((2, 4 * SWA_BLOCK, 4 * SWA_BLOCK), F32),
               pltpu.VMEM((2, 4 * SWA_BLOCK, 4 * SWA_BLOCK), BF16),
               pltpu.VMEM((2, 4 * SWA_BLOCK, LANES), F32),
               pltpu.VMEM((SWA_BLOCK, 2 * SWA_BLOCK), F32)]
    g_in, g_out, g_shape, g_bytes = _gate_stage_plan(
        w_in_t, layer, gate_row0, gate_branches, bsz * nblk, gate_tn, lambda b, i: b * nblk + i)
    blocks = (2 * _nbytes((blk, bw), F32) + 3 * _nbytes((blk, LANES), F32) + _nbytes((blk, bw), BF16)
              + g_bytes)
    resident = sum(_nbytes(sc.shape, sc.dtype) for sc in scratch)

    return pl.pallas_call(
        functools.partial(_swa_kernel, n_pairs=n_pairs, n_sub=n_sub),
        out_shape=[jax.ShapeDtypeStruct((bsz * seq, bw), BF16), g_shape],
        grid=(bsz, nblk),
        in_specs=[pl.BlockSpec(memory_space=pltpu.SMEM),
                  pl.BlockSpec((blk, bw), lambda b, i: (b * nblk + i, col_q // bw)),
                  pl.BlockSpec((blk, LANES), lambda b, i: (b * nblk + i, col_k // LANES)),
                  pl.BlockSpec((blk, LANES), lambda b, i: (b * nblk + i, col_v // LANES)),
                  pl.BlockSpec((blk, bw), lambda b, i: (b * nblk + i, col_g // bw)),
                  pl.BlockSpec((blk, 1), lambda b, i: (b * nblk + i, 0)),
                  pl.BlockSpec((1, LANES), lambda b, i: (0, 0)),
                  g_in],
        out_specs=[pl.BlockSpec((blk, bw), lambda b, i: (b * nblk + i, 0)), g_out],
        scratch_shapes=scratch,
        compiler_params=pltpu.CompilerParams(
            dimension_semantics=("parallel", "arbitrary"),
            vmem_limit_bytes=_vmem_limit(blocks, resident)),
        name="swa_mixer",
    )(sinks, h, h, h, h, pos_f, _rope_lane_table(), w_in_t)


def _mem_kernel(q_ref, g_ref, mk_ref, mv_ref, o_ref, *, hd):
    scale = hd ** -0.5
    for h in range(XA_HEADS):
        cols = slice(h * hd, (h + 1) * hd)
        qh = q_ref[:, cols].astype(BF16)
        s = lax.dot_general(qh, mk_ref[:, cols], (((1,), (1,)), ((), ())),
                            preferred_element_type=F32) * scale
        m = jnp.max(s, axis=-1, keepdims=True)
        e = jnp.exp(s - m)
        p = e / jnp.sum(e, axis=-1, keepdims=True)
        o = jnp.dot(p.astype(BF16), mv_ref[:, cols], preferred_element_type=F32)
        o_ref[:, cols] = (o * _silu(g_ref[:, cols])).astype(o_ref.dtype)


def _mem_attn(h, mkv, bsz, seq, mem_len, cols, ts):
    col_q, col_g = cols
    bw = mkv.shape[1] // 2
    hd = bw // XA_HEADS
    nblk = seq // ts
    blocks = 2 * _nbytes((ts, bw), F32) + 2 * _nbytes((mem_len, bw), BF16) + _nbytes((ts, bw), BF16)
    return pl.pallas_call(
        functools.partial(_mem_kernel, hd=hd),
        out_shape=jax.ShapeDtypeStruct((bsz * seq, bw), BF16),
        grid=(bsz, nblk),
        in_specs=[pl.BlockSpec((ts, bw), lambda b, i: (b * nblk + i, col_q // bw)),
                  pl.BlockSpec((ts, bw), lambda b, i: (b * nblk + i, col_g // bw)),
                  pl.BlockSpec((mem_len, bw), lambda b, i: (b, 0)),
                  pl.BlockSpec((mem_len, bw), lambda b, i: (b, 1))],
        out_specs=pl.BlockSpec((ts, bw), lambda b, i: (b * nblk + i, 0)),
        compiler_params=pltpu.CompilerParams(
            dimension_semantics=("parallel", "parallel"), vmem_limit_bytes=_vmem_limit(blocks)),
        name="mem_attn",
    )(h, h, mkv, mkv)


IN_TN = 2 * LANES
MERGE_TN = 2 * LANES


def _main_layout(d_model):
    bw = d_model // 4
    dkt, dvt = bw // 2, bw
    kvw = 2 * SWA_HEAD_DIM
    names = ["pool_u", "pool_g", "gq", "gk", "gv", "gg", "glr", "sq", "sk", "sv", "sg", "xq", "xg"]
    widths = [bw, bw, dkt, dkt, dvt, dvt, GLA_RANK, bw, kvw, kvw, bw, bw, bw]
    src = {}
    off = 0
    for nme, w in zip(names, widths):
        src[nme] = (off, w)
        off += w
    n_main = off
    order = ["pool_u", "pool_g", "gv", "gg", "sq", "sg", "xq", "xg", "gq", "gk", "sk", "sv", "glr"]
    dst = {}
    cols = []
    off = 0
    for nme in order:
        o, w = src[nme]
        w_dst = w if nme != "glr" else IN_TN
        assert off % w_dst == 0
        dst[nme] = off
        cols.append(o + np.arange(w_dst))
        off += w_dst
    cols = np.concatenate(cols)
    assert off % IN_TN == 0 and cols.max() < n_main
    starts = cols[::IN_TN]
    assert np.all(cols.reshape(-1, IN_TN) == starts[:, None] + np.arange(IN_TN))
    return starts, dst, n_main


def _hybrid_layer(l, x, xb, mem_b, pos_f, w_in_t, w_pool, pool_scale, w_gla_up, b_gla, gla_norm,
                  sinks, w_mem_kv, w_branch, w_out, ln_g, ln_b, alpha, emit_bf16, bsz, seq):
    m, d = x.shape
    col_starts, dst, n_main = _main_layout(d)
    w_up_pad = jnp.concatenate(
        [w_gla_up[l], jnp.zeros((IN_TN - GLA_RANK, w_gla_up.shape[-1]), w_gla_up.dtype)], axis=0)

    h = _in_proj(xb, w_in_t, l, col_starts, tm=2048, tn=IN_TN, per_step=2)
    mkv = _matmul(mem_b, w_mem_kv, l, BF16, tm=mem_b.shape[0], tn=512, name="mem_kv")

    o_pool = _pool(h, w_pool[l].astype(BF16), pool_scale[l], bsz, seq,
                   dst["pool_u"], dst["pool_g"], ts=512)
    half = N_BRANCH // 2
    o_gla, w_mg_a = _gla(h, w_up_pad, b_gla[l], gla_norm[l], bsz, seq,
                         (dst["gq"], dst["gk"], dst["gv"], dst["gg"], dst["glr"]), 512,
                         w_in_t, l, n_main, half, MERGE_TN)
    o_swa, w_mg_b = _swa(h, pos_f, sinks[l], bsz, seq, (dst["sq"], dst["sk"], dst["sv"], dst["sg"]),
                         w_in_t, l, n_main + half * d, half, MERGE_TN)
    o_mem = _mem_attn(h, mkv, bsz, seq, mem_b.shape[0] // bsz, (dst["xq"], dst["xg"]), ts=512)

    y = _merge(xb, (o_pool, o_gla, o_swa, o_mem), w_mg_a, w_mg_b, w_branch, l, tm=1024, tn=MERGE_TN)
    z = _matmul(y, w_out, l, F32, tm=2048, tn=256, name="out_proj", resid=x, alpha=alpha)
    return _layer_norm(z, ln_g[l], ln_b[l], tm=512, emit_bf16=emit_bf16)


def kernel(x, mem, positions, w_in, w_pool, pool_scale, w_gla_up, b_gla, gla_norm, sinks,
           w_mem_kv, w_branch, w_out, ln_g, ln_b):
    bsz, seq, d = x.shape
    depth = w_in.shape[0]
    alpha = (2 * depth) ** 0.25
    xf = x.reshape(bsz * seq, d)
    xb = xf.astype(BF16)
    mem_b = mem.reshape(bsz * mem.shape[1], d).astype(BF16)
    pos_f = positions.astype(F32).reshape(bsz * seq, 1)
    w_in_t = jnp.swapaxes(w_in, 1, 2)
    for l in range(depth):
        xf, xb = _hybrid_layer(
            l, xf, xb, mem_b, pos_f, w_in_t, w_pool, pool_scale, w_gla_up, b_gla,
            gla_norm, sinks, w_mem_kv, w_branch, w_out, ln_g, ln_b,
            alpha, l + 1 < depth, bsz, seq)
    return xf.reshape(bsz, seq, d)
```

```python
import functools
import math

import jax
import jax.numpy as jnp
import numpy as np
from jax import lax
from jax.experimental import pallas as pl
from jax.experimental.pallas import tpu as pltpu

F32 = jnp.float32
BF16 = jnp.bfloat16

N_BRANCH = 4
POOL_WINDOWS = (2, 4, 8, 16)
GLA_HEADS = 4
GLA_RANK = 16
GLA_TAU = 16.0
GLA_CHUNK = 64
GLA_SUB = 16
GLA_SAFE_LOG2 = 24.0
GLA_SAFE_KEY = 2.0 ** 100
SWA_HEAD_DIM = 64
SWA_GROUP = 8
WINDOW = 128
SWA_BLOCK = 128
ROPE_THETA = 500000.0
ROPE_DIM = SWA_HEAD_DIM // 4
XA_HEADS = 4
LN_EPS = 1e-5

LANES = 128
SUBLANES = 8
V7X_VMEM_BYTES = 64 * 1024 * 1024
VMEM_CAP = V7X_VMEM_BYTES - 8 * 1024 * 1024
VMEM_TEMP_BYTES = 12 * 1024 * 1024


def _vmem_limit(streamed_bytes, resident_bytes=0):
    return int(min(VMEM_CAP, 2 * streamed_bytes + resident_bytes + VMEM_TEMP_BYTES))


def _nbytes(shape, dtype):
    return int(np.prod(shape)) * jnp.dtype(dtype).itemsize


def _silu(g):
    return g * jax.nn.sigmoid(g)


def _div_pow2(x, n):
    assert n & (n - 1) == 0
    return x >> (n.bit_length() - 1)


def _mod_pow2(x, n):
    assert n & (n - 1) == 0
    return x & (n - 1)


NT_DIMS = (((1,), (1,)), ((), ()))


def _mm_kernel(a_ref, w_ref, o_ref):
    w = w_ref[...].astype(BF16)
    o_ref[...] = jnp.dot(a_ref[...], w, preferred_element_type=F32).astype(o_ref.dtype)


def _matmul(a, w, layer, out_dtype, tm, tn, name):
    m, k = a.shape
    n = w.shape[-1]
    assert m % tm == 0 and n % tn == 0
    resident = _nbytes((tm, k), a.dtype)
    streamed = _nbytes((k, tn), w.dtype) + _nbytes((tm, tn), out_dtype)
    return pl.pallas_call(
        _mm_kernel,
        out_shape=jax.ShapeDtypeStruct((m, n), out_dtype),
        grid=(m // tm, n // tn),
        in_specs=[pl.BlockSpec((tm, k), lambda i, j: (i, 0), pipeline_mode=pl.Buffered(1)),
                  pl.BlockSpec((None, k, tn), lambda i, j: (layer, 0, j))],
        out_specs=pl.BlockSpec((tm, tn), lambda i, j: (i, j)),
        compiler_params=pltpu.CompilerParams(
            dimension_semantics=("parallel", "arbitrary"),
            vmem_limit_bytes=_vmem_limit(streamed, resident)),
        name=name,
    )(a, w)


def _out_proj_kernel(y_ref, w_ref, x_ref, o_ref, *, alpha, tn):
    y = y_ref[...]
    for t in range(o_ref.shape[1] // tn):
        cols = slice(t * tn, (t + 1) * tn)
        o_ref[:, cols] = alpha * x_ref[:, cols] + jnp.dot(
            y, w_ref[:, cols], preferred_element_type=F32)


def _out_proj(y, w_out_b, x, alpha, tm, tn, per_step):
    m, k = y.shape
    n = w_out_b.shape[1]
    tb = per_step * tn
    assert m % tm == 0 and n % tb == 0
    resident = _nbytes((tm, k), y.dtype)
    streamed = _nbytes((k, tb), w_out_b.dtype) + 2 * _nbytes((tm, tb), F32)
    return pl.pallas_call(
        functools.partial(_out_proj_kernel, alpha=alpha, tn=tn),
        out_shape=jax.ShapeDtypeStruct((m, n), F32),
        grid=(m // tm, n // tb),
        in_specs=[pl.BlockSpec((tm, k), lambda i, j: (i, 0), pipeline_mode=pl.Buffered(1)),
                  pl.BlockSpec((k, tb), lambda i, j: (0, j)),
                  pl.BlockSpec((tm, tb), lambda i, j: (i, j))],
        out_specs=pl.BlockSpec((tm, tb), lambda i, j: (i, j)),
        compiler_params=pltpu.CompilerParams(
            dimension_semantics=("parallel", "arbitrary"),
            vmem_limit_bytes=_vmem_limit(streamed, resident)),
        name="out_proj",
    )(y, w_out_b, x)


def _in_proj_kernel(tab_ref, x_ref, *refs):
    *w_refs, o_ref = refs
    x = x_ref[...]
    tn = w_refs[0].shape[1]
    for t, w_ref in enumerate(w_refs):
        w = w_ref[0].astype(BF16)
        o_ref[:, t * tn:(t + 1) * tn] = lax.dot_general(x, w, NT_DIMS, preferred_element_type=F32)


def _in_proj(xb, w_in_t, layer, col_starts, tm, tn, per_step):
    m, k = xb.shape
    nblk = len(col_starts)
    assert m % tm == 0 and nblk % per_step == 0 and all(int(s) % SUBLANES == 0 for s in col_starts)
    resident = _nbytes((tm, k), xb.dtype)
    streamed = per_step * (_nbytes((tn, k), w_in_t.dtype) + _nbytes((tm, tn), F32))

    def window(t):
        return pl.BlockSpec(
            (pl.Element(1), pl.Element(tn), pl.Element(k)),
            lambda i, j, tab: (layer, pl.multiple_of(tab[j * per_step + t], SUBLANES), 0))

    return pl.pallas_call(
        _in_proj_kernel,
        out_shape=jax.ShapeDtypeStruct((m, nblk * tn), F32),
        grid_spec=pltpu.PrefetchScalarGridSpec(
            num_scalar_prefetch=1,
            grid=(m // tm, nblk // per_step),
            in_specs=[pl.BlockSpec((tm, k), lambda i, j, tab: (i, 0), pipeline_mode=pl.Buffered(1))]
            + [window(t) for t in range(per_step)],
            out_specs=pl.BlockSpec((tm, per_step * tn), lambda i, j, tab: (i, j))),
        compiler_params=pltpu.CompilerParams(
            dimension_semantics=("parallel", "arbitrary"),
            vmem_limit_bytes=_vmem_limit(streamed, resident)),
        name="in_proj",
    )(jnp.asarray(np.asarray(col_starts, np.int32)), xb, *([w_in_t] * per_step))


def _gate_stage_plan(w_in_t, layer, row0, n_branches, n_steps, gate_tn, step_of):
    k = w_in_t.shape[-1]
    rows = n_branches * k
    slab = rows // n_steps
    per_slab = slab // gate_tn
    slabs_per_branch = k // slab
    assert rows % n_steps == 0 and slab % gate_tn == 0 and k % slab == 0 and row0 % SUBLANES == 0
    in_spec = pl.BlockSpec(
        (pl.Element(1), pl.Element(slab), pl.Element(k)),
        lambda *ids: (layer, pl.multiple_of(row0 + step_of(*ids) * slab, SUBLANES), 0))
    out_spec = pl.BlockSpec(
        (per_slab, gate_tn, k),
        lambda *ids: (step_of(*ids) % slabs_per_branch, step_of(*ids) // slabs_per_branch, 0))
    out_shape = jax.ShapeDtypeStruct((k // gate_tn, n_branches * gate_tn, k), BF16)
    return in_spec, out_spec, out_shape, _nbytes((slab, k), w_in_t.dtype) + _nbytes((slab, k), BF16)


def _stage_gate_slab(src_ref, dst_ref):
    rows = dst_ref.shape[1]
    for t in range(dst_ref.shape[0]):
        dst_ref[t] = src_ref[0, t * rows:(t + 1) * rows, :].astype(dst_ref.dtype)


def _merge_kernel(x_ref, o0, o1, o2, o3, ga_ref, gb_ref, w_ref, wo_ref, y_ref, wos_ref):
    wos_ref[...] = wo_ref[...].astype(wos_ref.dtype)
    x = x_ref[...]
    tn = y_ref.shape[1]
    half = N_BRANCH // 2
    acc = None
    for b, o_ref in enumerate((o0, o1, o2, o3)):
        g_ref, bl = (ga_ref, b) if b < half else (gb_ref, b - half)
        gate = lax.dot_general(x, g_ref[bl * tn:(bl + 1) * tn, :], NT_DIMS, preferred_element_type=F32)
        proj = jnp.dot(o_ref[...], w_ref[b].astype(BF16), preferred_element_type=F32)
        term = jax.nn.sigmoid(gate) * proj
        acc = term if acc is None else acc + term
    y_ref[...] = acc.astype(y_ref.dtype)


def _merge(xb, branches, w_mg_a, w_mg_b, w_branch, w_out, layer, tm, tn):
    m, d = xb.shape
    bw = branches[0].shape[1]
    nj = d // tn
    n_steps = (m // tm) * nj
    sr = d // n_steps
    assert d % n_steps == 0 and sr % (2 * SUBLANES) == 0
    in_specs = [pl.BlockSpec((tm, d), lambda i, j: (i, 0), pipeline_mode=pl.Buffered(1))]
    in_specs += [pl.BlockSpec((tm, bw), lambda i, j: (i, 0), pipeline_mode=pl.Buffered(1))
                 for _ in range(N_BRANCH)]
    in_specs += [pl.BlockSpec((None, N_BRANCH // 2 * tn, d), lambda i, j: (j, 0, 0)),
                 pl.BlockSpec((None, N_BRANCH // 2 * tn, d), lambda i, j: (j, 0, 0)),
                 pl.BlockSpec((None, N_BRANCH, bw, tn), lambda i, j: (layer, 0, 0, j)),
                 pl.BlockSpec((None, sr, d), lambda i, j: (layer, i * nj + j, 0))]
    resident = _nbytes((tm, d), BF16) + N_BRANCH * _nbytes((tm, bw), BF16)
    streamed = (N_BRANCH * (_nbytes((tn, d), BF16) + _nbytes((bw, tn), w_branch.dtype))
                + _nbytes((tm, tn), BF16) + _nbytes((sr, d), w_out.dtype) + _nbytes((sr, d), BF16))
    return pl.pallas_call(
        _merge_kernel,
        out_shape=[jax.ShapeDtypeStruct((m, d), BF16), jax.ShapeDtypeStruct((d, d), BF16)],
        grid=(m // tm, nj),
        in_specs=in_specs,
        out_specs=[pl.BlockSpec((tm, tn), lambda i, j: (i, j)),
                   pl.BlockSpec((sr, d), lambda i, j: (i * nj + j, 0))],
        compiler_params=pltpu.CompilerParams(
            dimension_semantics=("parallel", "arbitrary"),
            vmem_limit_bytes=_vmem_limit(streamed, resident)),
        name="merge",
    )(xb, *branches, w_mg_a, w_mg_b, w_branch, w_out)


def _ln_kernel(z_ref, g_ref, b_ref, o_ref, ob_ref=None):
    z = z_ref[...]
    mu = jnp.mean(z, axis=-1, keepdims=True)
    zc = z - mu
    var = jnp.mean(zc * zc, axis=-1, keepdims=True)
    y = zc * lax.rsqrt(var + LN_EPS) * g_ref[...] + b_ref[...]
    o_ref[...] = y
    if ob_ref is not None:
        ob_ref[...] = y.astype(BF16)


def _layer_norm(z, g, b, tm, emit_bf16):
    m, d = z.shape
    out_shape = [jax.ShapeDtypeStruct((m, d), F32)]
    out_specs = [pl.BlockSpec((tm, d), lambda i: (i, 0))]
    if emit_bf16:
        out_shape.append(jax.ShapeDtypeStruct((m, d), BF16))
        out_specs.append(pl.BlockSpec((tm, d), lambda i: (i, 0)))
    blocks = 2 * _nbytes((tm, d), F32) + _nbytes((tm, d), BF16)
    res = pl.pallas_call(
        _ln_kernel,
        out_shape=out_shape,
        grid=(m // tm,),
        in_specs=[pl.BlockSpec((tm, d), lambda i: (i, 0)),
                  pl.BlockSpec((1, d), lambda i: (0, 0)),
                  pl.BlockSpec((1, d), lambda i: (0, 0))],
        out_specs=out_specs,
        compiler_params=pltpu.CompilerParams(
            dimension_semantics=("parallel",), vmem_limit_bytes=_vmem_limit(blocks)),
        name="layer_norm",
    )(z, g.reshape(1, d), b.reshape(1, d))
    return res if emit_bf16 else (res[0], None)


POOL_HIST = max(POOL_WINDOWS)
POOL_PAD = 2 * POOL_HIST


def _pool_kernel(u_ref, g_ref, wp_ref, sc_ref, o_ref, buf, lvl_a, lvl_b, *, ts, group):
    i = pl.program_id(1)
    ext = POOL_PAD + ts

    @pl.when(i == 0)
    def _():
        buf[0:POOL_PAD, :] = jnp.zeros((POOL_PAD, buf.shape[1]), F32)

    buf[POOL_PAD:ext, :] = u_ref[...]
    src = buf
    level_refs = []
    for lvl, dst in enumerate((lvl_a, lvl_b, lvl_a, lvl_b)[:len(POOL_WINDOWS)], start=1):
        lag = 2 ** (lvl - 1)
        r0 = SUBLANES * lvl
        c0 = (lvl - 1) * group
        dst[r0:ext, c0:] = src[r0:ext, c0:] + src[r0 - lag:ext - lag, c0:]
        level_refs.append(dst)
        src = dst

    t = i * ts + lax.broadcasted_iota(jnp.int32, (ts, 1), 0)
    outs = []
    for gi, w in enumerate(POOL_WINDOWS):
        assert w == 2 ** (gi + 1)
        cols = slice(gi * group, (gi + 1) * group)
        acc = level_refs[gi][POOL_PAD:ext, cols]
        cnt = jnp.minimum(t + 1, w).astype(F32)
        pooled = acc / cnt - buf[POOL_PAD:ext, cols]
        outs.append(jnp.dot(pooled.astype(BF16), wp_ref[gi], preferred_element_type=F32))
    o = jnp.concatenate(outs, axis=1) * sc_ref[...] * _silu(g_ref[...])
    o_ref[...] = o.astype(o_ref.dtype)
    buf[POOL_PAD - POOL_HIST:POOL_PAD, :] = buf[ext - POOL_HIST:ext, :]


def _pool(h, w_pool_b, pool_scale, bsz, seq, col_u, col_g, ts):
    bw = pool_scale.shape[-1]
    group = bw // len(POOL_WINDOWS)
    nblk = seq // ts
    blocks = 2 * _nbytes((ts, bw), F32) + _nbytes((ts, bw), BF16)
    scratch = [pltpu.VMEM((ts + POOL_PAD, bw), F32) for _ in range(3)]
    resident = sum(_nbytes(sc.shape, sc.dtype) for sc in scratch)
    return pl.pallas_call(
        functools.partial(_pool_kernel, ts=ts, group=group),
        out_shape=jax.ShapeDtypeStruct((bsz * seq, bw), BF16),
        grid=(bsz, nblk),
        in_specs=[pl.BlockSpec((ts, bw), lambda b, i: (b * nblk + i, col_u // bw)),
                  pl.BlockSpec((ts, bw), lambda b, i: (b * nblk + i, col_g // bw)),
                  pl.BlockSpec(w_pool_b.shape, lambda b, i: (0, 0, 0)),
                  pl.BlockSpec((1, bw), lambda b, i: (0, 0))],
        out_specs=pl.BlockSpec((ts, bw), lambda b, i: (b * nblk + i, 0)),
        scratch_shapes=scratch,
        compiler_params=pltpu.CompilerParams(
            dimension_semantics=("parallel", "arbitrary"),
            vmem_limit_bytes=_vmem_limit(blocks, resident)),
        name="pool_mixer",
    )(h, h, w_pool_b, pool_scale.reshape(1, bw))


def _split_bf16(x, parts):
    out = []
    r = x
    for _ in range(parts):
        p = r.astype(BF16)
        out.append(p)
        r = r - p.astype(F32)
    return out


LOG2E = math.log2(math.e)


def _gla_cumsum_operator(rows):
    r = np.arange(rows)
    tri = ((r[:, None] // GLA_CHUNK == r[None, :] // GLA_CHUNK) & (r[None, :] <= r[:, None]))
    return jnp.asarray(tri, BF16)


def _gla_kernel(q_ref, k_ref, v_ref, gg_ref, lr_ref, wup_ref, bg_ref, nrm_ref, tri_ref, wg_ref,
                o_ref, wgs_ref, state_ref, b_ref, *, rows, dk, dv):
    _stage_gate_slab(wg_ref, wgs_ref)
    i = pl.program_id(1)
    C = GLA_CHUNK
    nsub = C // GLA_SUB

    @pl.when(i == 0)
    def _():
        state_ref[...] = jnp.zeros(state_ref.shape, F32)

    lr_h, lr_m = _split_bf16(lr_ref[...], 2)
    w_h, w_m = _split_bf16(wup_ref[...], 2)
    z = (jnp.dot(lr_h, w_h, preferred_element_type=F32)
         + jnp.dot(lr_h, w_m, preferred_element_type=F32)
         + jnp.dot(lr_m, w_h, preferred_element_type=F32)) + bg_ref[...]
    log2_a = (jnp.minimum(z, 0.0) - jnp.log(1.0 + jnp.exp(-jnp.abs(z)))) * (LOG2E / GLA_TAU)

    bsum = None
    for part in _split_bf16(log2_a, 3):
        t = jnp.dot(tri_ref[...], part, preferred_element_type=F32)
        bsum = t if bsum is None else bsum + t
    b_ref[...] = bsum

    row = lax.broadcasted_iota(jnp.int32, (C, 1), 0)
    col = lax.broadcasted_iota(jnp.int32, (1, C), 1)
    row_sub = _div_pow2(row, GLA_SUB)
    col_sub = _div_pow2(col, GLA_SUB)
    lane_c = lax.broadcasted_iota(jnp.int32, (GLA_SUB, C), 1)
    scale = dk ** -0.5

    def sub_heads(x):
        return jnp.concatenate(
            [jnp.broadcast_to(x[a * GLA_SUB:a * GLA_SUB + 1, :], (GLA_SUB, dk))
             for a in range(nsub)], axis=0)

    def chunk_body(c, carry, *, bounded):
        r0 = pl.multiple_of(c * C, C)
        for h in range(GLA_HEADS):
            kc = slice(h * dk, (h + 1) * dk)
            vc = slice(h * dv, (h + 1) * dv)
            qs = q_ref[pl.ds(r0, C), kc] * scale
            k = k_ref[pl.ds(r0, C), kc]
            v = v_ref[pl.ds(r0, C), vc].astype(BF16)
            b = b_ref[pl.ds(r0, C), kc]
            b_last = b[C - 1:C, :]
            st = state_ref[h]

            q_in = (qs * jnp.exp2(b)).astype(BF16)
            o = lax.dot_general(q_in, st.astype(BF16), NT_DIMS, preferred_element_type=F32)

            q_t = qs * jnp.exp2(b - sub_heads(b))
            lhs = jnp.concatenate(
                [jnp.where(row_sub == a, q_t, 0.0) for a in range(nsub)], axis=1).astype(BF16)
            key_exp = [b[a * GLA_SUB:a * GLA_SUB + 1, :] - b for a in range(nsub)]
            if not bounded:
                key_exp = [jnp.minimum(e, 0.0) for e in key_exp]
            rhs = jnp.concatenate([k * jnp.exp2(e) for e in key_exp], axis=1).astype(BF16)
            a_fac = lax.dot_general(lhs, rhs, NT_DIMS, preferred_element_type=F32)

            if bounded:
                attn = jnp.where(row >= col, a_fac, 0.0)
            else:
                diag_blocks = []
                for a in range(nsub):
                    rs = slice(a * GLA_SUB, (a + 1) * GLA_SUB)
                    qa = qs[rs, :]
                    ba = b[rs, :]
                    blk = jnp.zeros((GLA_SUB, C), F32)
                    for jj in range(GLA_SUB):
                        j = a * GLA_SUB + jj
                        tj = qa * k[j:j + 1, :] * jnp.exp2(ba - b[j:j + 1, :])
                        cj = jnp.sum(tj, axis=1, keepdims=True)
                        blk = jnp.where(lane_c == j, cj, blk)
                    diag_blocks.append(blk)
                a_diag = jnp.concatenate(diag_blocks, axis=0)
                attn = jnp.where(row_sub > col_sub, a_fac,
                                 jnp.where((row_sub == col_sub) & (row >= col), a_diag, 0.0))
            o = o + jnp.dot(attn.astype(BF16), v, preferred_element_type=F32)

            k_dec = (k * jnp.exp2(b_last - b)).astype(BF16)
            upd = lax.dot_general(v, k_dec, (((0,), (0,)), ((), ())),
                                  preferred_element_type=F32)
            state_ref[h] = st * jnp.exp2(b_last) + upd

            ms = jnp.mean(o * o, axis=-1, keepdims=True)
            on = o * lax.rsqrt(ms + LN_EPS) * nrm_ref[:, vc]
            o_ref[pl.ds(r0, C), vc] = (on * _silu(gg_ref[pl.ds(r0, C), vc])).astype(o_ref.dtype)
        return carry

    bounded = (-jnp.min(bsum) < GLA_SAFE_LOG2) & (jnp.max(jnp.abs(k_ref[...])) < GLA_SAFE_KEY)

    @pl.when(bounded)
    def _():
        lax.fori_loop(0, rows // C, functools.partial(chunk_body, bounded=True), 0, unroll=2)

    @pl.when(jnp.logical_not(bounded))
    def _():
        lax.fori_loop(0, rows // C, functools.partial(chunk_body, bounded=False), 0)


def _gla(h, w_up_pad, b_gla, gla_norm, bsz, seq, cols, rows,
         w_in_t, layer, gate_row0, gate_branches, gate_tn):
    col_q, col_k, col_v, col_g, col_lr = cols
    dkt = b_gla.shape[-1]
    dvt = gla_norm.shape[-1]
    dk, dv = dkt // GLA_HEADS, dvt // GLA_HEADS
    lrw = w_up_pad.shape[0]
    nblk = seq // rows
    tri = _gla_cumsum_operator(rows)
    g_in, g_out, g_shape, g_bytes = _gate_stage_plan(
        w_in_t, layer, gate_row0, gate_branches, bsz * nblk, gate_tn, lambda b, i: b * nblk + i)
    blocks = (2 * _nbytes((rows, dkt), F32) + 2 * _nbytes((rows, dvt), F32)
              + _nbytes((rows, lrw), F32) + _nbytes((rows, dvt), BF16) + g_bytes)
    resident = (_nbytes((rows, dkt), F32) + _nbytes((GLA_HEADS, dv, dk), F32)
                + _nbytes(tri.shape, BF16))
    const = lambda b, i: (0, 0)
    return pl.pallas_call(
        functools.partial(_gla_kernel, rows=rows, dk=dk, dv=dv),
        out_shape=[jax.ShapeDtypeStruct((bsz * seq, dvt), BF16), g_shape],
        grid=(bsz, nblk),
        in_specs=[pl.BlockSpec((rows, dkt), lambda b, i: (b * nblk + i, col_q // dkt)),
                  pl.BlockSpec((rows, dkt), lambda b, i: (b * nblk + i, col_k // dkt)),
                  pl.BlockSpec((rows, dvt), lambda b, i: (b * nblk + i, col_v // dvt)),
                  pl.BlockSpec((rows, dvt), lambda b, i: (b * nblk + i, col_g // dvt)),
                  pl.BlockSpec((rows, lrw), lambda b, i: (b * nblk + i, col_lr // lrw)),
                  pl.BlockSpec((lrw, dkt), const),
                  pl.BlockSpec((1, dkt), const),
                  pl.BlockSpec((1, dvt), const),
                  pl.BlockSpec(tri.shape, const),
                  g_in],
        out_specs=[pl.BlockSpec((rows, dvt), lambda b, i: (b * nblk + i, 0)), g_out],
        scratch_shapes=[pltpu.VMEM((GLA_HEADS, dv, dk), F32),
                        pltpu.VMEM((rows, dkt), F32)],
        compiler_params=pltpu.CompilerParams(
            dimension_semantics=("parallel", "arbitrary"),
            vmem_limit_bytes=_vmem_limit(blocks, resident)),
        name="gla_mixer",
    )(h, h, h, h, h, w_up_pad, b_gla.reshape(1, dkt), gla_norm.reshape(1, dvt), tri, w_in_t)


def _rope_lane_table():
    half = ROPE_DIM // 2
    inv_freq = ROPE_THETA ** (-np.arange(0, ROPE_DIM, 2, dtype=np.float32) / ROPE_DIM)
    lane = np.arange(LANES) % SWA_HEAD_DIM
    tab = np.where(lane < ROPE_DIM, inv_freq[lane % half], 0.0).astype(np.float32)
    return jnp.asarray(tab.reshape(1, LANES))


SWA_ROW_CHUNK = 32
SWA_BLOCKS_PER_STEP = 2


def _swa_kernel(sink_ref, q_ref, k_ref, v_ref, g_ref, pos_ref, inv_ref, wg_ref, o_ref, wgs_ref,
                *scratch, n_pairs, n_sub):
    _stage_gate_slab(wg_ref, wgs_ref)
    step = pl.program_id(1)
    for sb in range(n_sub):
        rows = pl.ds(sb * SWA_BLOCK, SWA_BLOCK)
        _swa_block(step * n_sub + sb, sink_ref, q_ref.at[rows], k_ref.at[rows], v_ref.at[rows],
                   g_ref.at[rows], pos_ref.at[rows], inv_ref, o_ref.at[rows], *scratch,
                   n_pairs=n_pairs)


def _swa_block(n, sink_ref, q_ref, k_ref, v_ref, g_ref, pos_ref, inv_ref, o_ref,
               wk_ref, wv_ref, qr_ref, s_ref, p_ref, es_ref, bias_ref, *, n_pairs):
    blk = SWA_BLOCK
    half = ROPE_DIM // 2
    hd = SWA_HEAD_DIM
    ppg = n_pairs // 2
    rc_rows = SWA_ROW_CHUNK

    @pl.when(n == 0)
    def _():
        wk_ref[...] = jnp.zeros(wk_ref.shape, BF16)
        row = lax.broadcasted_iota(jnp.int32, (4 * blk, 2 * LANES), 0)
        col = lax.broadcasted_iota(jnp.int32, (4 * blk, 2 * LANES), 1)
        ones = ((col >= LANES) & ((col >= LANES + hd) == (row >= 2 * blk))).astype(BF16)
        for g in range(2):
            wv_ref[g] = ones

    lane = lax.broadcasted_iota(jnp.int32, (1, LANES), 1)
    lane_h = _mod_pow2(lane, hd)
    lo = lane < hd
    ang = pos_ref[...] * inv_ref[...]
    cos = jnp.cos(ang)
    sin = jnp.sin(ang)

    def rope(x):
        x_up = pltpu.roll(x, LANES - half, axis=1)
        x_dn = pltpu.roll(x, half, axis=1)
        return x * cos + jnp.where(lane_h < half, -x_up, x_dn) * sin

    for g in range(2):
        for part in range(2):
            base = part * 2 * blk
            wk_ref[g, base:base + blk, :] = wk_ref[g, base + blk:base + 2 * blk, :]
            wv_ref[g, base:base + blk, 0:LANES] = wv_ref[g, base + blk:base + 2 * blk, 0:LANES]
    k_r = rope(k_ref[...])
    v_c = v_ref[...]
    k_sw = pltpu.roll(k_r, hd, axis=1)
    v_sw = pltpu.roll(v_c, hd, axis=1)
    zero = jnp.zeros_like(k_r)
    for g in range(2):
        k_lo, k_hi = (k_r, k_sw) if g == 0 else (k_sw, k_r)
        v_lo, v_hi = (v_c, v_sw) if g == 0 else (v_sw, v_c)
        wk_ref[g, blk:2 * blk, :] = jnp.where(lo, k_lo, zero).astype(BF16)
        wk_ref[g, 3 * blk:4 * blk, :] = jnp.where(lo, zero, k_hi).astype(BF16)
        wv_ref[g, blk:2 * blk, 0:LANES] = jnp.where(lo, v_lo, zero).astype(BF16)
        wv_ref[g, 3 * blk:4 * blk, 0:LANES] = jnp.where(lo, zero, v_hi).astype(BF16)

    r = lax.broadcasted_iota(jnp.int32, (blk, 2 * blk), 0)
    c = lax.broadcasted_iota(jnp.int32, (blk, 2 * blk), 1)
    valid = (c > r) & (c <= r + WINDOW) & ((n > 0) | (c >= blk))
    bias_ref[...] = jnp.where(valid, 0.0, -jnp.inf).astype(F32)

    qscale = hd ** -0.5
    for p in range(n_pairs):
        g, pp = divmod(p, ppg)
        qr_ref[g, pp * blk:(pp + 1) * blk, :] = (
            rope(q_ref[:, p * LANES:(p + 1) * LANES]) * qscale).astype(BF16)

    for g in range(2):
        s_ref[g] = lax.dot_general(qr_ref[g], wk_ref[g], NT_DIMS, preferred_element_type=F32)

    for g in range(2):
        for pp in range(ppg):
            for hh in range(2):
                sink = sink_ref[2 * (g * ppg + pp) + hh]
                kc = slice(hh * 2 * blk, (hh + 1) * 2 * blk)
                for rc in range(blk // rc_rows):
                    rows = slice(pp * blk + rc * rc_rows, pp * blk + (rc + 1) * rc_rows)
                    sh = s_ref[g, rows, kc] + bias_ref[rc * rc_rows:(rc + 1) * rc_rows, :]
                    m = jnp.maximum(jnp.max(sh, axis=-1, keepdims=True), sink)
                    p_ref[g, rows, kc] = jnp.exp(sh - m).astype(BF16)
                    es_ref[g, rows, hh * hd:(hh + 1) * hd] = jnp.broadcast_to(
                        jnp.exp(sink - m), (rc_rows, hd))

    for g in range(2):
        o2 = jnp.dot(p_ref[g], wv_ref[g], preferred_element_type=F32)
        for pp in range(ppg):
            rows = slice(pp * blk, (pp + 1) * blk)
            cols = slice((g * ppg + pp) * LANES, (g * ppg + pp + 1) * LANES)
            den = o2[rows, LANES:] + es_ref[g, rows, :]
            o_ref[:, cols] = (o2[rows, :LANES] / den * _silu(g_ref[:, cols])).astype(o_ref.dtype)


def _swa(h, pos_f, sinks, bsz, seq, cols, w_in_t, layer, gate_row0, gate_branches, gate_tn):
    col_q, col_k, col_v, col_g = cols
    n_heads = sinks.shape[0]
    bw = n_heads * SWA_HEAD_DIM
    n_pairs = bw // LANES
    n_sub = SWA_BLOCKS_PER_STEP
    blk = n_sub * SWA_BLOCK
    nblk = seq // blk
    scratch = [pltpu.VMEM((2, 4 * SWA_BLOCK, LANES), BF16),
               pltpu.VMEM((2, 4 * SWA_BLOCK, 2 * LANES), BF16),
               pltpu.VMEM((2, 4 * SWA_BLOCK, LANES), BF16),
               pltpu.VMEM((2, 4 * SWA_BLOCK, 4 * SWA_BLOCK), F32),
               pltpu.VMEM((2, 4 * SWA_BLOCK, 4 * SWA_BLOCK), BF16),
               pltpu.VMEM((2, 4 * SWA_BLOCK, LANES), F32),
               pltpu.VMEM((SWA_BLOCK, 2 * SWA_BLOCK), F32)]
    g_in, g_out, g_shape, g_bytes = _gate_stage_plan(
        w_in_t, layer, gate_row0, gate_branches, bsz * nblk, gate_tn, lambda b, i: b * nblk + i)
    blocks = (2 * _nbytes((blk, bw), F32) + 3 * _nbytes((blk, LANES), F32) + _nbytes((blk, bw), BF16)
              + g_bytes)
    resident = sum(_nbytes(sc.shape, sc.dtype) for sc in scratch)

    return pl.pallas_call(
        functools.partial(_swa_kernel, n_pairs=n_pairs, n_sub=n_sub),
        out_shape=[jax.ShapeDtypeStruct((bsz * seq, bw), BF16), g_shape],
        grid=(bsz, nblk),
        in_specs=[pl.BlockSpec(memory_space=pltpu.SMEM),
                  pl.BlockSpec((blk, bw), lambda b, i: (b * nblk + i, col_q // bw)),
                  pl.BlockSpec((blk, LANES), lambda b, i: (b * nblk + i, col_k // LANES)),
                  pl.BlockSpec((blk, LANES), lambda b, i: (b * nblk + i, col_v // LANES)),
                  pl.BlockSpec((blk, bw), lambda b, i: (b * nblk + i, col_g // bw)),
                  pl.BlockSpec((blk, 1), lambda b, i: (b * nblk + i, 0)),
                  pl.BlockSpec((1, LANES), lambda b, i: (0, 0)),
                  g_in],
        out_specs=[pl.BlockSpec((blk, bw), lambda b, i: (b * nblk + i, 0)), g_out],
        scratch_shapes=scratch,
        compiler_params=pltpu.CompilerParams(
            dimension_semantics=("parallel", "arbitrary"),
            vmem_limit_bytes=_vmem_limit(blocks, resident)),
        name="swa_mixer",
    )(sinks, h, h, h, h, pos_f, _rope_lane_table(), w_in_t)


def _mem_kernel(q_ref, g_ref, mk_ref, mv_ref, o_ref, *, hd):
    scale = hd ** -0.5
    for h in range(XA_HEADS):
        cols = slice(h * hd, (h + 1) * hd)
        qh = q_ref[:, cols].astype(BF16)
        s = lax.dot_general(qh, mk_ref[:, cols], (((1,), (1,)), ((), ())),
                            preferred_element_type=F32) * scale
        m = jnp.max(s, axis=-1, keepdims=True)
        e = jnp.exp(s - m)
        p = e / jnp.sum(e, axis=-1, keepdims=True)
        o = jnp.dot(p.astype(BF16), mv_ref[:, cols], preferred_element_type=F32)
        o_ref[:, cols] = (o * _silu(g_ref[:, cols])).astype(o_ref.dtype)


def _mem_attn(h, mkv, bsz, seq, mem_len, cols, ts):
    col_q, col_g = cols
    bw = mkv.shape[1] // 2
    hd = bw // XA_HEADS
    nblk = seq // ts
    blocks = 2 * _nbytes((ts, bw), F32) + 2 * _nbytes((mem_len, bw), BF16) + _nbytes((ts, bw), BF16)
    return pl.pallas_call(
        functools.partial(_mem_kernel, hd=hd),
        out_shape=jax.ShapeDtypeStruct((bsz * seq, bw), BF16),
        grid=(bsz, nblk),
        in_specs=[pl.BlockSpec((ts, bw), lambda b, i: (b * nblk + i, col_q // bw)),
                  pl.BlockSpec((ts, bw), lambda b, i: (b * nblk + i, col_g // bw)),
                  pl.BlockSpec((mem_len, bw), lambda b, i: (b, 0)),
                  pl.BlockSpec((mem_len, bw), lambda b, i: (b, 1))],
        out_specs=pl.BlockSpec((ts, bw), lambda b, i: (b * nblk + i, 0)),
        compiler_params=pltpu.CompilerParams(
            dimension_semantics=("parallel", "parallel"), vmem_limit_bytes=_vmem_limit(blocks)),
        name="mem_attn",
    )(h, h, mkv, mkv)


IN_TN = 2 * LANES
MERGE_TN = 2 * LANES


def _main_layout(d_model):
    bw = d_model // 4
    dkt, dvt = bw // 2, bw
    kvw = 2 * SWA_HEAD_DIM
    names = ["pool_u", "pool_g", "gq", "gk", "gv", "gg", "glr", "sq", "sk", "sv", "sg", "xq", "xg"]
    widths = [bw, bw, dkt, dkt, dvt, dvt, GLA_RANK, bw, kvw, kvw, bw, bw, bw]
    src = {}
    off = 0
    for nme, w in zip(names, widths):
        src[nme] = (off, w)
        off += w
    n_main = off
    order = ["pool_u", "pool_g", "gv", "gg", "sq", "sg", "xq", "xg", "gq", "gk", "sk", "sv", "glr"]
    dst = {}
    cols = []
    off = 0
    for nme in order:
        o, w = src[nme]
        w_dst = w if nme != "glr" else IN_TN
        assert off % w_dst == 0
        dst[nme] = off
        cols.append(o + np.arange(w_dst))
        off += w_dst
    cols = np.concatenate(cols)
    assert off % IN_TN == 0 and cols.max() < n_main
    starts = cols[::IN_TN]
    assert np.all(cols.reshape(-1, IN_TN) == starts[:, None] + np.arange(IN_TN))
    return starts, dst, n_main


def _hybrid_layer(l, x, xb, mem_b, pos_f, w_in_t, w_pool, pool_scale, w_gla_up, b_gla, gla_norm,
                  sinks, w_mem_kv, w_branch, w_out, ln_g, ln_b, alpha, emit_bf16, bsz, seq):
    m, d = x.shape
    col_starts, dst, n_main = _main_layout(d)
    w_up_pad = jnp.concatenate(
        [w_gla_up[l], jnp.zeros((IN_TN - GLA_RANK, w_gla_up.shape[-1]), w_gla_up.dtype)], axis=0)

    h = _in_proj(xb, w_in_t, l, col_starts, tm=2048, tn=IN_TN, per_step=2)
    mkv = _matmul(mem_b, w_mem_kv, l, BF16, tm=mem_b.shape[0], tn=512, name="mem_kv")

    o_pool = _pool(h, w_pool[l].astype(BF16), pool_scale[l], bsz, seq,
                   dst["pool_u"], dst["pool_g"], ts=512)
    half = N_BRANCH // 2
    o_gla, w_mg_a = _gla(h, w_up_pad, b_gla[l], gla_norm[l], bsz, seq,
                         (dst["gq"], dst["gk"], dst["gv"], dst["gg"], dst["glr"]), 512,
                         w_in_t, l, n_main, half, MERGE_TN)
    o_swa, w_mg_b = _swa(h, pos_f, sinks[l], bsz, seq, (dst["sq"], dst["sk"], dst["sv"], dst["sg"]),
                         w_in_t, l, n_main + half * d, half, MERGE_TN)
    o_mem = _mem_attn(h, mkv, bsz, seq, mem_b.shape[0] // bsz, (dst["xq"], dst["xg"]), ts=512)

    y, w_out_b = _merge(xb, (o_pool, o_gla, o_swa, o_mem), w_mg_a, w_mg_b, w_branch, w_out, l,
                        tm=1024, tn=MERGE_TN)
    z = _out_proj(y, w_out_b, x, alpha, tm=2048, tn=256, per_step=2)
    return _layer_norm(z, ln_g[l], ln_b[l], tm=512, emit_bf16=emit_bf16)


def kernel(x, mem, positions, w_in, w_pool, pool_scale, w_gla_up, b_gla, gla_norm, sinks,
           w_mem_kv, w_branch, w_out, ln_g, ln_b):
    bsz, seq, d = x.shape
    depth = w_in.shape[0]
    alpha = (2 * depth) ** 0.25
    xf = x.reshape(bsz * seq, d)
    xb = xf.astype(BF16)
    mem_b = mem.reshape(bsz * mem.shape[1], d).astype(BF16)
    pos_f = positions.astype(F32).reshape(bsz * seq, 1)
    w_in_t = jnp.swapaxes(w_in, 1, 2)
    for l in range(depth):
        xf, xb = _hybrid_layer(
            l, xf, xb, mem_b, pos_f, w_in_t, w_pool, pool_scale, w_gla_up, b_gla,
            gla_norm, sinks, w_mem_kv, w_branch, w_out, ln_g, ln_b,
            alpha, l + 1 < depth, bsz, seq)
    return xf.reshape(bsz, seq, d)
```

```python
import functools
import math

import jax
import jax.numpy as jnp
import numpy as np
from jax import lax
from jax.experimental import pallas as pl
from jax.experimental.pallas import tpu as pltpu

F32 = jnp.float32
BF16 = jnp.bfloat16

N_BRANCH = 4
POOL_WINDOWS = (2, 4, 8, 16)
GLA_HEADS = 4
GLA_RANK = 16
GLA_TAU = 16.0
GLA_CHUNK = 64
GLA_SUB = 16
GLA_SAFE_LOG2 = 24.0
GLA_SAFE_KEY = 2.0 ** 100
SWA_HEAD_DIM = 64
WINDOW = 128
SWA_BLOCK = 128
ROPE_THETA = 500000.0
ROPE_DIM = SWA_HEAD_DIM // 4
XA_HEADS = 4
LN_EPS = 1e-5

LANES = 128
SUBLANES = 8
V7X_MXU_K = 256
V7X_VMEM_BYTES = 64 * 1024 * 1024
VMEM_CAP = V7X_VMEM_BYTES - 8 * 1024 * 1024
VMEM_TEMP_BYTES = 12 * 1024 * 1024


def _vmem_limit(streamed_bytes, resident_bytes=0):
    return int(min(VMEM_CAP, 2 * streamed_bytes + resident_bytes + VMEM_TEMP_BYTES))


def _nbytes(shape, dtype):
    return int(np.prod(shape)) * jnp.dtype(dtype).itemsize


def _silu(g):
    return g * jax.nn.sigmoid(g)


def _div_pow2(x, n):
    assert n & (n - 1) == 0
    return x >> (n.bit_length() - 1)


def _mod_pow2(x, n):
    assert n & (n - 1) == 0
    return x & (n - 1)


NT_DIMS = (((1,), (1,)), ((), ()))


def _mm_kernel(a_ref, w_ref, o_ref):
    w = w_ref[...].astype(BF16)
    o_ref[...] = jnp.dot(a_ref[...], w, preferred_element_type=F32).astype(o_ref.dtype)


def _matmul(a, w, layer, out_dtype, tm, tn, name):
    m, k = a.shape
    n = w.shape[-1]
    assert m % tm == 0 and n % tn == 0
    resident = _nbytes((tm, k), a.dtype)
    streamed = _nbytes((k, tn), w.dtype) + _nbytes((tm, tn), out_dtype)
    return pl.pallas_call(
        _mm_kernel,
        out_shape=jax.ShapeDtypeStruct((m, n), out_dtype),
        grid=(m // tm, n // tn),
        in_specs=[pl.BlockSpec((tm, k), lambda i, j: (i, 0), pipeline_mode=pl.Buffered(1)),
                  pl.BlockSpec((None, k, tn), lambda i, j: (layer, 0, j))],
        out_specs=pl.BlockSpec((tm, tn), lambda i, j: (i, j)),
        compiler_params=pltpu.CompilerParams(
            dimension_semantics=("parallel", "arbitrary"),
            vmem_limit_bytes=_vmem_limit(streamed, resident)),
        name=name,
    )(a, w)


def _out_proj_kernel(y_ref, w_ref, x_ref, o_ref, *, alpha, tn):
    y = y_ref[...]
    for t in range(o_ref.shape[1] // tn):
        cols = slice(t * tn, (t + 1) * tn)
        o_ref[:, cols] = alpha * x_ref[:, cols] + jnp.dot(
            y, w_ref[:, cols], preferred_element_type=F32)


def _out_proj(y, w_out_b, x, alpha, tm, tn, per_step):
    m, k = y.shape
    n = w_out_b.shape[1]
    tb = per_step * tn
    assert m % tm == 0 and n % tb == 0
    resident = _nbytes((tm, k), y.dtype)
    streamed = _nbytes((k, tb), w_out_b.dtype) + 2 * _nbytes((tm, tb), F32)
    return pl.pallas_call(
        functools.partial(_out_proj_kernel, alpha=alpha, tn=tn),
        out_shape=jax.ShapeDtypeStruct((m, n), F32),
        grid=(m // tm, n // tb),
        in_specs=[pl.BlockSpec((tm, k), lambda i, j: (i, 0), pipeline_mode=pl.Buffered(1)),
                  pl.BlockSpec((k, tb), lambda i, j: (0, j)),
                  pl.BlockSpec((tm, tb), lambda i, j: (i, j))],
        out_specs=pl.BlockSpec((tm, tb), lambda i, j: (i, j)),
        compiler_params=pltpu.CompilerParams(
            dimension_semantics=("parallel", "arbitrary"),
            vmem_limit_bytes=_vmem_limit(streamed, resident)),
        name="out_proj",
    )(y, w_out_b, x)


def _in_proj_kernel(tab_ref, x_ref, *refs):
    *w_refs, o_ref = refs
    x = x_ref[...]
    tn = w_refs[0].shape[1]
    for t, w_ref in enumerate(w_refs):
        w = w_ref[0].astype(BF16)
        o_ref[:, t * tn:(t + 1) * tn] = lax.dot_general(x, w, NT_DIMS, preferred_element_type=F32)


def _in_proj(xb, w_in_t, layer, col_starts, tm, tn, per_step):
    m, k = xb.shape
    nblk = len(col_starts)
    assert m % tm == 0 and nblk % per_step == 0 and all(int(s) % SUBLANES == 0 for s in col_starts)
    resident = _nbytes((tm, k), xb.dtype)
    streamed = per_step * (_nbytes((tn, k), w_in_t.dtype) + _nbytes((tm, tn), F32))

    def window(t):
        return pl.BlockSpec(
            (pl.Element(1), pl.Element(tn), pl.Element(k)),
            lambda i, j, tab: (layer, pl.multiple_of(tab[j * per_step + t], SUBLANES), 0))

    return pl.pallas_call(
        _in_proj_kernel,
        out_shape=jax.ShapeDtypeStruct((m, nblk * tn), F32),
        grid_spec=pltpu.PrefetchScalarGridSpec(
            num_scalar_prefetch=1,
            grid=(m // tm, nblk // per_step),
            in_specs=[pl.BlockSpec((tm, k), lambda i, j, tab: (i, 0), pipeline_mode=pl.Buffered(1))]
            + [window(t) for t in range(per_step)],
            out_specs=pl.BlockSpec((tm, per_step * tn), lambda i, j, tab: (i, j))),
        compiler_params=pltpu.CompilerParams(
            dimension_semantics=("parallel", "arbitrary"),
            vmem_limit_bytes=_vmem_limit(streamed, resident)),
        name="in_proj",
    )(jnp.asarray(np.asarray(col_starts, np.int32)), xb, *([w_in_t] * per_step))


def _gate_stage_plan(w_in_t, layer, row0, n_branches, n_steps, gate_tn, step_of):
    k = w_in_t.shape[-1]
    rows = n_branches * k
    slab = rows // n_steps
    per_slab = slab // gate_tn
    slabs_per_branch = k // slab
    assert rows % n_steps == 0 and slab % gate_tn == 0 and k % slab == 0 and row0 % SUBLANES == 0
    in_spec = pl.BlockSpec(
        (pl.Element(1), pl.Element(slab), pl.Element(k)),
        lambda *ids: (layer, pl.multiple_of(row0 + step_of(*ids) * slab, SUBLANES), 0))
    out_spec = pl.BlockSpec(
        (per_slab, gate_tn, k),
        lambda *ids: (step_of(*ids) % slabs_per_branch, step_of(*ids) // slabs_per_branch, 0))
    out_shape = jax.ShapeDtypeStruct((k // gate_tn, n_branches * gate_tn, k), BF16)
    return in_spec, out_spec, out_shape, _nbytes((slab, k), w_in_t.dtype) + _nbytes((slab, k), BF16)


def _stage_gate_slab(src_ref, dst_ref):
    rows = dst_ref.shape[1]
    for t in range(dst_ref.shape[0]):
        dst_ref[t] = src_ref[0, t * rows:(t + 1) * rows, :].astype(dst_ref.dtype)


def _merge_kernel(x_ref, o0, o1, o2, o3, ga_ref, gb_ref, w_ref, wo_ref, y_ref, wos_ref):
    wos_ref[...] = wo_ref[...].astype(wos_ref.dtype)
    x = x_ref[...]
    tn = y_ref.shape[1]
    half = N_BRANCH // 2
    acc = None
    for b, o_ref in enumerate((o0, o1, o2, o3)):
        g_ref, bl = (ga_ref, b) if b < half else (gb_ref, b - half)
        gate = lax.dot_general(x, g_ref[bl * tn:(bl + 1) * tn, :], NT_DIMS, preferred_element_type=F32)
        proj = jnp.dot(o_ref[...], w_ref[b].astype(BF16), preferred_element_type=F32)
        term = jax.nn.sigmoid(gate) * proj
        acc = term if acc is None else acc + term
    y_ref[...] = acc.astype(y_ref.dtype)


def _merge(xb, branches, w_mg_a, w_mg_b, w_branch, w_out, layer, tm, tn):
    m, d = xb.shape
    bw = branches[0].shape[1]
    nj = d // tn
    n_steps = (m // tm) * nj
    sr = d // n_steps
    assert d % n_steps == 0 and sr % (2 * SUBLANES) == 0
    in_specs = [pl.BlockSpec((tm, d), lambda i, j: (i, 0), pipeline_mode=pl.Buffered(1))]
    in_specs += [pl.BlockSpec((tm, bw), lambda i, j: (i, 0), pipeline_mode=pl.Buffered(1))
                 for _ in range(N_BRANCH)]
    in_specs += [pl.BlockSpec((None, N_BRANCH // 2 * tn, d), lambda i, j: (j, 0, 0)),
                 pl.BlockSpec((None, N_BRANCH // 2 * tn, d), lambda i, j: (j, 0, 0)),
                 pl.BlockSpec((None, N_BRANCH, bw, tn), lambda i, j: (layer, 0, 0, j)),
                 pl.BlockSpec((None, sr, d), lambda i, j: (layer, i * nj + j, 0))]
    resident = _nbytes((tm, d), BF16) + N_BRANCH * _nbytes((tm, bw), BF16)
    streamed = (N_BRANCH * (_nbytes((tn, d), BF16) + _nbytes((bw, tn), w_branch.dtype))
                + _nbytes((tm, tn), BF16) + _nbytes((sr, d), w_out.dtype) + _nbytes((sr, d), BF16))
    return pl.pallas_call(
        _merge_kernel,
        out_shape=[jax.ShapeDtypeStruct((m, d), BF16), jax.ShapeDtypeStruct((d, d), BF16)],
        grid=(m // tm, nj),
        in_specs=in_specs,
        out_specs=[pl.BlockSpec((tm, tn), lambda i, j: (i, j)),
                   pl.BlockSpec((sr, d), lambda i, j: (i * nj + j, 0))],
        compiler_params=pltpu.CompilerParams(
            dimension_semantics=("parallel", "arbitrary"),
            vmem_limit_bytes=_vmem_limit(streamed, resident)),
        name="merge",
    )(xb, *branches, w_mg_a, w_mg_b, w_branch, w_out)


def _ln_kernel(z_ref, g_ref, b_ref, o_ref, ob_ref=None):
    z = z_ref[...]
    mu = jnp.mean(z, axis=-1, keepdims=True)
    zc = z - mu
    var = jnp.mean(zc * zc, axis=-1, keepdims=True)
    y = zc * lax.rsqrt(var + LN_EPS) * g_ref[...] + b_ref[...]
    o_ref[...] = y
    if ob_ref is not None:
        ob_ref[...] = y.astype(BF16)


def _layer_norm(z, g, b, tm, emit_bf16):
    m, d = z.shape
    out_shape = [jax.ShapeDtypeStruct((m, d), F32)]
    out_specs = [pl.BlockSpec((tm, d), lambda i: (i, 0))]
    if emit_bf16:
        out_shape.append(jax.ShapeDtypeStruct((m, d), BF16))
        out_specs.append(pl.BlockSpec((tm, d), lambda i: (i, 0)))
    blocks = 2 * _nbytes((tm, d), F32) + _nbytes((tm, d), BF16)
    res = pl.pallas_call(
        _ln_kernel,
        out_shape=out_shape,
        grid=(m // tm,),
        in_specs=[pl.BlockSpec((tm, d), lambda i: (i, 0)),
                  pl.BlockSpec((1, d), lambda i: (0, 0)),
                  pl.BlockSpec((1, d), lambda i: (0, 0))],
        out_specs=out_specs,
        compiler_params=pltpu.CompilerParams(
            dimension_semantics=("parallel",), vmem_limit_bytes=_vmem_limit(blocks)),
        name="layer_norm",
    )(z, g.reshape(1, d), b.reshape(1, d))
    return res if emit_bf16 else (res[0], None)


POOL_HIST = max(POOL_WINDOWS)
POOL_PAD = 2 * POOL_HIST


def _pool_kernel(u_ref, g_ref, wp_ref, sc_ref, o_ref, buf, lvl_a, lvl_b, *, ts, group):
    i = pl.program_id(1)
    ext = POOL_PAD + ts

    @pl.when(i == 0)
    def _():
        buf[0:POOL_PAD, :] = jnp.zeros((POOL_PAD, buf.shape[1]), F32)

    buf[POOL_PAD:ext, :] = u_ref[...]
    src = buf
    level_refs = []
    for lvl, dst in enumerate((lvl_a, lvl_b, lvl_a, lvl_b)[:len(POOL_WINDOWS)], start=1):
        lag = 2 ** (lvl - 1)
        r0 = SUBLANES * lvl
        c0 = (lvl - 1) * group
        dst[r0:ext, c0:] = src[r0:ext, c0:] + src[r0 - lag:ext - lag, c0:]
        level_refs.append(dst)
        src = dst

    t = i * ts + lax.broadcasted_iota(jnp.int32, (ts, 1), 0)
    outs = []
    for gi, w in enumerate(POOL_WINDOWS):
        assert w == 2 ** (gi + 1)
        cols = slice(gi * group, (gi + 1) * group)
        acc = level_refs[gi][POOL_PAD:ext, cols]
        cnt = jnp.minimum(t + 1, w).astype(F32)
        pooled = acc / cnt - buf[POOL_PAD:ext, cols]
        outs.append(jnp.dot(pooled.astype(BF16), wp_ref[gi], preferred_element_type=F32))
    o = jnp.concatenate(outs, axis=1) * sc_ref[...] * _silu(g_ref[...])
    o_ref[...] = o.astype(o_ref.dtype)
    buf[POOL_PAD - POOL_HIST:POOL_PAD, :] = buf[ext - POOL_HIST:ext, :]


def _pool(h, w_pool_b, pool_scale, bsz, seq, col_u, col_g, ts):
    bw = pool_scale.shape[-1]
    group = bw // len(POOL_WINDOWS)
    nblk = seq // ts
    blocks = 2 * _nbytes((ts, bw), F32) + _nbytes((ts, bw), BF16)
    scratch = [pltpu.VMEM((ts + POOL_PAD, bw), F32) for _ in range(3)]
    resident = sum(_nbytes(sc.shape, sc.dtype) for sc in scratch)
    return pl.pallas_call(
        functools.partial(_pool_kernel, ts=ts, group=group),
        out_shape=jax.ShapeDtypeStruct((bsz * seq, bw), BF16),
        grid=(bsz, nblk),
        in_specs=[pl.BlockSpec((ts, bw), lambda b, i: (b * nblk + i, col_u // bw)),
                  pl.BlockSpec((ts, bw), lambda b, i: (b * nblk + i, col_g // bw)),
                  pl.BlockSpec(w_pool_b.shape, lambda b, i: (0, 0, 0)),
                  pl.BlockSpec((1, bw), lambda b, i: (0, 0))],
        out_specs=pl.BlockSpec((ts, bw), lambda b, i: (b * nblk + i, 0)),
        scratch_shapes=scratch,
        compiler_params=pltpu.CompilerParams(
            dimension_semantics=("parallel", "arbitrary"),
            vmem_limit_bytes=_vmem_limit(blocks, resident)),
        name="pool_mixer",
    )(h, h, w_pool_b, pool_scale.reshape(1, bw))


def _split_bf16(x, parts):
    out = []
    r = x
    for _ in range(parts):
        p = r.astype(BF16)
        out.append(p)
        r = r - p.astype(F32)
    return out


LOG2E = math.log2(math.e)


def _gla_cumsum_operator():
    r = np.arange(V7X_MXU_K)
    tri = ((r[:, None] // GLA_CHUNK == r[None, :] // GLA_CHUNK) & (r[None, :] <= r[:, None]))
    return jnp.asarray(tri, BF16)


def _gla_kernel(q_ref, k_ref, v_ref, gg_ref, lr_ref, wup_ref, bg_ref, nrm_ref, tri_ref, wg_ref,
                o_ref, wgs_ref, state_ref, b_ref, *, rows, dk, dv):
    _stage_gate_slab(wg_ref, wgs_ref)
    i = pl.program_id(1)
    C = GLA_CHUNK
    nsub = C // GLA_SUB

    @pl.when(i == 0)
    def _():
        state_ref[...] = jnp.zeros(state_ref.shape, F32)

    lr_h, lr_m = _split_bf16(lr_ref[...], 2)
    w_h, w_m = _split_bf16(wup_ref[...], 2)
    z = (jnp.dot(lr_h, w_h, preferred_element_type=F32)
         + jnp.dot(lr_h, w_m, preferred_element_type=F32)
         + jnp.dot(lr_m, w_h, preferred_element_type=F32)) + bg_ref[...]
    log2_a = (jnp.minimum(z, 0.0) - jnp.log(1.0 + jnp.exp(-jnp.abs(z)))) * (LOG2E / GLA_TAU)

    tb = tri_ref.shape[0]
    parts = _split_bf16(log2_a, 3)
    for r in range(rows // tb):
        acc = None
        for part in parts:
            t = jnp.dot(tri_ref[...], part[r * tb:(r + 1) * tb, :], preferred_element_type=F32)
            acc = t if acc is None else acc + t
        b_ref[r * tb:(r + 1) * tb, :] = acc

    row = lax.broadcasted_iota(jnp.int32, (C, 1), 0)
    col = lax.broadcasted_iota(jnp.int32, (1, C), 1)
    row_sub = _div_pow2(row, GLA_SUB)
    col_sub = _div_pow2(col, GLA_SUB)
    lane_c = lax.broadcasted_iota(jnp.int32, (GLA_SUB, C), 1)
    scale = dk ** -0.5

    def sub_heads(x):
        return jnp.concatenate(
            [jnp.broadcast_to(x[a * GLA_SUB:a * GLA_SUB + 1, :], (GLA_SUB, dk))
             for a in range(nsub)], axis=0)

    def chunk_body(c, carry, *, bounded):
        r0 = pl.multiple_of(c * C, C)
        for h in range(GLA_HEADS):
            kc = slice(h * dk, (h + 1) * dk)
            vc = slice(h * dv, (h + 1) * dv)
            qs = q_ref[pl.ds(r0, C), kc] * scale
            k = k_ref[pl.ds(r0, C), kc]
            v = v_ref[pl.ds(r0, C), vc].astype(BF16)
            b = b_ref[pl.ds(r0, C), kc]
            b_last = b[C - 1:C, :]
            st = state_ref[h]

            q_in = (qs * jnp.exp2(b)).astype(BF16)
            o = lax.dot_general(q_in, st.astype(BF16), NT_DIMS, preferred_element_type=F32)

            q_t = qs * jnp.exp2(b - sub_heads(b))
            lhs = jnp.concatenate(
                [jnp.where(row_sub == a, q_t, 0.0) for a in range(nsub)], axis=1).astype(BF16)
            key_exp = [b[a * GLA_SUB:a * GLA_SUB + 1, :] - b for a in range(nsub)]
            if not bounded:
                key_exp = [jnp.minimum(e, 0.0) for e in key_exp]
            rhs = jnp.concatenate([k * jnp.exp2(e) for e in key_exp], axis=1).astype(BF16)
            a_fac = lax.dot_general(lhs, rhs, NT_DIMS, preferred_element_type=F32)

            if bounded:
                attn = jnp.where(row >= col, a_fac, 0.0)
            else:
                diag_blocks = []
                for a in range(nsub):
                    rs = slice(a * GLA_SUB, (a + 1) * GLA_SUB)
                    qa = qs[rs, :]
                    ba = b[rs, :]
                    blk = jnp.zeros((GLA_SUB, C), F32)
                    for jj in range(GLA_SUB):
                        j = a * GLA_SUB + jj
                        tj = qa * k[j:j + 1, :] * jnp.exp2(ba - b[j:j + 1, :])
                        cj = jnp.sum(tj, axis=1, keepdims=True)
                        blk = jnp.where(lane_c == j, cj, blk)
                    diag_blocks.append(blk)
                a_diag = jnp.concatenate(diag_blocks, axis=0)
                attn = jnp.where(row_sub > col_sub, a_fac,
                                 jnp.where((row_sub == col_sub) & (row >= col), a_diag, 0.0))
            o = o + jnp.dot(attn.astype(BF16), v, preferred_element_type=F32)

            k_dec = (k * jnp.exp2(b_last - b)).astype(BF16)
            upd = lax.dot_general(v, k_dec, (((0,), (0,)), ((), ())),
                                  preferred_element_type=F32)
            state_ref[h] = st * jnp.exp2(b_last) + upd

            ms = jnp.mean(o * o, axis=-1, keepdims=True)
            on = o * lax.rsqrt(ms + LN_EPS) * nrm_ref[:, vc]
            o_ref[pl.ds(r0, C), vc] = (on * _silu(gg_ref[pl.ds(r0, C), vc])).astype(o_ref.dtype)
        return carry

    bounded = ((-jnp.min(b_ref[...]) < GLA_SAFE_LOG2)
               & (jnp.max(jnp.abs(k_ref[...])) < GLA_SAFE_KEY))

    @pl.when(bounded)
    def _():
        lax.fori_loop(0, rows // C, functools.partial(chunk_body, bounded=True), 0, unroll=2)

    @pl.when(jnp.logical_not(bounded))
    def _():
        lax.fori_loop(0, rows // C, functools.partial(chunk_body, bounded=False), 0)


def _gla(h, w_up_pad, b_gla, gla_norm, bsz, seq, cols, rows,
         w_in_t, layer, gate_row0, gate_branches, gate_tn):
    col_q, col_k, col_v, col_g, col_lr = cols
    dkt = b_gla.shape[-1]
    dvt = gla_norm.shape[-1]
    dk, dv = dkt // GLA_HEADS, dvt // GLA_HEADS
    lrw = w_up_pad.shape[0]
    nblk = seq // rows
    tri = _gla_cumsum_operator()
    assert rows % tri.shape[0] == 0 and tri.shape[0] % GLA_CHUNK == 0
    g_in, g_out, g_shape, g_bytes = _gate_stage_plan(
        w_in_t, layer, gate_row0, gate_branches, bsz * nblk, gate_tn, lambda b, i: b * nblk + i)
    blocks = (2 * _nbytes((rows, dkt), F32) + 2 * _nbytes((rows, dvt), F32)
              + _nbytes((rows, lrw), F32) + _nbytes((rows, dvt), BF16) + g_bytes)
    resident = (_nbytes((rows, dkt), F32) + _nbytes((GLA_HEADS, dv, dk), F32)
                + _nbytes(tri.shape, BF16))
    const = lambda b, i: (0, 0)
    return pl.pallas_call(
        functools.partial(_gla_kernel, rows=rows, dk=dk, dv=dv),
        out_shape=[jax.ShapeDtypeStruct((bsz * seq, dvt), BF16), g_shape],
        grid=(bsz, nblk),
        in_specs=[pl.BlockSpec((rows, dkt), lambda b, i: (b * nblk + i, col_q // dkt)),
                  pl.BlockSpec((rows, dkt), lambda b, i: (b * nblk + i, col_k // dkt)),
                  pl.BlockSpec((rows, dvt), lambda b, i: (b * nblk + i, col_v // dvt)),
                  pl.BlockSpec((rows, dvt), lambda b, i: (b * nblk + i, col_g // dvt)),
                  pl.BlockSpec((rows, lrw), lambda b, i: (b * nblk + i, col_lr // lrw)),
                  pl.BlockSpec((lrw, dkt), const),
                  pl.BlockSpec((1, dkt), const),
                  pl.BlockSpec((1, dvt), const),
                  pl.BlockSpec(tri.shape, const),
                  g_in],
        out_specs=[pl.BlockSpec((rows, dvt), lambda b, i: (b * nblk + i, 0)), g_out],
        scratch_shapes=[pltpu.VMEM((GLA_HEADS, dv, dk), F32),
                        pltpu.VMEM((rows, dkt), F32)],
        compiler_params=pltpu.CompilerParams(
            dimension_semantics=("parallel", "arbitrary"),
            vmem_limit_bytes=_vmem_limit(blocks, resident)),
        name="gla_mixer",
    )(h, h, h, h, h, w_up_pad, b_gla.reshape(1, dkt), gla_norm.reshape(1, dvt), tri, w_in_t)


def _rope_lane_table():
    half = ROPE_DIM // 2
    inv_freq = ROPE_THETA ** (-np.arange(0, ROPE_DIM, 2, dtype=np.float32) / ROPE_DIM)
    lane = np.arange(LANES) % SWA_HEAD_DIM
    tab = np.where(lane < ROPE_DIM, inv_freq[lane % half], 0.0).astype(np.float32)
    return jnp.asarray(tab.reshape(1, LANES))


SWA_ROW_CHUNK = 32
SWA_BLOCKS_PER_STEP = 2


def _swa_kernel(sink_ref, q_ref, k_ref, v_ref, g_ref, pos_ref, inv_ref, wg_ref, o_ref, wgs_ref,
                *scratch, n_pairs, n_sub):
    _stage_gate_slab(wg_ref, wgs_ref)
    step = pl.program_id(1)
    for sb in range(n_sub):
        rows = pl.ds(sb * SWA_BLOCK, SWA_BLOCK)
        _swa_block(step * n_sub + sb, sink_ref, q_ref.at[rows], k_ref.at[rows], v_ref.at[rows],
                   g_ref.at[rows], pos_ref.at[rows], inv_ref, o_ref.at[rows], *scratch,
                   n_pairs=n_pairs)


def _swa_block(n, sink_ref, q_ref, k_ref, v_ref, g_ref, pos_ref, inv_ref, o_ref,
               wk_ref, wv_ref, qr_ref, s_ref, p_ref, es_ref, bias_ref, *, n_pairs):
    blk = SWA_BLOCK
    half = ROPE_DIM // 2
    hd = SWA_HEAD_DIM
    ppg = n_pairs // 2
    rc_rows = SWA_ROW_CHUNK

    @pl.when(n == 0)
    def _():
        wk_ref[...] = jnp.zeros(wk_ref.shape, BF16)
        row = lax.broadcasted_iota(jnp.int32, (4 * blk, 2 * LANES), 0)
        col = lax.broadcasted_iota(jnp.int32, (4 * blk, 2 * LANES), 1)
        ones = ((col >= LANES) & ((col >= LANES + hd) == (row >= 2 * blk))).astype(BF16)
        for g in range(2):
            wv_ref[g] = ones

    lane = lax.broadcasted_iota(jnp.int32, (1, LANES), 1)
    lane_h = _mod_pow2(lane, hd)
    lo = lane < hd
    ang = pos_ref[...] * inv_ref[...]
    cos = jnp.cos(ang)
    sin = jnp.sin(ang)

    def rope(x):
        x_up = pltpu.roll(x, LANES - half, axis=1)
        x_dn = pltpu.roll(x, half, axis=1)
        return x * cos + jnp.where(lane_h < half, -x_up, x_dn) * sin

    for g in range(2):
        for part in range(2):
            base = part * 2 * blk
            wk_ref[g, base:base + blk, :] = wk_ref[g, base + blk:base + 2 * blk, :]
            wv_ref[g, base:base + blk, 0:LANES] = wv_ref[g, base + blk:base + 2 * blk, 0:LANES]
    k_r = rope(k_ref[...])
    v_c = v_ref[...]
    k_sw = pltpu.roll(k_r, hd, axis=1)
    v_sw = pltpu.roll(v_c, hd, axis=1)
    zero = jnp.zeros_like(k_r)
    for g in range(2):
        k_lo, k_hi = (k_r, k_sw) if g == 0 else (k_sw, k_r)
        v_lo, v_hi = (v_c, v_sw) if g == 0 else (v_sw, v_c)
        wk_ref[g, blk:2 * blk, :] = jnp.where(lo, k_lo, zero).astype(BF16)
        wk_ref[g, 3 * blk:4 * blk, :] = jnp.where(lo, zero, k_hi).astype(BF16)
        wv_ref[g, blk:2 * blk, 0:LANES] = jnp.where(lo, v_lo, zero).astype(BF16)
        wv_ref[g, 3 * blk:4 * blk, 0:LANES] = jnp.where(lo, zero, v_hi).astype(BF16)

    r = lax.broadcasted_iota(jnp.int32, (blk, 2 * blk), 0)
    c = lax.broadcasted_iota(jnp.int32, (blk, 2 * blk), 1)
    valid = (c > r) & (c <= r + WINDOW) & ((n > 0) | (c >= blk))
    bias_ref[...] = jnp.where(valid, 0.0, -jnp.inf).astype(F32)

    qscale = hd ** -0.5
    for p in range(n_pairs):
        g, pp = divmod(p, ppg)
        qr_ref[g, pp * blk:(pp + 1) * blk, :] = (
            rope(q_ref[:, p * LANES:(p + 1) * LANES]) * qscale).astype(BF16)

    for g in range(2):
        s_ref[g] = lax.dot_general(qr_ref[g], wk_ref[g], NT_DIMS, preferred_element_type=F32)

    for g in range(2):
        for pp in range(ppg):
            for hh in range(2):
                sink = sink_ref[2 * (g * ppg + pp) + hh]
                kc = slice(hh * 2 * blk, (hh + 1) * 2 * blk)
                for rc in range(blk // rc_rows):
                    rows = slice(pp * blk + rc * rc_rows, pp * blk + (rc + 1) * rc_rows)
                    sh = s_ref[g, rows, kc] + bias_ref[rc * rc_rows:(rc + 1) * rc_rows, :]
                    m = jnp.maximum(jnp.max(sh, axis=-1, keepdims=True), sink)
                    p_ref[g, rows, kc] = jnp.exp(sh - m).astype(BF16)
                    es_ref[g, rows, hh * hd:(hh + 1) * hd] = jnp.broadcast_to(
                        jnp.exp(sink - m), (rc_rows, hd))

    for g in range(2):
        o2 = jnp.dot(p_ref[g], wv_ref[g], preferred_element_type=F32)
        for pp in range(ppg):
            rows = slice(pp * blk, (pp + 1) * blk)
            cols = slice((g * ppg + pp) * LANES, (g * ppg + pp + 1) * LANES)
            den = o2[rows, LANES:] + es_ref[g, rows, :]
            o_ref[:, cols] = (o2[rows, :LANES] / den * _silu(g_ref[:, cols])).astype(o_ref.dtype)


def _swa(h, pos_f, sinks, bsz, seq, cols, w_in_t, layer, gate_row0, gate_branches, gate_tn):
    col_q, col_k, col_v, col_g = cols
    n_heads = sinks.shape[0]
    bw = n_heads * SWA_HEAD_DIM
    n_pairs = bw // LANES
    n_sub = SWA_BLOCKS_PER_STEP
    blk = n_sub * SWA_BLOCK
    nblk = seq // blk
    scratch = [pltpu.VMEM((2, 4 * SWA_BLOCK, LANES), BF16),
               pltpu.VMEM((2, 4 * SWA_BLOCK, 2 * LANES), BF16),
               pltpu.VMEM((2, 4 * SWA_BLOCK, LANES), BF16),
               pltpu.VMEM((2, 4 * SWA_BLOCK, 4 * SWA_BLOCK), F32),
               pltpu.VMEM((2, 4 * SWA_BLOCK, 4 * SWA_BLOCK), BF16),
               pltpu.VMEM((2, 4 * SWA_BLOCK, LANES), F32),
               pltpu.VMEM((SWA_BLOCK, 2 * SWA_BLOCK), F32)]
    g_in, g_out, g_shape, g_bytes = _gate_stage_plan(
        w_in_t, layer, gate_row0, gate_branches, bsz * nblk, gate_tn, lambda b, i: b * nblk + i)
    blocks = (2 * _nbytes((blk, bw), F32) + 3 * _nbytes((blk, LANES), F32) + _nbytes((blk, bw), BF16)
              + g_bytes)
    resident = sum(_nbytes(sc.shape, sc.dtype) for sc in scratch)

    return pl.pallas_call(
        functools.partial(_swa_kernel, n_pairs=n_pairs, n_sub=n_sub),
        out_shape=[jax.ShapeDtypeStruct((bsz * seq, bw), BF16), g_shape],
        grid=(bsz, nblk),
        in_specs=[pl.BlockSpec(memory_space=pltpu.SMEM),
                  pl.BlockSpec((blk, bw), lambda b, i: (b * nblk + i, col_q // bw)),
                  pl.BlockSpec((blk, LANES), lambda b, i: (b * nblk + i, col_k // LANES)),
                  pl.BlockSpec((blk, LANES), lambda b, i: (b * nblk + i, col_v // LANES)),
                  pl.BlockSpec((blk, bw), lambda b, i: (b * nblk + i, col_g // bw)),
                  pl.BlockSpec((blk, 1), lambda b, i: (b * nblk + i, 0)),
                  pl.BlockSpec((1, LANES), lambda b, i: (0, 0)),
                  g_in],
        out_specs=[pl.BlockSpec((blk, bw), lambda b, i: (b * nblk + i, 0)), g_out],
        scratch_shapes=scratch,
        compiler_params=pltpu.CompilerParams(
            dimension_semantics=("parallel", "arbitrary"),
            vmem_limit_bytes=_vmem_limit(blocks, resident)),
        name="swa_mixer",
    )(sinks, h, h, h, h, pos_f, _rope_lane_table(), w_in_t)


def _mem_kernel(q_ref, g_ref, mk_ref, mv_ref, o_ref, *, hd):
    scale = hd ** -0.5
    for h in range(XA_HEADS):
        cols = slice(h * hd, (h + 1) * hd)
        qh = q_ref[:, cols].astype(BF16)
        s = lax.dot_general(qh, mk_ref[:, cols], (((1,), (1,)), ((), ())),
                            preferred_element_type=F32) * scale
        m = jnp.max(s, axis=-1, keepdims=True)
        e = jnp.exp(s - m)
        p = e / jnp.sum(e, axis=-1, keepdims=True)
        o = jnp.dot(p.astype(BF16), mv_ref[:, cols], preferred_element_type=F32)
        o_ref[:, cols] = (o * _silu(g_ref[:, cols])).astype(o_ref.dtype)


def _mem_attn(h, mkv, bsz, seq, mem_len, cols, ts):
    col_q, col_g = cols
    bw = mkv.shape[1] // 2
    hd = bw // XA_HEADS
    nblk = seq // ts
    blocks = 2 * _nbytes((ts, bw), F32) + 2 * _nbytes((mem_len, bw), BF16) + _nbytes((ts, bw), BF16)
    return pl.pallas_call(
        functools.partial(_mem_kernel, hd=hd),
        out_shape=jax.ShapeDtypeStruct((bsz * seq, bw), BF16),
        grid=(bsz, nblk),
        in_specs=[pl.BlockSpec((ts, bw), lambda b, i: (b * nblk + i, col_q // bw)),
                  pl.BlockSpec((ts, bw), lambda b, i: (b * nblk + i, col_g // bw)),
                  pl.BlockSpec((mem_len, bw), lambda b, i: (b, 0)),
                  pl.BlockSpec((mem_len, bw), lambda b, i: (b, 1))],
        out_specs=pl.BlockSpec((ts, bw), lambda b, i: (b * nblk + i, 0)),
        compiler_params=pltpu.CompilerParams(
            dimension_semantics=("parallel", "parallel"), vmem_limit_bytes=_vmem_limit(blocks)),
        name="mem_attn",
    )(h, h, mkv, mkv)


IN_TN = 2 * LANES
MERGE_TN = 2 * LANES


def _main_layout(d_model):
    bw = d_model // 4
    dkt, dvt = bw // 2, bw
    kvw = 2 * SWA_HEAD_DIM
    names = ["pool_u", "pool_g", "gq", "gk", "gv", "gg", "glr", "sq", "sk", "sv", "sg", "xq", "xg"]
    widths = [bw, bw, dkt, dkt, dvt, dvt, GLA_RANK, bw, kvw, kvw, bw, bw, bw]
    src = {}
    off = 0
    for nme, w in zip(names, widths):
        src[nme] = (off, w)
        off += w
    n_main = off
    order = ["pool_u", "pool_g", "gv", "gg", "sq", "sg", "xq", "xg", "gq", "gk", "sk", "sv", "glr"]
    dst = {}
    cols = []
    off = 0
    for nme in order:
        o, w = src[nme]
        w_dst = w if nme != "glr" else IN_TN
        assert off % w_dst == 0
        dst[nme] = off
        cols.append(o + np.arange(w_dst))
        off += w_dst
    cols = np.concatenate(cols)
    assert off % IN_TN == 0 and cols.max() < n_main
    starts = cols[::IN_TN]
    assert np.all(cols.reshape(-1, IN_TN) == starts[:, None] + np.arange(IN_TN))
    return starts, dst, n_main


def _hybrid_layer(l, x, xb, mem_b, pos_f, w_in_t, w_pool, pool_scale, w_gla_up, b_gla, gla_norm,
                  sinks, w_mem_kv, w_branch, w_out, ln_g, ln_b, alpha, emit_bf16, bsz, seq):
    m, d = x.shape
    col_starts, dst, n_main = _main_layout(d)
    w_up_pad = jnp.concatenate(
        [w_gla_up[l], jnp.zeros((IN_TN - GLA_RANK, w_gla_up.shape[-1]), w_gla_up.dtype)], axis=0)

    h = _in_proj(xb, w_in_t, l, col_starts, tm=2048, tn=IN_TN, per_step=2)
    mkv = _matmul(mem_b, w_mem_kv, l, BF16, tm=mem_b.shape[0], tn=512, name="mem_kv")

    o_pool = _pool(h, w_pool[l].astype(BF16), pool_scale[l], bsz, seq,
                   dst["pool_u"], dst["pool_g"], ts=512)
    half = N_BRANCH // 2
    o_gla, w_mg_a = _gla(h, w_up_pad, b_gla[l], gla_norm[l], bsz, seq,
                         (dst["gq"], dst["gk"], dst["gv"], dst["gg"], dst["glr"]), 512,
                         w_in_t, l, n_main, half, MERGE_TN)
    o_swa, w_mg_b = _swa(h, pos_f, sinks[l], bsz, seq, (dst["sq"], dst["sk"], dst["sv"], dst["sg"]),
                         w_in_t, l, n_main + half * d, half, MERGE_TN)
    o_mem = _mem_attn(h, mkv, bsz, seq, mem_b.shape[0] // bsz, (dst["xq"], dst["xg"]), ts=512)

    y, w_out_b = _merge(xb, (o_pool, o_gla, o_swa, o_mem), w_mg_a, w_mg_b, w_branch, w_out, l,
                        tm=1024, tn=MERGE_TN)
    z = _out_proj(y, w_out_b, x, alpha, tm=2048, tn=256, per_step=2)
    return _layer_norm(z, ln_g[l], ln_b[l], tm=512, emit_bf16=emit_bf16)


def kernel(x, mem, positions, w_in, w_pool, pool_scale, w_gla_up, b_gla, gla_norm, sinks,
           w_mem_kv, w_branch, w_out, ln_g, ln_b):
    bsz, seq, d = x.shape
    depth = w_in.shape[0]
    alpha = (2 * depth) ** 0.25
    xf = x.reshape(bsz * seq, d)
    xb = xf.astype(BF16)
    mem_b = mem.reshape(bsz * mem.shape[1], d).astype(BF16)
    pos_f = positions.astype(F32).reshape(bsz * seq, 1)
    w_in_t = jnp.swapaxes(w_in, 1, 2)
    for l in range(depth):
        xf, xb = _hybrid_layer(
            l, xf, xb, mem_b, pos_f, w_in_t, w_pool, pool_scale, w_gla_up, b_gla,
            gla_norm, sinks, w_mem_kv, w_branch, w_out, ln_g, ln_b,
            alpha, l + 1 < depth, bsz, seq)
    return xf.reshape(bsz, seq, d)
```

```python
import functools
import math

import jax
import jax.numpy as jnp
import numpy as np
from jax import lax
from jax.experimental import pallas as pl
from jax.experimental.pallas import tpu as pltpu

F32 = jnp.float32
BF16 = jnp.bfloat16

N_BRANCH = 4
POOL_WINDOWS = (2, 4, 8, 16)
GLA_HEADS = 4
GLA_RANK = 16
GLA_TAU = 16.0
GLA_CHUNK = 64
GLA_SUB = 16
GLA_SAFE_LOG2 = 24.0
GLA_SAFE_KEY = 2.0 ** 100
SWA_HEAD_DIM = 64
WINDOW = 128
SWA_BLOCK = 128
ROPE_THETA = 500000.0
ROPE_DIM = SWA_HEAD_DIM // 4
XA_HEADS = 4
LN_EPS = 1e-5

LANES = 128
SUBLANES = 8
V7X_MXU_K = 256
V7X_VMEM_BYTES = 64 * 1024 * 1024
VMEM_CAP = V7X_VMEM_BYTES - 8 * 1024 * 1024
VMEM_TEMP_BYTES = 12 * 1024 * 1024


def _vmem_limit(streamed_bytes, resident_bytes=0):
    return int(min(VMEM_CAP, 2 * streamed_bytes + resident_bytes + VMEM_TEMP_BYTES))


def _nbytes(shape, dtype):
    return int(np.prod(shape)) * jnp.dtype(dtype).itemsize


def _silu(g):
    return g * jax.nn.sigmoid(g)


def _div_pow2(x, n):
    assert n & (n - 1) == 0
    return x >> (n.bit_length() - 1)


def _mod_pow2(x, n):
    assert n & (n - 1) == 0
    return x & (n - 1)


NT_DIMS = (((1,), (1,)), ((), ()))


def _mm_kernel(a_ref, w_ref, o_ref):
    w = w_ref[...].astype(BF16)
    o_ref[...] = jnp.dot(a_ref[...], w, preferred_element_type=F32).astype(o_ref.dtype)


def _matmul(a, w, layer, out_dtype, tm, tn, name):
    m, k = a.shape
    n = w.shape[-1]
    assert m % tm == 0 and n % tn == 0
    resident = _nbytes((tm, k), a.dtype)
    streamed = _nbytes((k, tn), w.dtype) + _nbytes((tm, tn), out_dtype)
    return pl.pallas_call(
        _mm_kernel,
        out_shape=jax.ShapeDtypeStruct((m, n), out_dtype),
        grid=(m // tm, n // tn),
        in_specs=[pl.BlockSpec((tm, k), lambda i, j: (i, 0), pipeline_mode=pl.Buffered(1)),
                  pl.BlockSpec((None, k, tn), lambda i, j: (layer, 0, j))],
        out_specs=pl.BlockSpec((tm, tn), lambda i, j: (i, j)),
        compiler_params=pltpu.CompilerParams(
            dimension_semantics=("parallel", "arbitrary"),
            vmem_limit_bytes=_vmem_limit(streamed, resident)),
        name=name,
    )(a, w)


def _out_proj_kernel(y_ref, w_ref, x_ref, o_ref, *, alpha, tn):
    y = y_ref[...]
    for t in range(o_ref.shape[1] // tn):
        cols = slice(t * tn, (t + 1) * tn)
        o_ref[:, cols] = alpha * x_ref[:, cols] + jnp.dot(
            y, w_ref[:, cols], preferred_element_type=F32)


def _out_proj(y, w_out_b, x, alpha, tm, tn, per_step):
    m, k = y.shape
    n = w_out_b.shape[1]
    tb = per_step * tn
    assert m % tm == 0 and n % tb == 0
    resident = _nbytes((tm, k), y.dtype)
    streamed = _nbytes((k, tb), w_out_b.dtype) + 2 * _nbytes((tm, tb), F32)
    return pl.pallas_call(
        functools.partial(_out_proj_kernel, alpha=alpha, tn=tn),
        out_shape=jax.ShapeDtypeStruct((m, n), F32),
        grid=(m // tm, n // tb),
        in_specs=[pl.BlockSpec((tm, k), lambda i, j: (i, 0), pipeline_mode=pl.Buffered(1)),
                  pl.BlockSpec((k, tb), lambda i, j: (0, j)),
                  pl.BlockSpec((tm, tb), lambda i, j: (i, j))],
        out_specs=pl.BlockSpec((tm, tb), lambda i, j: (i, j)),
        compiler_params=pltpu.CompilerParams(
            dimension_semantics=("parallel", "arbitrary"),
            vmem_limit_bytes=_vmem_limit(streamed, resident)),
        name="out_proj",
    )(y, w_out_b, x)


def _in_proj_kernel(tab_ref, x_ref, *refs):
    *w_refs, o_ref = refs
    x = x_ref[...]
    tn = w_refs[0].shape[1]
    for t, w_ref in enumerate(w_refs):
        w = w_ref[0].astype(BF16)
        o_ref[:, t * tn:(t + 1) * tn] = lax.dot_general(x, w, NT_DIMS, preferred_element_type=F32)


def _in_proj(xb, w_in_t, layer, col_starts, tm, tn, per_step):
    m, k = xb.shape
    nblk = len(col_starts)
    assert m % tm == 0 and nblk % per_step == 0 and all(int(s) % SUBLANES == 0 for s in col_starts)
    resident = _nbytes((tm, k), xb.dtype)
    streamed = per_step * (_nbytes((tn, k), w_in_t.dtype) + _nbytes((tm, tn), F32))

    def window(t):
        return pl.BlockSpec(
            (pl.Element(1), pl.Element(tn), pl.Element(k)),
            lambda i, j, tab: (layer, pl.multiple_of(tab[j * per_step + t], SUBLANES), 0))

    return pl.pallas_call(
        _in_proj_kernel,
        out_shape=jax.ShapeDtypeStruct((m, nblk * tn), F32),
        grid_spec=pltpu.PrefetchScalarGridSpec(
            num_scalar_prefetch=1,
            grid=(m // tm, nblk // per_step),
            in_specs=[pl.BlockSpec((tm, k), lambda i, j, tab: (i, 0), pipeline_mode=pl.Buffered(1))]
            + [window(t) for t in range(per_step)],
            out_specs=pl.BlockSpec((tm, per_step * tn), lambda i, j, tab: (i, j))),
        compiler_params=pltpu.CompilerParams(
            dimension_semantics=("parallel", "arbitrary"),
            vmem_limit_bytes=_vmem_limit(streamed, resident)),
        name="in_proj",
    )(jnp.asarray(np.asarray(col_starts, np.int32)), xb, *([w_in_t] * per_step))


def _gate_stage_plan(w_in_t, layer, row0, n_branches, n_steps, gate_tn, step_of):
    k = w_in_t.shape[-1]
    rows = n_branches * k
    slab = rows // n_steps
    per_slab = slab // gate_tn
    slabs_per_branch = k // slab
    assert rows % n_steps == 0 and slab % gate_tn == 0 and k % slab == 0 and row0 % SUBLANES == 0
    in_spec = pl.BlockSpec(
        (pl.Element(1), pl.Element(slab), pl.Element(k)),
        lambda *ids: (layer, pl.multiple_of(row0 + step_of(*ids) * slab, SUBLANES), 0))
    out_spec = pl.BlockSpec(
        (per_slab, gate_tn, k),
        lambda *ids: (step_of(*ids) % slabs_per_branch, step_of(*ids) // slabs_per_branch, 0))
    out_shape = jax.ShapeDtypeStruct((k // gate_tn, n_branches * gate_tn, k), BF16)
    return in_spec, out_spec, out_shape, _nbytes((slab, k), w_in_t.dtype) + _nbytes((slab, k), BF16)


def _stage_gate_slab(src_ref, dst_ref):
    rows = dst_ref.shape[1]
    for t in range(dst_ref.shape[0]):
        dst_ref[t] = src_ref[0, t * rows:(t + 1) * rows, :].astype(dst_ref.dtype)


def _merge_kernel(x_ref, o0, o1, o2, o3, ga_ref, gb_ref, w_ref, wo_ref, y_ref, wos_ref):
    wos_ref[...] = wo_ref[...].astype(wos_ref.dtype)
    x = x_ref[...]
    tn = y_ref.shape[1]
    half = N_BRANCH // 2
    acc = None
    for b, o_ref in enumerate((o0, o1, o2, o3)):
        g_ref, bl = (ga_ref, b) if b < half else (gb_ref, b - half)
        gate = lax.dot_general(x, g_ref[bl * tn:(bl + 1) * tn, :], NT_DIMS, preferred_element_type=F32)
        proj = jnp.dot(o_ref[...], w_ref[b].astype(BF16), preferred_element_type=F32)
        term = jax.nn.sigmoid(gate) * proj
        acc = term if acc is None else acc + term
    y_ref[...] = acc.astype(y_ref.dtype)


def _merge(xb, branches, w_mg_a, w_mg_b, w_branch, w_out, layer, tm, tn):
    m, d = xb.shape
    bw = branches[0].shape[1]
    nj = d // tn
    n_steps = (m // tm) * nj
    sr = d // n_steps
    assert d % n_steps == 0 and sr % (2 * SUBLANES) == 0
    in_specs = [pl.BlockSpec((tm, d), lambda i, j: (i, 0), pipeline_mode=pl.Buffered(1))]
    in_specs += [pl.BlockSpec((tm, bw), lambda i, j: (i, 0), pipeline_mode=pl.Buffered(1))
                 for _ in range(N_BRANCH)]
    in_specs += [pl.BlockSpec((None, N_BRANCH // 2 * tn, d), lambda i, j: (j, 0, 0)),
                 pl.BlockSpec((None, N_BRANCH // 2 * tn, d), lambda i, j: (j, 0, 0)),
                 pl.BlockSpec((None, N_BRANCH, bw, tn), lambda i, j: (layer, 0, 0, j)),
                 pl.BlockSpec((None, sr, d), lambda i, j: (layer, i * nj + j, 0))]
    resident = _nbytes((tm, d), BF16) + N_BRANCH * _nbytes((tm, bw), BF16)
    streamed = (N_BRANCH * (_nbytes((tn, d), BF16) + _nbytes((bw, tn), w_branch.dtype))
                + _nbytes((tm, tn), BF16) + _nbytes((sr, d), w_out.dtype) + _nbytes((sr, d), BF16))
    return pl.pallas_call(
        _merge_kernel,
        out_shape=[jax.ShapeDtypeStruct((m, d), BF16), jax.ShapeDtypeStruct((d, d), BF16)],
        grid=(m // tm, nj),
        in_specs=in_specs,
        out_specs=[pl.BlockSpec((tm, tn), lambda i, j: (i, j)),
                   pl.BlockSpec((sr, d), lambda i, j: (i * nj + j, 0))],
        compiler_params=pltpu.CompilerParams(
            dimension_semantics=("parallel", "arbitrary"),
            vmem_limit_bytes=_vmem_limit(streamed, resident)),
        name="merge",
    )(xb, *branches, w_mg_a, w_mg_b, w_branch, w_out)


def _ln_kernel(z_ref, g_ref, b_ref, o_ref, ob_ref=None):
    z = z_ref[...]
    mu = jnp.mean(z, axis=-1, keepdims=True)
    zc = z - mu
    var = jnp.mean(zc * zc, axis=-1, keepdims=True)
    y = zc * lax.rsqrt(var + LN_EPS) * g_ref[...] + b_ref[...]
    o_ref[...] = y
    if ob_ref is not None:
        ob_ref[...] = y.astype(BF16)


def _layer_norm(z, g, b, tm, emit_bf16):
    m, d = z.shape
    out_shape = [jax.ShapeDtypeStruct((m, d), F32)]
    out_specs = [pl.BlockSpec((tm, d), lambda i: (i, 0))]
    if emit_bf16:
        out_shape.append(jax.ShapeDtypeStruct((m, d), BF16))
        out_specs.append(pl.BlockSpec((tm, d), lambda i: (i, 0)))
    blocks = 2 * _nbytes((tm, d), F32) + _nbytes((tm, d), BF16)
    res = pl.pallas_call(
        _ln_kernel,
        out_shape=out_shape,
        grid=(m // tm,),
        in_specs=[pl.BlockSpec((tm, d), lambda i: (i, 0)),
                  pl.BlockSpec((1, d), lambda i: (0, 0)),
                  pl.BlockSpec((1, d), lambda i: (0, 0))],
        out_specs=out_specs,
        compiler_params=pltpu.CompilerParams(
            dimension_semantics=("parallel",), vmem_limit_bytes=_vmem_limit(blocks)),
        name="layer_norm",
    )(z, g.reshape(1, d), b.reshape(1, d))
    return res if emit_bf16 else (res[0], None)


POOL_HIST = max(POOL_WINDOWS)
POOL_PAD = 2 * POOL_HIST


def _pool_kernel(u_ref, g_ref, wp_ref, sc_ref, o_ref, buf, lvl_a, lvl_b, *, ts, group):
    i = pl.program_id(1)
    ext = POOL_PAD + ts

    @pl.when(i == 0)
    def _():
        buf[0:POOL_PAD, :] = jnp.zeros((POOL_PAD, buf.shape[1]), F32)

    buf[POOL_PAD:ext, :] = u_ref[...]
    src = buf
    level_refs = []
    for lvl, dst in enumerate((lvl_a, lvl_b, lvl_a, lvl_b)[:len(POOL_WINDOWS)], start=1):
        lag = 2 ** (lvl - 1)
        r0 = SUBLANES * lvl
        c0 = (lvl - 1) * group
        dst[r0:ext, c0:] = src[r0:ext, c0:] + src[r0 - lag:ext - lag, c0:]
        level_refs.append(dst)
        src = dst

    t = i * ts + lax.broadcasted_iota(jnp.int32, (ts, 1), 0)
    outs = []
    for gi, w in enumerate(POOL_WINDOWS):
        assert w == 2 ** (gi + 1)
        cols = slice(gi * group, (gi + 1) * group)
        acc = level_refs[gi][POOL_PAD:ext, cols]
        cnt = jnp.minimum(t + 1, w).astype(F32)
        pooled = acc / cnt - buf[POOL_PAD:ext, cols]
        outs.append(jnp.dot(pooled.astype(BF16), wp_ref[gi], preferred_element_type=F32))
    o = jnp.concatenate(outs, axis=1) * sc_ref[...] * _silu(g_ref[...])
    o_ref[...] = o.astype(o_ref.dtype)
    buf[POOL_PAD - POOL_HIST:POOL_PAD, :] = buf[ext - POOL_HIST:ext, :]


def _pool(h, w_pool_b, pool_scale, bsz, seq, col_u, col_g, ts):
    bw = pool_scale.shape[-1]
    group = bw // len(POOL_WINDOWS)
    nblk = seq // ts
    blocks = 2 * _nbytes((ts, bw), F32) + _nbytes((ts, bw), BF16)
    scratch = [pltpu.VMEM((ts + POOL_PAD, bw), F32) for _ in range(3)]
    resident = sum(_nbytes(sc.shape, sc.dtype) for sc in scratch)
    return pl.pallas_call(
        functools.partial(_pool_kernel, ts=ts, group=group),
        out_shape=jax.ShapeDtypeStruct((bsz * seq, bw), BF16),
        grid=(bsz, nblk),
        in_specs=[pl.BlockSpec((ts, bw), lambda b, i: (b * nblk + i, col_u // bw)),
                  pl.BlockSpec((ts, bw), lambda b, i: (b * nblk + i, col_g // bw)),
                  pl.BlockSpec(w_pool_b.shape, lambda b, i: (0, 0, 0)),
                  pl.BlockSpec((1, bw), lambda b, i: (0, 0))],
        out_specs=pl.BlockSpec((ts, bw), lambda b, i: (b * nblk + i, 0)),
        scratch_shapes=scratch,
        compiler_params=pltpu.CompilerParams(
            dimension_semantics=("parallel", "arbitrary"),
            vmem_limit_bytes=_vmem_limit(blocks, resident)),
        name="pool_mixer",
    )(h, h, w_pool_b, pool_scale.reshape(1, bw))


def _split_bf16(x, parts):
    out = []
    r = x
    for _ in range(parts):
        p = r.astype(BF16)
        out.append(p)
        r = r - p.astype(F32)
    return out


LOG2E = math.log2(math.e)


def _gla_cumsum_operator():
    r = np.arange(V7X_MXU_K)
    tri = ((r[:, None] // GLA_CHUNK == r[None, :] // GLA_CHUNK) & (r[None, :] <= r[:, None]))
    return jnp.asarray(tri, BF16)


def _gla_kernel(q_ref, k_ref, v_ref, gg_ref, lr_ref, wup_ref, bg_ref, nrm_ref, tri_ref, wg_ref,
                o_ref, wgs_ref, state_ref, b_ref, *, rows, dk, dv):
    _stage_gate_slab(wg_ref, wgs_ref)
    i = pl.program_id(1)
    C = GLA_CHUNK
    nsub = C // GLA_SUB

    @pl.when(i == 0)
    def _():
        state_ref[...] = jnp.zeros(state_ref.shape, F32)

    lr_h, lr_m = _split_bf16(lr_ref[...], 2)
    w_h, w_m = _split_bf16(wup_ref[...], 2)
    z = (jnp.dot(lr_h, w_h, preferred_element_type=F32)
         + jnp.dot(lr_h, w_m, preferred_element_type=F32)
         + jnp.dot(lr_m, w_h, preferred_element_type=F32)) + bg_ref[...]
    log2_a = (jnp.minimum(z, 0.0) - jnp.log(1.0 + jnp.exp(-jnp.abs(z)))) * (LOG2E / GLA_TAU)

    tb = tri_ref.shape[0]
    parts = _split_bf16(log2_a, 3)
    for r in range(rows // tb):
        acc = None
        for part in parts:
            t = jnp.dot(tri_ref[...], part[r * tb:(r + 1) * tb, :], preferred_element_type=F32)
            acc = t if acc is None else acc + t
        b_ref[r * tb:(r + 1) * tb, :] = acc

    row = lax.broadcasted_iota(jnp.int32, (C, 1), 0)
    col = lax.broadcasted_iota(jnp.int32, (1, C), 1)
    row_sub = _div_pow2(row, GLA_SUB)
    col_sub = _div_pow2(col, GLA_SUB)
    lane_c = lax.broadcasted_iota(jnp.int32, (GLA_SUB, C), 1)
    scale = dk ** -0.5

    def sub_heads(x):
        return jnp.concatenate(
            [jnp.broadcast_to(x[a * GLA_SUB:a * GLA_SUB + 1, :], (GLA_SUB, dk))
             for a in range(nsub)], axis=0)

    def chunk_body(c, carry, *, bounded):
        r0 = pl.multiple_of(c * C, C)
        for h in range(GLA_HEADS):
            kc = slice(h * dk, (h + 1) * dk)
            vc = slice(h * dv, (h + 1) * dv)
            qs = q_ref[pl.ds(r0, C), kc] * scale
            k = k_ref[pl.ds(r0, C), kc]
            v = v_ref[pl.ds(r0, C), vc].astype(BF16)
            b = b_ref[pl.ds(r0, C), kc]
            b_last = b[C - 1:C, :]
            st = state_ref[h]

            q_in = (qs * jnp.exp2(b)).astype(BF16)
            o = lax.dot_general(q_in, st.astype(BF16), NT_DIMS, preferred_element_type=F32)

            q_t = qs * jnp.exp2(b - sub_heads(b))
            lhs = jnp.concatenate(
                [jnp.where(row_sub == a, q_t, 0.0) for a in range(nsub)], axis=1).astype(BF16)
            key_exp = [b[a * GLA_SUB:a * GLA_SUB + 1, :] - b for a in range(nsub)]
            if not bounded:
                key_exp = [jnp.minimum(e, 0.0) for e in key_exp]
            rhs = jnp.concatenate([k * jnp.exp2(e) for e in key_exp], axis=1).astype(BF16)
            a_fac = lax.dot_general(lhs, rhs, NT_DIMS, preferred_element_type=F32)

            if bounded:
                attn = jnp.where(row >= col, a_fac, 0.0)
            else:
                diag_blocks = []
                for a in range(nsub):
                    rs = slice(a * GLA_SUB, (a + 1) * GLA_SUB)
                    qa = qs[rs, :]
                    ba = b[rs, :]
                    blk = jnp.zeros((GLA_SUB, C), F32)
                    for jj in range(GLA_SUB):
                        j = a * GLA_SUB + jj
                        tj = qa * k[j:j + 1, :] * jnp.exp2(ba - b[j:j + 1, :])
                        cj = jnp.sum(tj, axis=1, keepdims=True)
                        blk = jnp.where(lane_c == j, cj, blk)
                    diag_blocks.append(blk)
                a_diag = jnp.concatenate(diag_blocks, axis=0)
                attn = jnp.where(row_sub > col_sub, a_fac,
                                 jnp.where((row_sub == col_sub) & (row >= col), a_diag, 0.0))
            o = o + jnp.dot(attn.astype(BF16), v, preferred_element_type=F32)

            k_dec = (k * jnp.exp2(b_last - b)).astype(BF16)
            upd = lax.dot_general(v, k_dec, (((0,), (0,)), ((), ())),
                                  preferred_element_type=F32)
            state_ref[h] = st * jnp.exp2(b_last) + upd

            ms = jnp.mean(o * o, axis=-1, keepdims=True)
            on = o * lax.rsqrt(ms + LN_EPS) * nrm_ref[:, vc]
            o_ref[pl.ds(r0, C), vc] = (on * _silu(gg_ref[pl.ds(r0, C), vc])).astype(o_ref.dtype)
        return carry

    bounded = ((-jnp.min(b_ref[...]) < GLA_SAFE_LOG2)
               & (jnp.max(jnp.abs(k_ref[...])) < GLA_SAFE_KEY))

    @pl.when(bounded)
    def _():
        lax.fori_loop(0, rows // C, functools.partial(chunk_body, bounded=True), 0, unroll=2)

    @pl.when(jnp.logical_not(bounded))
    def _():
        lax.fori_loop(0, rows // C, functools.partial(chunk_body, bounded=False), 0)


def _gla(h, w_up_pad, b_gla, gla_norm, bsz, seq, cols, rows,
         w_in_t, layer, gate_row0, gate_branches, gate_tn):
    col_q, col_k, col_v, col_g, col_lr = cols
    dkt = b_gla.shape[-1]
    dvt = gla_norm.shape[-1]
    dk, dv = dkt // GLA_HEADS, dvt // GLA_HEADS
    lrw = w_up_pad.shape[0]
    nblk = seq // rows
    tri = _gla_cumsum_operator()
    assert rows % tri.shape[0] == 0 and tri.shape[0] % GLA_CHUNK == 0
    g_in, g_out, g_shape, g_bytes = _gate_stage_plan(
        w_in_t, layer, gate_row0, gate_branches, bsz * nblk, gate_tn, lambda b, i: b * nblk + i)
    blocks = (2 * _nbytes((rows, dkt), F32) + 2 * _nbytes((rows, dvt), F32)
              + _nbytes((rows, lrw), F32) + _nbytes((rows, dvt), BF16) + g_bytes)
    resident = (_nbytes((rows, dkt), F32) + _nbytes((GLA_HEADS, dv, dk), F32)
                + _nbytes(tri.shape, BF16))
    const = lambda b, i: (0, 0)
    return pl.pallas_call(
        functools.partial(_gla_kernel, rows=rows, dk=dk, dv=dv),
        out_shape=[jax.ShapeDtypeStruct((bsz * seq, dvt), BF16), g_shape],
        grid=(bsz, nblk),
        in_specs=[pl.BlockSpec((rows, dkt), lambda b, i: (b * nblk + i, col_q // dkt)),
                  pl.BlockSpec((rows, dkt), lambda b, i: (b * nblk + i, col_k // dkt)),
                  pl.BlockSpec((rows, dvt), lambda b, i: (b * nblk + i, col_v // dvt)),
                  pl.BlockSpec((rows, dvt), lambda b, i: (b * nblk + i, col_g // dvt)),
                  pl.BlockSpec((rows, lrw), lambda b, i: (b * nblk + i, col_lr // lrw)),
                  pl.BlockSpec((lrw, dkt), const),
                  pl.BlockSpec((1, dkt), const),
                  pl.BlockSpec((1, dvt), const),
                  pl.BlockSpec(tri.shape, const),
                  g_in],
        out_specs=[pl.BlockSpec((rows, dvt), lambda b, i: (b * nblk + i, 0)), g_out],
        scratch_shapes=[pltpu.VMEM((GLA_HEADS, dv, dk), F32),
                        pltpu.VMEM((rows, dkt), F32)],
        compiler_params=pltpu.CompilerParams(
            dimension_semantics=("parallel", "arbitrary"),
            vmem_limit_bytes=_vmem_limit(blocks, resident)),
        name="gla_mixer",
    )(h, h, h, h, h, w_up_pad, b_gla.reshape(1, dkt), gla_norm.reshape(1, dvt), tri, w_in_t)


def _rope_lane_table():
    half = ROPE_DIM // 2
    inv_freq = ROPE_THETA ** (-np.arange(0, ROPE_DIM, 2, dtype=np.float32) / ROPE_DIM)
    lane = np.arange(LANES) % SWA_HEAD_DIM
    tab = np.where(lane < ROPE_DIM, inv_freq[lane % half], 0.0).astype(np.float32)
    return jnp.asarray(tab.reshape(1, LANES))


SWA_ROW_CHUNK = 32
SWA_BLOCKS_PER_STEP = 2


def _swa_kernel(sink_ref, q_ref, k_ref, v_ref, g_ref, pos_ref, inv_ref, wg_ref, o_ref, wgs_ref,
                *scratch, n_pairs, n_sub):
    _stage_gate_slab(wg_ref, wgs_ref)
    step = pl.program_id(1)
    for sb in range(n_sub):
        rows = pl.ds(sb * SWA_BLOCK, SWA_BLOCK)
        _swa_block(step * n_sub + sb, sink_ref, q_ref.at[rows], k_ref.at[rows], v_ref.at[rows],
                   g_ref.at[rows], pos_ref.at[rows], inv_ref, o_ref.at[rows], *scratch,
                   n_pairs=n_pairs)


def _swa_block(n, sink_ref, q_ref, k_ref, v_ref, g_ref, pos_ref, inv_ref, o_ref,
               wk_ref, wv_ref, qr_ref, s_ref, p_ref, es_ref, bias_ref, *, n_pairs):
    blk = SWA_BLOCK
    half = ROPE_DIM // 2
    hd = SWA_HEAD_DIM
    ppg = n_pairs // 2
    rc_rows = SWA_ROW_CHUNK

    @pl.when(n == 0)
    def _():
        wk_ref[...] = jnp.zeros(wk_ref.shape, BF16)
        row = lax.broadcasted_iota(jnp.int32, (4 * blk, 2 * LANES), 0)
        col = lax.broadcasted_iota(jnp.int32, (4 * blk, 2 * LANES), 1)
        ones = ((col >= LANES) & ((col >= LANES + hd) == (row >= 2 * blk))).astype(BF16)
        for g in range(2):
            wv_ref[g] = ones

    lane = lax.broadcasted_iota(jnp.int32, (1, LANES), 1)
    lane_h = _mod_pow2(lane, hd)
    lo = lane < hd
    ang = pos_ref[...] * inv_ref[...]
    cos = jnp.cos(ang)
    sin = jnp.sin(ang)

    def rope(x):
        x_up = pltpu.roll(x, LANES - half, axis=1)
        x_dn = pltpu.roll(x, half, axis=1)
        return x * cos + jnp.where(lane_h < half, -x_up, x_dn) * sin

    for g in range(2):
        for part in range(2):
            base = part * 2 * blk
            wk_ref[g, base:base + blk, :] = wk_ref[g, base + blk:base + 2 * blk, :]
            wv_ref[g, base:base + blk, 0:LANES] = wv_ref[g, base + blk:base + 2 * blk, 0:LANES]
    k_r = rope(k_ref[...])
    v_c = v_ref[...]
    k_sw = pltpu.roll(k_r, hd, axis=1)
    v_sw = pltpu.roll(v_c, hd, axis=1)
    zero = jnp.zeros_like(k_r)
    for g in range(2):
        k_lo, k_hi = (k_r, k_sw) if g == 0 else (k_sw, k_r)
        v_lo, v_hi = (v_c, v_sw) if g == 0 else (v_sw, v_c)
        wk_ref[g, blk:2 * blk, :] = jnp.where(lo, k_lo, zero).astype(BF16)
        wk_ref[g, 3 * blk:4 * blk, :] = jnp.where(lo, zero, k_hi).astype(BF16)
        wv_ref[g, blk:2 * blk, 0:LANES] = jnp.where(lo, v_lo, zero).astype(BF16)
        wv_ref[g, 3 * blk:4 * blk, 0:LANES] = jnp.where(lo, zero, v_hi).astype(BF16)

    r = lax.broadcasted_iota(jnp.int32, (blk, 2 * blk), 0)
    c = lax.broadcasted_iota(jnp.int32, (blk, 2 * blk), 1)
    valid = (c > r) & (c <= r + WINDOW) & ((n > 0) | (c >= blk))
    bias_ref[...] = jnp.where(valid, 0.0, -jnp.inf).astype(F32)

    qscale = hd ** -0.5
    for p in range(n_pairs):
        g, pp = divmod(p, ppg)
        qr_ref[g, pp * blk:(pp + 1) * blk, :] = (
            rope(q_ref[:, p * LANES:(p + 1) * LANES]) * qscale).astype(BF16)

    for g in range(2):
        s_ref[g] = lax.dot_general(qr_ref[g], wk_ref[g], NT_DIMS, preferred_element_type=F32)

    for g in range(2):
        for pp in range(ppg):
            for hh in range(2):
                sink = sink_ref[2 * (g * ppg + pp) + hh]
                kc = slice(hh * 2 * blk, (hh + 1) * 2 * blk)
                for rc in range(blk // rc_rows):
                    rows = slice(pp * blk + rc * rc_rows, pp * blk + (rc + 1) * rc_rows)
                    sh = s_ref[g, rows, kc] + bias_ref[rc * rc_rows:(rc + 1) * rc_rows, :]
                    m = jnp.maximum(jnp.max(sh, axis=-1, keepdims=True), sink)
                    p_ref[g, rows, kc] = jnp.exp(sh - m).astype(BF16)
                    es_ref[g, rows, hh * hd:(hh + 1) * hd] = jnp.broadcast_to(
                        jnp.exp(sink - m), (rc_rows, hd))

    for g in range(2):
        o2 = jnp.dot(p_ref[g], wv_ref[g], preferred_element_type=F32)
        for pp in range(ppg):
            rows = slice(pp * blk, (pp + 1) * blk)
            cols = slice((g * ppg + pp) * LANES, (g * ppg + pp + 1) * LANES)
            den = o2[rows, LANES:] + es_ref[g, rows, :]
            o_ref[:, cols] = (o2[rows, :LANES] / den * _silu(g_ref[:, cols])).astype(o_ref.dtype)


def _swa(h, pos_f, sinks, bsz, seq, cols, w_in_t, layer, gate_row0, gate_branches, gate_tn):
    col_q, col_k, col_v, col_g = cols
    n_heads = sinks.shape[0]
    bw = n_heads * SWA_HEAD_DIM
    n_pairs = bw // LANES
    assert LANES == 2 * SWA_HEAD_DIM and n_pairs % 2 == 0
    n_sub = SWA_BLOCKS_PER_STEP
    blk = n_sub * SWA_BLOCK
    nblk = seq // blk
    scratch = [pltpu.VMEM((2, 4 * SWA_BLOCK, LANES), BF16),
               pltpu.VMEM((2, 4 * SWA_BLOCK, 2 * LANES), BF16),
               pltpu.VMEM((2, 4 * SWA_BLOCK, LANES), BF16),
               pltpu.VMEM((2, 4 * SWA_BLOCK, 4 * SWA_BLOCK), F32),
               pltpu.VMEM((2, 4 * SWA_BLOCK, 4 * SWA_BLOCK), BF16),
               pltpu.VMEM((2, 4 * SWA_BLOCK, LANES), F32),
               pltpu.VMEM((SWA_BLOCK, 2 * SWA_BLOCK), F32)]
    g_in, g_out, g_shape, g_bytes = _gate_stage_plan(
        w_in_t, layer, gate_row0, gate_branches, bsz * nblk, gate_tn, lambda b, i: b * nblk + i)
    blocks = (2 * _nbytes((blk, bw), F32) + 3 * _nbytes((blk, LANES), F32) + _nbytes((blk, bw), BF16)
              + g_bytes)
    resident = sum(_nbytes(sc.shape, sc.dtype) for sc in scratch)

    return pl.pallas_call(
        functools.partial(_swa_kernel, n_pairs=n_pairs, n_sub=n_sub),
        out_shape=[jax.ShapeDtypeStruct((bsz * seq, bw), BF16), g_shape],
        grid=(bsz, nblk),
        in_specs=[pl.BlockSpec(memory_space=pltpu.SMEM),
                  pl.BlockSpec((blk, bw), lambda b, i: (b * nblk + i, col_q // bw)),
                  pl.BlockSpec((blk, LANES), lambda b, i: (b * nblk + i, col_k // LANES)),
                  pl.BlockSpec((blk, LANES), lambda b, i: (b * nblk + i, col_v // LANES)),
                  pl.BlockSpec((blk, bw), lambda b, i: (b * nblk + i, col_g // bw)),
                  pl.BlockSpec((blk, 1), lambda b, i: (b * nblk + i, 0)),
                  pl.BlockSpec((1, LANES), lambda b, i: (0, 0)),
                  g_in],
        out_specs=[pl.BlockSpec((blk, bw), lambda b, i: (b * nblk + i, 0)), g_out],
        scratch_shapes=scratch,
        compiler_params=pltpu.CompilerParams(
            dimension_semantics=("parallel", "arbitrary"),
            vmem_limit_bytes=_vmem_limit(blocks, resident)),
        name="swa_mixer",
    )(sinks, h, h, h, h, pos_f, _rope_lane_table(), w_in_t)


def _mem_kernel(q_ref, g_ref, mk_ref, mv_ref, o_ref, *, hd):
    scale = hd ** -0.5
    for h in range(XA_HEADS):
        cols = slice(h * hd, (h + 1) * hd)
        qh = q_ref[:, cols].astype(BF16)
        s = lax.dot_general(qh, mk_ref[:, cols], NT_DIMS, preferred_element_type=F32) * scale
        m = jnp.max(s, axis=-1, keepdims=True)
        e = jnp.exp(s - m)
        p = e / jnp.sum(e, axis=-1, keepdims=True)
        o = jnp.dot(p.astype(BF16), mv_ref[:, cols], preferred_element_type=F32)
        o_ref[:, cols] = (o * _silu(g_ref[:, cols])).astype(o_ref.dtype)


def _mem_attn(h, mkv, bsz, seq, mem_len, cols, ts):
    col_q, col_g = cols
    bw = mkv.shape[1] // 2
    hd = bw // XA_HEADS
    nblk = seq // ts
    blocks = 2 * _nbytes((ts, bw), F32) + 2 * _nbytes((mem_len, bw), BF16) + _nbytes((ts, bw), BF16)
    return pl.pallas_call(
        functools.partial(_mem_kernel, hd=hd),
        out_shape=jax.ShapeDtypeStruct((bsz * seq, bw), BF16),
        grid=(bsz, nblk),
        in_specs=[pl.BlockSpec((ts, bw), lambda b, i: (b * nblk + i, col_q // bw)),
                  pl.BlockSpec((ts, bw), lambda b, i: (b * nblk + i, col_g // bw)),
                  pl.BlockSpec((mem_len, bw), lambda b, i: (b, 0)),
                  pl.BlockSpec((mem_len, bw), lambda b, i: (b, 1))],
        out_specs=pl.BlockSpec((ts, bw), lambda b, i: (b * nblk + i, 0)),
        compiler_params=pltpu.CompilerParams(
            dimension_semantics=("parallel", "parallel"), vmem_limit_bytes=_vmem_limit(blocks)),
        name="mem_attn",
    )(h, h, mkv, mkv)


IN_TN = 2 * LANES
MERGE_TN = 2 * LANES


def _main_layout(d_model):
    bw = d_model // 4
    dkt, dvt = bw // 2, bw
    kvw = 2 * SWA_HEAD_DIM
    names = ["pool_u", "pool_g", "gq", "gk", "gv", "gg", "glr", "sq", "sk", "sv", "sg", "xq", "xg"]
    widths = [bw, bw, dkt, dkt, dvt, dvt, GLA_RANK, bw, kvw, kvw, bw, bw, bw]
    src = {}
    off = 0
    for nme, w in zip(names, widths):
        src[nme] = (off, w)
        off += w
    n_main = off
    order = ["pool_u", "pool_g", "gv", "gg", "sq", "sg", "xq", "xg", "gq", "gk", "sk", "sv", "glr"]
    dst = {}
    cols = []
    off = 0
    for nme in order:
        o, w = src[nme]
        w_dst = w if nme != "glr" else IN_TN
        assert off % w_dst == 0
        dst[nme] = off
        cols.append(o + np.arange(w_dst))
        off += w_dst
    cols = np.concatenate(cols)
    assert off % IN_TN == 0 and cols.max() < n_main
    starts = cols[::IN_TN]
    assert np.all(cols.reshape(-1, IN_TN) == starts[:, None] + np.arange(IN_TN))
    return starts, dst, n_main


def _hybrid_layer(l, x, xb, mem_b, pos_f, w_in_t, w_pool, pool_scale, w_gla_up, b_gla, gla_norm,
                  sinks, w_mem_kv, w_branch, w_out, ln_g, ln_b, alpha, emit_bf16, bsz, seq):
    m, d = x.shape
    col_starts, dst, n_main = _main_layout(d)
    w_up_pad = jnp.concatenate(
        [w_gla_up[l], jnp.zeros((IN_TN - GLA_RANK, w_gla_up.shape[-1]), w_gla_up.dtype)], axis=0)

    h = _in_proj(xb, w_in_t, l, col_starts, tm=2048, tn=IN_TN, per_step=2)
    mkv = _matmul(mem_b, w_mem_kv, l, BF16, tm=mem_b.shape[0], tn=512, name="mem_kv")

    o_pool = _pool(h, w_pool[l].astype(BF16), pool_scale[l], bsz, seq,
                   dst["pool_u"], dst["pool_g"], ts=512)
    half = N_BRANCH // 2
    o_gla, w_mg_a = _gla(h, w_up_pad, b_gla[l], gla_norm[l], bsz, seq,
                         (dst["gq"], dst["gk"], dst["gv"], dst["gg"], dst["glr"]), 512,
                         w_in_t, l, n_main, half, MERGE_TN)
    o_swa, w_mg_b = _swa(h, pos_f, sinks[l], bsz, seq, (dst["sq"], dst["sk"], dst["sv"], dst["sg"]),
                         w_in_t, l, n_main + half * d, half, MERGE_TN)
    o_mem = _mem_attn(h, mkv, bsz, seq, mem_b.shape[0] // bsz, (dst["xq"], dst["xg"]), ts=512)

    y, w_out_b = _merge(xb, (o_pool, o_gla, o_swa, o_mem), w_mg_a, w_mg_b, w_branch, w_out, l,
                        tm=1024, tn=MERGE_TN)
    z = _out_proj(y, w_out_b, x, alpha, tm=2048, tn=256, per_step=2)
    return _layer_norm(z, ln_g[l], ln_b[l], tm=512, emit_bf16=emit_bf16)


def kernel(x, mem, positions, w_in, w_pool, pool_scale, w_gla_up, b_gla, gla_norm, sinks,
           w_mem_kv, w_branch, w_out, ln_g, ln_b):
    bsz, seq, d = x.shape
    depth = w_in.shape[0]
    alpha = (2 * depth) ** 0.25
    xf = x.reshape(bsz * seq, d)
    xb = xf.astype(BF16)
    mem_b = mem.reshape(bsz * mem.shape[1], d).astype(BF16)
    pos_f = positions.astype(F32).reshape(bsz * seq, 1)
    w_in_t = jnp.swapaxes(w_in, 1, 2)
    for l in range(depth):
        xf, xb = _hybrid_layer(
            l, xf, xb, mem_b, pos_f, w_in_t, w_pool, pool_scale, w_gla_up, b_gla,
            gla_norm, sinks, w_mem_kv, w_branch, w_out, ln_g, ln_b,
            alpha, l + 1 < depth, bsz, seq)
    return xf.reshape(bsz, seq, d)
```

```python
import functools
import math

import jax
import jax.numpy as jnp
import numpy as np
from jax import lax
from jax.experimental import pallas as pl
from jax.experimental.pallas import tpu as pltpu

F32 = jnp.float32
BF16 = jnp.bfloat16

N_BRANCH = 4
POOL_WINDOWS = (2, 4, 8, 16)
GLA_HEADS = 4
GLA_RANK = 16
GLA_TAU = 16.0
GLA_CHUNK = 64
GLA_SUB = 16
GLA_SAFE_LOG2 = 24.0
GLA_SAFE_KEY = 2.0 ** 100
SWA_HEAD_DIM = 64
WINDOW = 128
SWA_BLOCK = 128
ROPE_THETA = 500000.0
ROPE_DIM = SWA_HEAD_DIM // 4
XA_HEADS = 4
LN_EPS = 1e-5

LANES = 128
SUBLANES = 8
V7X_MXU_K = 256
V7X_VMEM_BYTES = 64 * 1024 * 1024
VMEM_CAP = V7X_VMEM_BYTES - 8 * 1024 * 1024
VMEM_TEMP_BYTES = 12 * 1024 * 1024


def _vmem_limit(streamed_bytes, resident_bytes=0):
    return int(min(VMEM_CAP, 2 * streamed_bytes + resident_bytes + VMEM_TEMP_BYTES))


def _nbytes(shape, dtype):
    return int(np.prod(shape)) * jnp.dtype(dtype).itemsize


def _silu(g):
    return g * jax.nn.sigmoid(g)


def _div_pow2(x, n):
    assert n & (n - 1) == 0
    return x >> (n.bit_length() - 1)


def _mod_pow2(x, n):
    assert n & (n - 1) == 0
    return x & (n - 1)


NT_DIMS = (((1,), (1,)), ((), ()))


def _mm_kernel(a_ref, w_ref, o_ref):
    w = w_ref[...].astype(BF16)
    o_ref[...] = jnp.dot(a_ref[...], w, preferred_element_type=F32).astype(o_ref.dtype)


def _matmul(a, w, layer, out_dtype, tm, tn, name):
    m, k = a.shape
    n = w.shape[-1]
    assert m % tm == 0 and n % tn == 0
    resident = _nbytes((tm, k), a.dtype)
    streamed = _nbytes((k, tn), w.dtype) + _nbytes((tm, tn), out_dtype)
    return pl.pallas_call(
        _mm_kernel,
        out_shape=jax.ShapeDtypeStruct((m, n), out_dtype),
        grid=(m // tm, n // tn),
        in_specs=[pl.BlockSpec((tm, k), lambda i, j: (i, 0), pipeline_mode=pl.Buffered(1)),
                  pl.BlockSpec((None, k, tn), lambda i, j: (layer, 0, j))],
        out_specs=pl.BlockSpec((tm, tn), lambda i, j: (i, j)),
        compiler_params=pltpu.CompilerParams(
            dimension_semantics=("parallel", "arbitrary"),
            vmem_limit_bytes=_vmem_limit(streamed, resident)),
        name=name,
    )(a, w)


def _out_proj_kernel(y_ref, w_ref, x_ref, o_ref, *, alpha, tn):
    y = y_ref[...]
    for t in range(o_ref.shape[1] // tn):
        cols = slice(t * tn, (t + 1) * tn)
        o_ref[:, cols] = alpha * x_ref[:, cols] + jnp.dot(
            y, w_ref[:, cols], preferred_element_type=F32)


def _out_proj(y, w_out_b, x, alpha, tm, tn, per_step):
    m, k = y.shape
    n = w_out_b.shape[1]
    tb = per_step * tn
    assert m % tm == 0 and n % tb == 0
    streamed = (_nbytes((tm, k), y.dtype) + _nbytes((k, tb), w_out_b.dtype)
                + 2 * _nbytes((tm, tb), F32))
    return pl.pallas_call(
        functools.partial(_out_proj_kernel, alpha=alpha, tn=tn),
        out_shape=jax.ShapeDtypeStruct((m, n), F32),
        grid=(m // tm, n // tb),
        in_specs=[pl.BlockSpec((tm, k), lambda i, j: (i, 0)),
                  pl.BlockSpec((k, tb), lambda i, j: (0, j)),
                  pl.BlockSpec((tm, tb), lambda i, j: (i, j))],
        out_specs=pl.BlockSpec((tm, tb), lambda i, j: (i, j)),
        compiler_params=pltpu.CompilerParams(
            dimension_semantics=("parallel", "arbitrary"),
            vmem_limit_bytes=_vmem_limit(streamed)),
        name="out_proj",
    )(y, w_out_b, x)


def _in_proj_kernel(tab_ref, x_ref, *refs):
    *w_refs, o_ref = refs
    x = x_ref[...]
    tn = w_refs[0].shape[1]
    for t, w_ref in enumerate(w_refs):
        w = w_ref[0].astype(BF16)
        o_ref[:, t * tn:(t + 1) * tn] = lax.dot_general(x, w, NT_DIMS, preferred_element_type=F32)


def _in_proj(xb, w_in_t, layer, col_starts, tm, tn, per_step):
    m, k = xb.shape
    nblk = len(col_starts)
    assert m % tm == 0 and nblk % per_step == 0 and all(int(s) % SUBLANES == 0 for s in col_starts)
    resident = _nbytes((tm, k), xb.dtype)
    streamed = per_step * (_nbytes((tn, k), w_in_t.dtype) + _nbytes((tm, tn), F32))

    def window(t):
        return pl.BlockSpec(
            (pl.Element(1), pl.Element(tn), pl.Element(k)),
            lambda i, j, tab: (layer, pl.multiple_of(tab[j * per_step + t], SUBLANES), 0))

    return pl.pallas_call(
        _in_proj_kernel,
        out_shape=jax.ShapeDtypeStruct((m, nblk * tn), F32),
        grid_spec=pltpu.PrefetchScalarGridSpec(
            num_scalar_prefetch=1,
            grid=(m // tm, nblk // per_step),
            in_specs=[pl.BlockSpec((tm, k), lambda i, j, tab: (i, 0), pipeline_mode=pl.Buffered(1))]
            + [window(t) for t in range(per_step)],
            out_specs=pl.BlockSpec((tm, per_step * tn), lambda i, j, tab: (i, j))),
        compiler_params=pltpu.CompilerParams(
            dimension_semantics=("parallel", "arbitrary"),
            vmem_limit_bytes=_vmem_limit(streamed, resident)),
        name="in_proj",
    )(jnp.asarray(np.asarray(col_starts, np.int32)), xb, *([w_in_t] * per_step))


def _gate_stage_plan(w_in_t, layer, row0, n_branches, n_steps, gate_tn, step_of):
    k = w_in_t.shape[-1]
    rows = n_branches * k
    slab = rows // n_steps
    per_slab = slab // gate_tn
    slabs_per_branch = k // slab
    assert rows % n_steps == 0 and slab % gate_tn == 0 and k % slab == 0 and row0 % SUBLANES == 0
    in_spec = pl.BlockSpec(
        (pl.Element(1), pl.Element(slab), pl.Element(k)),
        lambda *ids: (layer, pl.multiple_of(row0 + step_of(*ids) * slab, SUBLANES), 0))
    out_spec = pl.BlockSpec(
        (per_slab, gate_tn, k),
        lambda *ids: (step_of(*ids) % slabs_per_branch, step_of(*ids) // slabs_per_branch, 0))
    out_shape = jax.ShapeDtypeStruct((k // gate_tn, n_branches * gate_tn, k), BF16)
    return in_spec, out_spec, out_shape, _nbytes((slab, k), w_in_t.dtype) + _nbytes((slab, k), BF16)


def _stage_gate_slab(src_ref, dst_ref):
    rows = dst_ref.shape[1]
    for t in range(dst_ref.shape[0]):
        dst_ref[t] = src_ref[0, t * rows:(t + 1) * rows, :].astype(dst_ref.dtype)


def _merge_kernel(x_ref, o0, o1, o2, o3, ga_ref, gb_ref, w_ref, wo_ref, y_ref, wos_ref):
    wos_ref[...] = wo_ref[...].astype(wos_ref.dtype)
    x = x_ref[...]
    tn = y_ref.shape[1]
    half = N_BRANCH // 2
    acc = None
    for b, o_ref in enumerate((o0, o1, o2, o3)):
        g_ref, bl = (ga_ref, b) if b < half else (gb_ref, b - half)
        gate = lax.dot_general(x, g_ref[bl * tn:(bl + 1) * tn, :], NT_DIMS, preferred_element_type=F32)
        proj = jnp.dot(o_ref[...], w_ref[b].astype(BF16), preferred_element_type=F32)
        term = jax.nn.sigmoid(gate) * proj
        acc = term if acc is None else acc + term
    y_ref[...] = acc.astype(y_ref.dtype)


def _merge(xb, branches, w_mg_a, w_mg_b, w_branch, w_out, layer, tm, tn):
    m, d = xb.shape
    bw = branches[0].shape[1]
    nj = d // tn
    n_steps = (m // tm) * nj
    sr = d // n_steps
    assert d % n_steps == 0 and sr % (2 * SUBLANES) == 0
    in_specs = [pl.BlockSpec((tm, d), lambda i, j: (i, 0), pipeline_mode=pl.Buffered(1))]
    in_specs += [pl.BlockSpec((tm, bw), lambda i, j: (i, 0), pipeline_mode=pl.Buffered(1))
                 for _ in range(N_BRANCH)]
    in_specs += [pl.BlockSpec((None, N_BRANCH // 2 * tn, d), lambda i, j: (j, 0, 0)),
                 pl.BlockSpec((None, N_BRANCH // 2 * tn, d), lambda i, j: (j, 0, 0)),
                 pl.BlockSpec((None, N_BRANCH, bw, tn), lambda i, j: (layer, 0, 0, j)),
                 pl.BlockSpec((None, sr, d), lambda i, j: (layer, i * nj + j, 0))]
    resident = _nbytes((tm, d), BF16) + N_BRANCH * _nbytes((tm, bw), BF16)
    streamed = (N_BRANCH * (_nbytes((tn, d), BF16) + _nbytes((bw, tn), w_branch.dtype))
                + _nbytes((tm, tn), BF16) + _nbytes((sr, d), w_out.dtype) + _nbytes((sr, d), BF16))
    return pl.pallas_call(
        _merge_kernel,
        out_shape=[jax.ShapeDtypeStruct((m, d), BF16), jax.ShapeDtypeStruct((d, d), BF16)],
        grid=(m // tm, nj),
        in_specs=in_specs,
        out_specs=[pl.BlockSpec((tm, tn), lambda i, j: (i, j)),
                   pl.BlockSpec((sr, d), lambda i, j: (i * nj + j, 0))],
        compiler_params=pltpu.CompilerParams(
            dimension_semantics=("parallel", "arbitrary"),
            vmem_limit_bytes=_vmem_limit(streamed, resident)),
        name="merge",
    )(xb, *branches, w_mg_a, w_mg_b, w_branch, w_out)


def _ln_kernel(z_ref, g_ref, b_ref, o_ref, ob_ref=None):
    z = z_ref[...]
    mu = jnp.mean(z, axis=-1, keepdims=True)
    zc = z - mu
    var = jnp.mean(zc * zc, axis=-1, keepdims=True)
    y = zc * lax.rsqrt(var + LN_EPS) * g_ref[...] + b_ref[...]
    o_ref[...] = y
    if ob_ref is not None:
        ob_ref[...] = y.astype(BF16)


def _layer_norm(z, g, b, tm, emit_bf16):
    m, d = z.shape
    out_shape = [jax.ShapeDtypeStruct((m, d), F32)]
    out_specs = [pl.BlockSpec((tm, d), lambda i: (i, 0))]
    if emit_bf16:
        out_shape.append(jax.ShapeDtypeStruct((m, d), BF16))
        out_specs.append(pl.BlockSpec((tm, d), lambda i: (i, 0)))
    blocks = 2 * _nbytes((tm, d), F32) + _nbytes((tm, d), BF16)
    res = pl.pallas_call(
        _ln_kernel,
        out_shape=out_shape,
        grid=(m // tm,),
        in_specs=[pl.BlockSpec((tm, d), lambda i: (i, 0)),
                  pl.BlockSpec((1, d), lambda i: (0, 0)),
                  pl.BlockSpec((1, d), lambda i: (0, 0))],
        out_specs=out_specs,
        compiler_params=pltpu.CompilerParams(
            dimension_semantics=("parallel",), vmem_limit_bytes=_vmem_limit(blocks)),
        name="layer_norm",
    )(z, g.reshape(1, d), b.reshape(1, d))
    return res if emit_bf16 else (res[0], None)


POOL_HIST = max(POOL_WINDOWS)
POOL_PAD = 2 * POOL_HIST


def _pool_kernel(u_ref, g_ref, wp_ref, sc_ref, o_ref, buf, lvl_a, lvl_b, *, ts, group):
    i = pl.program_id(1)
    ext = POOL_PAD + ts

    @pl.when(i == 0)
    def _():
        buf[0:POOL_PAD, :] = jnp.zeros((POOL_PAD, buf.shape[1]), F32)

    buf[POOL_PAD:ext, :] = u_ref[...]
    src = buf
    level_refs = []
    for lvl, dst in enumerate((lvl_a, lvl_b, lvl_a, lvl_b)[:len(POOL_WINDOWS)], start=1):
        lag = 2 ** (lvl - 1)
        r0 = SUBLANES * lvl
        c0 = (lvl - 1) * group
        dst[r0:ext, c0:] = src[r0:ext, c0:] + src[r0 - lag:ext - lag, c0:]
        level_refs.append(dst)
        src = dst

    t = i * ts + lax.broadcasted_iota(jnp.int32, (ts, 1), 0)
    outs = []
    for gi, w in enumerate(POOL_WINDOWS):
        assert w == 2 ** (gi + 1)
        cols = slice(gi * group, (gi + 1) * group)
        acc = level_refs[gi][POOL_PAD:ext, cols]
        cnt = jnp.minimum(t + 1, w).astype(F32)
        pooled = acc / cnt - buf[POOL_PAD:ext, cols]
        outs.append(jnp.dot(pooled.astype(BF16), wp_ref[gi], preferred_element_type=F32))
    o = jnp.concatenate(outs, axis=1) * sc_ref[...] * _silu(g_ref[...])
    o_ref[...] = o.astype(o_ref.dtype)
    buf[POOL_PAD - POOL_HIST:POOL_PAD, :] = buf[ext - POOL_HIST:ext, :]


def _pool(h, w_pool_b, pool_scale, bsz, seq, col_u, col_g, ts):
    bw = pool_scale.shape[-1]
    group = bw // len(POOL_WINDOWS)
    nblk = seq // ts
    blocks = 2 * _nbytes((ts, bw), F32) + _nbytes((ts, bw), BF16)
    scratch = [pltpu.VMEM((ts + POOL_PAD, bw), F32) for _ in range(3)]
    resident = sum(_nbytes(sc.shape, sc.dtype) for sc in scratch)
    return pl.pallas_call(
        functools.partial(_pool_kernel, ts=ts, group=group),
        out_shape=jax.ShapeDtypeStruct((bsz * seq, bw), BF16),
        grid=(bsz, nblk),
        in_specs=[pl.BlockSpec((ts, bw), lambda b, i: (b * nblk + i, col_u // bw)),
                  pl.BlockSpec((ts, bw), lambda b, i: (b * nblk + i, col_g // bw)),
                  pl.BlockSpec(w_pool_b.shape, lambda b, i: (0, 0, 0)),
                  pl.BlockSpec((1, bw), lambda b, i: (0, 0))],
        out_specs=pl.BlockSpec((ts, bw), lambda b, i: (b * nblk + i, 0)),
        scratch_shapes=scratch,
        compiler_params=pltpu.CompilerParams(
            dimension_semantics=("parallel", "arbitrary"),
            vmem_limit_bytes=_vmem_limit(blocks, resident)),
        name="pool_mixer",
    )(h, h, w_pool_b, pool_scale.reshape(1, bw))


def _split_bf16(x, parts):
    out = []
    r = x
    for _ in range(parts):
        p = r.astype(BF16)
        out.append(p)
        r = r - p.astype(F32)
    return out


LOG2E = math.log2(math.e)


def _gla_cumsum_operator():
    r = np.arange(V7X_MXU_K)
    tri = ((r[:, None] // GLA_CHUNK == r[None, :] // GLA_CHUNK) & (r[None, :] <= r[:, None]))
    return jnp.asarray(tri, BF16)


def _gla_kernel(q_ref, k_ref, v_ref, gg_ref, lr_ref, wup_ref, bg_ref, nrm_ref, tri_ref, wg_ref,
                o_ref, wgs_ref, state_ref, b_ref, *, rows, dk, dv):
    _stage_gate_slab(wg_ref, wgs_ref)
    i = pl.program_id(1)
    C = GLA_CHUNK
    nsub = C // GLA_SUB

    @pl.when(i == 0)
    def _():
        state_ref[...] = jnp.zeros(state_ref.shape, F32)

    lr_h, lr_m = _split_bf16(lr_ref[...], 2)
    w_h, w_m = _split_bf16(wup_ref[...], 2)
    z = (jnp.dot(lr_h, w_h, preferred_element_type=F32)
         + jnp.dot(lr_h, w_m, preferred_element_type=F32)
         + jnp.dot(lr_m, w_h, preferred_element_type=F32)) + bg_ref[...]
    log2_a = (jnp.minimum(z, 0.0) - jnp.log(1.0 + jnp.exp(-jnp.abs(z)))) * (LOG2E / GLA_TAU)

    tb = tri_ref.shape[0]
    parts = _split_bf16(log2_a, 3)
    for r in range(rows // tb):
        acc = None
        for part in parts:
            t = jnp.dot(tri_ref[...], part[r * tb:(r + 1) * tb, :], preferred_element_type=F32)
            acc = t if acc is None else acc + t
        b_ref[r * tb:(r + 1) * tb, :] = acc

    row = lax.broadcasted_iota(jnp.int32, (C, 1), 0)
    col = lax.broadcasted_iota(jnp.int32, (1, C), 1)
    row_sub = _div_pow2(row, GLA_SUB)
    col_sub = _div_pow2(col, GLA_SUB)
    lane_c = lax.broadcasted_iota(jnp.int32, (GLA_SUB, C), 1)
    scale = dk ** -0.5

    def sub_heads(x):
        return jnp.concatenate(
            [jnp.broadcast_to(x[a * GLA_SUB:a * GLA_SUB + 1, :], (GLA_SUB, dk))
             for a in range(nsub)], axis=0)

    def chunk_body(c, carry, *, bounded):
        r0 = pl.multiple_of(c * C, C)
        for h in range(GLA_HEADS):
            kc = slice(h * dk, (h + 1) * dk)
            vc = slice(h * dv, (h + 1) * dv)
            qs = q_ref[pl.ds(r0, C), kc] * scale
            k = k_ref[pl.ds(r0, C), kc]
            v = v_ref[pl.ds(r0, C), vc].astype(BF16)
            b = b_ref[pl.ds(r0, C), kc]
            b_last = b[C - 1:C, :]
            st = state_ref[h]

            q_in = (qs * jnp.exp2(b)).astype(BF16)
            o = lax.dot_general(q_in, st.astype(BF16), NT_DIMS, preferred_element_type=F32)

            q_t = qs * jnp.exp2(b - sub_heads(b))
            lhs = jnp.concatenate(
                [jnp.where(row_sub == a, q_t, 0.0) for a in range(nsub)], axis=1).astype(BF16)
            key_exp = [b[a * GLA_SUB:a * GLA_SUB + 1, :] - b for a in range(nsub)]
            if not bounded:
                key_exp = [jnp.minimum(e, 0.0) for e in key_exp]
            rhs = jnp.concatenate([k * jnp.exp2(e) for e in key_exp], axis=1).astype(BF16)
            a_fac = lax.dot_general(lhs, rhs, NT_DIMS, preferred_element_type=F32)

            if bounded:
                attn = jnp.where(row >= col, a_fac, 0.0)
            else:
                diag_blocks = []
                for a in range(nsub):
                    rs = slice(a * GLA_SUB, (a + 1) * GLA_SUB)
                    qa = qs[rs, :]
                    ba = b[rs, :]
                    blk = jnp.zeros((GLA_SUB, C), F32)
                    for jj in range(GLA_SUB):
                        j = a * GLA_SUB + jj
                        tj = qa * k[j:j + 1, :] * jnp.exp2(ba - b[j:j + 1, :])
                        cj = jnp.sum(tj, axis=1, keepdims=True)
                        blk = jnp.where(lane_c == j, cj, blk)
                    diag_blocks.append(blk)
                a_diag = jnp.concatenate(diag_blocks, axis=0)
                attn = jnp.where(row_sub > col_sub, a_fac,
                                 jnp.where((row_sub == col_sub) & (row >= col), a_diag, 0.0))
            o = o + jnp.dot(attn.astype(BF16), v, preferred_element_type=F32)

            k_dec = (k * jnp.exp2(b_last - b)).astype(BF16)
            upd = lax.dot_general(v, k_dec, (((0,), (0,)), ((), ())),
                                  preferred_element_type=F32)
            state_ref[h] = st * jnp.exp2(b_last) + upd

            ms = jnp.mean(o * o, axis=-1, keepdims=True)
            on = o * lax.rsqrt(ms + LN_EPS) * nrm_ref[:, vc]
            o_ref[pl.ds(r0, C), vc] = (on * _silu(gg_ref[pl.ds(r0, C), vc])).astype(o_ref.dtype)
        return carry

    bounded = ((-jnp.min(b_ref[...]) < GLA_SAFE_LOG2)
               & (jnp.max(jnp.abs(k_ref[...])) < GLA_SAFE_KEY))

    @pl.when(bounded)
    def _():
        lax.fori_loop(0, rows // C, functools.partial(chunk_body, bounded=True), 0, unroll=2)

    @pl.when(jnp.logical_not(bounded))
    def _():
        lax.fori_loop(0, rows // C, functools.partial(chunk_body, bounded=False), 0)


def _gla(h, w_up_pad, b_gla, gla_norm, bsz, seq, cols, rows,
         w_in_t, layer, gate_row0, gate_branches, gate_tn):
    col_q, col_k, col_v, col_g, col_lr = cols
    dkt = b_gla.shape[-1]
    dvt = gla_norm.shape[-1]
    dk, dv = dkt // GLA_HEADS, dvt // GLA_HEADS
    lrw = w_up_pad.shape[0]
    nblk = seq // rows
    tri = _gla_cumsum_operator()
    assert rows % tri.shape[0] == 0 and tri.shape[0] % GLA_CHUNK == 0
    g_in, g_out, g_shape, g_bytes = _gate_stage_plan(
        w_in_t, layer, gate_row0, gate_branches, bsz * nblk, gate_tn, lambda b, i: b * nblk + i)
    blocks = (2 * _nbytes((rows, dkt), F32) + 2 * _nbytes((rows, dvt), F32)
              + _nbytes((rows, lrw), F32) + _nbytes((rows, dvt), BF16) + g_bytes)
    resident = (_nbytes((rows, dkt), F32) + _nbytes((GLA_HEADS, dv, dk), F32)
                + _nbytes(tri.shape, BF16))
    const = lambda b, i: (0, 0)
    return pl.pallas_call(
        functools.partial(_gla_kernel, rows=rows, dk=dk, dv=dv),
        out_shape=[jax.ShapeDtypeStruct((bsz * seq, dvt), BF16), g_shape],
        grid=(bsz, nblk),
        in_specs=[pl.BlockSpec((rows, dkt), lambda b, i: (b * nblk + i, col_q // dkt)),
                  pl.BlockSpec((rows, dkt), lambda b, i: (b * nblk + i, col_k // dkt)),
                  pl.BlockSpec((rows, dvt), lambda b, i: (b * nblk + i, col_v // dvt)),
                  pl.BlockSpec((rows, dvt), lambda b, i: (b * nblk + i, col_g // dvt)),
                  pl.BlockSpec((rows, lrw), lambda b, i: (b * nblk + i, col_lr // lrw)),
                  pl.BlockSpec((lrw, dkt), const),
                  pl.BlockSpec((1, dkt), const),
                  pl.BlockSpec((1, dvt), const),
                  pl.BlockSpec(tri.shape, const),
                  g_in],
        out_specs=[pl.BlockSpec((rows, dvt), lambda b, i: (b * nblk + i, 0)), g_out],
        scratch_shapes=[pltpu.VMEM((GLA_HEADS, dv, dk), F32),
                        pltpu.VMEM((rows, dkt), F32)],
        compiler_params=pltpu.CompilerParams(
            dimension_semantics=("parallel", "arbitrary"),
            vmem_limit_bytes=_vmem_limit(blocks, resident)),
        name="gla_mixer",
    )(h, h, h, h, h, w_up_pad, b_gla.reshape(1, dkt), gla_norm.reshape(1, dvt), tri, w_in_t)


def _rope_lane_table():
    half = ROPE_DIM // 2
    inv_freq = ROPE_THETA ** (-np.arange(0, ROPE_DIM, 2, dtype=np.float32) / ROPE_DIM)
    lane = np.arange(LANES) % SWA_HEAD_DIM
    tab = np.where(lane < ROPE_DIM, inv_freq[lane % half], 0.0).astype(np.float32)
    return jnp.asarray(tab.reshape(1, LANES))


SWA_ROW_CHUNK = 32
SWA_BLOCKS_PER_STEP = 2


def _swa_kernel(sink_ref, q_ref, k_ref, v_ref, g_ref, pos_ref, inv_ref, wg_ref, o_ref, wgs_ref,
                *scratch, n_pairs, n_sub):
    _stage_gate_slab(wg_ref, wgs_ref)
    step = pl.program_id(1)
    for sb in range(n_sub):
        rows = pl.ds(sb * SWA_BLOCK, SWA_BLOCK)
        _swa_block(step * n_sub + sb, sink_ref, q_ref.at[rows], k_ref.at[rows], v_ref.at[rows],
                   g_ref.at[rows], pos_ref.at[rows], inv_ref, o_ref.at[rows], *scratch,
                   n_pairs=n_pairs)


def _swa_block(n, sink_ref, q_ref, k_ref, v_ref, g_ref, pos_ref, inv_ref, o_ref,
               wk_ref, wv_ref, qr_ref, s_ref, p_ref, es_ref, bias_ref, *, n_pairs):
    blk = SWA_BLOCK
    half = ROPE_DIM // 2
    hd = SWA_HEAD_DIM
    ppg = n_pairs // 2
    rc_rows = SWA_ROW_CHUNK

    @pl.when(n == 0)
    def _():
        wk_ref[...] = jnp.zeros(wk_ref.shape, BF16)
        row = lax.broadcasted_iota(jnp.int32, (4 * blk, 2 * LANES), 0)
        col = lax.broadcasted_iota(jnp.int32, (4 * blk, 2 * LANES), 1)
        ones = ((col >= LANES) & ((col >= LANES + hd) == (row >= 2 * blk))).astype(BF16)
        for g in range(2):
            wv_ref[g] = ones

    lane = lax.broadcasted_iota(jnp.int32, (1, LANES), 1)
    lane_h = _mod_pow2(lane, hd)
    lo = lane < hd
    ang = pos_ref[...] * inv_ref[...]
    cos = jnp.cos(ang)
    sin = jnp.sin(ang)

    def rope(x):
        x_up = pltpu.roll(x, LANES - half, axis=1)
        x_dn = pltpu.roll(x, half, axis=1)
        return x * cos + jnp.where(lane_h < half, -x_up, x_dn) * sin

    for g in range(2):
        for part in range(2):
            base = part * 2 * blk
            wk_ref[g, base:base + blk, :] = wk_ref[g, base + blk:base + 2 * blk, :]
            wv_ref[g, base:base + blk, 0:LANES] = wv_ref[g, base + blk:base + 2 * blk, 0:LANES]
    k_r = rope(k_ref[...])
    v_c = v_ref[...]
    k_sw = pltpu.roll(k_r, hd, axis=1)
    v_sw = pltpu.roll(v_c, hd, axis=1)
    zero = jnp.zeros_like(k_r)
    for g in range(2):
        k_lo, k_hi = (k_r, k_sw) if g == 0 else (k_sw, k_r)
        v_lo, v_hi = (v_c, v_sw) if g == 0 else (v_sw, v_c)
        wk_ref[g, blk:2 * blk, :] = jnp.where(lo, k_lo, zero).astype(BF16)
        wk_ref[g, 3 * blk:4 * blk, :] = jnp.where(lo, zero, k_hi).astype(BF16)
        wv_ref[g, blk:2 * blk, 0:LANES] = jnp.where(lo, v_lo, zero).astype(BF16)
        wv_ref[g, 3 * blk:4 * blk, 0:LANES] = jnp.where(lo, zero, v_hi).astype(BF16)

    r = lax.broadcasted_iota(jnp.int32, (blk, 2 * blk), 0)
    c = lax.broadcasted_iota(jnp.int32, (blk, 2 * blk), 1)
    valid = (c > r) & (c <= r + WINDOW) & ((n > 0) | (c >= blk))
    bias_ref[...] = jnp.where(valid, 0.0, -jnp.inf).astype(F32)

    qscale = hd ** -0.5
    for p in range(n_pairs):
        g, pp = divmod(p, ppg)
        qr_ref[g, pp * blk:(pp + 1) * blk, :] = (
            rope(q_ref[:, p * LANES:(p + 1) * LANES]) * qscale).astype(BF16)

    for g in range(2):
        s_ref[g] = lax.dot_general(qr_ref[g], wk_ref[g], NT_DIMS, preferred_element_type=F32)

    for g in range(2):
        for pp in range(ppg):
            for hh in range(2):
                sink = sink_ref[2 * (g * ppg + pp) + hh]
                kc = slice(hh * 2 * blk, (hh + 1) * 2 * blk)
                for rc in range(blk // rc_rows):
                    rows = slice(pp * blk + rc * rc_rows, pp * blk + (rc + 1) * rc_rows)
                    sh = s_ref[g, rows, kc] + bias_ref[rc * rc_rows:(rc + 1) * rc_rows, :]
                    m = jnp.maximum(jnp.max(sh, axis=-1, keepdims=True), sink)
                    p_ref[g, rows, kc] = jnp.exp(sh - m).astype(BF16)
                    es_ref[g, rows, hh * hd:(hh + 1) * hd] = jnp.broadcast_to(
                        jnp.exp(sink - m), (rc_rows, hd))

    for g in range(2):
        o2 = jnp.dot(p_ref[g], wv_ref[g], preferred_element_type=F32)
        for pp in range(ppg):
            rows = slice(pp * blk, (pp + 1) * blk)
            cols = slice((g * ppg + pp) * LANES, (g * ppg + pp + 1) * LANES)
            den = o2[rows, LANES:] + es_ref[g, rows, :]
            o_ref[:, cols] = (o2[rows, :LANES] / den * _silu(g_ref[:, cols])).astype(o_ref.dtype)


def _swa(h, pos_f, sinks, bsz, seq, cols, w_in_t, layer, gate_row0, gate_branches, gate_tn):
    col_q, col_k, col_v, col_g = cols
    n_heads = sinks.shape[0]
    bw = n_heads * SWA_HEAD_DIM
    n_pairs = bw // LANES
    assert LANES == 2 * SWA_HEAD_DIM and n_pairs % 2 == 0
    n_sub = SWA_BLOCKS_PER_STEP
    blk = n_sub * SWA_BLOCK
    nblk = seq // blk
    scratch = [pltpu.VMEM((2, 4 * SWA_BLOCK, LANES), BF16),
               pltpu.VMEM((2, 4 * SWA_BLOCK, 2 * LANES), BF16),
               pltpu.VMEM((2, 4 * SWA_BLOCK, LANES), BF16),
               pltpu.VMEM((2, 4 * SWA_BLOCK, 4 * SWA_BLOCK), F32),
               pltpu.VMEM((2, 4 * SWA_BLOCK, 4 * SWA_BLOCK), BF16),
               pltpu.VMEM((2, 4 * SWA_BLOCK, LANES), F32),
               pltpu.VMEM((SWA_BLOCK, 2 * SWA_BLOCK), F32)]
    g_in, g_out, g_shape, g_bytes = _gate_stage_plan(
        w_in_t, layer, gate_row0, gate_branches, bsz * nblk, gate_tn, lambda b, i: b * nblk + i)
    blocks = (2 * _nbytes((blk, bw), F32) + 3 * _nbytes((blk, LANES), F32) + _nbytes((blk, bw), BF16)
              + g_bytes)
    resident = sum(_nbytes(sc.shape, sc.dtype) for sc in scratch)

    return pl.pallas_call(
        functools.partial(_swa_kernel, n_pairs=n_pairs, n_sub=n_sub),
        out_shape=[jax.ShapeDtypeStruct((bsz * seq, bw), BF16), g_shape],
        grid=(bsz, nblk),
        in_specs=[pl.BlockSpec(memory_space=pltpu.SMEM),
                  pl.BlockSpec((blk, bw), lambda b, i: (b * nblk + i, col_q // bw)),
                  pl.BlockSpec((blk, LANES), lambda b, i: (b * nblk + i, col_k // LANES)),
                  pl.BlockSpec((blk, LANES), lambda b, i: (b * nblk + i, col_v // LANES)),
                  pl.BlockSpec((blk, bw), lambda b, i: (b * nblk + i, col_g // bw)),
                  pl.BlockSpec((blk, 1), lambda b, i: (b * nblk + i, 0)),
                  pl.BlockSpec((1, LANES), lambda b, i: (0, 0)),
                  g_in],
        out_specs=[pl.BlockSpec((blk, bw), lambda b, i: (b * nblk + i, 0)), g_out],
        scratch_shapes=scratch,
        compiler_params=pltpu.CompilerParams(
            dimension_semantics=("parallel", "arbitrary"),
            vmem_limit_bytes=_vmem_limit(blocks, resident)),
        name="swa_mixer",
    )(sinks, h, h, h, h, pos_f, _rope_lane_table(), w_in_t)


def _mem_kernel(q_ref, g_ref, mk_ref, mv_ref, o_ref, *, hd):
    scale = hd ** -0.5
    for h in range(XA_HEADS):
        cols = slice(h * hd, (h + 1) * hd)
        qh = q_ref[:, cols].astype(BF16)
        s = lax.dot_general(qh, mk_ref[:, cols], NT_DIMS, preferred_element_type=F32) * scale
        m = jnp.max(s, axis=-1, keepdims=True)
        e = jnp.exp(s - m)
        p = e / jnp.sum(e, axis=-1, keepdims=True)
        o = jnp.dot(p.astype(BF16), mv_ref[:, cols], preferred_element_type=F32)
        o_ref[:, cols] = (o * _silu(g_ref[:, cols])).astype(o_ref.dtype)


def _mem_attn(h, mkv, bsz, seq, mem_len, cols, ts):
    col_q, col_g = cols
    bw = mkv.shape[1] // 2
    hd = bw // XA_HEADS
    nblk = seq // ts
    blocks = 2 * _nbytes((ts, bw), F32) + 2 * _nbytes((mem_len, bw), BF16) + _nbytes((ts, bw), BF16)
    return pl.pallas_call(
        functools.partial(_mem_kernel, hd=hd),
        out_shape=jax.ShapeDtypeStruct((bsz * seq, bw), BF16),
        grid=(bsz, nblk),
        in_specs=[pl.BlockSpec((ts, bw), lambda b, i: (b * nblk + i, col_q // bw)),
                  pl.BlockSpec((ts, bw), lambda b, i: (b * nblk + i, col_g // bw)),
                  pl.BlockSpec((mem_len, bw), lambda b, i: (b, 0)),
                  pl.BlockSpec((mem_len, bw), lambda b, i: (b, 1))],
        out_specs=pl.BlockSpec((ts, bw), lambda b, i: (b * nblk + i, 0)),
        compiler_params=pltpu.CompilerParams(
            dimension_semantics=("parallel", "parallel"), vmem_limit_bytes=_vmem_limit(blocks)),
        name="mem_attn",
    )(h, h, mkv, mkv)


IN_TN = 2 * LANES
MERGE_TN = 2 * LANES


def _main_layout(d_model):
    bw = d_model // 4
    dkt, dvt = bw // 2, bw
    kvw = 2 * SWA_HEAD_DIM
    names = ["pool_u", "pool_g", "gq", "gk", "gv", "gg", "glr", "sq", "sk", "sv", "sg", "xq", "xg"]
    widths = [bw, bw, dkt, dkt, dvt, dvt, GLA_RANK, bw, kvw, kvw, bw, bw, bw]
    src = {}
    off = 0
    for nme, w in zip(names, widths):
        src[nme] = (off, w)
        off += w
    n_main = off
    order = ["pool_u", "pool_g", "gv", "gg", "sq", "sg", "xq", "xg", "gq", "gk", "sk", "sv", "glr"]
    dst = {}
    cols = []
    off = 0
    for nme in order:
        o, w = src[nme]
        w_dst = w if nme != "glr" else IN_TN
        assert off % w_dst == 0
        dst[nme] = off
        cols.append(o + np.arange(w_dst))
        off += w_dst
    cols = np.concatenate(cols)
    assert off % IN_TN == 0 and cols.max() < n_main
    starts = cols[::IN_TN]
    assert np.all(cols.reshape(-1, IN_TN) == starts[:, None] + np.arange(IN_TN))
    return starts, dst, n_main


def _hybrid_layer(l, x, xb, mem_b, pos_f, w_in_t, w_pool, pool_scale, w_gla_up, b_gla, gla_norm,
                  sinks, w_mem_kv, w_branch, w_out, ln_g, ln_b, alpha, emit_bf16, bsz, seq):
    m, d = x.shape
    col_starts, dst, n_main = _main_layout(d)
    w_up_pad = jnp.concatenate(
        [w_gla_up[l], jnp.zeros((IN_TN - GLA_RANK, w_gla_up.shape[-1]), w_gla_up.dtype)], axis=0)

    h = _in_proj(xb, w_in_t, l, col_starts, tm=2048, tn=IN_TN, per_step=2)
    mkv = _matmul(mem_b, w_mem_kv, l, BF16, tm=mem_b.shape[0], tn=512, name="mem_kv")

    o_pool = _pool(h, w_pool[l].astype(BF16), pool_scale[l], bsz, seq,
                   dst["pool_u"], dst["pool_g"], ts=512)
    half = N_BRANCH // 2
    o_gla, w_mg_a = _gla(h, w_up_pad, b_gla[l], gla_norm[l], bsz, seq,
                         (dst["gq"], dst["gk"], dst["gv"], dst["gg"], dst["glr"]), 512,
                         w_in_t, l, n_main, half, MERGE_TN)
    o_swa, w_mg_b = _swa(h, pos_f, sinks[l], bsz, seq, (dst["sq"], dst["sk"], dst["sv"], dst["sg"]),
                         w_in_t, l, n_main + half * d, half, MERGE_TN)
    o_mem = _mem_attn(h, mkv, bsz, seq, mem_b.shape[0] // bsz, (dst["xq"], dst["xg"]), ts=512)

    y, w_out_b = _merge(xb, (o_pool, o_gla, o_swa, o_mem), w_mg_a, w_mg_b, w_branch, w_out, l,
                        tm=1024, tn=MERGE_TN)
    z = _out_proj(y, w_out_b, x, alpha, tm=1024, tn=256, per_step=4)
    return _layer_norm(z, ln_g[l], ln_b[l], tm=512, emit_bf16=emit_bf16)


def kernel(x, mem, positions, w_in, w_pool, pool_scale, w_gla_up, b_gla, gla_norm, sinks,
           w_mem_kv, w_branch, w_out, ln_g, ln_b):
    bsz, seq, d = x.shape
    depth = w_in.shape[0]
    alpha = (2 * depth) ** 0.25
    xf = x.reshape(bsz * seq, d)
    xb = xf.astype(BF16)
    mem_b = mem.reshape(bsz * mem.shape[1], d).astype(BF16)
    pos_f = positions.astype(F32).reshape(bsz * seq, 1)
    w_in_t = jnp.swapaxes(w_in, 1, 2)
    for l in range(depth):
        xf, xb = _hybrid_layer(
            l, xf, xb, mem_b, pos_f, w_in_t, w_pool, pool_scale, w_gla_up, b_gla,
            gla_norm, sinks, w_mem_kv, w_branch, w_out, ln_g, ln_b,
            alpha, l + 1 < depth, bsz, seq)
    return xf.reshape(bsz, seq, d)
```

```python
import functools
import math

import jax
import jax.numpy as jnp
import numpy as np
from jax import lax
from jax.experimental import pallas as pl
from jax.experimental.pallas import tpu as pltpu

F32 = jnp.float32
BF16 = jnp.bfloat16

N_BRANCH = 4
POOL_WINDOWS = (2, 4, 8, 16)
GLA_HEADS = 4
GLA_RANK = 16
GLA_TAU = 16.0
GLA_CHUNK = 64
GLA_SUB = 16
GLA_SAFE_LOG2 = 24.0
GLA_SAFE_KEY = 2.0 ** 100
SWA_HEAD_DIM = 64
WINDOW = 128
SWA_BLOCK = 128
ROPE_THETA = 500000.0
ROPE_DIM = SWA_HEAD_DIM // 4
XA_HEADS = 4
LN_EPS = 1e-5

LANES = 128
SUBLANES = 8
V7X_MXU_K = 256
V7X_VMEM_BYTES = 64 * 1024 * 1024
VMEM_CAP = V7X_VMEM_BYTES - 8 * 1024 * 1024
VMEM_TEMP_BYTES = 12 * 1024 * 1024


def _vmem_limit(streamed_bytes, resident_bytes=0):
    return int(min(VMEM_CAP, 2 * streamed_bytes + resident_bytes + VMEM_TEMP_BYTES))


def _nbytes(shape, dtype):
    return int(np.prod(shape)) * jnp.dtype(dtype).itemsize


def _silu(g):
    return g * jax.nn.sigmoid(g)


def _div_pow2(x, n):
    assert n & (n - 1) == 0
    return x >> (n.bit_length() - 1)


def _mod_pow2(x, n):
    assert n & (n - 1) == 0
    return x & (n - 1)


NT_DIMS = (((1,), (1,)), ((), ()))


def _mm_kernel(a_ref, w_ref, o_ref):
    w = w_ref[...].astype(BF16)
    o_ref[...] = jnp.dot(a_ref[...], w, preferred_element_type=F32).astype(o_ref.dtype)


def _matmul(a, w, layer, out_dtype, tm, tn, name):
    m, k = a.shape
    n = w.shape[-1]
    assert m % tm == 0 and n % tn == 0
    resident = _nbytes((tm, k), a.dtype)
    streamed = _nbytes((k, tn), w.dtype) + _nbytes((tm, tn), out_dtype)
    return pl.pallas_call(
        _mm_kernel,
        out_shape=jax.ShapeDtypeStruct((m, n), out_dtype),
        grid=(m // tm, n // tn),
        in_specs=[pl.BlockSpec((tm, k), lambda i, j: (i, 0), pipeline_mode=pl.Buffered(1)),
                  pl.BlockSpec((None, k, tn), lambda i, j: (layer, 0, j))],
        out_specs=pl.BlockSpec((tm, tn), lambda i, j: (i, j)),
        compiler_params=pltpu.CompilerParams(
            dimension_semantics=("parallel", "arbitrary"),
            vmem_limit_bytes=_vmem_limit(streamed, resident)),
        name=name,
    )(a, w)


def _out_proj_kernel(y_ref, w_ref, x_ref, o_ref, *, alpha, tn):
    y = y_ref[...]
    for t in range(o_ref.shape[1] // tn):
        cols = slice(t * tn, (t + 1) * tn)
        o_ref[:, cols] = alpha * x_ref[:, cols] + jnp.dot(
            y, w_ref[:, cols], preferred_element_type=F32)


def _out_proj(y, w_out_b, x, alpha, tm, tn, per_step):
    m, k = y.shape
    n = w_out_b.shape[1]
    tb = per_step * tn
    assert m % tm == 0 and n % tb == 0
    streamed = (_nbytes((tm, k), y.dtype) + _nbytes((k, tb), w_out_b.dtype)
                + 2 * _nbytes((tm, tb), F32))
    return pl.pallas_call(
        functools.partial(_out_proj_kernel, alpha=alpha, tn=tn),
        out_shape=jax.ShapeDtypeStruct((m, n), F32),
        grid=(m // tm, n // tb),
        in_specs=[pl.BlockSpec((tm, k), lambda i, j: (i, 0)),
                  pl.BlockSpec((k, tb), lambda i, j: (0, j)),
                  pl.BlockSpec((tm, tb), lambda i, j: (i, j))],
        out_specs=pl.BlockSpec((tm, tb), lambda i, j: (i, j)),
        compiler_params=pltpu.CompilerParams(
            dimension_semantics=("parallel", "arbitrary"),
            vmem_limit_bytes=_vmem_limit(streamed)),
        name="out_proj",
    )(y, w_out_b, x)


def _in_proj_kernel(tab_ref, x_ref, *refs, per_step):
    w_refs, wo_ref, o_ref, wos_ref = refs[:per_step], refs[per_step], refs[-2], refs[-1]
    wos_ref[...] = wo_ref[...].astype(wos_ref.dtype)
    x = x_ref[...]
    tn = w_refs[0].shape[1]
    for t, w_ref in enumerate(w_refs):
        w = w_ref[0].astype(BF16)
        o_ref[:, t * tn:(t + 1) * tn] = lax.dot_general(x, w, NT_DIMS, preferred_element_type=F32)


def _in_proj(xb, w_in_t, w_out, layer, col_starts, tm, tn, per_step):
    m, k = xb.shape
    nblk = len(col_starts)
    assert m % tm == 0 and nblk % per_step == 0 and all(int(s) % SUBLANES == 0 for s in col_starts)
    nj = nblk // per_step
    d_out, n_out = w_out.shape[1:]
    sr = 4 * 2 * SUBLANES
    n_slabs = d_out // sr
    assert d_out % sr == 0 and n_slabs <= (m // tm) * nj
    resident = _nbytes((tm, k), xb.dtype)
    streamed = (per_step * (_nbytes((tn, k), w_in_t.dtype) + _nbytes((tm, tn), F32))
                + _nbytes((sr, n_out), w_out.dtype) + _nbytes((sr, n_out), BF16))

    def window(t):
        return pl.BlockSpec(
            (pl.Element(1), pl.Element(tn), pl.Element(k)),
            lambda i, j, tab: (layer, pl.multiple_of(tab[j * per_step + t], SUBLANES), 0))

    def slab(i, j):
        return jnp.minimum(i * nj + j, n_slabs - 1)

    return pl.pallas_call(
        functools.partial(_in_proj_kernel, per_step=per_step),
        out_shape=[jax.ShapeDtypeStruct((m, nblk * tn), F32),
                   jax.ShapeDtypeStruct((d_out, n_out), BF16)],
        grid_spec=pltpu.PrefetchScalarGridSpec(
            num_scalar_prefetch=1,
            grid=(m // tm, nj),
            in_specs=[pl.BlockSpec((tm, k), lambda i, j, tab: (i, 0), pipeline_mode=pl.Buffered(1))]
            + [window(t) for t in range(per_step)]
            + [pl.BlockSpec((None, sr, n_out), lambda i, j, tab: (layer, slab(i, j), 0))],
            out_specs=[pl.BlockSpec((tm, per_step * tn), lambda i, j, tab: (i, j)),
                       pl.BlockSpec((sr, n_out), lambda i, j, tab: (slab(i, j), 0))]),
        compiler_params=pltpu.CompilerParams(
            dimension_semantics=("arbitrary", "arbitrary"),
            vmem_limit_bytes=_vmem_limit(streamed, resident)),
        name="in_proj",
    )(jnp.asarray(np.asarray(col_starts, np.int32)), xb, *([w_in_t] * per_step), w_out)


def _gate_stage_plan(w_in_t, layer, row0, n_branches, n_steps, gate_tn, step_of):
    k = w_in_t.shape[-1]
    rows = n_branches * k
    slab = rows // n_steps
    per_slab = slab // gate_tn
    slabs_per_branch = k // slab
    assert rows % n_steps == 0 and slab % gate_tn == 0 and k % slab == 0 and row0 % SUBLANES == 0
    in_spec = pl.BlockSpec(
        (pl.Element(1), pl.Element(slab), pl.Element(k)),
        lambda *ids: (layer, pl.multiple_of(row0 + step_of(*ids) * slab, SUBLANES), 0))
    out_spec = pl.BlockSpec(
        (per_slab, gate_tn, k),
        lambda *ids: (step_of(*ids) % slabs_per_branch, step_of(*ids) // slabs_per_branch, 0))
    out_shape = jax.ShapeDtypeStruct((k // gate_tn, n_branches * gate_tn, k), BF16)
    return in_spec, out_spec, out_shape, _nbytes((slab, k), w_in_t.dtype) + _nbytes((slab, k), BF16)


def _stage_gate_slab(src_ref, dst_ref):
    rows = dst_ref.shape[1]
    for t in range(dst_ref.shape[0]):
        dst_ref[t] = src_ref[0, t * rows:(t + 1) * rows, :].astype(dst_ref.dtype)


def _merge_kernel(x_ref, o0, o1, o2, o3, ga_ref, gb_ref, w_ref, y_ref):
    x = x_ref[...]
    tn = y_ref.shape[1]
    half = N_BRANCH // 2
    acc = None
    for b, o_ref in enumerate((o0, o1, o2, o3)):
        g_ref, bl = (ga_ref, b) if b < half else (gb_ref, b - half)
        gate = lax.dot_general(x, g_ref[bl * tn:(bl + 1) * tn, :], NT_DIMS, preferred_element_type=F32)
        proj = jnp.dot(o_ref[...], w_ref[b].astype(BF16), preferred_element_type=F32)
        term = jax.nn.sigmoid(gate) * proj
        acc = term if acc is None else acc + term
    y_ref[...] = acc.astype(y_ref.dtype)


def _merge(xb, branches, w_mg_a, w_mg_b, w_branch, layer, tm, tn):
    m, d = xb.shape
    bw = branches[0].shape[1]
    nj = d // tn
    in_specs = [pl.BlockSpec((tm, d), lambda i, j: (i, 0), pipeline_mode=pl.Buffered(1))]
    in_specs += [pl.BlockSpec((tm, bw), lambda i, j: (i, 0), pipeline_mode=pl.Buffered(1))
                 for _ in range(N_BRANCH)]
    in_specs += [pl.BlockSpec((None, N_BRANCH // 2 * tn, d), lambda i, j: (j, 0, 0)),
                 pl.BlockSpec((None, N_BRANCH // 2 * tn, d), lambda i, j: (j, 0, 0)),
                 pl.BlockSpec((None, N_BRANCH, bw, tn), lambda i, j: (layer, 0, 0, j))]
    resident = _nbytes((tm, d), BF16) + N_BRANCH * _nbytes((tm, bw), BF16)
    streamed = (N_BRANCH * (_nbytes((tn, d), BF16) + _nbytes((bw, tn), w_branch.dtype))
                + _nbytes((tm, tn), BF16))
    return pl.pallas_call(
        _merge_kernel,
        out_shape=jax.ShapeDtypeStruct((m, d), BF16),
        grid=(m // tm, nj),
        in_specs=in_specs,
        out_specs=pl.BlockSpec((tm, tn), lambda i, j: (i, j)),
        compiler_params=pltpu.CompilerParams(
            dimension_semantics=("parallel", "arbitrary"),
            vmem_limit_bytes=_vmem_limit(streamed, resident)),
        name="merge",
    )(xb, *branches, w_mg_a, w_mg_b, w_branch)


def _ln_kernel(z_ref, g_ref, b_ref, o_ref, ob_ref=None):
    z = z_ref[...]
    mu = jnp.mean(z, axis=-1, keepdims=True)
    zc = z - mu
    var = jnp.mean(zc * zc, axis=-1, keepdims=True)
    y = zc * lax.rsqrt(var + LN_EPS) * g_ref[...] + b_ref[...]
    o_ref[...] = y
    if ob_ref is not None:
        ob_ref[...] = y.astype(BF16)


def _layer_norm(z, g, b, tm, emit_bf16):
    m, d = z.shape
    out_shape = [jax.ShapeDtypeStruct((m, d), F32)]
    out_specs = [pl.BlockSpec((tm, d), lambda i: (i, 0))]
    if emit_bf16:
        out_shape.append(jax.ShapeDtypeStruct((m, d), BF16))
        out_specs.append(pl.BlockSpec((tm, d), lambda i: (i, 0)))
    blocks = 2 * _nbytes((tm, d), F32) + _nbytes((tm, d), BF16)
    res = pl.pallas_call(
        _ln_kernel,
        out_shape=out_shape,
        grid=(m // tm,),
        in_specs=[pl.BlockSpec((tm, d), lambda i: (i, 0)),
                  pl.BlockSpec((1, d), lambda i: (0, 0)),
                  pl.BlockSpec((1, d), lambda i: (0, 0))],
        out_specs=out_specs,
        compiler_params=pltpu.CompilerParams(
            dimension_semantics=("parallel",), vmem_limit_bytes=_vmem_limit(blocks)),
        name="layer_norm",
    )(z, g.reshape(1, d), b.reshape(1, d))
    return res if emit_bf16 else (res[0], None)


POOL_HIST = max(POOL_WINDOWS)
POOL_PAD = 2 * POOL_HIST


def _pool_kernel(u_ref, g_ref, wp_ref, sc_ref, o_ref, buf, lvl_a, lvl_b, *, ts, group):
    i = pl.program_id(1)
    ext = POOL_PAD + ts

    @pl.when(i == 0)
    def _():
        buf[0:POOL_PAD, :] = jnp.zeros((POOL_PAD, buf.shape[1]), F32)

    buf[POOL_PAD:ext, :] = u_ref[...]
    src = buf
    level_refs = []
    for lvl, dst in enumerate((lvl_a, lvl_b, lvl_a, lvl_b)[:len(POOL_WINDOWS)], start=1):
        lag = 2 ** (lvl - 1)
        r0 = SUBLANES * lvl
        c0 = (lvl - 1) * group
        dst[r0:ext, c0:] = src[r0:ext, c0:] + src[r0 - lag:ext - lag, c0:]
        level_refs.append(dst)
        src = dst

    t = i * ts + lax.broadcasted_iota(jnp.int32, (ts, 1), 0)
    outs = []
    for gi, w in enumerate(POOL_WINDOWS):
        assert w == 2 ** (gi + 1)
        cols = slice(gi * group, (gi + 1) * group)
        acc = level_refs[gi][POOL_PAD:ext, cols]
        cnt = jnp.minimum(t + 1, w).astype(F32)
        pooled = acc / cnt - buf[POOL_PAD:ext, cols]
        outs.append(jnp.dot(pooled.astype(BF16), wp_ref[gi], preferred_element_type=F32))
    o = jnp.concatenate(outs, axis=1) * sc_ref[...] * _silu(g_ref[...])
    o_ref[...] = o.astype(o_ref.dtype)
    buf[POOL_PAD - POOL_HIST:POOL_PAD, :] = buf[ext - POOL_HIST:ext, :]


def _pool(h, w_pool_b, pool_scale, bsz, seq, col_u, col_g, ts):
    bw = pool_scale.shape[-1]
    group = bw // len(POOL_WINDOWS)
    nblk = seq // ts
    blocks = 2 * _nbytes((ts, bw), F32) + _nbytes((ts, bw), BF16)
    scratch = [pltpu.VMEM((ts + POOL_PAD, bw), F32) for _ in range(3)]
    resident = sum(_nbytes(sc.shape, sc.dtype) for sc in scratch)
    return pl.pallas_call(
        functools.partial(_pool_kernel, ts=ts, group=group),
        out_shape=jax.ShapeDtypeStruct((bsz * seq, bw), BF16),
        grid=(bsz, nblk),
        in_specs=[pl.BlockSpec((ts, bw), lambda b, i: (b * nblk + i, col_u // bw)),
                  pl.BlockSpec((ts, bw), lambda b, i: (b * nblk + i, col_g // bw)),
                  pl.BlockSpec(w_pool_b.shape, lambda b, i: (0, 0, 0)),
                  pl.BlockSpec((1, bw), lambda b, i: (0, 0))],
        out_specs=pl.BlockSpec((ts, bw), lambda b, i: (b * nblk + i, 0)),
        scratch_shapes=scratch,
        compiler_params=pltpu.CompilerParams(
            dimension_semantics=("parallel", "arbitrary"),
            vmem_limit_bytes=_vmem_limit(blocks, resident)),
        name="pool_mixer",
    )(h, h, w_pool_b, pool_scale.reshape(1, bw))


def _split_bf16(x, parts):
    out = []
    r = x
    for _ in range(parts):
        p = r.astype(BF16)
        out.append(p)
        r = r - p.astype(F32)
    return out


LOG2E = math.log2(math.e)


def _gla_cumsum_operator():
    r = np.arange(V7X_MXU_K)
    tri = ((r[:, None] // GLA_CHUNK == r[None, :] // GLA_CHUNK) & (r[None, :] <= r[:, None]))
    return jnp.asarray(tri, BF16)


def _gla_kernel(q_ref, k_ref, v_ref, gg_ref, lr_ref, wup_ref, bg_ref, nrm_ref, tri_ref, wg_ref,
                o_ref, wgs_ref, state_ref, b_ref, *, rows, dk, dv):
    _stage_gate_slab(wg_ref, wgs_ref)
    i = pl.program_id(1)
    C = GLA_CHUNK
    nsub = C // GLA_SUB

    @pl.when(i == 0)
    def _():
        state_ref[...] = jnp.zeros(state_ref.shape, F32)

    lr_h, lr_m = _split_bf16(lr_ref[...], 2)
    w_h, w_m = _split_bf16(wup_ref[...], 2)
    z = (jnp.dot(lr_h, w_h, preferred_element_type=F32)
         + jnp.dot(lr_h, w_m, preferred_element_type=F32)
         + jnp.dot(lr_m, w_h, preferred_element_type=F32)) + bg_ref[...]
    log2_a = (jnp.minimum(z, 0.0) - jnp.log(1.0 + jnp.exp(-jnp.abs(z)))) * (LOG2E / GLA_TAU)

    tb = tri_ref.shape[0]
    parts = _split_bf16(log2_a, 3)
    for r in range(rows // tb):
        acc = None
        for part in parts:
            t = jnp.dot(tri_ref[...], part[r * tb:(r + 1) * tb, :], preferred_element_type=F32)
            acc = t if acc is None else acc + t
        b_ref[r * tb:(r + 1) * tb, :] = acc

    row = lax.broadcasted_iota(jnp.int32, (C, 1), 0)
    col = lax.broadcasted_iota(jnp.int32, (1, C), 1)
    row_sub = _div_pow2(row, GLA_SUB)
    col_sub = _div_pow2(col, GLA_SUB)
    lane_c = lax.broadcasted_iota(jnp.int32, (GLA_SUB, C), 1)
    scale = dk ** -0.5

    def sub_heads(x):
        return jnp.concatenate(
            [jnp.broadcast_to(x[a * GLA_SUB:a * GLA_SUB + 1, :], (GLA_SUB, dk))
             for a in range(nsub)], axis=0)

    def chunk_body(c, carry, *, bounded):
        r0 = pl.multiple_of(c * C, C)
        for h in range(GLA_HEADS):
            kc = slice(h * dk, (h + 1) * dk)
            vc = slice(h * dv, (h + 1) * dv)
            qs = q_ref[pl.ds(r0, C), kc] * scale
            k = k_ref[pl.ds(r0, C), kc]
            v = v_ref[pl.ds(r0, C), vc].astype(BF16)
            b = b_ref[pl.ds(r0, C), kc]
            b_last = b[C - 1:C, :]
            st = state_ref[h]

            q_in = (qs * jnp.exp2(b)).astype(BF16)
            o = lax.dot_general(q_in, st.astype(BF16), NT_DIMS, preferred_element_type=F32)

            q_t = qs * jnp.exp2(b - sub_heads(b))
            lhs = jnp.concatenate(
                [jnp.where(row_sub == a, q_t, 0.0) for a in range(nsub)], axis=1).astype(BF16)
            key_exp = [b[a * GLA_SUB:a * GLA_SUB + 1, :] - b for a in range(nsub)]
            if not bounded:
                key_exp = [jnp.minimum(e, 0.0) for e in key_exp]
            rhs = jnp.concatenate([k * jnp.exp2(e) for e in key_exp], axis=1).astype(BF16)
            a_fac = lax.dot_general(lhs, rhs, NT_DIMS, preferred_element_type=F32)

            if bounded:
                attn = jnp.where(row >= col, a_fac, 0.0)
            else:
                diag_blocks = []
                for a in range(nsub):
                    rs = slice(a * GLA_SUB, (a + 1) * GLA_SUB)
                    qa = qs[rs, :]
                    ba = b[rs, :]
                    blk = jnp.zeros((GLA_SUB, C), F32)
                    for jj in range(GLA_SUB):
                        j = a * GLA_SUB + jj
                        tj = qa * k[j:j + 1, :] * jnp.exp2(ba - b[j:j + 1, :])
                        cj = jnp.sum(tj, axis=1, keepdims=True)
                        blk = jnp.where(lane_c == j, cj, blk)
                    diag_blocks.append(blk)
                a_diag = jnp.concatenate(diag_blocks, axis=0)
                attn = jnp.where(row_sub > col_sub, a_fac,
                                 jnp.where((row_sub == col_sub) & (row >= col), a_diag, 0.0))
            o = o + jnp.dot(attn.astype(BF16), v, preferred_element_type=F32)

            k_dec = (k * jnp.exp2(b_last - b)).astype(BF16)
            upd = lax.dot_general(v, k_dec, (((0,), (0,)), ((), ())),
                                  preferred_element_type=F32)
            state_ref[h] = st * jnp.exp2(b_last) + upd

            ms = jnp.mean(o * o, axis=-1, keepdims=True)
            on = o * lax.rsqrt(ms + LN_EPS) * nrm_ref[:, vc]
            o_ref[pl.ds(r0, C), vc] = (on * _silu(gg_ref[pl.ds(r0, C), vc])).astype(o_ref.dtype)
        return carry

    bounded = ((-jnp.min(b_ref[...]) < GLA_SAFE_LOG2)
               & (jnp.max(jnp.abs(k_ref[...])) < GLA_SAFE_KEY))

    @pl.when(bounded)
    def _():
        lax.fori_loop(0, rows // C, functools.partial(chunk_body, bounded=True), 0, unroll=2)

    @pl.when(jnp.logical_not(bounded))
    def _():
        lax.fori_loop(0, rows // C, functools.partial(chunk_body, bounded=False), 0)


def _gla(h, w_up_pad, b_gla, gla_norm, bsz, seq, cols, rows,
         w_in_t, layer, gate_row0, gate_branches, gate_tn):
    col_q, col_k, col_v, col_g, col_lr = cols
    dkt = b_gla.shape[-1]
    dvt = gla_norm.shape[-1]
    dk, dv = dkt // GLA_HEADS, dvt // GLA_HEADS
    lrw = w_up_pad.shape[0]
    nblk = seq // rows
    tri = _gla_cumsum_operator()
    assert rows % tri.shape[0] == 0 and tri.shape[0] % GLA_CHUNK == 0
    g_in, g_out, g_shape, g_bytes = _gate_stage_plan(
        w_in_t, layer, gate_row0, gate_branches, bsz * nblk, gate_tn, lambda b, i: b * nblk + i)
    blocks = (2 * _nbytes((rows, dkt), F32) + 2 * _nbytes((rows, dvt), F32)
              + _nbytes((rows, lrw), F32) + _nbytes((rows, dvt), BF16) + g_bytes)
    resident = (_nbytes((rows, dkt), F32) + _nbytes((GLA_HEADS, dv, dk), F32)
                + _nbytes(tri.shape, BF16))
    const = lambda b, i: (0, 0)
    return pl.pallas_call(
        functools.partial(_gla_kernel, rows=rows, dk=dk, dv=dv),
        out_shape=[jax.ShapeDtypeStruct((bsz * seq, dvt), BF16), g_shape],
        grid=(bsz, nblk),
        in_specs=[pl.BlockSpec((rows, dkt), lambda b, i: (b * nblk + i, col_q // dkt)),
                  pl.BlockSpec((rows, dkt), lambda b, i: (b * nblk + i, col_k // dkt)),
                  pl.BlockSpec((rows, dvt), lambda b, i: (b * nblk + i, col_v // dvt)),
                  pl.BlockSpec((rows, dvt), lambda b, i: (b * nblk + i, col_g // dvt)),
                  pl.BlockSpec((rows, lrw), lambda b, i: (b * nblk + i, col_lr // lrw)),
                  pl.BlockSpec((lrw, dkt), const),
                  pl.BlockSpec((1, dkt), const),
                  pl.BlockSpec((1, dvt), const),
                  pl.BlockSpec(tri.shape, const),
                  g_in],
        out_specs=[pl.BlockSpec((rows, dvt), lambda b, i: (b * nblk + i, 0)), g_out],
        scratch_shapes=[pltpu.VMEM((GLA_HEADS, dv, dk), F32),
                        pltpu.VMEM((rows, dkt), F32)],
        compiler_params=pltpu.CompilerParams(
            dimension_semantics=("parallel", "arbitrary"),
            vmem_limit_bytes=_vmem_limit(blocks, resident)),
        name="gla_mixer",
    )(h, h, h, h, h, w_up_pad, b_gla.reshape(1, dkt), gla_norm.reshape(1, dvt), tri, w_in_t)


def _rope_lane_table():
    half = ROPE_DIM // 2
    inv_freq = ROPE_THETA ** (-np.arange(0, ROPE_DIM, 2, dtype=np.float32) / ROPE_DIM)
    lane = np.arange(LANES) % SWA_HEAD_DIM
    tab = np.where(lane < ROPE_DIM, inv_freq[lane % half], 0.0).astype(np.float32)
    return jnp.asarray(tab.reshape(1, LANES))


SWA_ROW_CHUNK = 32
SWA_BLOCKS_PER_STEP = 2


def _swa_kernel(sink_ref, q_ref, k_ref, v_ref, g_ref, pos_ref, inv_ref, wg_ref, o_ref, wgs_ref,
                *scratch, n_pairs, n_sub):
    _stage_gate_slab(wg_ref, wgs_ref)
    step = pl.program_id(1)
    for sb in range(n_sub):
        rows = pl.ds(sb * SWA_BLOCK, SWA_BLOCK)
        _swa_block(step * n_sub + sb, sink_ref, q_ref.at[rows], k_ref.at[rows], v_ref.at[rows],
                   g_ref.at[rows], pos_ref.at[rows], inv_ref, o_ref.at[rows], *scratch,
                   n_pairs=n_pairs)


def _swa_block(n, sink_ref, q_ref, k_ref, v_ref, g_ref, pos_ref, inv_ref, o_ref,
               wk_ref, wv_ref, qr_ref, s_ref, p_ref, es_ref, bias_ref, *, n_pairs):
    blk = SWA_BLOCK
    half = ROPE_DIM // 2
    hd = SWA_HEAD_DIM
    ppg = n_pairs // 2
    rc_rows = SWA_ROW_CHUNK

    @pl.when(n == 0)
    def _():
        wk_ref[...] = jnp.zeros(wk_ref.shape, BF16)
        row = lax.broadcasted_iota(jnp.int32, (4 * blk, 2 * LANES), 0)
        col = lax.broadcasted_iota(jnp.int32, (4 * blk, 2 * LANES), 1)
        ones = ((col >= LANES) & ((col >= LANES + hd) == (row >= 2 * blk))).astype(BF16)
        for g in range(2):
            wv_ref[g] = ones

    lane = lax.broadcasted_iota(jnp.int32, (1, LANES), 1)
    lane_h = _mod_pow2(lane, hd)
    lo = lane < hd
    ang = pos_ref[...] * inv_ref[...]
    cos = jnp.cos(ang)
    sin = jnp.sin(ang)

    def rope(x):
        x_up = pltpu.roll(x, LANES - half, axis=1)
        x_dn = pltpu.roll(x, half, axis=1)
        return x * cos + jnp.where(lane_h < half, -x_up, x_dn) * sin

    for g in range(2):
        for part in range(2):
            base = part * 2 * blk
            wk_ref[g, base:base + blk, :] = wk_ref[g, base + blk:base + 2 * blk, :]
            wv_ref[g, base:base + blk, 0:LANES] = wv_ref[g, base + blk:base + 2 * blk, 0:LANES]
    k_r = rope(k_ref[...])
    v_c = v_ref[...]
    k_sw = pltpu.roll(k_r, hd, axis=1)
    v_sw = pltpu.roll(v_c, hd, axis=1)
    zero = jnp.zeros_like(k_r)
    for g in range(2):
        k_lo, k_hi = (k_r, k_sw) if g == 0 else (k_sw, k_r)
        v_lo, v_hi = (v_c, v_sw) if g == 0 else (v_sw, v_c)
        wk_ref[g, blk:2 * blk, :] = jnp.where(lo, k_lo, zero).astype(BF16)
        wk_ref[g, 3 * blk:4 * blk, :] = jnp.where(lo, zero, k_hi).astype(BF16)
        wv_ref[g, blk:2 * blk, 0:LANES] = jnp.where(lo, v_lo, zero).astype(BF16)
        wv_ref[g, 3 * blk:4 * blk, 0:LANES] = jnp.where(lo, zero, v_hi).astype(BF16)

    r = lax.broadcasted_iota(jnp.int32, (blk, 2 * blk), 0)
    c = lax.broadcasted_iota(jnp.int32, (blk, 2 * blk), 1)
    valid = (c > r) & (c <= r + WINDOW) & ((n > 0) | (c >= blk))
    bias_ref[...] = jnp.where(valid, 0.0, -jnp.inf).astype(F32)

    qscale = hd ** -0.5
    for p in range(n_pairs):
        g, pp = divmod(p, ppg)
        qr_ref[g, pp * blk:(pp + 1) * blk, :] = (
            rope(q_ref[:, p * LANES:(p + 1) * LANES]) * qscale).astype(BF16)

    for g in range(2):
        s_ref[g] = lax.dot_general(qr_ref[g], wk_ref[g], NT_DIMS, preferred_element_type=F32)

    for g in range(2):
        for pp in range(ppg):
            for hh in range(2):
                sink = sink_ref[2 * (g * ppg + pp) + hh]
                kc = slice(hh * 2 * blk, (hh + 1) * 2 * blk)
                for rc in range(blk // rc_rows):
                    rows = slice(pp * blk + rc * rc_rows, pp * blk + (rc + 1) * rc_rows)
                    sh = s_ref[g, rows, kc] + bias_ref[rc * rc_rows:(rc + 1) * rc_rows, :]
                    m = jnp.maximum(jnp.max(sh, axis=-1, keepdims=True), sink)
                    p_ref[g, rows, kc] = jnp.exp(sh - m).astype(BF16)
                    es_ref[g, rows, hh * hd:(hh + 1) * hd] = jnp.broadcast_to(
                        jnp.exp(sink - m), (rc_rows, hd))

    for g in range(2):
        o2 = jnp.dot(p_ref[g], wv_ref[g], preferred_element_type=F32)
        for pp in range(ppg):
            rows = slice(pp * blk, (pp + 1) * blk)
            cols = slice((g * ppg + pp) * LANES, (g * ppg + pp + 1) * LANES)
            den = o2[rows, LANES:] + es_ref[g, rows, :]
            o_ref[:, cols] = (o2[rows, :LANES] / den * _silu(g_ref[:, cols])).astype(o_ref.dtype)


def _swa(h, pos_f, sinks, bsz, seq, cols, w_in_t, layer, gate_row0, gate_branches, gate_tn):
    col_q, col_k, col_v, col_g = cols
    n_heads = sinks.shape[0]
    bw = n_heads * SWA_HEAD_DIM
    n_pairs = bw // LANES
    assert LANES == 2 * SWA_HEAD_DIM and n_pairs % 2 == 0
    n_sub = SWA_BLOCKS_PER_STEP
    blk = n_sub * SWA_BLOCK
    nblk = seq // blk
    scratch = [pltpu.VMEM((2, 4 * SWA_BLOCK, LANES), BF16),
               pltpu.VMEM((2, 4 * SWA_BLOCK, 2 * LANES), BF16),
               pltpu.VMEM((2, 4 * SWA_BLOCK, LANES), BF16),
               pltpu.VMEM((2, 4 * SWA_BLOCK, 4 * SWA_BLOCK), F32),
               pltpu.VMEM((2, 4 * SWA_BLOCK, 4 * SWA_BLOCK), BF16),
               pltpu.VMEM((2, 4 * SWA_BLOCK, LANES), F32),
               pltpu.VMEM((SWA_BLOCK, 2 * SWA_BLOCK), F32)]
    g_in, g_out, g_shape, g_bytes = _gate_stage_plan(
        w_in_t, layer, gate_row0, gate_branches, bsz * nblk, gate_tn, lambda b, i: b * nblk + i)
    blocks = (2 * _nbytes((blk, bw), F32) + 3 * _nbytes((blk, LANES), F32) + _nbytes((blk, bw), BF16)
              + g_bytes)
    resident = sum(_nbytes(sc.shape, sc.dtype) for sc in scratch)

    return pl.pallas_call(
        functools.partial(_swa_kernel, n_pairs=n_pairs, n_sub=n_sub),
        out_shape=[jax.ShapeDtypeStruct((bsz * seq, bw), BF16), g_shape],
        grid=(bsz, nblk),
        in_specs=[pl.BlockSpec(memory_space=pltpu.SMEM),
                  pl.BlockSpec((blk, bw), lambda b, i: (b * nblk + i, col_q // bw)),
                  pl.BlockSpec((blk, LANES), lambda b, i: (b * nblk + i, col_k // LANES)),
                  pl.BlockSpec((blk, LANES), lambda b, i: (b * nblk + i, col_v // LANES)),
                  pl.BlockSpec((blk, bw), lambda b, i: (b * nblk + i, col_g // bw)),
                  pl.BlockSpec((blk, 1), lambda b, i: (b * nblk + i, 0)),
                  pl.BlockSpec((1, LANES), lambda b, i: (0, 0)),
                  g_in],
        out_specs=[pl.BlockSpec((blk, bw), lambda b, i: (b * nblk + i, 0)), g_out],
        scratch_shapes=scratch,
        compiler_params=pltpu.CompilerParams(
            dimension_semantics=("parallel", "arbitrary"),
            vmem_limit_bytes=_vmem_limit(blocks, resident)),
        name="swa_mixer",
    )(sinks, h, h, h, h, pos_f, _rope_lane_table(), w_in_t)


def _mem_kernel(q_ref, g_ref, mk_ref, mv_ref, o_ref, *, hd):
    scale = hd ** -0.5
    for h in range(XA_HEADS):
        cols = slice(h * hd, (h + 1) * hd)
        qh = q_ref[:, cols].astype(BF16)
        s = lax.dot_general(qh, mk_ref[:, cols], NT_DIMS, preferred_element_type=F32) * scale
        m = jnp.max(s, axis=-1, keepdims=True)
        e = jnp.exp(s - m)
        p = e / jnp.sum(e, axis=-1, keepdims=True)
        o = jnp.dot(p.astype(BF16), mv_ref[:, cols], preferred_element_type=F32)
        o_ref[:, cols] = (o * _silu(g_ref[:, cols])).astype(o_ref.dtype)


def _mem_attn(h, mkv, bsz, seq, mem_len, cols, ts):
    col_q, col_g = cols
    bw = mkv.shape[1] // 2
    hd = bw // XA_HEADS
    nblk = seq // ts
    blocks = 2 * _nbytes((ts, bw), F32) + 2 * _nbytes((mem_len, bw), BF16) + _nbytes((ts, bw), BF16)
    return pl.pallas_call(
        functools.partial(_mem_kernel, hd=hd),
        out_shape=jax.ShapeDtypeStruct((bsz * seq, bw), BF16),
        grid=(bsz, nblk),
        in_specs=[pl.BlockSpec((ts, bw), lambda b, i: (b * nblk + i, col_q // bw)),
                  pl.BlockSpec((ts, bw), lambda b, i: (b * nblk + i, col_g // bw)),
                  pl.BlockSpec((mem_len, bw), lambda b, i: (b, 0)),
                  pl.BlockSpec((mem_len, bw), lambda b, i: (b, 1))],
        out_specs=pl.BlockSpec((ts, bw), lambda b, i: (b * nblk + i, 0)),
        compiler_params=pltpu.CompilerParams(
            dimension_semantics=("parallel", "parallel"), vmem_limit_bytes=_vmem_limit(blocks)),
        name="mem_attn",
    )(h, h, mkv, mkv)


IN_TN = 2 * LANES
MERGE_TN = 2 * LANES


def _main_layout(d_model):
    bw = d_model // 4
    dkt, dvt = bw // 2, bw
    kvw = 2 * SWA_HEAD_DIM
    names = ["pool_u", "pool_g", "gq", "gk", "gv", "gg", "glr", "sq", "sk", "sv", "sg", "xq", "xg"]
    widths = [bw, bw, dkt, dkt, dvt, dvt, GLA_RANK, bw, kvw, kvw, bw, bw, bw]
    src = {}
    off = 0
    for nme, w in zip(names, widths):
        src[nme] = (off, w)
        off += w
    n_main = off
    order = ["pool_u", "pool_g", "gv", "gg", "sq", "sg", "xq", "xg", "gq", "gk", "sk", "sv", "glr"]
    dst = {}
    cols = []
    off = 0
    for nme in order:
        o, w = src[nme]
        w_dst = w if nme != "glr" else IN_TN
        assert off % w_dst == 0
        dst[nme] = off
        cols.append(o + np.arange(w_dst))
        off += w_dst
    cols = np.concatenate(cols)
    assert off % IN_TN == 0 and cols.max() < n_main
    starts = cols[::IN_TN]
    assert np.all(cols.reshape(-1, IN_TN) == starts[:, None] + np.arange(IN_TN))
    return starts, dst, n_main


def _hybrid_layer(l, x, xb, mem_b, pos_f, w_in_t, w_pool, pool_scale, w_gla_up, b_gla, gla_norm,
                  sinks, w_mem_kv, w_branch, w_out, ln_g, ln_b, alpha, emit_bf16, bsz, seq):
    m, d = x.shape
    col_starts, dst, n_main = _main_layout(d)
    w_up_pad = jnp.concatenate(
        [w_gla_up[l], jnp.zeros((IN_TN - GLA_RANK, w_gla_up.shape[-1]), w_gla_up.dtype)], axis=0)

    h, w_out_b = _in_proj(xb, w_in_t, w_out, l, col_starts, tm=2048, tn=IN_TN, per_step=2)
    mkv = _matmul(mem_b, w_mem_kv, l, BF16, tm=mem_b.shape[0], tn=512, name="mem_kv")

    o_pool = _pool(h, w_pool[l].astype(BF16), pool_scale[l], bsz, seq,
                   dst["pool_u"], dst["pool_g"], ts=512)
    half = N_BRANCH // 2
    o_gla, w_mg_a = _gla(h, w_up_pad, b_gla[l], gla_norm[l], bsz, seq,
                         (dst["gq"], dst["gk"], dst["gv"], dst["gg"], dst["glr"]), 512,
                         w_in_t, l, n_main, half, MERGE_TN)
    o_swa, w_mg_b = _swa(h, pos_f, sinks[l], bsz, seq, (dst["sq"], dst["sk"], dst["sv"], dst["sg"]),
                         w_in_t, l, n_main + half * d, half, MERGE_TN)
    o_mem = _mem_attn(h, mkv, bsz, seq, mem_b.shape[0] // bsz, (dst["xq"], dst["xg"]), ts=512)

    y = _merge(xb, (o_pool, o_gla, o_swa, o_mem), w_mg_a, w_mg_b, w_branch, l, tm=1024, tn=MERGE_TN)
    z = _out_proj(y, w_out_b, x, alpha, tm=1024, tn=256, per_step=4)
    return _layer_norm(z, ln_g[l], ln_b[l], tm=512, emit_bf16=emit_bf16)


def kernel(x, mem, positions, w_in, w_pool, pool_scale, w_gla_up, b_gla, gla_norm, sinks,
           w_mem_kv, w_branch, w_out, ln_g, ln_b):
    bsz, seq, d = x.shape
    depth = w_in.shape[0]
    alpha = (2 * depth) ** 0.25
    xf = x.reshape(bsz * seq, d)
    xb = xf.astype(BF16)
    mem_b = mem.reshape(bsz * mem.shape[1], d).astype(BF16)
    pos_f = positions.astype(F32).reshape(bsz * seq, 1)
    w_in_t = jnp.swapaxes(w_in, 1, 2)
    for l in range(depth):
        xf, xb = _hybrid_layer(
            l, xf, xb, mem_b, pos_f, w_in_t, w_pool, pool_scale, w_gla_up, b_gla,
            gla_norm, sinks, w_mem_kv, w_branch, w_out, ln_g, ln_b,
            alpha, l + 1 < depth, bsz, seq)
    return xf.reshape(bsz, seq, d)
```

```python
import functools
import math

import jax
import jax.numpy as jnp
import numpy as np
from jax import lax
from jax.experimental import pallas as pl
from jax.experimental.pallas import tpu as pltpu

F32 = jnp.float32
BF16 = jnp.bfloat16

N_BRANCH = 4
POOL_WINDOWS = (2, 4, 8, 16)
GLA_HEADS = 4
GLA_RANK = 16
GLA_TAU = 16.0
GLA_CHUNK = 64
GLA_SUB = 16
GLA_SAFE_LOG2 = 24.0
GLA_SAFE_KEY = 2.0 ** 100
SWA_HEAD_DIM = 64
WINDOW = 128
SWA_BLOCK = 128
ROPE_THETA = 500000.0
ROPE_DIM = SWA_HEAD_DIM // 4
XA_HEADS = 4
LN_EPS = 1e-5

LANES = 128
SUBLANES = 8
V7X_MXU_K = 256
V7X_VMEM_BYTES = 64 * 1024 * 1024
VMEM_CAP = V7X_VMEM_BYTES - 8 * 1024 * 1024
VMEM_TEMP_BYTES = 12 * 1024 * 1024


def _vmem_limit(streamed_bytes, resident_bytes=0):
    return int(min(VMEM_CAP, 2 * streamed_bytes + resident_bytes + VMEM_TEMP_BYTES))


def _nbytes(shape, dtype):
    return int(np.prod(shape)) * jnp.dtype(dtype).itemsize


def _silu(g):
    return g * jax.nn.sigmoid(g)


def _div_pow2(x, n):
    assert n & (n - 1) == 0
    return x >> (n.bit_length() - 1)


def _mod_pow2(x, n):
    assert n & (n - 1) == 0
    return x & (n - 1)


NT_DIMS = (((1,), (1,)), ((), ()))


def _mm_kernel(a_ref, w_ref, o_ref):
    w = w_ref[...].astype(BF16)
    o_ref[...] = jnp.dot(a_ref[...], w, preferred_element_type=F32).astype(o_ref.dtype)


def _matmul(a, w, layer, out_dtype, tm, tn, name):
    m, k = a.shape
    n = w.shape[-1]
    assert m % tm == 0 and n % tn == 0
    resident = _nbytes((tm, k), a.dtype)
    streamed = _nbytes((k, tn), w.dtype) + _nbytes((tm, tn), out_dtype)
    return pl.pallas_call(
        _mm_kernel,
        out_shape=jax.ShapeDtypeStruct((m, n), out_dtype),
        grid=(m // tm, n // tn),
        in_specs=[pl.BlockSpec((tm, k), lambda i, j: (i, 0), pipeline_mode=pl.Buffered(1)),
                  pl.BlockSpec((None, k, tn), lambda i, j: (layer, 0, j))],
        out_specs=pl.BlockSpec((tm, tn), lambda i, j: (i, j)),
        compiler_params=pltpu.CompilerParams(
            dimension_semantics=("parallel", "arbitrary"),
            vmem_limit_bytes=_vmem_limit(streamed, resident)),
        name=name,
    )(a, w)


def _out_proj_kernel(y_ref, w_ref, x_ref, o_ref, *, alpha, tn):
    y = y_ref[...]
    for t in range(o_ref.shape[1] // tn):
        cols = slice(t * tn, (t + 1) * tn)
        o_ref[:, cols] = alpha * x_ref[:, cols] + jnp.dot(
            y, w_ref[:, cols], preferred_element_type=F32)


def _out_proj(y, w_out_b, x, alpha, tm, tn, per_step):
    m, k = y.shape
    n = w_out_b.shape[1]
    tb = per_step * tn
    assert m % tm == 0 and n % tb == 0
    streamed = (_nbytes((tm, k), y.dtype) + _nbytes((k, tb), w_out_b.dtype)
                + 2 * _nbytes((tm, tb), F32))
    return pl.pallas_call(
        functools.partial(_out_proj_kernel, alpha=alpha, tn=tn),
        out_shape=jax.ShapeDtypeStruct((m, n), F32),
        grid=(m // tm, n // tb),
        in_specs=[pl.BlockSpec((tm, k), lambda i, j: (i, 0)),
                  pl.BlockSpec((k, tb), lambda i, j: (0, j)),
                  pl.BlockSpec((tm, tb), lambda i, j: (i, j))],
        out_specs=pl.BlockSpec((tm, tb), lambda i, j: (i, j)),
        compiler_params=pltpu.CompilerParams(
            dimension_semantics=("parallel", "arbitrary"),
            vmem_limit_bytes=_vmem_limit(streamed)),
        name="out_proj",
    )(y, w_out_b, x)


def _in_proj_kernel(tab_ref, x_ref, *refs, per_step):
    w_refs, wo_ref, o_ref, wos_ref = refs[:per_step], refs[per_step], refs[-2], refs[-1]
    wos_ref[...] = wo_ref[...].astype(wos_ref.dtype)
    x = x_ref[...]
    tn = w_refs[0].shape[1]
    for t, w_ref in enumerate(w_refs):
        w = w_ref[0].astype(BF16)
        o_ref[:, t * tn:(t + 1) * tn] = lax.dot_general(x, w, NT_DIMS, preferred_element_type=F32)


def _in_proj(xb, w_in_t, w_out, layer, col_starts, tm, tn, per_step):
    m, k = xb.shape
    nblk = len(col_starts)
    assert m % tm == 0 and nblk % per_step == 0 and all(int(s) % SUBLANES == 0 for s in col_starts)
    nj = nblk // per_step
    d_out, n_out = w_out.shape[1:]
    sr = 4 * 2 * SUBLANES
    n_slabs = d_out // sr
    assert d_out % sr == 0 and n_slabs <= (m // tm) * nj
    resident = _nbytes((tm, k), xb.dtype)
    streamed = (per_step * (_nbytes((tn, k), w_in_t.dtype) + _nbytes((tm, tn), F32))
                + _nbytes((sr, n_out), w_out.dtype) + _nbytes((sr, n_out), BF16))

    def window(t):
        return pl.BlockSpec(
            (pl.Element(1), pl.Element(tn), pl.Element(k)),
            lambda i, j, tab: (layer, pl.multiple_of(tab[j * per_step + t], SUBLANES), 0))

    def slab(i, j):
        return jnp.minimum(i * nj + j, n_slabs - 1)

    return pl.pallas_call(
        functools.partial(_in_proj_kernel, per_step=per_step),
        out_shape=[jax.ShapeDtypeStruct((m, nblk * tn), F32),
                   jax.ShapeDtypeStruct((d_out, n_out), BF16)],
        grid_spec=pltpu.PrefetchScalarGridSpec(
            num_scalar_prefetch=1,
            grid=(m // tm, nj),
            in_specs=[pl.BlockSpec((tm, k), lambda i, j, tab: (i, 0), pipeline_mode=pl.Buffered(1))]
            + [window(t) for t in range(per_step)]
            + [pl.BlockSpec((None, sr, n_out), lambda i, j, tab: (layer, slab(i, j), 0))],
            out_specs=[pl.BlockSpec((tm, per_step * tn), lambda i, j, tab: (i, j)),
                       pl.BlockSpec((sr, n_out), lambda i, j, tab: (slab(i, j), 0))]),
        compiler_params=pltpu.CompilerParams(
            dimension_semantics=("arbitrary", "arbitrary"),
            vmem_limit_bytes=_vmem_limit(streamed, resident)),
        name="in_proj",
    )(jnp.asarray(np.asarray(col_starts, np.int32)), xb, *([w_in_t] * per_step), w_out)


def _gate_stage_plan(w_in_t, layer, row0, n_branches, n_steps, gate_tn, step_of):
    k = w_in_t.shape[-1]
    rows = n_branches * k
    slab = rows // n_steps
    per_slab = slab // gate_tn
    slabs_per_branch = k // slab
    assert rows % n_steps == 0 and slab % gate_tn == 0 and k % slab == 0 and row0 % SUBLANES == 0
    in_spec = pl.BlockSpec(
        (pl.Element(1), pl.Element(slab), pl.Element(k)),
        lambda *ids: (layer, pl.multiple_of(row0 + step_of(*ids) * slab, SUBLANES), 0))
    out_spec = pl.BlockSpec(
        (per_slab, gate_tn, k),
        lambda *ids: (step_of(*ids) % slabs_per_branch, step_of(*ids) // slabs_per_branch, 0))
    out_shape = jax.ShapeDtypeStruct((k // gate_tn, n_branches * gate_tn, k), BF16)
    return in_spec, out_spec, out_shape, _nbytes((slab, k), w_in_t.dtype) + _nbytes((slab, k), BF16)


def _stage_gate_slab(src_ref, dst_ref):
    rows = dst_ref.shape[1]
    for t in range(dst_ref.shape[0]):
        dst_ref[t] = src_ref[0, t * rows:(t + 1) * rows, :].astype(dst_ref.dtype)


def _merge_kernel(x_ref, o0, o1, o2, o3, ga_ref, gb_ref, w_ref, y_ref):
    x = x_ref[...]
    tn = y_ref.shape[1]
    half = N_BRANCH // 2
    acc = None
    for b, o_ref in enumerate((o0, o1, o2, o3)):
        g_ref, bl = (ga_ref, b) if b < half else (gb_ref, b - half)
        gate = lax.dot_general(x, g_ref[bl * tn:(bl + 1) * tn, :], NT_DIMS, preferred_element_type=F32)
        proj = jnp.dot(o_ref[...], w_ref[b].astype(BF16), preferred_element_type=F32)
        term = jax.nn.sigmoid(gate) * proj
        acc = term if acc is None else acc + term
    y_ref[...] = acc.astype(y_ref.dtype)


def _merge(xb, branches, w_mg_a, w_mg_b, w_branch, layer, tm, tn):
    m, d = xb.shape
    bw = branches[0].shape[1]
    nj = d // tn
    in_specs = [pl.BlockSpec((tm, d), lambda i, j: (i, 0), pipeline_mode=pl.Buffered(1))]
    in_specs += [pl.BlockSpec((tm, bw), lambda i, j: (i, 0), pipeline_mode=pl.Buffered(1))
                 for _ in range(N_BRANCH)]
    in_specs += [pl.BlockSpec((None, N_BRANCH // 2 * tn, d), lambda i, j: (j, 0, 0)),
                 pl.BlockSpec((None, N_BRANCH // 2 * tn, d), lambda i, j: (j, 0, 0)),
                 pl.BlockSpec((None, N_BRANCH, bw, tn), lambda i, j: (layer, 0, 0, j))]
    resident = _nbytes((tm, d), BF16) + N_BRANCH * _nbytes((tm, bw), BF16)
    streamed = (N_BRANCH * (_nbytes((tn, d), BF16) + _nbytes((bw, tn), w_branch.dtype))
                + _nbytes((tm, tn), BF16))
    return pl.pallas_call(
        _merge_kernel,
        out_shape=jax.ShapeDtypeStruct((m, d), BF16),
        grid=(m // tm, nj),
        in_specs=in_specs,
        out_specs=pl.BlockSpec((tm, tn), lambda i, j: (i, j)),
        compiler_params=pltpu.CompilerParams(
            dimension_semantics=("parallel", "arbitrary"),
            vmem_limit_bytes=_vmem_limit(streamed, resident)),
        name="merge",
    )(xb, *branches, w_mg_a, w_mg_b, w_branch)


def _ln_kernel(z_ref, g_ref, b_ref, o_ref, ob_ref=None):
    z = z_ref[...]
    mu = jnp.mean(z, axis=-1, keepdims=True)
    zc = z - mu
    var = jnp.mean(zc * zc, axis=-1, keepdims=True)
    y = zc * lax.rsqrt(var + LN_EPS) * g_ref[...] + b_ref[...]
    o_ref[...] = y
    if ob_ref is not None:
        ob_ref[...] = y.astype(BF16)


def _layer_norm(z, g, b, tm, emit_bf16):
    m, d = z.shape
    out_shape = [jax.ShapeDtypeStruct((m, d), F32)]
    out_specs = [pl.BlockSpec((tm, d), lambda i: (i, 0))]
    if emit_bf16:
        out_shape.append(jax.ShapeDtypeStruct((m, d), BF16))
        out_specs.append(pl.BlockSpec((tm, d), lambda i: (i, 0)))
    blocks = 2 * _nbytes((tm, d), F32) + _nbytes((tm, d), BF16)
    res = pl.pallas_call(
        _ln_kernel,
        out_shape=out_shape,
        grid=(m // tm,),
        in_specs=[pl.BlockSpec((tm, d), lambda i: (i, 0)),
                  pl.BlockSpec((1, d), lambda i: (0, 0)),
                  pl.BlockSpec((1, d), lambda i: (0, 0))],
        out_specs=out_specs,
        compiler_params=pltpu.CompilerParams(
            dimension_semantics=("parallel",), vmem_limit_bytes=_vmem_limit(blocks)),
        name="layer_norm",
    )(z, g.reshape(1, d), b.reshape(1, d))
    return res if emit_bf16 else (res[0], None)


POOL_HIST = max(POOL_WINDOWS)
POOL_PAD = 2 * POOL_HIST


def _pool_kernel(u_ref, g_ref, wp_ref, sc_ref, o_ref, buf, lvl_a, lvl_b, *, ts, group):
    i = pl.program_id(1)
    ext = POOL_PAD + ts

    @pl.when(i == 0)
    def _():
        buf[0:POOL_PAD, :] = jnp.zeros((POOL_PAD, buf.shape[1]), F32)

    buf[POOL_PAD:ext, :] = u_ref[...]
    src = buf
    level_refs = []
    for lvl, dst in enumerate((lvl_a, lvl_b, lvl_a, lvl_b)[:len(POOL_WINDOWS)], start=1):
        lag = 2 ** (lvl - 1)
        r0 = SUBLANES * lvl
        c0 = (lvl - 1) * group
        dst[r0:ext, c0:] = src[r0:ext, c0:] + src[r0 - lag:ext - lag, c0:]
        level_refs.append(dst)
        src = dst

    t = i * ts + lax.broadcasted_iota(jnp.int32, (ts, 1), 0)
    outs = []
    for gi, w in enumerate(POOL_WINDOWS):
        assert w == 2 ** (gi + 1)
        cols = slice(gi * group, (gi + 1) * group)
        acc = level_refs[gi][POOL_PAD:ext, cols]
        cnt = jnp.minimum(t + 1, w).astype(F32)
        pooled = acc / cnt - buf[POOL_PAD:ext, cols]
        outs.append(jnp.dot(pooled.astype(BF16), wp_ref[gi], preferred_element_type=F32))
    o = jnp.concatenate(outs, axis=1) * sc_ref[...] * _silu(g_ref[...])
    o_ref[...] = o.astype(o_ref.dtype)
    buf[POOL_PAD - POOL_HIST:POOL_PAD, :] = buf[ext - POOL_HIST:ext, :]


def _split_bf16(x, parts):
    out = []
    r = x
    for _ in range(parts):
        p = r.astype(BF16)
        out.append(p)
        r = r - p.astype(F32)
    return out


LOG2E = math.log2(math.e)


def _gla_cumsum_operator():
    r = np.arange(V7X_MXU_K)
    tri = ((r[:, None] // GLA_CHUNK == r[None, :] // GLA_CHUNK) & (r[None, :] <= r[:, None]))
    return jnp.asarray(tri, BF16)


def _gla_kernel(q_ref, k_ref, v_ref, gg_ref, lr_ref, wup_ref, bg_ref, nrm_ref, tri_ref, wg_ref,
                o_ref, wgs_ref, state_ref, b_ref, *, rows, dk, dv):
    _stage_gate_slab(wg_ref, wgs_ref)
    i = pl.program_id(1)
    C = GLA_CHUNK
    nsub = C // GLA_SUB

    @pl.when(i == 0)
    def _():
        state_ref[...] = jnp.zeros(state_ref.shape, F32)

    lr_h, lr_m = _split_bf16(lr_ref[...], 2)
    w_h, w_m = _split_bf16(wup_ref[...], 2)
    z = (jnp.dot(lr_h, w_h, preferred_element_type=F32)
         + jnp.dot(lr_h, w_m, preferred_element_type=F32)
         + jnp.dot(lr_m, w_h, preferred_element_type=F32)) + bg_ref[...]
    log2_a = (jnp.minimum(z, 0.0) - jnp.log(1.0 + jnp.exp(-jnp.abs(z)))) * (LOG2E / GLA_TAU)

    tb = tri_ref.shape[0]
    parts = _split_bf16(log2_a, 3)
    for r in range(rows // tb):
        acc = None
        for part in parts:
            t = jnp.dot(tri_ref[...], part[r * tb:(r + 1) * tb, :], preferred_element_type=F32)
            acc = t if acc is None else acc + t
        b_ref[r * tb:(r + 1) * tb, :] = acc

    row = lax.broadcasted_iota(jnp.int32, (C, 1), 0)
    col = lax.broadcasted_iota(jnp.int32, (1, C), 1)
    row_sub = _div_pow2(row, GLA_SUB)
    col_sub = _div_pow2(col, GLA_SUB)
    lane_c = lax.broadcasted_iota(jnp.int32, (GLA_SUB, C), 1)
    scale = dk ** -0.5

    def sub_heads(x):
        return jnp.concatenate(
            [jnp.broadcast_to(x[a * GLA_SUB:a * GLA_SUB + 1, :], (GLA_SUB, dk))
             for a in range(nsub)], axis=0)

    def chunk_body(c, carry, *, bounded):
        r0 = pl.multiple_of(c * C, C)
        for h in range(GLA_HEADS):
            kc = slice(h * dk, (h + 1) * dk)
            vc = slice(h * dv, (h + 1) * dv)
            qs = q_ref[pl.ds(r0, C), kc] * scale
            k = k_ref[pl.ds(r0, C), kc]
            v = v_ref[pl.ds(r0, C), vc].astype(BF16)
            b = b_ref[pl.ds(r0, C), kc]
            b_last = b[C - 1:C, :]
            st = state_ref[h]

            q_in = (qs * jnp.exp2(b)).astype(BF16)
            o = lax.dot_general(q_in, st.astype(BF16), NT_DIMS, preferred_element_type=F32)

            q_t = qs * jnp.exp2(b - sub_heads(b))
            lhs = jnp.concatenate(
                [jnp.where(row_sub == a, q_t, 0.0) for a in range(nsub)], axis=1).astype(BF16)
            key_exp = [b[a * GLA_SUB:a * GLA_SUB + 1, :] - b for a in range(nsub)]
            if not bounded:
                key_exp = [jnp.minimum(e, 0.0) for e in key_exp]
            rhs = jnp.concatenate([k * jnp.exp2(e) for e in key_exp], axis=1).astype(BF16)
            a_fac = lax.dot_general(lhs, rhs, NT_DIMS, preferred_element_type=F32)

            if bounded:
                attn = jnp.where(row >= col, a_fac, 0.0)
            else:
                diag_blocks = []
                for a in range(nsub):
                    rs = slice(a * GLA_SUB, (a + 1) * GLA_SUB)
                    qa = qs[rs, :]
                    ba = b[rs, :]
                    blk = jnp.zeros((GLA_SUB, C), F32)
                    for jj in range(GLA_SUB):
                        j = a * GLA_SUB + jj
                        tj = qa * k[j:j + 1, :] * jnp.exp2(ba - b[j:j + 1, :])
                        cj = jnp.sum(tj, axis=1, keepdims=True)
                        blk = jnp.where(lane_c == j, cj, blk)
                    diag_blocks.append(blk)
                a_diag = jnp.concatenate(diag_blocks, axis=0)
                attn = jnp.where(row_sub > col_sub, a_fac,
                                 jnp.where((row_sub == col_sub) & (row >= col), a_diag, 0.0))
            o = o + jnp.dot(attn.astype(BF16), v, preferred_element_type=F32)

            k_dec = (k * jnp.exp2(b_last - b)).astype(BF16)
            upd = lax.dot_general(v, k_dec, (((0,), (0,)), ((), ())),
                                  preferred_element_type=F32)
            state_ref[h] = st * jnp.exp2(b_last) + upd

            ms = jnp.mean(o * o, axis=-1, keepdims=True)
            on = o * lax.rsqrt(ms + LN_EPS) * nrm_ref[:, vc]
            o_ref[pl.ds(r0, C), vc] = (on * _silu(gg_ref[pl.ds(r0, C), vc])).astype(o_ref.dtype)
        return carry

    bounded = ((-jnp.min(b_ref[...]) < GLA_SAFE_LOG2)
               & (jnp.max(jnp.abs(k_ref[...])) < GLA_SAFE_KEY))

    @pl.when(bounded)
    def _():
        lax.fori_loop(0, rows // C, functools.partial(chunk_body, bounded=True), 0, unroll=2)

    @pl.when(jnp.logical_not(bounded))
    def _():
        lax.fori_loop(0, rows // C, functools.partial(chunk_body, bounded=False), 0)


def _gla(h, w_up_pad, b_gla, gla_norm, bsz, seq, cols, rows,
         w_in_t, layer, gate_row0, gate_branches, gate_tn):
    col_q, col_k, col_v, col_g, col_lr = cols
    dkt = b_gla.shape[-1]
    dvt = gla_norm.shape[-1]
    dk, dv = dkt // GLA_HEADS, dvt // GLA_HEADS
    lrw = w_up_pad.shape[0]
    nblk = seq // rows
    tri = _gla_cumsum_operator()
    assert rows % tri.shape[0] == 0 and tri.shape[0] % GLA_CHUNK == 0
    g_in, g_out, g_shape, g_bytes = _gate_stage_plan(
        w_in_t, layer, gate_row0, gate_branches, bsz * nblk, gate_tn, lambda b, i: b * nblk + i)
    blocks = (2 * _nbytes((rows, dkt), F32) + 2 * _nbytes((rows, dvt), F32)
              + _nbytes((rows, lrw), F32) + _nbytes((rows, dvt), BF16) + g_bytes)
    resident = (_nbytes((rows, dkt), F32) + _nbytes((GLA_HEADS, dv, dk), F32)
                + _nbytes(tri.shape, BF16))
    const = lambda b, i: (0, 0)
    return pl.pallas_call(
        functools.partial(_gla_kernel, rows=rows, dk=dk, dv=dv),
        out_shape=[jax.ShapeDtypeStruct((bsz * seq, dvt), BF16), g_shape],
        grid=(bsz, nblk),
        in_specs=[pl.BlockSpec((rows, dkt), lambda b, i: (b * nblk + i, col_q // dkt)),
                  pl.BlockSpec((rows, dkt), lambda b, i: (b * nblk + i, col_k // dkt)),
                  pl.BlockSpec((rows, dvt), lambda b, i: (b * nblk + i, col_v // dvt)),
                  pl.BlockSpec((rows, dvt), lambda b, i: (b * nblk + i, col_g // dvt)),
                  pl.BlockSpec((rows, lrw), lambda b, i: (b * nblk + i, col_lr // lrw)),
                  pl.BlockSpec((lrw, dkt), const),
                  pl.BlockSpec((1, dkt), const),
                  pl.BlockSpec((1, dvt), const),
                  pl.BlockSpec(tri.shape, const),
                  g_in],
        out_specs=[pl.BlockSpec((rows, dvt), lambda b, i: (b * nblk + i, 0)), g_out],
        scratch_shapes=[pltpu.VMEM((GLA_HEADS, dv, dk), F32),
                        pltpu.VMEM((rows, dkt), F32)],
        compiler_params=pltpu.CompilerParams(
            dimension_semantics=("parallel", "arbitrary"),
            vmem_limit_bytes=_vmem_limit(blocks, resident)),
        name="gla_mixer",
    )(h, h, h, h, h, w_up_pad, b_gla.reshape(1, dkt), gla_norm.reshape(1, dvt), tri, w_in_t)


def _rope_lane_table():
    half = ROPE_DIM // 2
    inv_freq = ROPE_THETA ** (-np.arange(0, ROPE_DIM, 2, dtype=np.float32) / ROPE_DIM)
    lane = np.arange(LANES) % SWA_HEAD_DIM
    tab = np.where(lane < ROPE_DIM, inv_freq[lane % half], 0.0).astype(np.float32)
    return jnp.asarray(tab.reshape(1, LANES))


SWA_ROW_CHUNK = 32
SWA_BLOCKS_PER_STEP = 2


def _swa_kernel(sink_ref, q_ref, k_ref, v_ref, g_ref, pos_ref, inv_ref, wg_ref, o_ref, wgs_ref,
                *scratch, n_pairs, n_sub):
    _stage_gate_slab(wg_ref, wgs_ref)
    step = pl.program_id(1)
    for sb in range(n_sub):
        rows = pl.ds(sb * SWA_BLOCK, SWA_BLOCK)
        _swa_block(step * n_sub + sb, sink_ref, q_ref.at[rows], k_ref.at[rows], v_ref.at[rows],
                   g_ref.at[rows], pos_ref.at[rows], inv_ref, o_ref.at[rows], *scratch,
                   n_pairs=n_pairs)


def _swa_block(n, sink_ref, q_ref, k_ref, v_ref, g_ref, pos_ref, inv_ref, o_ref,
               wk_ref, wv_ref, qr_ref, s_ref, p_ref, es_ref, bias_ref, *, n_pairs):
    blk = SWA_BLOCK
    half = ROPE_DIM // 2
    hd = SWA_HEAD_DIM
    ppg = n_pairs // 2
    rc_rows = SWA_ROW_CHUNK

    @pl.when(n == 0)
    def _():
        wk_ref[...] = jnp.zeros(wk_ref.shape, BF16)
        row = lax.broadcasted_iota(jnp.int32, (4 * blk, 2 * LANES), 0)
        col = lax.broadcasted_iota(jnp.int32, (4 * blk, 2 * LANES), 1)
        ones = ((col >= LANES) & ((col >= LANES + hd) == (row >= 2 * blk))).astype(BF16)
        for g in range(2):
            wv_ref[g] = ones

    lane = lax.broadcasted_iota(jnp.int32, (1, LANES), 1)
    lane_h = _mod_pow2(lane, hd)
    lo = lane < hd
    ang = pos_ref[...] * inv_ref[...]
    cos = jnp.cos(ang)
    sin = jnp.sin(ang)

    def rope(x):
        x_up = pltpu.roll(x, LANES - half, axis=1)
        x_dn = pltpu.roll(x, half, axis=1)
        return x * cos + jnp.where(lane_h < half, -x_up, x_dn) * sin

    for g in range(2):
        for part in range(2):
            base = part * 2 * blk
            wk_ref[g, base:base + blk, :] = wk_ref[g, base + blk:base + 2 * blk, :]
            wv_ref[g, base:base + blk, 0:LANES] = wv_ref[g, base + blk:base + 2 * blk, 0:LANES]
    k_r = rope(k_ref[...])
    v_c = v_ref[...]
    k_sw = pltpu.roll(k_r, hd, axis=1)
    v_sw = pltpu.roll(v_c, hd, axis=1)
    zero = jnp.zeros_like(k_r)
    for g in range(2):
        k_lo, k_hi = (k_r, k_sw) if g == 0 else (k_sw, k_r)
        v_lo, v_hi = (v_c, v_sw) if g == 0 else (v_sw, v_c)
        wk_ref[g, blk:2 * blk, :] = jnp.where(lo, k_lo, zero).astype(BF16)
        wk_ref[g, 3 * blk:4 * blk, :] = jnp.where(lo, zero, k_hi).astype(BF16)
        wv_ref[g, blk:2 * blk, 0:LANES] = jnp.where(lo, v_lo, zero).astype(BF16)
        wv_ref[g, 3 * blk:4 * blk, 0:LANES] = jnp.where(lo, zero, v_hi).astype(BF16)

    r = lax.broadcasted_iota(jnp.int32, (blk, 2 * blk), 0)
    c = lax.broadcasted_iota(jnp.int32, (blk, 2 * blk), 1)
    valid = (c > r) & (c <= r + WINDOW) & ((n > 0) | (c >= blk))
    bias_ref[...] = jnp.where(valid, 0.0, -jnp.inf).astype(F32)

    qscale = hd ** -0.5
    for p in range(n_pairs):
        g, pp = divmod(p, ppg)
        qr_ref[g, pp * blk:(pp + 1) * blk, :] = (
            rope(q_ref[:, p * LANES:(p + 1) * LANES]) * qscale).astype(BF16)

    for g in range(2):
        s_ref[g] = lax.dot_general(qr_ref[g], wk_ref[g], NT_DIMS, preferred_element_type=F32)

    for g in range(2):
        for pp in range(ppg):
            for hh in range(2):
                sink = sink_ref[2 * (g * ppg + pp) + hh]
                kc = slice(hh * 2 * blk, (hh + 1) * 2 * blk)
                for rc in range(blk // rc_rows):
                    rows = slice(pp * blk + rc * rc_rows, pp * blk + (rc + 1) * rc_rows)
                    sh = s_ref[g, rows, kc] + bias_ref[rc * rc_rows:(rc + 1) * rc_rows, :]
                    m = jnp.maximum(jnp.max(sh, axis=-1, keepdims=True), sink)
                    p_ref[g, rows, kc] = jnp.exp(sh - m).astype(BF16)
                    es_ref[g, rows, hh * hd:(hh + 1) * hd] = jnp.broadcast_to(
                        jnp.exp(sink - m), (rc_rows, hd))

    for g in range(2):
        o2 = jnp.dot(p_ref[g], wv_ref[g], preferred_element_type=F32)
        for pp in range(ppg):
            rows = slice(pp * blk, (pp + 1) * blk)
            cols = slice((g * ppg + pp) * LANES, (g * ppg + pp + 1) * LANES)
            den = o2[rows, LANES:] + es_ref[g, rows, :]
            o_ref[:, cols] = (o2[rows, :LANES] / den * _silu(g_ref[:, cols])).astype(o_ref.dtype)


def _swa(h, pos_f, sinks, bsz, seq, cols, w_in_t, layer, gate_row0, gate_branches, gate_tn):
    col_q, col_k, col_v, col_g = cols
    n_heads = sinks.shape[0]
    bw = n_heads * SWA_HEAD_DIM
    n_pairs = bw // LANES
    assert LANES == 2 * SWA_HEAD_DIM and n_pairs % 2 == 0
    n_sub = SWA_BLOCKS_PER_STEP
    blk = n_sub * SWA_BLOCK
    nblk = seq // blk
    scratch = [pltpu.VMEM((2, 4 * SWA_BLOCK, LANES), BF16),
               pltpu.VMEM((2, 4 * SWA_BLOCK, 2 * LANES), BF16),
               pltpu.VMEM((2, 4 * SWA_BLOCK, LANES), BF16),
               pltpu.VMEM((2, 4 * SWA_BLOCK, 4 * SWA_BLOCK), F32),
               pltpu.VMEM((2, 4 * SWA_BLOCK, 4 * SWA_BLOCK), BF16),
               pltpu.VMEM((2, 4 * SWA_BLOCK, LANES), F32),
               pltpu.VMEM((SWA_BLOCK, 2 * SWA_BLOCK), F32)]
    g_in, g_out, g_shape, g_bytes = _gate_stage_plan(
        w_in_t, layer, gate_row0, gate_branches, bsz * nblk, gate_tn, lambda b, i: b * nblk + i)
    blocks = (2 * _nbytes((blk, bw), F32) + 3 * _nbytes((blk, LANES), F32) + _nbytes((blk, bw), BF16)
              + g_bytes)
    resident = sum(_nbytes(sc.shape, sc.dtype) for sc in scratch)

    return pl.pallas_call(
        functools.partial(_swa_kernel, n_pairs=n_pairs, n_sub=n_sub),
        out_shape=[jax.ShapeDtypeStruct((bsz * seq, bw), BF16), g_shape],
        grid=(bsz, nblk),
        in_specs=[pl.BlockSpec(memory_space=pltpu.SMEM),
                  pl.BlockSpec((blk, bw), lambda b, i: (b * nblk + i, col_q // bw)),
                  pl.BlockSpec((blk, LANES), lambda b, i: (b * nblk + i, col_k // LANES)),
                  pl.BlockSpec((blk, LANES), lambda b, i: (b * nblk + i, col_v // LANES)),
                  pl.BlockSpec((blk, bw), lambda b, i: (b * nblk + i, col_g // bw)),
                  pl.BlockSpec((blk, 1), lambda b, i: (b * nblk + i, 0)),
                  pl.BlockSpec((1, LANES), lambda b, i: (0, 0)),
                  g_in],
        out_specs=[pl.BlockSpec((blk, bw), lambda b, i: (b * nblk + i, 0)), g_out],
        scratch_shapes=scratch,
        compiler_params=pltpu.CompilerParams(
            dimension_semantics=("parallel", "arbitrary"),
            vmem_limit_bytes=_vmem_limit(blocks, resident)),
        name="swa_mixer",
    )(sinks, h, h, h, h, pos_f, _rope_lane_table(), w_in_t)


def _mem_kernel(q_ref, g_ref, mk_ref, mv_ref, o_ref, *, hd):
    scale = hd ** -0.5
    for h in range(XA_HEADS):
        cols = slice(h * hd, (h + 1) * hd)
        qh = q_ref[:, cols].astype(BF16)
        s = lax.dot_general(qh, mk_ref[:, cols], NT_DIMS, preferred_element_type=F32) * scale
        m = jnp.max(s, axis=-1, keepdims=True)
        e = jnp.exp(s - m)
        p = e / jnp.sum(e, axis=-1, keepdims=True)
        o = jnp.dot(p.astype(BF16), mv_ref[:, cols], preferred_element_type=F32)
        o_ref[:, cols] = (o * _silu(g_ref[:, cols])).astype(o_ref.dtype)


def _pool_mem_kernel(u_ref, pg_ref, wp_ref, sc_ref, q_ref, qg_ref, mk_ref, mv_ref,
                     op_ref, om_ref, buf, lvl_a, lvl_b, *, ts, group, hd):
    _pool_kernel(u_ref, pg_ref, wp_ref, sc_ref, op_ref, buf, lvl_a, lvl_b, ts=ts, group=group)
    _mem_kernel(q_ref, qg_ref, mk_ref, mv_ref, om_ref, hd=hd)


def _pool_mem(h, w_pool_b, pool_scale, mkv, bsz, seq, mem_len, pool_cols, mem_cols, ts):
    col_u, col_pg = pool_cols
    col_q, col_qg = mem_cols
    bw = pool_scale.shape[-1]
    assert mkv.shape[1] == 2 * bw
    group = bw // len(POOL_WINDOWS)
    hd = bw // XA_HEADS
    nblk = seq // ts
    blocks = 4 * _nbytes((ts, bw), F32) + 2 * _nbytes((ts, bw), BF16) + 2 * _nbytes((mem_len, bw), BF16)
    scratch = [pltpu.VMEM((ts + POOL_PAD, bw), F32) for _ in range(3)]
    resident = sum(_nbytes(sc.shape, sc.dtype) for sc in scratch)
    row = lambda b, i: b * nblk + i
    return pl.pallas_call(
        functools.partial(_pool_mem_kernel, ts=ts, group=group, hd=hd),
        out_shape=[jax.ShapeDtypeStruct((bsz * seq, bw), BF16)] * 2,
        grid=(bsz, nblk),
        in_specs=[pl.BlockSpec((ts, bw), lambda b, i: (row(b, i), col_u // bw)),
                  pl.BlockSpec((ts, bw), lambda b, i: (row(b, i), col_pg // bw)),
                  pl.BlockSpec(w_pool_b.shape, lambda b, i: (0, 0, 0)),
                  pl.BlockSpec((1, bw), lambda b, i: (0, 0)),
                  pl.BlockSpec((ts, bw), lambda b, i: (row(b, i), col_q // bw)),
                  pl.BlockSpec((ts, bw), lambda b, i: (row(b, i), col_qg // bw)),
                  pl.BlockSpec((mem_len, bw), lambda b, i: (b, 0)),
                  pl.BlockSpec((mem_len, bw), lambda b, i: (b, 1))],
        out_specs=[pl.BlockSpec((ts, bw), lambda b, i: (row(b, i), 0))] * 2,
        scratch_shapes=scratch,
        compiler_params=pltpu.CompilerParams(
            dimension_semantics=("parallel", "arbitrary"),
            vmem_limit_bytes=_vmem_limit(blocks, resident)),
        name="pool_mem_mixers",
    )(h, h, w_pool_b, pool_scale.reshape(1, bw), h, h, mkv, mkv)


IN_TN = 2 * LANES
MERGE_TN = 2 * LANES


def _main_layout(d_model):
    bw = d_model // 4
    dkt, dvt = bw // 2, bw
    kvw = 2 * SWA_HEAD_DIM
    names = ["pool_u", "pool_g", "gq", "gk", "gv", "gg", "glr", "sq", "sk", "sv", "sg", "xq", "xg"]
    widths = [bw, bw, dkt, dkt, dvt, dvt, GLA_RANK, bw, kvw, kvw, bw, bw, bw]
    src = {}
    off = 0
    for nme, w in zip(names, widths):
        src[nme] = (off, w)
        off += w
    n_main = off
    order = ["pool_u", "pool_g", "gv", "gg", "sq", "sg", "xq", "xg", "gq", "gk", "sk", "sv", "glr"]
    dst = {}
    cols = []
    off = 0
    for nme in order:
        o, w = src[nme]
        w_dst = w if nme != "glr" else IN_TN
        assert off % w_dst == 0
        dst[nme] = off
        cols.append(o + np.arange(w_dst))
        off += w_dst
    cols = np.concatenate(cols)
    assert off % IN_TN == 0 and cols.max() < n_main
    starts = cols[::IN_TN]
    assert np.all(cols.reshape(-1, IN_TN) == starts[:, None] + np.arange(IN_TN))
    return starts, dst, n_main


def _hybrid_layer(l, x, xb, mem_b, pos_f, w_in_t, w_pool, pool_scale, w_gla_up, b_gla, gla_norm,
                  sinks, w_mem_kv, w_branch, w_out, ln_g, ln_b, alpha, emit_bf16, bsz, seq):
    m, d = x.shape
    col_starts, dst, n_main = _main_layout(d)
    w_up_pad = jnp.concatenate(
        [w_gla_up[l], jnp.zeros((IN_TN - GLA_RANK, w_gla_up.shape[-1]), w_gla_up.dtype)], axis=0)

    h, w_out_b = _in_proj(xb, w_in_t, w_out, l, col_starts, tm=2048, tn=IN_TN, per_step=2)
    mkv = _matmul(mem_b, w_mem_kv, l, BF16, tm=mem_b.shape[0], tn=512, name="mem_kv")

    o_pool, o_mem = _pool_mem(h, w_pool[l].astype(BF16), pool_scale[l], mkv, bsz, seq,
                              mem_b.shape[0] // bsz, (dst["pool_u"], dst["pool_g"]),
                              (dst["xq"], dst["xg"]), ts=512)
    half = N_BRANCH // 2
    o_gla, w_mg_a = _gla(h, w_up_pad, b_gla[l], gla_norm[l], bsz, seq,
                         (dst["gq"], dst["gk"], dst["gv"], dst["gg"], dst["glr"]), 512,
                         w_in_t, l, n_main, half, MERGE_TN)
    o_swa, w_mg_b = _swa(h, pos_f, sinks[l], bsz, seq, (dst["sq"], dst["sk"], dst["sv"], dst["sg"]),
                         w_in_t, l, n_main + half * d, half, MERGE_TN)

    y = _merge(xb, (o_pool, o_gla, o_swa, o_mem), w_mg_a, w_mg_b, w_branch, l, tm=1024, tn=MERGE_TN)
    z = _out_proj(y, w_out_b, x, alpha, tm=1024, tn=256, per_step=4)
    return _layer_norm(z, ln_g[l], ln_b[l], tm=512, emit_bf16=emit_bf16)


def kernel(x, mem, positions, w_in, w_pool, pool_scale, w_gla_up, b_gla, gla_norm, sinks,
           w_mem_kv, w_branch, w_out, ln_g, ln_b):
    bsz, seq, d = x.shape
    depth = w_in.shape[0]
    alpha = (2 * depth) ** 0.25
    xf = x.reshape(bsz * seq, d)
    xb = xf.astype(BF16)
    mem_b = mem.reshape(bsz * mem.shape[1], d).astype(BF16)
    pos_f = positions.astype(F32).reshape(bsz * seq, 1)
    w_in_t = jnp.swapaxes(w_in, 1, 2)
    for l in range(depth):
        xf, xb = _hybrid_layer(
            l, xf, xb, mem_b, pos_f, w_in_t, w_pool, pool_scale, w_gla_up, b_gla,
            gla_norm, sinks, w_mem_kv, w_branch, w_out, ln_g, ln_b,
            alpha, l + 1 < depth, bsz, seq)
    return xf.reshape(bsz, seq, d)
```

```python
import functools
import math

import jax
import jax.numpy as jnp
import numpy as np
from jax import lax
from jax.experimental import pallas as pl
from jax.experimental.pallas import tpu as pltpu

F32 = jnp.float32
BF16 = jnp.bfloat16

N_BRANCH = 4
POOL_WINDOWS = (2, 4, 8, 16)
GLA_HEADS = 4
GLA_RANK = 16
GLA_TAU = 16.0
GLA_CHUNK = 64
GLA_SUB = 16
GLA_SAFE_LOG2 = 24.0
GLA_SAFE_KEY = 2.0 ** 100
SWA_HEAD_DIM = 64
WINDOW = 128
SWA_BLOCK = 128
ROPE_THETA = 500000.0
ROPE_DIM = SWA_HEAD_DIM // 4
XA_HEADS = 4
LN_EPS = 1e-5

LANES = 128
SUBLANES = 8
V7X_MXU_K = 256
V7X_VMEM_BYTES = 64 * 1024 * 1024
VMEM_CAP = V7X_VMEM_BYTES - 8 * 1024 * 1024
VMEM_TEMP_BYTES = 12 * 1024 * 1024


def _vmem_limit(streamed_bytes, resident_bytes=0):
    return int(min(VMEM_CAP, 2 * streamed_bytes + resident_bytes + VMEM_TEMP_BYTES))


def _nbytes(shape, dtype):
    return int(np.prod(shape)) * jnp.dtype(dtype).itemsize


def _silu(g):
    return g * jax.nn.sigmoid(g)


def _div_pow2(x, n):
    assert n & (n - 1) == 0
    return x >> (n.bit_length() - 1)


def _mod_pow2(x, n):
    assert n & (n - 1) == 0
    return x & (n - 1)


NT_DIMS = (((1,), (1,)), ((), ()))


def _mm_kernel(a_ref, w_ref, o_ref):
    w = w_ref[...].astype(BF16)
    o_ref[...] = jnp.dot(a_ref[...], w, preferred_element_type=F32).astype(o_ref.dtype)


def _matmul(a, w, layer, out_dtype, tm, tn, name):
    m, k = a.shape
    n = w.shape[-1]
    assert m % tm == 0 and n % tn == 0
    resident = _nbytes((tm, k), a.dtype)
    streamed = _nbytes((k, tn), w.dtype) + _nbytes((tm, tn), out_dtype)
    return pl.pallas_call(
        _mm_kernel,
        out_shape=jax.ShapeDtypeStruct((m, n), out_dtype),
        grid=(m // tm, n // tn),
        in_specs=[pl.BlockSpec((tm, k), lambda i, j: (i, 0), pipeline_mode=pl.Buffered(1)),
                  pl.BlockSpec((None, k, tn), lambda i, j: (layer, 0, j))],
        out_specs=pl.BlockSpec((tm, tn), lambda i, j: (i, j)),
        compiler_params=pltpu.CompilerParams(
            dimension_semantics=("parallel", "arbitrary"),
            vmem_limit_bytes=_vmem_limit(streamed, resident)),
        name=name,
    )(a, w)


def _out_proj_kernel(y_ref, w_ref, x_ref, o_ref, *, alpha, tn):
    y = y_ref[...]
    for t in range(o_ref.shape[1] // tn):
        cols = slice(t * tn, (t + 1) * tn)
        o_ref[:, cols] = alpha * x_ref[:, cols] + jnp.dot(
            y, w_ref[:, cols], preferred_element_type=F32)


def _out_proj(y, w_out_b, x, alpha, tm, tn, per_step):
    m, k = y.shape
    n = w_out_b.shape[1]
    tb = per_step * tn
    assert m % tm == 0 and n % tb == 0
    streamed = (_nbytes((tm, k), y.dtype) + _nbytes((k, tb), w_out_b.dtype)
                + 2 * _nbytes((tm, tb), F32))
    return pl.pallas_call(
        functools.partial(_out_proj_kernel, alpha=alpha, tn=tn),
        out_shape=jax.ShapeDtypeStruct((m, n), F32),
        grid=(m // tm, n // tb),
        in_specs=[pl.BlockSpec((tm, k), lambda i, j: (i, 0)),
                  pl.BlockSpec((k, tb), lambda i, j: (0, j)),
                  pl.BlockSpec((tm, tb), lambda i, j: (i, j))],
        out_specs=pl.BlockSpec((tm, tb), lambda i, j: (i, j)),
        compiler_params=pltpu.CompilerParams(
            dimension_semantics=("parallel", "arbitrary"),
            vmem_limit_bytes=_vmem_limit(streamed)),
        name="out_proj",
    )(y, w_out_b, x)


def _in_proj_kernel(tab_ref, x_ref, *refs, per_step):
    w_refs, wo_ref, o_ref, wos_ref = refs[:per_step], refs[per_step], refs[-2], refs[-1]
    wos_ref[...] = wo_ref[...].astype(wos_ref.dtype)
    x = x_ref[...]
    tn = w_refs[0].shape[1]
    for t, w_ref in enumerate(w_refs):
        w = w_ref[0].astype(BF16)
        o_ref[:, t * tn:(t + 1) * tn] = lax.dot_general(x, w, NT_DIMS, preferred_element_type=F32)


def _in_proj(xb, w_in_t, w_out, layer, col_starts, tm, tn, per_step):
    m, k = xb.shape
    nblk = len(col_starts)
    assert m % tm == 0 and nblk % per_step == 0 and all(int(s) % SUBLANES == 0 for s in col_starts)
    nj = nblk // per_step
    d_out, n_out = w_out.shape[1:]
    sr = 4 * 2 * SUBLANES
    n_slabs = d_out // sr
    assert d_out % sr == 0 and n_slabs <= (m // tm) * nj
    resident = _nbytes((tm, k), xb.dtype)
    streamed = (per_step * (_nbytes((tn, k), w_in_t.dtype) + _nbytes((tm, tn), F32))
                + _nbytes((sr, n_out), w_out.dtype) + _nbytes((sr, n_out), BF16))

    def window(t):
        return pl.BlockSpec(
            (pl.Element(1), pl.Element(tn), pl.Element(k)),
            lambda i, j, tab: (layer, pl.multiple_of(tab[j * per_step + t], SUBLANES), 0))

    def slab(i, j):
        return jnp.minimum(i * nj + j, n_slabs - 1)

    return pl.pallas_call(
        functools.partial(_in_proj_kernel, per_step=per_step),
        out_shape=[jax.ShapeDtypeStruct((m, nblk * tn), F32),
                   jax.ShapeDtypeStruct((d_out, n_out), BF16)],
        grid_spec=pltpu.PrefetchScalarGridSpec(
            num_scalar_prefetch=1,
            grid=(m // tm, nj),
            in_specs=[pl.BlockSpec((tm, k), lambda i, j, tab: (i, 0), pipeline_mode=pl.Buffered(1))]
            + [window(t) for t in range(per_step)]
            + [pl.BlockSpec((None, sr, n_out), lambda i, j, tab: (layer, slab(i, j), 0))],
            out_specs=[pl.BlockSpec((tm, per_step * tn), lambda i, j, tab: (i, j)),
                       pl.BlockSpec((sr, n_out), lambda i, j, tab: (slab(i, j), 0))]),
        compiler_params=pltpu.CompilerParams(
            dimension_semantics=("arbitrary", "arbitrary"),
            vmem_limit_bytes=_vmem_limit(streamed, resident)),
        name="in_proj",
    )(jnp.asarray(np.asarray(col_starts, np.int32)), xb, *([w_in_t] * per_step), w_out)


def _gate_stage_plan(w_in_t, layer, row0, n_branches, n_steps, gate_tn, step_of):
    k = w_in_t.shape[-1]
    rows = n_branches * k
    slab = rows // n_steps
    per_slab = slab // gate_tn
    slabs_per_branch = k // slab
    assert rows % n_steps == 0 and slab % gate_tn == 0 and k % slab == 0 and row0 % SUBLANES == 0
    in_spec = pl.BlockSpec(
        (pl.Element(1), pl.Element(slab), pl.Element(k)),
        lambda *ids: (layer, pl.multiple_of(row0 + step_of(*ids) * slab, SUBLANES), 0))
    out_spec = pl.BlockSpec(
        (per_slab, gate_tn, k),
        lambda *ids: (step_of(*ids) % slabs_per_branch, step_of(*ids) // slabs_per_branch, 0))
    out_shape = jax.ShapeDtypeStruct((k // gate_tn, n_branches * gate_tn, k), BF16)
    return in_spec, out_spec, out_shape, _nbytes((slab, k), w_in_t.dtype) + _nbytes((slab, k), BF16)


def _stage_gate_slab(src_ref, dst_ref):
    rows = dst_ref.shape[1]
    for t in range(dst_ref.shape[0]):
        dst_ref[t] = src_ref[0, t * rows:(t + 1) * rows, :].astype(dst_ref.dtype)


def _merge_kernel(x_ref, o0, o1, o2, o3, ga_ref, gb_ref, w_ref, y_ref):
    x = x_ref[...]
    tn = y_ref.shape[1]
    half = N_BRANCH // 2
    acc = None
    for b, o_ref in enumerate((o0, o1, o2, o3)):
        g_ref, bl = (ga_ref, b) if b < half else (gb_ref, b - half)
        gate = lax.dot_general(x, g_ref[bl * tn:(bl + 1) * tn, :], NT_DIMS, preferred_element_type=F32)
        proj = jnp.dot(o_ref[...], w_ref[b].astype(BF16), preferred_element_type=F32)
        term = jax.nn.sigmoid(gate) * proj
        acc = term if acc is None else acc + term
    y_ref[...] = acc.astype(y_ref.dtype)


def _merge(xb, branches, w_mg_a, w_mg_b, w_branch, layer, tm, tn):
    m, d = xb.shape
    bw = branches[0].shape[1]
    nj = d // tn
    in_specs = [pl.BlockSpec((tm, d), lambda i, j: (i, 0), pipeline_mode=pl.Buffered(1))]
    in_specs += [pl.BlockSpec((tm, bw), lambda i, j: (i, 0), pipeline_mode=pl.Buffered(1))
                 for _ in range(N_BRANCH)]
    in_specs += [pl.BlockSpec((None, N_BRANCH // 2 * tn, d), lambda i, j: (j, 0, 0)),
                 pl.BlockSpec((None, N_BRANCH // 2 * tn, d), lambda i, j: (j, 0, 0)),
                 pl.BlockSpec((None, N_BRANCH, bw, tn), lambda i, j: (layer, 0, 0, j))]
    resident = _nbytes((tm, d), BF16) + N_BRANCH * _nbytes((tm, bw), BF16)
    streamed = (N_BRANCH * (_nbytes((tn, d), BF16) + _nbytes((bw, tn), w_branch.dtype))
                + _nbytes((tm, tn), BF16))
    return pl.pallas_call(
        _merge_kernel,
        out_shape=jax.ShapeDtypeStruct((m, d), BF16),
        grid=(m // tm, nj),
        in_specs=in_specs,
        out_specs=pl.BlockSpec((tm, tn), lambda i, j: (i, j)),
        compiler_params=pltpu.CompilerParams(
            dimension_semantics=("parallel", "arbitrary"),
            vmem_limit_bytes=_vmem_limit(streamed, resident)),
        name="merge",
    )(xb, *branches, w_mg_a, w_mg_b, w_branch)


def _ln_kernel(z_ref, g_ref, b_ref, o_ref, ob_ref=None):
    z = z_ref[...]
    mu = jnp.mean(z, axis=-1, keepdims=True)
    zc = z - mu
    var = jnp.mean(zc * zc, axis=-1, keepdims=True)
    y = zc * lax.rsqrt(var + LN_EPS) * g_ref[...] + b_ref[...]
    o_ref[...] = y
    if ob_ref is not None:
        ob_ref[...] = y.astype(BF16)


def _layer_norm(z, g, b, tm, emit_bf16):
    m, d = z.shape
    out_shape = [jax.ShapeDtypeStruct((m, d), F32)]
    out_specs = [pl.BlockSpec((tm, d), lambda i: (i, 0))]
    if emit_bf16:
        out_shape.append(jax.ShapeDtypeStruct((m, d), BF16))
        out_specs.append(pl.BlockSpec((tm, d), lambda i: (i, 0)))
    blocks = 2 * _nbytes((tm, d), F32) + _nbytes((tm, d), BF16)
    res = pl.pallas_call(
        _ln_kernel,
        out_shape=out_shape,
        grid=(m // tm,),
        in_specs=[pl.BlockSpec((tm, d), lambda i: (i, 0)),
                  pl.BlockSpec((1, d), lambda i: (0, 0)),
                  pl.BlockSpec((1, d), lambda i: (0, 0))],
        out_specs=out_specs,
        compiler_params=pltpu.CompilerParams(
            dimension_semantics=("parallel",), vmem_limit_bytes=_vmem_limit(blocks)),
        name="layer_norm",
    )(z, g.reshape(1, d), b.reshape(1, d))
    return res if emit_bf16 else (res[0], None)


POOL_HIST = max(POOL_WINDOWS)
POOL_PAD = 2 * POOL_HIST


def _pool_kernel(u_ref, g_ref, wp_ref, sc_ref, o_ref, buf, lvl_a, lvl_b, *, ts, group):
    i = pl.program_id(1)
    ext = POOL_PAD + ts

    @pl.when(i == 0)
    def _():
        buf[0:POOL_PAD, :] = jnp.zeros((POOL_PAD, buf.shape[1]), F32)

    buf[POOL_PAD:ext, :] = u_ref[...]
    src = buf
    level_refs = []
    for lvl, dst in enumerate((lvl_a, lvl_b, lvl_a, lvl_b)[:len(POOL_WINDOWS)], start=1):
        lag = 2 ** (lvl - 1)
        r0 = SUBLANES * lvl
        c0 = (lvl - 1) * group
        dst[r0:ext, c0:] = src[r0:ext, c0:] + src[r0 - lag:ext - lag, c0:]
        level_refs.append(dst)
        src = dst

    t = i * ts + lax.broadcasted_iota(jnp.int32, (ts, 1), 0)
    outs = []
    for gi, w in enumerate(POOL_WINDOWS):
        assert w == 2 ** (gi + 1)
        cols = slice(gi * group, (gi + 1) * group)
        acc = level_refs[gi][POOL_PAD:ext, cols]
        cnt = jnp.minimum(t + 1, w).astype(F32)
        pooled = acc / cnt - buf[POOL_PAD:ext, cols]
        outs.append(jnp.dot(pooled.astype(BF16), wp_ref[gi], preferred_element_type=F32))
    o = jnp.concatenate(outs, axis=1) * sc_ref[...] * _silu(g_ref[...])
    o_ref[...] = o.astype(o_ref.dtype)
    buf[POOL_PAD - POOL_HIST:POOL_PAD, :] = buf[ext - POOL_HIST:ext, :]


def _split_bf16(x, parts):
    out = []
    r = x
    for _ in range(parts):
        p = r.astype(BF16)
        out.append(p)
        r = r - p.astype(F32)
    return out


LOG2E = math.log2(math.e)


def _gla_cumsum_operator():
    r = np.arange(V7X_MXU_K)
    tri = ((r[:, None] // GLA_CHUNK == r[None, :] // GLA_CHUNK) & (r[None, :] <= r[:, None]))
    return jnp.asarray(tri, BF16)


def _gla_kernel(q_ref, k_ref, v_ref, gg_ref, lr_ref, wup_ref, bg_ref, nrm_ref, tri_ref, wg_ref,
                o_ref, wgs_ref, state_ref, b_ref, *, rows, dk, dv):
    _stage_gate_slab(wg_ref, wgs_ref)
    i = pl.program_id(1)
    C = GLA_CHUNK
    nsub = C // GLA_SUB

    @pl.when(i == 0)
    def _():
        state_ref[...] = jnp.zeros(state_ref.shape, F32)

    lr_h, lr_m = _split_bf16(lr_ref[...], 2)
    w_h, w_m = _split_bf16(wup_ref[...], 2)
    z = (jnp.dot(lr_h, w_h, preferred_element_type=F32)
         + jnp.dot(lr_h, w_m, preferred_element_type=F32)
         + jnp.dot(lr_m, w_h, preferred_element_type=F32)) + bg_ref[...]
    log2_a = (jnp.minimum(z, 0.0) - jnp.log(1.0 + jnp.exp(-jnp.abs(z)))) * (LOG2E / GLA_TAU)

    tb = tri_ref.shape[0]
    parts = _split_bf16(log2_a, 3)
    for r in range(rows // tb):
        acc = None
        for part in parts:
            t = jnp.dot(tri_ref[...], part[r * tb:(r + 1) * tb, :], preferred_element_type=F32)
            acc = t if acc is None else acc + t
        b_ref[r * tb:(r + 1) * tb, :] = acc

    row = lax.broadcasted_iota(jnp.int32, (C, 1), 0)
    col = lax.broadcasted_iota(jnp.int32, (1, C), 1)
    row_sub = _div_pow2(row, GLA_SUB)
    col_sub = _div_pow2(col, GLA_SUB)
    lane_c = lax.broadcasted_iota(jnp.int32, (GLA_SUB, C), 1)
    scale = dk ** -0.5

    def sub_heads(x):
        return jnp.concatenate(
            [jnp.broadcast_to(x[a * GLA_SUB:a * GLA_SUB + 1, :], (GLA_SUB, dk))
             for a in range(nsub)], axis=0)

    def block_diag(tiles):
        n = len(tiles)
        zero = jnp.zeros_like(tiles[0])
        return jnp.concatenate(
            [jnp.concatenate([t if c == r else zero for c in range(n)], axis=1)
             for r, t in enumerate(tiles)], axis=0)

    def chunk_body(c, carry, *, bounded):
        r0 = pl.multiple_of(c * C, C)
        q_ins, states, attns, vs = [], [], [], []
        for h in range(GLA_HEADS):
            kc = slice(h * dk, (h + 1) * dk)
            vc = slice(h * dv, (h + 1) * dv)
            qs = q_ref[pl.ds(r0, C), kc] * scale
            k = k_ref[pl.ds(r0, C), kc]
            v = v_ref[pl.ds(r0, C), vc].astype(BF16)
            b = b_ref[pl.ds(r0, C), kc]
            b_last = b[C - 1:C, :]
            st = state_ref[h]
            q_ins.append((qs * jnp.exp2(b)).astype(BF16))
            states.append(st.astype(BF16))

            q_t = qs * jnp.exp2(b - sub_heads(b))
            lhs = jnp.concatenate(
                [jnp.where(row_sub == a, q_t, 0.0) for a in range(nsub)], axis=1).astype(BF16)
            key_exp = [b[a * GLA_SUB:a * GLA_SUB + 1, :] - b for a in range(nsub)]
            if not bounded:
                key_exp = [jnp.minimum(e, 0.0) for e in key_exp]
            rhs = jnp.concatenate([k * jnp.exp2(e) for e in key_exp], axis=1).astype(BF16)
            a_fac = lax.dot_general(lhs, rhs, NT_DIMS, preferred_element_type=F32)

            if bounded:
                attn = jnp.where(row >= col, a_fac, 0.0)
            else:
                diag_blocks = []
                for a in range(nsub):
                    rs = slice(a * GLA_SUB, (a + 1) * GLA_SUB)
                    qa = qs[rs, :]
                    ba = b[rs, :]
                    blk = jnp.zeros((GLA_SUB, C), F32)
                    for jj in range(GLA_SUB):
                        j = a * GLA_SUB + jj
                        tj = qa * k[j:j + 1, :] * jnp.exp2(ba - b[j:j + 1, :])
                        cj = jnp.sum(tj, axis=1, keepdims=True)
                        blk = jnp.where(lane_c == j, cj, blk)
                    diag_blocks.append(blk)
                a_diag = jnp.concatenate(diag_blocks, axis=0)
                attn = jnp.where(row_sub > col_sub, a_fac,
                                 jnp.where((row_sub == col_sub) & (row >= col), a_diag, 0.0))
            attns.append(attn.astype(BF16))
            vs.append(v)

            k_dec = (k * jnp.exp2(b_last - b)).astype(BF16)
            upd = lax.dot_general(v, k_dec, (((0,), (0,)), ((), ())),
                                  preferred_element_type=F32)
            state_ref[h] = st * jnp.exp2(b_last) + upd

        o_all = lax.dot_general(jnp.concatenate(q_ins, axis=1), block_diag(states), NT_DIMS,
                                preferred_element_type=F32)
        o_all = o_all + jnp.dot(jnp.concatenate(attns, axis=1), block_diag(vs),
                                preferred_element_type=F32)
        for h in range(GLA_HEADS):
            vc = slice(h * dv, (h + 1) * dv)
            o = o_all[:, vc]
            ms = jnp.mean(o * o, axis=-1, keepdims=True)
            on = o * lax.rsqrt(ms + LN_EPS) * nrm_ref[:, vc]
            o_ref[pl.ds(r0, C), vc] = (on * _silu(gg_ref[pl.ds(r0, C), vc])).astype(o_ref.dtype)
        return carry

    bounded = ((-jnp.min(b_ref[...]) < GLA_SAFE_LOG2)
               & (jnp.max(jnp.abs(k_ref[...])) < GLA_SAFE_KEY))

    @pl.when(bounded)
    def _():
        lax.fori_loop(0, rows // C, functools.partial(chunk_body, bounded=True), 0, unroll=2)

    @pl.when(jnp.logical_not(bounded))
    def _():
        lax.fori_loop(0, rows // C, functools.partial(chunk_body, bounded=False), 0)


def _gla(h, w_up_pad, b_gla, gla_norm, bsz, seq, cols, rows,
         w_in_t, layer, gate_row0, gate_branches, gate_tn):
    col_q, col_k, col_v, col_g, col_lr = cols
    dkt = b_gla.shape[-1]
    dvt = gla_norm.shape[-1]
    dk, dv = dkt // GLA_HEADS, dvt // GLA_HEADS
    lrw = w_up_pad.shape[0]
    nblk = seq // rows
    tri = _gla_cumsum_operator()
    assert rows % tri.shape[0] == 0 and tri.shape[0] % GLA_CHUNK == 0
    g_in, g_out, g_shape, g_bytes = _gate_stage_plan(
        w_in_t, layer, gate_row0, gate_branches, bsz * nblk, gate_tn, lambda b, i: b * nblk + i)
    blocks = (2 * _nbytes((rows, dkt), F32) + 2 * _nbytes((rows, dvt), F32)
              + _nbytes((rows, lrw), F32) + _nbytes((rows, dvt), BF16) + g_bytes)
    resident = (_nbytes((rows, dkt), F32) + _nbytes((GLA_HEADS, dv, dk), F32)
                + _nbytes(tri.shape, BF16))
    const = lambda b, i: (0, 0)
    return pl.pallas_call(
        functools.partial(_gla_kernel, rows=rows, dk=dk, dv=dv),
        out_shape=[jax.ShapeDtypeStruct((bsz * seq, dvt), BF16), g_shape],
        grid=(bsz, nblk),
        in_specs=[pl.BlockSpec((rows, dkt), lambda b, i: (b * nblk + i, col_q // dkt)),
                  pl.BlockSpec((rows, dkt), lambda b, i: (b * nblk + i, col_k // dkt)),
                  pl.BlockSpec((rows, dvt), lambda b, i: (b * nblk + i, col_v // dvt)),
                  pl.BlockSpec((rows, dvt), lambda b, i: (b * nblk + i, col_g // dvt)),
                  pl.BlockSpec((rows, lrw), lambda b, i: (b * nblk + i, col_lr // lrw)),
                  pl.BlockSpec((lrw, dkt), const),
                  pl.BlockSpec((1, dkt), const),
                  pl.BlockSpec((1, dvt), const),
                  pl.BlockSpec(tri.shape, const),
                  g_in],
        out_specs=[pl.BlockSpec((rows, dvt), lambda b, i: (b * nblk + i, 0)), g_out],
        scratch_shapes=[pltpu.VMEM((GLA_HEADS, dv, dk), F32),
                        pltpu.VMEM((rows, dkt), F32)],
        compiler_params=pltpu.CompilerParams(
            dimension_semantics=("parallel", "arbitrary"),
            vmem_limit_bytes=_vmem_limit(blocks, resident)),
        name="gla_mixer",
    )(h, h, h, h, h, w_up_pad, b_gla.reshape(1, dkt), gla_norm.reshape(1, dvt), tri, w_in_t)


def _rope_lane_table():
    half = ROPE_DIM // 2
    inv_freq = ROPE_THETA ** (-np.arange(0, ROPE_DIM, 2, dtype=np.float32) / ROPE_DIM)
    lane = np.arange(LANES) % SWA_HEAD_DIM
    tab = np.where(lane < ROPE_DIM, inv_freq[lane % half], 0.0).astype(np.float32)
    return jnp.asarray(tab.reshape(1, LANES))


SWA_ROW_CHUNK = 32
SWA_BLOCKS_PER_STEP = 2


def _swa_kernel(sink_ref, q_ref, k_ref, v_ref, g_ref, pos_ref, inv_ref, wg_ref, o_ref, wgs_ref,
                *scratch, n_pairs, n_sub):
    _stage_gate_slab(wg_ref, wgs_ref)
    step = pl.program_id(1)
    for sb in range(n_sub):
        rows = pl.ds(sb * SWA_BLOCK, SWA_BLOCK)
        _swa_block(step * n_sub + sb, sink_ref, q_ref.at[rows], k_ref.at[rows], v_ref.at[rows],
                   g_ref.at[rows], pos_ref.at[rows], inv_ref, o_ref.at[rows], *scratch,
                   n_pairs=n_pairs)


def _swa_block(n, sink_ref, q_ref, k_ref, v_ref, g_ref, pos_ref, inv_ref, o_ref,
               wk_ref, wv_ref, qr_ref, s_ref, p_ref, es_ref, bias_ref, *, n_pairs):
    blk = SWA_BLOCK
    half = ROPE_DIM // 2
    hd = SWA_HEAD_DIM
    ppg = n_pairs // 2
    rc_rows = SWA_ROW_CHUNK

    @pl.when(n == 0)
    def _():
        wk_ref[...] = jnp.zeros(wk_ref.shape, BF16)
        row = lax.broadcasted_iota(jnp.int32, (4 * blk, 2 * LANES), 0)
        col = lax.broadcasted_iota(jnp.int32, (4 * blk, 2 * LANES), 1)
        ones = ((col >= LANES) & ((col >= LANES + hd) == (row >= 2 * blk))).astype(BF16)
        for g in range(2):
            wv_ref[g] = ones

    lane = lax.broadcasted_iota(jnp.int32, (1, LANES), 1)
    lane_h = _mod_pow2(lane, hd)
    lo = lane < hd
    ang = pos_ref[...] * inv_ref[...]
    cos = jnp.cos(ang)
    sin = jnp.sin(ang)

    def rope(x):
        x_up = pltpu.roll(x, LANES - half, axis=1)
        x_dn = pltpu.roll(x, half, axis=1)
        return x * cos + jnp.where(lane_h < half, -x_up, x_dn) * sin

    for g in range(2):
        for part in range(2):
            base = part * 2 * blk
            wk_ref[g, base:base + blk, :] = wk_ref[g, base + blk:base + 2 * blk, :]
            wv_ref[g, base:base + blk, 0:LANES] = wv_ref[g, base + blk:base + 2 * blk, 0:LANES]
    k_r = rope(k_ref[...])
    v_c = v_ref[...]
    k_sw = pltpu.roll(k_r, hd, axis=1)
    v_sw = pltpu.roll(v_c, hd, axis=1)
    zero = jnp.zeros_like(k_r)
    for g in range(2):
        k_lo, k_hi = (k_r, k_sw) if g == 0 else (k_sw, k_r)
        v_lo, v_hi = (v_c, v_sw) if g == 0 else (v_sw, v_c)
        wk_ref[g, blk:2 * blk, :] = jnp.where(lo, k_lo, zero).astype(BF16)
        wk_ref[g, 3 * blk:4 * blk, :] = jnp.where(lo, zero, k_hi).astype(BF16)
        wv_ref[g, blk:2 * blk, 0:LANES] = jnp.where(lo, v_lo, zero).astype(BF16)
        wv_ref[g, 3 * blk:4 * blk, 0:LANES] = jnp.where(lo, zero, v_hi).astype(BF16)

    r = lax.broadcasted_iota(jnp.int32, (blk, 2 * blk), 0)
    c = lax.broadcasted_iota(jnp.int32, (blk, 2 * blk), 1)
    valid = (c > r) & (c <= r + WINDOW) & ((n > 0) | (c >= blk))
    bias_ref[...] = jnp.where(valid, 0.0, -jnp.inf).astype(F32)

    qscale = hd ** -0.5
    for p in range(n_pairs):
        g, pp = divmod(p, ppg)
        qr_ref[g, pp * blk:(pp + 1) * blk, :] = (
            rope(q_ref[:, p * LANES:(p + 1) * LANES]) * qscale).astype(BF16)

    for g in range(2):
        s_ref[g] = lax.dot_general(qr_ref[g], wk_ref[g], NT_DIMS, preferred_element_type=F32)

    for g in range(2):
        for pp in range(ppg):
            for hh in range(2):
                sink = sink_ref[2 * (g * ppg + pp) + hh]
                kc = slice(hh * 2 * blk, (hh + 1) * 2 * blk)
                for rc in range(blk // rc_rows):
                    rows = slice(pp * blk + rc * rc_rows, pp * blk + (rc + 1) * rc_rows)
                    sh = s_ref[g, rows, kc] + bias_ref[rc * rc_rows:(rc + 1) * rc_rows, :]
                    m = jnp.maximum(jnp.max(sh, axis=-1, keepdims=True), sink)
                    p_ref[g, rows, kc] = jnp.exp(sh - m).astype(BF16)
                    es_ref[g, rows, hh * hd:(hh + 1) * hd] = jnp.broadcast_to(
                        jnp.exp(sink - m), (rc_rows, hd))

    for g in range(2):
        o2 = jnp.dot(p_ref[g], wv_ref[g], preferred_element_type=F32)
        for pp in range(ppg):
            rows = slice(pp * blk, (pp + 1) * blk)
            cols = slice((g * ppg + pp) * LANES, (g * ppg + pp + 1) * LANES)
            den = o2[rows, LANES:] + es_ref[g, rows, :]
            o_ref[:, cols] = (o2[rows, :LANES] / den * _silu(g_ref[:, cols])).astype(o_ref.dtype)


def _swa(h, pos_f, sinks, bsz, seq, cols, w_in_t, layer, gate_row0, gate_branches, gate_tn):
    col_q, col_k, col_v, col_g = cols
    n_heads = sinks.shape[0]
    bw = n_heads * SWA_HEAD_DIM
    n_pairs = bw // LANES
    assert LANES == 2 * SWA_HEAD_DIM and n_pairs % 2 == 0
    n_sub = SWA_BLOCKS_PER_STEP
    blk = n_sub * SWA_BLOCK
    nblk = seq // blk
    scratch = [pltpu.VMEM((2, 4 * SWA_BLOCK, LANES), BF16),
               pltpu.VMEM((2, 4 * SWA_BLOCK, 2 * LANES), BF16),
               pltpu.VMEM((2, 4 * SWA_BLOCK, LANES), BF16),
               pltpu.VMEM((2, 4 * SWA_BLOCK, 4 * SWA_BLOCK), F32),
               pltpu.VMEM((2, 4 * SWA_BLOCK, 4 * SWA_BLOCK), BF16),
               pltpu.VMEM((2, 4 * SWA_BLOCK, LANES), F32),
               pltpu.VMEM((SWA_BLOCK, 2 * SWA_BLOCK), F32)]
    g_in, g_out, g_shape, g_bytes = _gate_stage_plan(
        w_in_t, layer, gate_row0, gate_branches, bsz * nblk, gate_tn, lambda b, i: b * nblk + i)
    blocks = (2 * _nbytes((blk, bw), F32) + 3 * _nbytes((blk, LANES), F32) + _nbytes((blk, bw), BF16)
              + g_bytes)
    resident = sum(_nbytes(sc.shape, sc.dtype) for sc in scratch)

    return pl.pallas_call(
        functools.partial(_swa_kernel, n_pairs=n_pairs, n_sub=n_sub),
        out_shape=[jax.ShapeDtypeStruct((bsz * seq, bw), BF16), g_shape],
        grid=(bsz, nblk),
        in_specs=[pl.BlockSpec(memory_space=pltpu.SMEM),
                  pl.BlockSpec((blk, bw), lambda b, i: (b * nblk + i, col_q // bw)),
                  pl.BlockSpec((blk, LANES), lambda b, i: (b * nblk + i, col_k // LANES)),
                  pl.BlockSpec((blk, LANES), lambda b, i: (b * nblk + i, col_v // LANES)),
                  pl.BlockSpec((blk, bw), lambda b, i: (b * nblk + i, col_g // bw)),
                  pl.BlockSpec((blk, 1), lambda b, i: (b * nblk + i, 0)),
                  pl.BlockSpec((1, LANES), lambda b, i: (0, 0)),
                  g_in],
        out_specs=[pl.BlockSpec((blk, bw), lambda b, i: (b * nblk + i, 0)), g_out],
        scratch_shapes=scratch,
        compiler_params=pltpu.CompilerParams(
            dimension_semantics=("parallel", "arbitrary"),
            vmem_limit_bytes=_vmem_limit(blocks, resident)),
        name="swa_mixer",
    )(sinks, h, h, h, h, pos_f, _rope_lane_table(), w_in_t)


def _mem_kernel(q_ref, g_ref, mk_ref, mv_ref, o_ref, *, hd):
    scale = hd ** -0.5
    for h in range(XA_HEADS):
        cols = slice(h * hd, (h + 1) * hd)
        qh = q_ref[:, cols].astype(BF16)
        s = lax.dot_general(qh, mk_ref[:, cols], NT_DIMS, preferred_element_type=F32) * scale
        m = jnp.max(s, axis=-1, keepdims=True)
        e = jnp.exp(s - m)
        p = e / jnp.sum(e, axis=-1, keepdims=True)
        o = jnp.dot(p.astype(BF16), mv_ref[:, cols], preferred_element_type=F32)
        o_ref[:, cols] = (o * _silu(g_ref[:, cols])).astype(o_ref.dtype)


def _pool_mem_kernel(u_ref, pg_ref, wp_ref, sc_ref, q_ref, qg_ref, mk_ref, mv_ref,
                     op_ref, om_ref, buf, lvl_a, lvl_b, *, ts, group, hd):
    _pool_kernel(u_ref, pg_ref, wp_ref, sc_ref, op_ref, buf, lvl_a, lvl_b, ts=ts, group=group)
    _mem_kernel(q_ref, qg_ref, mk_ref, mv_ref, om_ref, hd=hd)


def _pool_mem(h, w_pool_b, pool_scale, mkv, bsz, seq, mem_len, pool_cols, mem_cols, ts):
    col_u, col_pg = pool_cols
    col_q, col_qg = mem_cols
    bw = pool_scale.shape[-1]
    assert mkv.shape[1] == 2 * bw
    group = bw // len(POOL_WINDOWS)
    hd = bw // XA_HEADS
    nblk = seq // ts
    blocks = 4 * _nbytes((ts, bw), F32) + 2 * _nbytes((ts, bw), BF16) + 2 * _nbytes((mem_len, bw), BF16)
    scratch = [pltpu.VMEM((ts + POOL_PAD, bw), F32) for _ in range(3)]
    resident = sum(_nbytes(sc.shape, sc.dtype) for sc in scratch)
    row = lambda b, i: b * nblk + i
    return pl.pallas_call(
        functools.partial(_pool_mem_kernel, ts=ts, group=group, hd=hd),
        out_shape=[jax.ShapeDtypeStruct((bsz * seq, bw), BF16)] * 2,
        grid=(bsz, nblk),
        in_specs=[pl.BlockSpec((ts, bw), lambda b, i: (row(b, i), col_u // bw)),
                  pl.BlockSpec((ts, bw), lambda b, i: (row(b, i), col_pg // bw)),
                  pl.BlockSpec(w_pool_b.shape, lambda b, i: (0, 0, 0)),
                  pl.BlockSpec((1, bw), lambda b, i: (0, 0)),
                  pl.BlockSpec((ts, bw), lambda b, i: (row(b, i), col_q // bw)),
                  pl.BlockSpec((ts, bw), lambda b, i: (row(b, i), col_qg // bw)),
                  pl.BlockSpec((mem_len, bw), lambda b, i: (b, 0)),
                  pl.BlockSpec((mem_len, bw), lambda b, i: (b, 1))],
        out_specs=[pl.BlockSpec((ts, bw), lambda b, i: (row(b, i), 0))] * 2,
        scratch_shapes=scratch,
        compiler_params=pltpu.CompilerParams(
            dimension_semantics=("parallel", "arbitrary"),
            vmem_limit_bytes=_vmem_limit(blocks, resident)),
        name="pool_mem_mixers",
    )(h, h, w_pool_b, pool_scale.reshape(1, bw), h, h, mkv, mkv)


IN_TN = 2 * LANES
MERGE_TN = 2 * LANES


def _main_layout(d_model):
    bw = d_model // 4
    dkt, dvt = bw // 2, bw
    kvw = 2 * SWA_HEAD_DIM
    names = ["pool_u", "pool_g", "gq", "gk", "gv", "gg", "glr", "sq", "sk", "sv", "sg", "xq", "xg"]
    widths = [bw, bw, dkt, dkt, dvt, dvt, GLA_RANK, bw, kvw, kvw, bw, bw, bw]
    src = {}
    off = 0
    for nme, w in zip(names, widths):
        src[nme] = (off, w)
        off += w
    n_main = off
    order = ["pool_u", "pool_g", "gv", "gg", "sq", "sg", "xq", "xg", "gq", "gk", "sk", "sv", "glr"]
    dst = {}
    cols = []
    off = 0
    for nme in order:
        o, w = src[nme]
        w_dst = w if nme != "glr" else IN_TN
        assert off % w_dst == 0
        dst[nme] = off
        cols.append(o + np.arange(w_dst))
        off += w_dst
    cols = np.concatenate(cols)
    assert off % IN_TN == 0 and cols.max() < n_main
    starts = cols[::IN_TN]
    assert np.all(cols.reshape(-1, IN_TN) == starts[:, None] + np.arange(IN_TN))
    return starts, dst, n_main


def _hybrid_layer(l, x, xb, mem_b, pos_f, w_in_t, w_pool, pool_scale, w_gla_up, b_gla, gla_norm,
                  sinks, w_mem_kv, w_branch, w_out, ln_g, ln_b, alpha, emit_bf16, bsz, seq):
    m, d = x.shape
    col_starts, dst, n_main = _main_layout(d)
    w_up_pad = jnp.concatenate(
        [w_gla_up[l], jnp.zeros((IN_TN - GLA_RANK, w_gla_up.shape[-1]), w_gla_up.dtype)], axis=0)

    h, w_out_b = _in_proj(xb, w_in_t, w_out, l, col_starts, tm=2048, tn=IN_TN, per_step=2)
    mkv = _matmul(mem_b, w_mem_kv, l, BF16, tm=mem_b.shape[0], tn=512, name="mem_kv")

    o_pool, o_mem = _pool_mem(h, w_pool[l].astype(BF16), pool_scale[l], mkv, bsz, seq,
                              mem_b.shape[0] // bsz, (dst["pool_u"], dst["pool_g"]),
                              (dst["xq"], dst["xg"]), ts=512)
    half = N_BRANCH // 2
    o_gla, w_mg_a = _gla(h, w_up_pad, b_gla[l], gla_norm[l], bsz, seq,
                         (dst["gq"], dst["gk"], dst["gv"], dst["gg"], dst["glr"]), 512,
                         w_in_t, l, n_main, half, MERGE_TN)
    o_swa, w_mg_b = _swa(h, pos_f, sinks[l], bsz, seq, (dst["sq"], dst["sk"], dst["sv"], dst["sg"]),
                         w_in_t, l, n_main + half * d, half, MERGE_TN)

    y = _merge(xb, (o_pool, o_gla, o_swa, o_mem), w_mg_a, w_mg_b, w_branch, l, tm=1024, tn=MERGE_TN)
    z = _out_proj(y, w_out_b, x, alpha, tm=1024, tn=256, per_step=4)
    return _layer_norm(z, ln_g[l], ln_b[l], tm=512, emit_bf16=emit_bf16)


def kernel(x, mem, positions, w_in, w_pool, pool_scale, w_gla_up, b_gla, gla_norm, sinks,
           w_mem_kv, w_branch, w_out, ln_g, ln_b):
    bsz, seq, d = x.shape
    depth = w_in.shape[0]
    alpha = (2 * depth) ** 0.25
    xf = x.reshape(bsz * seq, d)
    xb = xf.astype(BF16)
    mem_b = mem.reshape(bsz * mem.shape[1], d).astype(BF16)
    pos_f = positions.astype(F32).reshape(bsz * seq, 1)
    w_in_t = jnp.swapaxes(w_in, 1, 2)
    for l in range(depth):
        xf, xb = _hybrid_layer(
            l, xf, xb, mem_b, pos_f, w_in_t, w_pool, pool_scale, w_gla_up, b_gla,
            gla_norm, sinks, w_mem_kv, w_branch, w_out, ln_g, ln_b,
            alpha, l + 1 < depth, bsz, seq)
    return xf.reshape(bsz, seq, d)
```

```python
import functools
import math

import jax
import jax.numpy as jnp
import numpy as np
from jax import lax
from jax.experimental import pallas as pl
from jax.experimental.pallas import tpu as pltpu

F32 = jnp.float32
BF16 = jnp.bfloat16

N_BRANCH = 4
POOL_WINDOWS = (2, 4, 8, 16)
GLA_HEADS = 4
GLA_RANK = 16
GLA_TAU = 16.0
GLA_CHUNK = 64
GLA_SUB = 16
GLA_SAFE_LOG2 = 24.0
GLA_SAFE_KEY = 2.0 ** 100
SWA_HEAD_DIM = 64
WINDOW = 128
SWA_BLOCK = 128
ROPE_THETA = 500000.0
ROPE_DIM = SWA_HEAD_DIM // 4
XA_HEADS = 4
LN_EPS = 1e-5

LANES = 128
SUBLANES = 8
V7X_MXU_K = 256
V7X_VMEM_BYTES = 64 * 1024 * 1024
VMEM_CAP = V7X_VMEM_BYTES - 8 * 1024 * 1024
VMEM_TEMP_BYTES = 12 * 1024 * 1024


def _vmem_limit(streamed_bytes, resident_bytes=0):
    return int(min(VMEM_CAP, 2 * streamed_bytes + resident_bytes + VMEM_TEMP_BYTES))


def _nbytes(shape, dtype):
    return int(np.prod(shape)) * jnp.dtype(dtype).itemsize


def _silu(g):
    return g * jax.nn.sigmoid(g)


def _div_pow2(x, n):
    assert n & (n - 1) == 0
    return x >> (n.bit_length() - 1)


def _mod_pow2(x, n):
    assert n & (n - 1) == 0
    return x & (n - 1)


NT_DIMS = (((1,), (1,)), ((), ()))


def _mm_kernel(a_ref, w_ref, o_ref):
    w = w_ref[...].astype(BF16)
    o_ref[...] = jnp.dot(a_ref[...], w, preferred_element_type=F32).astype(o_ref.dtype)


def _matmul(a, w, layer, out_dtype, tm, tn, name):
    m, k = a.shape
    n = w.shape[-1]
    assert m % tm == 0 and n % tn == 0
    resident = _nbytes((tm, k), a.dtype)
    streamed = _nbytes((k, tn), w.dtype) + _nbytes((tm, tn), out_dtype)
    return pl.pallas_call(
        _mm_kernel,
        out_shape=jax.ShapeDtypeStruct((m, n), out_dtype),
        grid=(m // tm, n // tn),
        in_specs=[pl.BlockSpec((tm, k), lambda i, j: (i, 0), pipeline_mode=pl.Buffered(1)),
                  pl.BlockSpec((None, k, tn), lambda i, j: (layer, 0, j))],
        out_specs=pl.BlockSpec((tm, tn), lambda i, j: (i, j)),
        compiler_params=pltpu.CompilerParams(
            dimension_semantics=("parallel", "arbitrary"),
            vmem_limit_bytes=_vmem_limit(streamed, resident)),
        name=name,
    )(a, w)


def _out_proj_kernel(y_ref, w_ref, x_ref, o_ref, *, alpha, tn):
    y = y_ref[...]
    for t in range(o_ref.shape[1] // tn):
        cols = slice(t * tn, (t + 1) * tn)
        o_ref[:, cols] = alpha * x_ref[:, cols] + jnp.dot(
            y, w_ref[:, cols], preferred_element_type=F32)


def _out_proj(y, w_out_b, x, alpha, tm, tn, per_step):
    m, k = y.shape
    n = w_out_b.shape[1]
    tb = per_step * tn
    assert m % tm == 0 and n % tb == 0
    streamed = (_nbytes((tm, k), y.dtype) + _nbytes((k, tb), w_out_b.dtype)
                + 2 * _nbytes((tm, tb), F32))
    return pl.pallas_call(
        functools.partial(_out_proj_kernel, alpha=alpha, tn=tn),
        out_shape=jax.ShapeDtypeStruct((m, n), F32),
        grid=(m // tm, n // tb),
        in_specs=[pl.BlockSpec((tm, k), lambda i, j: (i, 0)),
                  pl.BlockSpec((k, tb), lambda i, j: (0, j)),
                  pl.BlockSpec((tm, tb), lambda i, j: (i, j))],
        out_specs=pl.BlockSpec((tm, tb), lambda i, j: (i, j)),
        compiler_params=pltpu.CompilerParams(
            dimension_semantics=("parallel", "arbitrary"),
            vmem_limit_bytes=_vmem_limit(streamed)),
        name="out_proj",
    )(y, w_out_b, x)


def _in_proj_kernel(tab_ref, x_ref, *refs, per_step):
    w_refs, wo_ref, o_ref, wos_ref = refs[:per_step], refs[per_step], refs[-2], refs[-1]
    wos_ref[...] = wo_ref[...].astype(wos_ref.dtype)
    x = x_ref[...]
    tn = w_refs[0].shape[1]
    for t, w_ref in enumerate(w_refs):
        w = w_ref[0].astype(BF16)
        o_ref[:, t * tn:(t + 1) * tn] = lax.dot_general(x, w, NT_DIMS, preferred_element_type=F32)


def _in_proj(xb, w_in_t, w_out, layer, col_starts, tm, tn, per_step):
    m, k = xb.shape
    nblk = len(col_starts)
    assert m % tm == 0 and nblk % per_step == 0 and all(int(s) % SUBLANES == 0 for s in col_starts)
    nj = nblk // per_step
    d_out, n_out = w_out.shape[1:]
    sr = 4 * 2 * SUBLANES
    n_slabs = d_out // sr
    assert d_out % sr == 0 and n_slabs <= (m // tm) * nj
    resident = _nbytes((tm, k), xb.dtype)
    streamed = (per_step * (_nbytes((tn, k), w_in_t.dtype) + _nbytes((tm, tn), F32))
                + _nbytes((sr, n_out), w_out.dtype) + _nbytes((sr, n_out), BF16))

    def window(t):
        return pl.BlockSpec(
            (pl.Element(1), pl.Element(tn), pl.Element(k)),
            lambda i, j, tab: (layer, pl.multiple_of(tab[j * per_step + t], SUBLANES), 0))

    def slab(i, j):
        return jnp.minimum(i * nj + j, n_slabs - 1)

    return pl.pallas_call(
        functools.partial(_in_proj_kernel, per_step=per_step),
        out_shape=[jax.ShapeDtypeStruct((m, nblk * tn), F32),
                   jax.ShapeDtypeStruct((d_out, n_out), BF16)],
        grid_spec=pltpu.PrefetchScalarGridSpec(
            num_scalar_prefetch=1,
            grid=(m // tm, nj),
            in_specs=[pl.BlockSpec((tm, k), lambda i, j, tab: (i, 0), pipeline_mode=pl.Buffered(1))]
            + [window(t) for t in range(per_step)]
            + [pl.BlockSpec((None, sr, n_out), lambda i, j, tab: (layer, slab(i, j), 0))],
            out_specs=[pl.BlockSpec((tm, per_step * tn), lambda i, j, tab: (i, j)),
                       pl.BlockSpec((sr, n_out), lambda i, j, tab: (slab(i, j), 0))]),
        compiler_params=pltpu.CompilerParams(
            dimension_semantics=("arbitrary", "arbitrary"),
            vmem_limit_bytes=_vmem_limit(streamed, resident)),
        name="in_proj",
    )(jnp.asarray(np.asarray(col_starts, np.int32)), xb, *([w_in_t] * per_step), w_out)


def _gate_stage_plan(w_in_t, layer, row0, n_branches, n_steps, gate_tn, step_of):
    k = w_in_t.shape[-1]
    rows = n_branches * k
    slab = rows // n_steps
    per_slab = slab // gate_tn
    slabs_per_branch = k // slab
    assert rows % n_steps == 0 and slab % gate_tn == 0 and k % slab == 0 and row0 % SUBLANES == 0
    in_spec = pl.BlockSpec(
        (pl.Element(1), pl.Element(slab), pl.Element(k)),
        lambda *ids: (layer, pl.multiple_of(row0 + step_of(*ids) * slab, SUBLANES), 0))
    out_spec = pl.BlockSpec(
        (per_slab, gate_tn, k),
        lambda *ids: (step_of(*ids) % slabs_per_branch, step_of(*ids) // slabs_per_branch, 0))
    out_shape = jax.ShapeDtypeStruct((k // gate_tn, n_branches * gate_tn, k), BF16)
    return in_spec, out_spec, out_shape, _nbytes((slab, k), w_in_t.dtype) + _nbytes((slab, k), BF16)


def _stage_gate_slab(src_ref, dst_ref):
    rows = dst_ref.shape[1]
    for t in range(dst_ref.shape[0]):
        dst_ref[t] = src_ref[0, t * rows:(t + 1) * rows, :].astype(dst_ref.dtype)


def _merge_kernel(x_ref, o0, o1, o2, o3, ga_ref, gb_ref, w_ref, y_ref):
    x = x_ref[...]
    tn = y_ref.shape[1]
    half = N_BRANCH // 2
    acc = None
    for b, o_ref in enumerate((o0, o1, o2, o3)):
        g_ref, bl = (ga_ref, b) if b < half else (gb_ref, b - half)
        gate = lax.dot_general(x, g_ref[bl * tn:(bl + 1) * tn, :], NT_DIMS, preferred_element_type=F32)
        proj = jnp.dot(o_ref[...], w_ref[b].astype(BF16), preferred_element_type=F32)
        term = jax.nn.sigmoid(gate) * proj
        acc = term if acc is None else acc + term
    y_ref[...] = acc.astype(y_ref.dtype)


def _merge(xb, branches, w_mg_a, w_mg_b, w_branch, layer, tm, tn):
    m, d = xb.shape
    bw = branches[0].shape[1]
    nj = d // tn
    in_specs = [pl.BlockSpec((tm, d), lambda i, j: (i, 0), pipeline_mode=pl.Buffered(1))]
    in_specs += [pl.BlockSpec((tm, bw), lambda i, j: (i, 0), pipeline_mode=pl.Buffered(1))
                 for _ in range(N_BRANCH)]
    in_specs += [pl.BlockSpec((None, N_BRANCH // 2 * tn, d), lambda i, j: (j, 0, 0)),
                 pl.BlockSpec((None, N_BRANCH // 2 * tn, d), lambda i, j: (j, 0, 0)),
                 pl.BlockSpec((None, N_BRANCH, bw, tn), lambda i, j: (layer, 0, 0, j))]
    resident = _nbytes((tm, d), BF16) + N_BRANCH * _nbytes((tm, bw), BF16)
    streamed = (N_BRANCH * (_nbytes((tn, d), BF16) + _nbytes((bw, tn), w_branch.dtype))
                + _nbytes((tm, tn), BF16))
    return pl.pallas_call(
        _merge_kernel,
        out_shape=jax.ShapeDtypeStruct((m, d), BF16),
        grid=(m // tm, nj),
        in_specs=in_specs,
        out_specs=pl.BlockSpec((tm, tn), lambda i, j: (i, j)),
        compiler_params=pltpu.CompilerParams(
            dimension_semantics=("parallel", "arbitrary"),
            vmem_limit_bytes=_vmem_limit(streamed, resident)),
        name="merge",
    )(xb, *branches, w_mg_a, w_mg_b, w_branch)


def _ln_kernel(z_ref, g_ref, b_ref, o_ref, ob_ref=None):
    z = z_ref[...]
    mu = jnp.mean(z, axis=-1, keepdims=True)
    zc = z - mu
    var = jnp.mean(zc * zc, axis=-1, keepdims=True)
    y = zc * lax.rsqrt(var + LN_EPS) * g_ref[...] + b_ref[...]
    o_ref[...] = y
    if ob_ref is not None:
        ob_ref[...] = y.astype(BF16)


def _layer_norm(z, g, b, tm, emit_bf16):
    m, d = z.shape
    out_shape = [jax.ShapeDtypeStruct((m, d), F32)]
    out_specs = [pl.BlockSpec((tm, d), lambda i: (i, 0))]
    if emit_bf16:
        out_shape.append(jax.ShapeDtypeStruct((m, d), BF16))
        out_specs.append(pl.BlockSpec((tm, d), lambda i: (i, 0)))
    blocks = 2 * _nbytes((tm, d), F32) + _nbytes((tm, d), BF16)
    res = pl.pallas_call(
        _ln_kernel,
        out_shape=out_shape,
        grid=(m // tm,),
        in_specs=[pl.BlockSpec((tm, d), lambda i: (i, 0)),
                  pl.BlockSpec((1, d), lambda i: (0, 0)),
                  pl.BlockSpec((1, d), lambda i: (0, 0))],
        out_specs=out_specs,
        compiler_params=pltpu.CompilerParams(
            dimension_semantics=("parallel",), vmem_limit_bytes=_vmem_limit(blocks)),
        name="layer_norm",
    )(z, g.reshape(1, d), b.reshape(1, d))
    return res if emit_bf16 else (res[0], None)


POOL_HIST = max(POOL_WINDOWS)
POOL_PAD = 2 * POOL_HIST


def _pool_kernel(u_ref, g_ref, wp_ref, sc_ref, o_ref, buf, lvl_a, lvl_b, *, ts, group):
    i = pl.program_id(1)
    ext = POOL_PAD + ts

    @pl.when(i == 0)
    def _():
        buf[0:POOL_PAD, :] = jnp.zeros((POOL_PAD, buf.shape[1]), F32)

    buf[POOL_PAD:ext, :] = u_ref[...]
    src = buf
    level_refs = []
    for lvl, dst in enumerate((lvl_a, lvl_b, lvl_a, lvl_b)[:len(POOL_WINDOWS)], start=1):
        lag = 2 ** (lvl - 1)
        r0 = SUBLANES * lvl
        c0 = (lvl - 1) * group
        dst[r0:ext, c0:] = src[r0:ext, c0:] + src[r0 - lag:ext - lag, c0:]
        level_refs.append(dst)
        src = dst

    t = i * ts + lax.broadcasted_iota(jnp.int32, (ts, 1), 0)
    outs = []
    for gi, w in enumerate(POOL_WINDOWS):
        assert w == 2 ** (gi + 1)
        cols = slice(gi * group, (gi + 1) * group)
        acc = level_refs[gi][POOL_PAD:ext, cols]
        cnt = jnp.minimum(t + 1, w).astype(F32)
        pooled = acc / cnt - buf[POOL_PAD:ext, cols]
        outs.append(jnp.dot(pooled.astype(BF16), wp_ref[gi], preferred_element_type=F32))
    o = jnp.concatenate(outs, axis=1) * sc_ref[...] * _silu(g_ref[...])
    o_ref[...] = o.astype(o_ref.dtype)
    buf[POOL_PAD - POOL_HIST:POOL_PAD, :] = buf[ext - POOL_HIST:ext, :]


def _split_bf16(x, parts):
    out = []
    r = x
    for _ in range(parts):
        p = r.astype(BF16)
        out.append(p)
        r = r - p.astype(F32)
    return out


LOG2E = math.log2(math.e)


def _gla_cumsum_operator():
    r = np.arange(V7X_MXU_K)
    tri = ((r[:, None] // GLA_CHUNK == r[None, :] // GLA_CHUNK) & (r[None, :] <= r[:, None]))
    return jnp.asarray(tri, BF16)


def _gla_kernel(q_ref, k_ref, v_ref, gg_ref, lr_ref, wup_ref, bg_ref, nrm_ref, tri_ref, wg_ref,
                o_ref, wgs_ref, state_ref, b_ref, *, rows, dk, dv):
    _stage_gate_slab(wg_ref, wgs_ref)
    i = pl.program_id(1)
    C = GLA_CHUNK
    nsub = C // GLA_SUB

    @pl.when(i == 0)
    def _():
        state_ref[...] = jnp.zeros(state_ref.shape, F32)

    lr_h, lr_m = _split_bf16(lr_ref[...], 2)
    w_h, w_m = _split_bf16(wup_ref[...], 2)
    z = jnp.dot(jnp.concatenate([lr_h, lr_h, lr_m], axis=1),
                jnp.concatenate([w_h, w_m, w_h], axis=0), preferred_element_type=F32) + bg_ref[...]
    log2_a = (jnp.minimum(z, 0.0) - jnp.log(1.0 + jnp.exp(-jnp.abs(z)))) * (LOG2E / GLA_TAU)

    tb = tri_ref.shape[0]
    parts = _split_bf16(log2_a, 3)
    tri3 = jnp.concatenate([tri_ref[...]] * len(parts), axis=1)
    for r in range(rows // tb):
        stacked = jnp.concatenate([part[r * tb:(r + 1) * tb, :] for part in parts], axis=0)
        b_ref[r * tb:(r + 1) * tb, :] = jnp.dot(tri3, stacked, preferred_element_type=F32)

    row = lax.broadcasted_iota(jnp.int32, (C, 1), 0)
    col = lax.broadcasted_iota(jnp.int32, (1, C), 1)
    row_sub = _div_pow2(row, GLA_SUB)
    col_sub = _div_pow2(col, GLA_SUB)
    lane_c = lax.broadcasted_iota(jnp.int32, (GLA_SUB, C), 1)
    scale = dk ** -0.5

    def sub_heads(x):
        return jnp.concatenate(
            [jnp.broadcast_to(x[a * GLA_SUB:a * GLA_SUB + 1, :], (GLA_SUB, dk))
             for a in range(nsub)], axis=0)

    def block_diag(tiles):
        n = len(tiles)
        zero = jnp.zeros_like(tiles[0])
        return jnp.concatenate(
            [jnp.concatenate([t if c == r else zero for c in range(n)], axis=1)
             for r, t in enumerate(tiles)], axis=0)

    def chunk_body(c, carry, *, bounded):
        r0 = pl.multiple_of(c * C, C)
        q_ins, states, attns, vs = [], [], [], []
        for h in range(GLA_HEADS):
            kc = slice(h * dk, (h + 1) * dk)
            vc = slice(h * dv, (h + 1) * dv)
            qs = q_ref[pl.ds(r0, C), kc] * scale
            k = k_ref[pl.ds(r0, C), kc]
            v = v_ref[pl.ds(r0, C), vc].astype(BF16)
            b = b_ref[pl.ds(r0, C), kc]
            b_last = b[C - 1:C, :]
            st = state_ref[h]
            q_ins.append((qs * jnp.exp2(b)).astype(BF16))
            states.append(st.astype(BF16))

            q_t = qs * jnp.exp2(b - sub_heads(b))
            lhs = jnp.concatenate(
                [jnp.where(row_sub == a, q_t, 0.0) for a in range(nsub)], axis=1).astype(BF16)
            key_exp = [b[a * GLA_SUB:a * GLA_SUB + 1, :] - b for a in range(nsub)]
            if not bounded:
                key_exp = [jnp.minimum(e, 0.0) for e in key_exp]
            rhs = jnp.concatenate([k * jnp.exp2(e) for e in key_exp], axis=1).astype(BF16)
            a_fac = lax.dot_general(lhs, rhs, NT_DIMS, preferred_element_type=F32)

            if bounded:
                attn = jnp.where(row >= col, a_fac, 0.0)
            else:
                diag_blocks = []
                for a in range(nsub):
                    rs = slice(a * GLA_SUB, (a + 1) * GLA_SUB)
                    qa = qs[rs, :]
                    ba = b[rs, :]
                    blk = jnp.zeros((GLA_SUB, C), F32)
                    for jj in range(GLA_SUB):
                        j = a * GLA_SUB + jj
                        tj = qa * k[j:j + 1, :] * jnp.exp2(ba - b[j:j + 1, :])
                        cj = jnp.sum(tj, axis=1, keepdims=True)
                        blk = jnp.where(lane_c == j, cj, blk)
                    diag_blocks.append(blk)
                a_diag = jnp.concatenate(diag_blocks, axis=0)
                attn = jnp.where(row_sub > col_sub, a_fac,
                                 jnp.where((row_sub == col_sub) & (row >= col), a_diag, 0.0))
            attns.append(attn.astype(BF16))
            vs.append(v)

            k_dec = (k * jnp.exp2(b_last - b)).astype(BF16)
            upd = lax.dot_general(v, k_dec, (((0,), (0,)), ((), ())),
                                  preferred_element_type=F32)
            state_ref[h] = st * jnp.exp2(b_last) + upd

        o_all = lax.dot_general(jnp.concatenate(q_ins, axis=1), block_diag(states), NT_DIMS,
                                preferred_element_type=F32)
        o_all = o_all + jnp.dot(jnp.concatenate(attns, axis=1), block_diag(vs),
                                preferred_element_type=F32)
        for h in range(GLA_HEADS):
            vc = slice(h * dv, (h + 1) * dv)
            o = o_all[:, vc]
            ms = jnp.mean(o * o, axis=-1, keepdims=True)
            on = o * lax.rsqrt(ms + LN_EPS) * nrm_ref[:, vc]
            o_ref[pl.ds(r0, C), vc] = (on * _silu(gg_ref[pl.ds(r0, C), vc])).astype(o_ref.dtype)
        return carry

    bounded = ((-jnp.min(b_ref[...]) < GLA_SAFE_LOG2)
               & (jnp.max(jnp.abs(k_ref[...])) < GLA_SAFE_KEY))

    @pl.when(bounded)
    def _():
        lax.fori_loop(0, rows // C, functools.partial(chunk_body, bounded=True), 0, unroll=4)

    @pl.when(jnp.logical_not(bounded))
    def _():
        lax.fori_loop(0, rows // C, functools.partial(chunk_body, bounded=False), 0)


def _gla(h, w_up_pad, b_gla, gla_norm, bsz, seq, cols, rows,
         w_in_t, layer, gate_row0, gate_branches, gate_tn):
    col_q, col_k, col_v, col_g, col_lr = cols
    dkt = b_gla.shape[-1]
    dvt = gla_norm.shape[-1]
    dk, dv = dkt // GLA_HEADS, dvt // GLA_HEADS
    lrw = w_up_pad.shape[0]
    nblk = seq // rows
    tri = _gla_cumsum_operator()
    assert rows % tri.shape[0] == 0 and tri.shape[0] % GLA_CHUNK == 0
    g_in, g_out, g_shape, g_bytes = _gate_stage_plan(
        w_in_t, layer, gate_row0, gate_branches, bsz * nblk, gate_tn, lambda b, i: b * nblk + i)
    blocks = (2 * _nbytes((rows, dkt), F32) + 2 * _nbytes((rows, dvt), F32)
              + _nbytes((rows, lrw), F32) + _nbytes((rows, dvt), BF16) + g_bytes)
    resident = (_nbytes((rows, dkt), F32) + _nbytes((GLA_HEADS, dv, dk), F32)
                + _nbytes(tri.shape, BF16))
    const = lambda b, i: (0, 0)
    return pl.pallas_call(
        functools.partial(_gla_kernel, rows=rows, dk=dk, dv=dv),
        out_shape=[jax.ShapeDtypeStruct((bsz * seq, dvt), BF16), g_shape],
        grid=(bsz, nblk),
        in_specs=[pl.BlockSpec((rows, dkt), lambda b, i: (b * nblk + i, col_q // dkt)),
                  pl.BlockSpec((rows, dkt), lambda b, i: (b * nblk + i, col_k // dkt)),
                  pl.BlockSpec((rows, dvt), lambda b, i: (b * nblk + i, col_v // dvt)),
                  pl.BlockSpec((rows, dvt), lambda b, i: (b * nblk + i, col_g // dvt)),
                  pl.BlockSpec((rows, lrw), lambda b, i: (b * nblk + i, col_lr // lrw)),
                  pl.BlockSpec((lrw, dkt), const),
                  pl.BlockSpec((1, dkt), const),
                  pl.BlockSpec((1, dvt), const),
                  pl.BlockSpec(tri.shape, const),
                  g_in],
        out_specs=[pl.BlockSpec((rows, dvt), lambda b, i: (b * nblk + i, 0)), g_out],
        scratch_shapes=[pltpu.VMEM((GLA_HEADS, dv, dk), F32),
                        pltpu.VMEM((rows, dkt), F32)],
        compiler_params=pltpu.CompilerParams(
            dimension_semantics=("parallel", "arbitrary"),
            vmem_limit_bytes=_vmem_limit(blocks, resident)),
        name="gla_mixer",
    )(h, h, h, h, h, w_up_pad, b_gla.reshape(1, dkt), gla_norm.reshape(1, dvt), tri, w_in_t)


def _rope_lane_table():
    half = ROPE_DIM // 2
    inv_freq = ROPE_THETA ** (-np.arange(0, ROPE_DIM, 2, dtype=np.float32) / ROPE_DIM)
    lane = np.arange(LANES) % SWA_HEAD_DIM
    tab = np.where(lane < ROPE_DIM, inv_freq[lane % half], 0.0).astype(np.float32)
    return jnp.asarray(tab.reshape(1, LANES))


SWA_ROW_CHUNK = 32
SWA_BLOCKS_PER_STEP = 2


def _swa_kernel(sink_ref, q_ref, k_ref, v_ref, g_ref, pos_ref, inv_ref, wg_ref, o_ref, wgs_ref,
                *scratch, n_pairs, n_sub):
    _stage_gate_slab(wg_ref, wgs_ref)
    step = pl.program_id(1)
    for sb in range(n_sub):
        rows = pl.ds(sb * SWA_BLOCK, SWA_BLOCK)
        _swa_block(step * n_sub + sb, sink_ref, q_ref.at[rows], k_ref.at[rows], v_ref.at[rows],
                   g_ref.at[rows], pos_ref.at[rows], inv_ref, o_ref.at[rows], *scratch,
                   n_pairs=n_pairs)


def _swa_block(n, sink_ref, q_ref, k_ref, v_ref, g_ref, pos_ref, inv_ref, o_ref,
               wk_ref, wv_ref, qr_ref, s_ref, p_ref, es_ref, bias_ref, *, n_pairs):
    blk = SWA_BLOCK
    half = ROPE_DIM // 2
    hd = SWA_HEAD_DIM
    ppg = n_pairs // 2
    rc_rows = SWA_ROW_CHUNK

    @pl.when(n == 0)
    def _():
        wk_ref[...] = jnp.zeros(wk_ref.shape, BF16)
        row = lax.broadcasted_iota(jnp.int32, (4 * blk, 2 * LANES), 0)
        col = lax.broadcasted_iota(jnp.int32, (4 * blk, 2 * LANES), 1)
        ones = ((col >= LANES) & ((col >= LANES + hd) == (row >= 2 * blk))).astype(BF16)
        for g in range(2):
            wv_ref[g] = ones

    lane = lax.broadcasted_iota(jnp.int32, (1, LANES), 1)
    lane_h = _mod_pow2(lane, hd)
    lo = lane < hd
    ang = pos_ref[...] * inv_ref[...]
    cos = jnp.cos(ang)
    sin = jnp.sin(ang)

    def rope(x):
        x_up = pltpu.roll(x, LANES - half, axis=1)
        x_dn = pltpu.roll(x, half, axis=1)
        return x * cos + jnp.where(lane_h < half, -x_up, x_dn) * sin

    for g in range(2):
        for part in range(2):
            base = part * 2 * blk
            wk_ref[g, base:base + blk, :] = wk_ref[g, base + blk:base + 2 * blk, :]
            wv_ref[g, base:base + blk, 0:LANES] = wv_ref[g, base + blk:base + 2 * blk, 0:LANES]
    k_r = rope(k_ref[...])
    v_c = v_ref[...]
    k_sw = pltpu.roll(k_r, hd, axis=1)
    v_sw = pltpu.roll(v_c, hd, axis=1)
    zero = jnp.zeros_like(k_r)
    for g in range(2):
        k_lo, k_hi = (k_r, k_sw) if g == 0 else (k_sw, k_r)
        v_lo, v_hi = (v_c, v_sw) if g == 0 else (v_sw, v_c)
        wk_ref[g, blk:2 * blk, :] = jnp.where(lo, k_lo, zero).astype(BF16)
        wk_ref[g, 3 * blk:4 * blk, :] = jnp.where(lo, zero, k_hi).astype(BF16)
        wv_ref[g, blk:2 * blk, 0:LANES] = jnp.where(lo, v_lo, zero).astype(BF16)
        wv_ref[g, 3 * blk:4 * blk, 0:LANES] = jnp.where(lo, zero, v_hi).astype(BF16)

    r = lax.broadcasted_iota(jnp.int32, (blk, 2 * blk), 0)
    c = lax.broadcasted_iota(jnp.int32, (blk, 2 * blk), 1)
    valid = (c > r) & (c <= r + WINDOW) & ((n > 0) | (c >= blk))
    bias_ref[...] = jnp.where(valid, 0.0, -jnp.inf).astype(F32)

    qscale = hd ** -0.5
    for p in range(n_pairs):
        g, pp = divmod(p, ppg)
        qr_ref[g, pp * blk:(pp + 1) * blk, :] = (
            rope(q_ref[:, p * LANES:(p + 1) * LANES]) * qscale).astype(BF16)

    for g in range(2):
        s_ref[g] = lax.dot_general(qr_ref[g], wk_ref[g], NT_DIMS, preferred_element_type=F32)

    for g in range(2):
        for pp in range(ppg):
            for hh in range(2):
                sink = sink_ref[2 * (g * ppg + pp) + hh]
                kc = slice(hh * 2 * blk, (hh + 1) * 2 * blk)
                for rc in range(blk // rc_rows):
                    rows = slice(pp * blk + rc * rc_rows, pp * blk + (rc + 1) * rc_rows)
                    sh = s_ref[g, rows, kc] + bias_ref[rc * rc_rows:(rc + 1) * rc_rows, :]
                    m = jnp.maximum(jnp.max(sh, axis=-1, keepdims=True), sink)
                    p_ref[g, rows, kc] = jnp.exp(sh - m).astype(BF16)
                    es_ref[g, rows, hh * hd:(hh + 1) * hd] = jnp.broadcast_to(
                        jnp.exp(sink - m), (rc_rows, hd))

    for g in range(2):
        o2 = jnp.dot(p_ref[g], wv_ref[g], preferred_element_type=F32)
        for pp in range(ppg):
            rows = slice(pp * blk, (pp + 1) * blk)
            cols = slice((g * ppg + pp) * LANES, (g * ppg + pp + 1) * LANES)
            den = o2[rows, LANES:] + es_ref[g, rows, :]
            o_ref[:, cols] = (o2[rows, :LANES] / den * _silu(g_ref[:, cols])).astype(o_ref.dtype)


def _swa(h, pos_f, sinks, bsz, seq, cols, w_in_t, layer, gate_row0, gate_branches, gate_tn):
    col_q, col_k, col_v, col_g = cols
    n_heads = sinks.shape[0]
    bw = n_heads * SWA_HEAD_DIM
    n_pairs = bw // LANES
    assert LANES == 2 * SWA_HEAD_DIM and n_pairs % 2 == 0
    n_sub = SWA_BLOCKS_PER_STEP
    blk = n_sub * SWA_BLOCK
    nblk = seq // blk
    scratch = [pltpu.VMEM((2, 4 * SWA_BLOCK, LANES), BF16),
               pltpu.VMEM((2, 4 * SWA_BLOCK, 2 * LANES), BF16),
               pltpu.VMEM((2, 4 * SWA_BLOCK, LANES), BF16),
               pltpu.VMEM((2, 4 * SWA_BLOCK, 4 * SWA_BLOCK), F32),
               pltpu.VMEM((2, 4 * SWA_BLOCK, 4 * SWA_BLOCK), BF16),
               pltpu.VMEM((2, 4 * SWA_BLOCK, LANES), F32),
               pltpu.VMEM((SWA_BLOCK, 2 * SWA_BLOCK), F32)]
    g_in, g_out, g_shape, g_bytes = _gate_stage_plan(
        w_in_t, layer, gate_row0, gate_branches, bsz * nblk, gate_tn, lambda b, i: b * nblk + i)
    blocks = (2 * _nbytes((blk, bw), F32) + 3 * _nbytes((blk, LANES), F32) + _nbytes((blk, bw), BF16)
              + g_bytes)
    resident = sum(_nbytes(sc.shape, sc.dtype) for sc in scratch)

    return pl.pallas_call(
        functools.partial(_swa_kernel, n_pairs=n_pairs, n_sub=n_sub),
        out_shape=[jax.ShapeDtypeStruct((bsz * seq, bw), BF16), g_shape],
        grid=(bsz, nblk),
        in_specs=[pl.BlockSpec(memory_space=pltpu.SMEM),
                  pl.BlockSpec((blk, bw), lambda b, i: (b * nblk + i, col_q // bw)),
                  pl.BlockSpec((blk, LANES), lambda b, i: (b * nblk + i, col_k // LANES)),
                  pl.BlockSpec((blk, LANES), lambda b, i: (b * nblk + i, col_v // LANES)),
                  pl.BlockSpec((blk, bw), lambda b, i: (b * nblk + i, col_g // bw)),
                  pl.BlockSpec((blk, 1), lambda b, i: (b * nblk + i, 0)),
                  pl.BlockSpec((1, LANES), lambda b, i: (0, 0)),
                  g_in],
        out_specs=[pl.BlockSpec((blk, bw), lambda b, i: (b * nblk + i, 0)), g_out],
        scratch_shapes=scratch,
        compiler_params=pltpu.CompilerParams(
            dimension_semantics=("parallel", "arbitrary"),
            vmem_limit_bytes=_vmem_limit(blocks, resident)),
        name="swa_mixer",
    )(sinks, h, h, h, h, pos_f, _rope_lane_table(), w_in_t)


def _mem_kernel(q_ref, g_ref, mk_ref, mv_ref, o_ref, *, hd):
    scale = hd ** -0.5
    for h in range(XA_HEADS):
        cols = slice(h * hd, (h + 1) * hd)
        qh = q_ref[:, cols].astype(BF16)
        s = lax.dot_general(qh, mk_ref[:, cols], NT_DIMS, preferred_element_type=F32) * scale
        m = jnp.max(s, axis=-1, keepdims=True)
        e = jnp.exp(s - m)
        p = e / jnp.sum(e, axis=-1, keepdims=True)
        o = jnp.dot(p.astype(BF16), mv_ref[:, cols], preferred_element_type=F32)
        o_ref[:, cols] = (o * _silu(g_ref[:, cols])).astype(o_ref.dtype)


def _pool_mem_kernel(u_ref, pg_ref, wp_ref, sc_ref, q_ref, qg_ref, mk_ref, mv_ref,
                     op_ref, om_ref, buf, lvl_a, lvl_b, *, ts, group, hd):
    _pool_kernel(u_ref, pg_ref, wp_ref, sc_ref, op_ref, buf, lvl_a, lvl_b, ts=ts, group=group)
    _mem_kernel(q_ref, qg_ref, mk_ref, mv_ref, om_ref, hd=hd)


def _pool_mem(h, w_pool_b, pool_scale, mkv, bsz, seq, mem_len, pool_cols, mem_cols, ts):
    col_u, col_pg = pool_cols
    col_q, col_qg = mem_cols
    bw = pool_scale.shape[-1]
    assert mkv.shape[1] == 2 * bw
    group = bw // len(POOL_WINDOWS)
    hd = bw // XA_HEADS
    nblk = seq // ts
    blocks = 4 * _nbytes((ts, bw), F32) + 2 * _nbytes((ts, bw), BF16) + 2 * _nbytes((mem_len, bw), BF16)
    scratch = [pltpu.VMEM((ts + POOL_PAD, bw), F32) for _ in range(3)]
    resident = sum(_nbytes(sc.shape, sc.dtype) for sc in scratch)
    row = lambda b, i: b * nblk + i
    return pl.pallas_call(
        functools.partial(_pool_mem_kernel, ts=ts, group=group, hd=hd),
        out_shape=[jax.ShapeDtypeStruct((bsz * seq, bw), BF16)] * 2,
        grid=(bsz, nblk),
        in_specs=[pl.BlockSpec((ts, bw), lambda b, i: (row(b, i), col_u // bw)),
                  pl.BlockSpec((ts, bw), lambda b, i: (row(b, i), col_pg // bw)),
                  pl.BlockSpec(w_pool_b.shape, lambda b, i: (0, 0, 0)),
                  pl.BlockSpec((1, bw), lambda b, i: (0, 0)),
                  pl.BlockSpec((ts, bw), lambda b, i: (row(b, i), col_q // bw)),
                  pl.BlockSpec((ts, bw), lambda b, i: (row(b, i), col_qg // bw)),
                  pl.BlockSpec((mem_len, bw), lambda b, i: (b, 0)),
                  pl.BlockSpec((mem_len, bw), lambda b, i: (b, 1))],
        out_specs=[pl.BlockSpec((ts, bw), lambda b, i: (row(b, i), 0))] * 2,
        scratch_shapes=scratch,
        compiler_params=pltpu.CompilerParams(
            dimension_semantics=("parallel", "arbitrary"),
            vmem_limit_bytes=_vmem_limit(blocks, resident)),
        name="pool_mem_mixers",
    )(h, h, w_pool_b, pool_scale.reshape(1, bw), h, h, mkv, mkv)


IN_TN = 2 * LANES
MERGE_TN = 2 * LANES


def _main_layout(d_model):
    bw = d_model // 4
    dkt, dvt = bw // 2, bw
    kvw = 2 * SWA_HEAD_DIM
    names = ["pool_u", "pool_g", "gq", "gk", "gv", "gg", "glr", "sq", "sk", "sv", "sg", "xq", "xg"]
    widths = [bw, bw, dkt, dkt, dvt, dvt, GLA_RANK, bw, kvw, kvw, bw, bw, bw]
    src = {}
    off = 0
    for nme, w in zip(names, widths):
        src[nme] = (off, w)
        off += w
    n_main = off
    order = ["pool_u", "pool_g", "gv", "gg", "sq", "sg", "xq", "xg", "gq", "gk", "sk", "sv", "glr"]
    dst = {}
    cols = []
    off = 0
    for nme in order:
        o, w = src[nme]
        w_dst = w if nme != "glr" else IN_TN
        assert off % w_dst == 0
        dst[nme] = off
        cols.append(o + np.arange(w_dst))
        off += w_dst
    cols = np.concatenate(cols)
    assert off % IN_TN == 0 and cols.max() < n_main
    starts = cols[::IN_TN]
    assert np.all(cols.reshape(-1, IN_TN) == starts[:, None] + np.arange(IN_TN))
    return starts, dst, n_main


def _hybrid_layer(l, x, xb, mem_b, pos_f, w_in_t, w_pool, pool_scale, w_gla_up, b_gla, gla_norm,
                  sinks, w_mem_kv, w_branch, w_out, ln_g, ln_b, alpha, emit_bf16, bsz, seq):
    m, d = x.shape
    col_starts, dst, n_main = _main_layout(d)
    w_up_pad = jnp.concatenate(
        [w_gla_up[l], jnp.zeros((IN_TN - GLA_RANK, w_gla_up.shape[-1]), w_gla_up.dtype)], axis=0)

    h, w_out_b = _in_proj(xb, w_in_t, w_out, l, col_starts, tm=2048, tn=IN_TN, per_step=2)
    mkv = _matmul(mem_b, w_mem_kv, l, BF16, tm=mem_b.shape[0], tn=512, name="mem_kv")

    o_pool, o_mem = _pool_mem(h, w_pool[l].astype(BF16), pool_scale[l], mkv, bsz, seq,
                              mem_b.shape[0] // bsz, (dst["pool_u"], dst["pool_g"]),
                              (dst["xq"], dst["xg"]), ts=512)
    half = N_BRANCH // 2
    o_gla, w_mg_a = _gla(h, w_up_pad, b_gla[l], gla_norm[l], bsz, seq,
                         (dst["gq"], dst["gk"], dst["gv"], dst["gg"], dst["glr"]), 512,
                         w_in_t, l, n_main, half, MERGE_TN)
    o_swa, w_mg_b = _swa(h, pos_f, sinks[l], bsz, seq, (dst["sq"], dst["sk"], dst["sv"], dst["sg"]),
                         w_in_t, l, n_main + half * d, half, MERGE_TN)

    y = _merge(xb, (o_pool, o_gla, o_swa, o_mem), w_mg_a, w_mg_b, w_branch, l, tm=1024, tn=MERGE_TN)
    z = _out_proj(y, w_out_b, x, alpha, tm=1024, tn=256, per_step=4)
    return _layer_norm(z, ln_g[l], ln_b[l], tm=512, emit_bf16=emit_bf16)


def kernel(x, mem, positions, w_in, w_pool, pool_scale, w_gla_up, b_gla, gla_norm, sinks,
           w_mem_kv, w_branch, w_out, ln_g, ln_b):
    bsz, seq, d = x.shape
    depth = w_in.shape[0]
    alpha = (2 * depth) ** 0.25
    xf = x.reshape(bsz * seq, d)
    xb = xf.astype(BF16)
    mem_b = mem.reshape(bsz * mem.shape[1], d).astype(BF16)
    pos_f = positions.astype(F32).reshape(bsz * seq, 1)
    w_in_t = jnp.swapaxes(w_in, 1, 2)
    for l in range(depth):
        xf, xb = _hybrid_layer(
            l, xf, xb, mem_b, pos_f, w_in_t, w_pool, pool_scale, w_gla_up, b_gla,
            gla_norm, sinks, w_mem_kv, w_branch, w_out, ln_g, ln_b,
            alpha, l + 1 < depth, bsz, seq)
    return xf.reshape(bsz, seq, d)
```
